```python
import math
import jax
import jax.numpy as jnp
from jax import lax
import numpy as np

D_MODEL = 2048
BATCH = 4
SEQ = 2048
DEPTH = 2
DEC_BATCH = 128
DEC_SEQ = 8
PAST_LEN = 16384
PAGE_SIZE = 128

D_MIX = 2 * D_MODEL
HEAD_DIM = 64
CONV_WIDTH = 4
LRU_WIDTH = D_MIX // 4
LRU_HEADS = 8
LRU_BLOCK = LRU_WIDTH // LRU_HEADS
LRU_C = 8.0
SSD_WIDTH = D_MIX // 2
SSD_HEADS = SSD_WIDTH // HEAD_DIM
SSD_GROUPS = 8
SSD_STATE = 128
SSD_CHUNK = 128
SSD_CONV_DIM = SSD_WIDTH + 2 * SSD_GROUPS * SSD_STATE
RWKV_WIDTH = D_MIX - LRU_WIDTH - SSD_WIDTH
RWKV_HEADS = RWKV_WIDTH // HEAD_DIM
RWKV_R_W = max(32, int(round(1.8 * RWKV_WIDTH ** 0.5 / 32)) * 32)
RWKV_R_A = max(32, int(round(1.8 * RWKV_WIDTH ** 0.5 / 32)) * 32)
RWKV_R_G = max(32, int(round(0.6 * RWKV_WIDTH ** 0.8 / 32)) * 32)
RWKV_SHIFT = 3 * RWKV_WIDTH + RWKV_R_W + RWKV_R_A + RWKV_R_G
RWKV_GN_EPS = 64e-5
D_FF = -(-(8 * D_MODEL) // (3 * 256)) * 256
ALPHA = (2 * DEPTH) ** 0.25
BETA = (8 * DEPTH) ** -0.25
IN_COLS = 2 * LRU_WIDTH + SSD_WIDTH + SSD_CONV_DIM + SSD_HEADS + RWKV_SHIFT
IN_SPLITS = (LRU_WIDTH,
             2 * LRU_WIDTH,
             2 * LRU_WIDTH + SSD_WIDTH,
             2 * LRU_WIDTH + SSD_WIDTH + SSD_CONV_DIM,
             2 * LRU_WIDTH + SSD_WIDTH + SSD_CONV_DIM + SSD_HEADS)
RWKV_SPLITS = (RWKV_WIDTH, 2 * RWKV_WIDTH, 3 * RWKV_WIDTH,
               3 * RWKV_WIDTH + RWKV_R_W, 3 * RWKV_WIDTH + RWKV_R_W + RWKV_R_A)

kernel_name = 'hymba_style_rglru_ssd_rwkv7_step'


def _layer_norm(x, g, b, eps=1e-5):
    xf = x.astype(jnp.float32)
    mu = jnp.mean(xf, axis=-1, keepdims=True)
    var = jnp.mean(jnp.square(xf - mu), axis=-1, keepdims=True)
    return ((xf - mu) * lax.rsqrt(var + eps) * g.astype(jnp.float32) + b.astype(jnp.float32)).astype(x.dtype)


def _causal_conv(u, buf, w, b):
    L = u.shape[1]
    full = jnp.concatenate([buf.astype(u.dtype), u], axis=1)
    out = b + sum(full[:, k:k + L] * w[k] for k in range(CONV_WIDTH))
    return out, full[:, full.shape[1] - (CONV_WIDTH - 1):]


def _rglru(xc, h0, wa, ba, wx, bx, lam):
    f32 = jnp.float32
    Bn, L, W = xc.shape
    xf = xc.astype(f32)
    xb = xf.reshape(Bn, L, LRU_HEADS, LRU_BLOCK)
    r = jax.nn.sigmoid(jnp.einsum('blhi,hij->blhj', xb, wa.astype(f32)).reshape(Bn, L, W) + ba.astype(f32))
    i = jax.nn.sigmoid(jnp.einsum('blhi,hij->blhj', xb, wx.astype(f32)).reshape(Bn, L, W) + bx.astype(f32))
    log_a = -LRU_C * r * jax.nn.softplus(-lam.astype(f32))
    a = jnp.exp(log_a)
    b = jnp.sqrt(-jnp.expm1(2.0 * log_a)) * (i * xf)
    b = b.at[:, 0].add(a[:, 0] * h0.astype(f32))

    def comb(e1, e2):
        return (e1[0] * e2[0], e2[0] * e1[1] + e2[1])

    _, h = lax.associative_scan(comb, (a, b), axis=1)
    return h, h[:, -1]


def _ssd(x, dt, A, Bm, Cm, s0):
    b_, L, H, P = x.shape
    G, N = Bm.shape[2], Bm.shape[3]
    E = H // G
    T = min(SSD_CHUNK, L)
    NC = L // T
    xr = x.reshape(b_, NC, T, G, E, P)
    dtr = dt.reshape(b_, NC, T, G, E)
    Br = Bm.reshape(b_, NC, T, G, N)
    Cr = Cm.reshape(b_, NC, T, G, N)
    dA = jnp.moveaxis(dtr * A.reshape(G, E), 2, -1)
    cum = jnp.cumsum(dA, axis=-1)
    mask = jnp.tril(jnp.ones((T, T), dtype=bool))
    seg = cum[..., :, None] - cum[..., None, :]
    Lm = jnp.exp(jnp.where(mask, seg, -jnp.inf))
    xdt = xr * dtr[..., None]
    cb = jnp.einsum('bcign,bcjgn->bcgij', Cr, Br)
    y_diag = jnp.einsum('bcgij,bcgeij,bcjgep->bcigep', cb, Lm, xdt)
    decay_end = jnp.exp(cum[..., -1:] - cum)
    states = jnp.einsum('bcjgn,bcgej,bcjgep->bcgepn', Br, decay_end, xdt)
    chunk_decay = jnp.exp(cum[..., -1])

    def step(s, inp):
        st, dec = inp
        return s * dec[..., None, None] + st, s

    s_fin, s_in = lax.scan(step, s0.reshape(b_, G, E, P, N),
                           (jnp.moveaxis(states, 1, 0), jnp.moveaxis(chunk_decay, 1, 0)))
    s_in = jnp.moveaxis(s_in, 0, 1)
    y_off = jnp.einsum('bcign,bcgei,bcgepn->bcigep', Cr, jnp.exp(cum), s_in)
    y = (y_diag + y_off).reshape(b_, L, H, P)
    return y, s_fin.reshape(b_, H, P, N)


def _rwkv7(p, shift0, s0, pr):
    f32 = jnp.float32
    Bn, L, _ = p.shape
    prev = jnp.concatenate([shift0[:, None].astype(p.dtype), p[:, :-1]], axis=1)
    xs = (p + (prev - p) * pr['rwkv_mu']).astype(f32)
    r, k, v, xw, xa, xg = jnp.split(xs, RWKV_SPLITS, axis=-1)
    w_log = -jax.nn.softplus(-(pr['rwkv_w0'].astype(f32) + jnp.tanh(xw) @ pr['rwkv_w_up'].astype(f32))) - 0.5
    decay = jnp.exp(-jnp.exp(w_log))
    a = jax.nn.sigmoid(pr['rwkv_a0'].astype(f32) + xa @ pr['rwkv_a_up'].astype(f32))
    g = jax.nn.sigmoid(xg) @ pr['rwkv_g_up'].astype(f32)

    def heads(t):
        return t.reshape(Bn, L, RWKV_HEADS, HEAD_DIM)

    kk = heads(k * pr['rwkv_k_k'].astype(f32))
    kk = kk / jnp.maximum(jnp.sqrt(jnp.sum(kk * kk, axis=-1, keepdims=True)), 1e-12)
    k = k * (1.0 + (a - 1.0) * pr['rwkv_k_a'].astype(f32))
    rh, kh, vh, wh, ah = heads(r), heads(k), heads(v), heads(decay), heads(a)

    def step(S, inp):
        r_t, k_t, v_t, w_t, kk_t, a_t = inp
        sa = jnp.einsum('bhvk,bhk->bhv', S, -kk_t)
        S = (S * w_t[:, :, None, :] + sa[..., None] * (kk_t * a_t)[:, :, None, :]
             + v_t[..., None] * k_t[:, :, None, :])
        return S, jnp.einsum('bhvk,bhk->bhv', S, r_t)

    seqs = tuple(jnp.moveaxis(t, 1, 0) for t in (rh, kh, vh, wh, kk, ah))
    s_fin, o = lax.scan(step, s0.astype(f32), seqs)
    o = jnp.moveaxis(o, 0, 1)
    mu = jnp.mean(o, axis=-1, keepdims=True)
    var = jnp.mean(jnp.square(o - mu), axis=-1, keepdims=True)
    o = ((o - mu) * lax.rsqrt(var + RWKV_GN_EPS)).reshape(Bn, L, RWKV_WIDTH)
    o = o * pr['rwkv_ln_g'].astype(f32) + pr['rwkv_ln_b'].astype(f32)
    bonus = jnp.sum(rh * kh * pr['rwkv_r_k'].astype(f32), axis=-1, keepdims=True) * vh
    o = (o + bonus.reshape(Bn, L, RWKV_WIDTH)) * g
    return o, p[:, -1], s_fin


def _layer(x, st, pr):
    f32 = jnp.float32
    lru_conv0, lru_h0, ssd_conv0, ssd_s0, rw_shift0, rw_s0 = st
    Bn, L, _ = x.shape
    proj = x @ pr['w_in']
    gate, lx, z, xbc, dtr, rw = jnp.split(proj, IN_SPLITS, axis=-1)
    lx, lru_conv1 = _causal_conv(lx, lru_conv0, pr['lru_conv_w'], pr['lru_conv_b'])
    h, lru_h1 = _rglru(lx, lru_h0, pr['lru_wa'], pr['lru_ba'], pr['lru_wx'], pr['lru_bx'], pr['lru_lambda'])
    out_a = h * jax.nn.gelu(gate.astype(f32))
    xbc, ssd_conv1 = _causal_conv(xbc, ssd_conv0, pr['ssd_conv_w'], pr['ssd_conv_b'])
    xbc = jax.nn.silu(xbc.astype(f32))
    xs, bm, cm = jnp.split(xbc, (SSD_WIDTH, SSD_WIDTH + SSD_GROUPS * SSD_STATE), axis=-1)
    dt = jax.nn.softplus(dtr.astype(f32) + pr['ssd_dt_bias'].astype(f32))
    A = -jnp.exp(pr['ssd_a_log'].astype(f32))
    xh = xs.reshape(Bn, L, SSD_HEADS, HEAD_DIM)
    y, ssd_s1 = _ssd(xh, dt, A, bm.reshape(Bn, L, SSD_GROUPS, SSD_STATE),
                     cm.reshape(Bn, L, SSD_GROUPS, SSD_STATE), ssd_s0.astype(f32))
    y = (y + pr['ssd_d'].astype(f32)[:, None] * xh).reshape(Bn, L, SSD_WIDTH) * jax.nn.silu(z.astype(f32))
    yg = y.reshape(Bn, L, SSD_GROUPS, SSD_WIDTH // SSD_GROUPS)
    yg = yg * lax.rsqrt(jnp.mean(yg * yg, axis=-1, keepdims=True) + 1e-5)
    out_b = yg.reshape(Bn, L, SSD_WIDTH) * pr['ssd_norm_g'].astype(f32)
    out_c, rw_shift1, rw_s1 = _rwkv7(rw, rw_shift0, rw_s0, pr)
    mix = jnp.concatenate([out_a, out_b, out_c], axis=-1).astype(x.dtype) @ pr['w_out']
    x = _layer_norm(ALPHA * x + mix, pr['ln1_g'], pr['ln1_b'])
    ffn = (jax.nn.silu(x @ pr['w_gate']) * (x @ pr['w_up'])) @ pr['w_down']
    x = _layer_norm(ALPHA * x + ffn, pr['ln2_g'], pr['ln2_b'])
    return x, (lru_conv1, lru_h1, ssd_conv1, ssd_s1, rw_shift1, rw_s1)


def setup_inputs(seed: int = 0) -> dict:
    key = jax.random.key(seed)
    kit = iter(jax.random.split(key, 64))
    f32 = jnp.float32

    def nrm(shape, scale):
        return jax.random.normal(next(kit), shape, f32) * scale

    def uni(shape, lo, hi):
        return jax.random.uniform(next(kit), shape, f32, lo, hi)

    x_prompt = nrm((BATCH, SEQ, D_MODEL), 1.0)
    x_sample = nrm((DEC_BATCH, DEC_SEQ, D_MODEL), 1.0)
    state_lru_conv = nrm((DEPTH, DEC_BATCH, CONV_WIDTH - 1, LRU_WIDTH), 1.0)
    state_lru_h = nrm((DEPTH, DEC_BATCH, LRU_WIDTH), 0.5)
    state_ssd_conv = nrm((DEPTH, DEC_BATCH, CONV_WIDTH - 1, SSD_CONV_DIM), 1.0)
    state_ssd = nrm((DEPTH, DEC_BATCH, SSD_HEADS, HEAD_DIM, SSD_STATE), 0.3)
    state_rwkv_shift = nrm((DEPTH, DEC_BATCH, RWKV_SHIFT), 1.0)
    state_rwkv = nrm((DEPTH, DEC_BATCH, RWKV_HEADS, HEAD_DIM, HEAD_DIM), 0.1)

    w_in = nrm((DEPTH, D_MODEL, IN_COLS), D_MODEL ** -0.5)
    lru_conv_w = nrm((DEPTH, CONV_WIDTH, LRU_WIDTH), CONV_WIDTH ** -0.5)
    lru_conv_b = nrm((DEPTH, LRU_WIDTH), 0.01)
    lru_wa = nrm((DEPTH, LRU_HEADS, LRU_BLOCK, LRU_BLOCK), LRU_BLOCK ** -0.5)
    lru_ba = nrm((DEPTH, LRU_WIDTH), 0.01)
    lru_wx = nrm((DEPTH, LRU_HEADS, LRU_BLOCK, LRU_BLOCK), LRU_BLOCK ** -0.5)
    lru_bx = nrm((DEPTH, LRU_WIDTH), 0.01)
    a_c = uni((DEPTH, LRU_WIDTH), 0.9, 0.999) ** (1.0 / LRU_C)
    lru_lambda = jnp.log(a_c) - jnp.log1p(-a_c)
    ssd_conv_w = nrm((DEPTH, CONV_WIDTH, SSD_CONV_DIM), CONV_WIDTH ** -0.5)
    ssd_conv_b = nrm((DEPTH, SSD_CONV_DIM), 0.01)
    dt0 = jnp.exp(uni((DEPTH, SSD_HEADS), math.log(1e-3), math.log(1e-1)))
    ssd_dt_bias = dt0 + jnp.log(-jnp.expm1(-dt0))
    ssd_a_log = jnp.log(uni((DEPTH, SSD_HEADS), 1.0, 16.0))
    ssd_d = 1.0 + nrm((DEPTH, SSD_HEADS), 0.01)
    ssd_norm_g = 1.0 + nrm((DEPTH, SSD_WIDTH), 0.01)
    rwkv_mu = uni((DEPTH, RWKV_SHIFT), 0.0, 1.0)
    rwkv_w0 = uni((DEPTH, RWKV_WIDTH), -6.0, 0.0)
    rwkv_w_up = nrm((DEPTH, RWKV_R_W, RWKV_WIDTH), 0.5 * RWKV_R_W ** -0.5)
    rwkv_a0 = nrm((DEPTH, RWKV_WIDTH), 0.1)
    rwkv_a_up = nrm((DEPTH, RWKV_R_A, RWKV_WIDTH), 0.5 * RWKV_R_A ** -0.5)
    rwkv_g_up = nrm((DEPTH, RWKV_R_G, RWKV_WIDTH), RWKV_R_G ** -0.5)
    rwkv_k_k = 0.85 + nrm((DEPTH, RWKV_WIDTH), 0.02)
    rwkv_k_a = 1.0 + nrm((DEPTH, RWKV_WIDTH), 0.02)
    rwkv_r_k = nrm((DEPTH, RWKV_HEADS, HEAD_DIM), 0.1)
    rwkv_ln_g = 1.0 + nrm((DEPTH, RWKV_WIDTH), 0.01)
    rwkv_ln_b = nrm((DEPTH, RWKV_WIDTH), 0.01)
    w_out = nrm((DEPTH, D_MIX, D_MODEL), BETA * D_MIX ** -0.5)
    ln1_g = 1.0 + nrm((DEPTH, D_MODEL), 0.01)
    ln1_b = nrm((DEPTH, D_MODEL), 0.01)
    w_gate = nrm((DEPTH, D_MODEL, D_FF), D_MODEL ** -0.5)
    w_up = nrm((DEPTH, D_MODEL, D_FF), D_MODEL ** -0.5)
    w_down = nrm((DEPTH, D_FF, D_MODEL), BETA * D_FF ** -0.5)
    ln2_g = 1.0 + nrm((DEPTH, D_MODEL), 0.01)
    ln2_b = nrm((DEPTH, D_MODEL), 0.01)
    return {
        'x_prompt': x_prompt, 'x_sample': x_sample,
        'state_lru_conv': state_lru_conv, 'state_lru_h': state_lru_h,
        'state_ssd_conv': state_ssd_conv, 'state_ssd': state_ssd,
        'state_rwkv_shift': state_rwkv_shift, 'state_rwkv': state_rwkv,
        'w_in': w_in,
        'lru_conv_w': lru_conv_w, 'lru_conv_b': lru_conv_b,
        'lru_wa': lru_wa, 'lru_ba': lru_ba, 'lru_wx': lru_wx, 'lru_bx': lru_bx, 'lru_lambda': lru_lambda,
        'ssd_conv_w': ssd_conv_w, 'ssd_conv_b': ssd_conv_b, 'ssd_dt_bias': ssd_dt_bias,
        'ssd_a_log': ssd_a_log, 'ssd_d': ssd_d, 'ssd_norm_g': ssd_norm_g,
        'rwkv_mu': rwkv_mu, 'rwkv_w0': rwkv_w0, 'rwkv_w_up': rwkv_w_up, 'rwkv_a0': rwkv_a0,
        'rwkv_a_up': rwkv_a_up, 'rwkv_g_up': rwkv_g_up, 'rwkv_k_k': rwkv_k_k, 'rwkv_k_a': rwkv_k_a,
        'rwkv_r_k': rwkv_r_k, 'rwkv_ln_g': rwkv_ln_g, 'rwkv_ln_b': rwkv_ln_b,
        'w_out': w_out, 'ln1_g': ln1_g, 'ln1_b': ln1_b,
        'w_gate': w_gate, 'w_up': w_up, 'w_down': w_down, 'ln2_g': ln2_g, 'ln2_b': ln2_b,
    }


def reference(x_prompt, x_sample, state_lru_conv, state_lru_h, state_ssd_conv, state_ssd,
              state_rwkv_shift, state_rwkv, w_in, lru_conv_w, lru_conv_b, lru_wa, lru_ba, lru_wx,
              lru_bx, lru_lambda, ssd_conv_w, ssd_conv_b, ssd_dt_bias, ssd_a_log, ssd_d, ssd_norm_g,
              rwkv_mu, rwkv_w0, rwkv_w_up, rwkv_a0, rwkv_a_up, rwkv_g_up, rwkv_k_k, rwkv_k_a,
              rwkv_r_k, rwkv_ln_g, rwkv_ln_b, w_out, ln1_g, ln1_b, w_gate, w_up, w_down, ln2_g, ln2_b):
    f32 = jnp.float32
    bp = x_prompt.shape[0]
    yp, ys = x_prompt, x_sample
    new_p = [[] for _ in range(6)]
    new_s = [[] for _ in range(6)]
    for l in range(DEPTH):
        pr = {
            'w_in': w_in[l], 'lru_conv_w': lru_conv_w[l], 'lru_conv_b': lru_conv_b[l],
            'lru_wa': lru_wa[l], 'lru_ba': lru_ba[l], 'lru_wx': lru_wx[l], 'lru_bx': lru_bx[l],
            'lru_lambda': lru_lambda[l], 'ssd_conv_w': ssd_conv_w[l], 'ssd_conv_b': ssd_conv_b[l],
            'ssd_dt_bias': ssd_dt_bias[l], 'ssd_a_log': ssd_a_log[l], 'ssd_d': ssd_d[l],
            'ssd_norm_g': ssd_norm_g[l], 'rwkv_mu': rwkv_mu[l], 'rwkv_w0': rwkv_w0[l],
            'rwkv_w_up': rwkv_w_up[l], 'rwkv_a0': rwkv_a0[l], 'rwkv_a_up': rwkv_a_up[l],
            'rwkv_g_up': rwkv_g_up[l], 'rwkv_k_k': rwkv_k_k[l], 'rwkv_k_a': rwkv_k_a[l],
            'rwkv_r_k': rwkv_r_k[l], 'rwkv_ln_g': rwkv_ln_g[l], 'rwkv_ln_b': rwkv_ln_b[l],
            'w_out': w_out[l], 'ln1_g': ln1_g[l], 'ln1_b': ln1_b[l],
            'w_gate': w_gate[l], 'w_up': w_up[l], 'w_down': w_down[l],
            'ln2_g': ln2_g[l], 'ln2_b': ln2_b[l],
        }
        zero_state = (
            jnp.zeros((bp, CONV_WIDTH - 1, LRU_WIDTH), x_prompt.dtype),
            jnp.zeros((bp, LRU_WIDTH), f32),
            jnp.zeros((bp, CONV_WIDTH - 1, SSD_CONV_DIM), x_prompt.dtype),
            jnp.zeros((bp, SSD_HEADS, HEAD_DIM, SSD_STATE), f32),
            jnp.zeros((bp, RWKV_SHIFT), x_prompt.dtype),
            jnp.zeros((bp, RWKV_HEADS, HEAD_DIM, HEAD_DIM), f32),
        )
        yp, sp = _layer(yp, zero_state, pr)
        ys, ss = _layer(ys, (state_lru_conv[l], state_lru_h[l], state_ssd_conv[l], state_ssd[l],
                             state_rwkv_shift[l], state_rwkv[l]), pr)
        for i in range(6):
            new_p[i].append(sp[i])
            new_s[i].append(ss[i])
    p_lru_conv = jnp.stack(new_p[0])
    p_lru_h = jnp.stack(new_p[1])
    p_ssd_conv = jnp.stack(new_p[2])
    p_ssd = jnp.stack(new_p[3])
    p_rwkv_shift = jnp.stack(new_p[4])
    p_rwkv = jnp.stack(new_p[5])
    s_lru_conv = jnp.stack(new_s[0])
    s_lru_h = jnp.stack(new_s[1])
    s_ssd_conv = jnp.stack(new_s[2])
    s_ssd = jnp.stack(new_s[3])
    s_rwkv_shift = jnp.stack(new_s[4])
    s_rwkv = jnp.stack(new_s[5])
    return (yp, ys, p_lru_conv, p_lru_h, p_ssd_conv, p_ssd, p_rwkv_shift, p_rwkv,
            s_lru_conv, s_lru_h, s_ssd_conv, s_ssd, s_rwkv_shift, s_rwkv)
```

```python
import functools
import math

import jax
import jax.numpy as jnp
from jax import lax
from jax.experimental import pallas as pl
from jax.experimental.pallas import tpu as pltpu

F32 = jnp.float32
BF16 = jnp.bfloat16

D_MODEL = 2048
DEPTH = 2
D_MIX = 2 * D_MODEL
HEAD_DIM = 64
CONV_WIDTH = 4
LRU_WIDTH = D_MIX // 4
LRU_HEADS = 8
LRU_BLOCK = LRU_WIDTH // LRU_HEADS
LRU_C = 8.0
SSD_WIDTH = D_MIX // 2
SSD_HEADS = SSD_WIDTH // HEAD_DIM
SSD_GROUPS = 8
SSD_STATE = 128
SSD_CHUNK = 128
SSD_CONV_DIM = SSD_WIDTH + 2 * SSD_GROUPS * SSD_STATE
SSD_GROUP_WIDTH = SSD_WIDTH // SSD_GROUPS
SSD_HEADS_PER_GROUP = SSD_HEADS // SSD_GROUPS
RWKV_WIDTH = D_MIX - LRU_WIDTH - SSD_WIDTH
RWKV_HEADS = RWKV_WIDTH // HEAD_DIM
RWKV_R_W = max(32, int(round(1.8 * RWKV_WIDTH ** 0.5 / 32)) * 32)
RWKV_R_A = max(32, int(round(1.8 * RWKV_WIDTH ** 0.5 / 32)) * 32)
RWKV_R_G = max(32, int(round(0.6 * RWKV_WIDTH ** 0.8 / 32)) * 32)
RWKV_SHIFT = 3 * RWKV_WIDTH + RWKV_R_W + RWKV_R_A + RWKV_R_G
RWKV_GN_EPS = 64e-5
RWKV_CHUNK = 64
D_FF = -(-(8 * D_MODEL) // (3 * 256)) * 256
ALPHA = (2 * DEPTH) ** 0.25
LN_EPS = 1e-5

LANES = 128
SUBLANES = 8
VMEM_LIMIT = 56 * 1024 * 1024

C_GATE = 0
C_LX = C_GATE + LRU_WIDTH
C_Z = C_LX + LRU_WIDTH
C_XBC = C_Z + SSD_WIDTH
C_R = C_XBC + SSD_CONV_DIM
C_K = C_R + RWKV_WIDTH
C_V = C_K + RWKV_WIDTH
C_G = C_V + RWKV_WIDTH
G_PAD = 2 * LANES
C_WA = C_G + G_PAD
WA_PAD = LANES
C_DT = C_WA + WA_PAD
DT_PAD = LANES
N_PROJ = C_DT + DT_PAD
assert RWKV_R_G <= G_PAD and RWKV_R_W + RWKV_R_A == WA_PAD and SSD_HEADS <= DT_PAD
RW_PACK = 3 * RWKV_WIDTH + G_PAD + WA_PAD


def _cparams(*sem):
    return pltpu.CompilerParams(dimension_semantics=sem, vmem_limit_bytes=VMEM_LIMIT)


_NN = (((1,), (0,)), ((), ()))
_NT = (((1,), (1,)), ((), ()))
_TN = (((0,), (0,)), ((), ()))


def _dg(a, b, dims):
    return lax.dot_general(a, b, dims, preferred_element_type=F32)


def _dot1(a, b, dims=_NN):
    return _dg(a.astype(BF16), b.astype(BF16), dims)


def _split2(x):
    hi = x.astype(BF16)
    lo = (x - hi.astype(F32)).astype(BF16)
    return hi, lo


def _split3(x):
    hi = x.astype(BF16)
    r1 = x - hi.astype(F32)
    mid = r1.astype(BF16)
    lo = (r1 - mid.astype(F32)).astype(BF16)
    return hi, mid, lo


def _dot3(a, b, dims=_NN):
    ah, al = _split2(a)
    bh, bl = _split2(b)
    return _dg(ah, bh, dims) + (_dg(ah, bl, dims) + _dg(al, bh, dims))


def _dotx(a, w_exact):
    hi, mid, lo = _split3(a)
    return _dg(hi, w_exact, _NN) + (_dg(mid, w_exact, _NN) + _dg(lo, w_exact, _NN))


def _dotx_left(w_exact, a):
    hi, mid, lo = _split3(a)
    return _dg(w_exact, hi, _NN) + (_dg(w_exact, mid, _NN) + _dg(w_exact, lo, _NN))


def _iota2(shape, dim):
    return lax.broadcasted_iota(jnp.int32, shape, dim)


def _softplus(x):
    return jnp.maximum(x, 0.0) + jnp.log1p(jnp.exp(-jnp.abs(x)))


def _sigmoid(x):
    return 1.0 / (1.0 + jnp.exp(-x))


def _silu(x):
    return x * _sigmoid(x)


def _gelu_tanh(x):
    c = math.sqrt(2.0 / math.pi)
    return 0.5 * x * (1.0 + jnp.tanh(c * (x + 0.044715 * (x * x * x))))


def _layer_norm(y, g, b):
    mu = jnp.mean(y, axis=-1, keepdims=True)
    d = y - mu
    var = jnp.mean(d * d, axis=-1, keepdims=True)
    return d * lax.rsqrt(var + LN_EPS) * g + b


def _proj_kernel(x_ref, w_ref, o_ref, xb_ref):
    @pl.when(pl.program_id(1) == 0)
    def _():
        xb_ref[...] = x_ref[...].astype(BF16)

    o_ref[...] = jnp.dot(xb_ref[...], w_ref[...], preferred_element_type=F32)


def _proj(x, w):
    m = x.shape[0]
    tm = min(m, 1024)
    tn = 512
    return pl.pallas_call(
        _proj_kernel,
        grid=(m // tm, N_PROJ // tn),
        in_specs=[pl.BlockSpec((tm, D_MODEL), lambda i, j: (i, 0)),
                  pl.BlockSpec((D_MODEL, tn), lambda i, j: (0, j))],
        out_specs=pl.BlockSpec((tm, tn), lambda i, j: (i, j)),
        out_shape=jax.ShapeDtypeStruct((m, N_PROJ), F32),
        scratch_shapes=[pltpu.VMEM((tm, D_MODEL), BF16)],
        compiler_params=_cparams("arbitrary", "arbitrary"),
        name="proj",
    )(x, w)


OUT_KSTEPS = D_MIX // LRU_WIDTH


def _outproj_kernel(a_ref, b_ref, c_ref, w_ref, x_ref, g_ref, beta_ref, o_ref, acc_ref):
    k = pl.program_id(1)

    @pl.when(k == 0)
    def _():
        acc_ref[...] = jnp.dot(a_ref[...], w_ref[...], preferred_element_type=F32)

    @pl.when(jnp.logical_and(k > 0, k < OUT_KSTEPS - 1))
    def _():
        acc_ref[...] += jnp.dot(b_ref[...], w_ref[...], preferred_element_type=F32)

    @pl.when(k == OUT_KSTEPS - 1)
    def _():
        mix = acc_ref[...] + jnp.dot(c_ref[...], w_ref[...], preferred_element_type=F32)
        o_ref[...] = _layer_norm(ALPHA * x_ref[...] + mix, g_ref[...], beta_ref[...])


def _outproj_ln(out_a, out_b, out_c, w_out, x, g, beta):
    m = x.shape[0]
    tm = min(m, 512)
    tk = LRU_WIDTH
    nb = SSD_WIDTH // tk
    return pl.pallas_call(
        _outproj_kernel,
        grid=(m // tm, OUT_KSTEPS),
        in_specs=[pl.BlockSpec((tm, tk), lambda i, k: (i, 0)),
                  pl.BlockSpec((tm, tk), lambda i, k: (i, jnp.clip(k - 1, 0, nb - 1))),
                  pl.BlockSpec((tm, tk), lambda i, k: (i, 0)),
                  pl.BlockSpec((tk, D_MODEL), lambda i, k: (k, 0)),
                  pl.BlockSpec((tm, D_MODEL), lambda i, k: (i, 0)),
                  pl.BlockSpec((1, D_MODEL), lambda i, k: (0, 0)),
                  pl.BlockSpec((1, D_MODEL), lambda i, k: (0, 0))],
        out_specs=pl.BlockSpec((tm, D_MODEL), lambda i, k: (i, 0)),
        out_shape=jax.ShapeDtypeStruct((m, D_MODEL), F32),
        scratch_shapes=[pltpu.VMEM((tm, D_MODEL), F32)],
        compiler_params=_cparams("arbitrary", "arbitrary"),
        name="outproj_ln",
    )(out_a, out_b, out_c, w_out, x, g, beta)


def _ffn_up_kernel(x_ref, wg_ref, wu_ref, o_ref, xb_ref):
    @pl.when(pl.program_id(1) == 0)
    def _():
        xb_ref[...] = x_ref[...].astype(BF16)

    xb = xb_ref[...]
    gate = jnp.dot(xb, wg_ref[...], preferred_element_type=F32)
    up = jnp.dot(xb, wu_ref[...], preferred_element_type=F32)
    o_ref[...] = (_silu(gate) * up).astype(BF16)


def _ffn_up(x, wg, wu):
    m = x.shape[0]
    tm = min(m, 1024)
    tn = 512
    return pl.pallas_call(
        _ffn_up_kernel,
        grid=(m // tm, D_FF // tn),
        in_specs=[pl.BlockSpec((tm, D_MODEL), lambda i, j: (i, 0)),
                  pl.BlockSpec((D_MODEL, tn), lambda i, j: (0, j)),
                  pl.BlockSpec((D_MODEL, tn), lambda i, j: (0, j))],
        out_specs=pl.BlockSpec((tm, tn), lambda i, j: (i, j)),
        out_shape=jax.ShapeDtypeStruct((m, D_FF), BF16),
        scratch_shapes=[pltpu.VMEM((tm, D_MODEL), BF16)],
        compiler_params=_cparams("arbitrary", "arbitrary"),
        name="ffn_up",
    )(x, wg, wu)


FFN_DOWN_KSTEPS = 4


def _ffn_down_kernel(h_ref, w_ref, x_ref, g_ref, beta_ref, o_ref, acc_ref):
    k = pl.program_id(1)

    @pl.when(k == 0)
    def _():
        acc_ref[...] = jnp.dot(h_ref[...], w_ref[...], preferred_element_type=F32)

    @pl.when(jnp.logical_and(k > 0, k < FFN_DOWN_KSTEPS - 1))
    def _():
        acc_ref[...] += jnp.dot(h_ref[...], w_ref[...], preferred_element_type=F32)

    @pl.when(k == FFN_DOWN_KSTEPS - 1)
    def _():
        ffn = acc_ref[...] + jnp.dot(h_ref[...], w_ref[...], preferred_element_type=F32)
        o_ref[...] = _layer_norm(ALPHA * x_ref[...] + ffn, g_ref[...], beta_ref[...])


def _ffn_down_ln(h, w_down, x, g, beta):
    m = x.shape[0]
    tm = min(m, 512)
    tk = D_FF // FFN_DOWN_KSTEPS
    assert tk * FFN_DOWN_KSTEPS == D_FF and tk % LANES == 0
    return pl.pallas_call(
        _ffn_down_kernel,
        grid=(m // tm, FFN_DOWN_KSTEPS),
        in_specs=[pl.BlockSpec((tm, tk), lambda i, k: (i, k)),
                  pl.BlockSpec((tk, D_MODEL), lambda i, k: (k, 0)),
                  pl.BlockSpec((tm, D_MODEL), lambda i, k: (i, 0)),
                  pl.BlockSpec((1, D_MODEL), lambda i, k: (0, 0)),
                  pl.BlockSpec((1, D_MODEL), lambda i, k: (0, 0))],
        out_specs=pl.BlockSpec((tm, D_MODEL), lambda i, k: (i, 0)),
        out_shape=jax.ShapeDtypeStruct((m, D_MODEL), F32),
        scratch_shapes=[pltpu.VMEM((tm, D_MODEL), F32)],
        compiler_params=_cparams("arbitrary", "arbitrary"),
        name="ffn_down_ln",
    )(h, w_down, x, g, beta)


def _conv_taps(u, first_rows, cw, cb, t_in_seq):
    out = cb + cw[CONV_WIDTH - 1:CONV_WIDTH, :] * u
    for j in range(1, CONV_WIDTH):
        shifted = jnp.where(t_in_seq < j, first_rows(j), pltpu.roll(u, j, 0))
        out = out + cw[CONV_WIDTH - 1 - j:CONV_WIDTH - j, :] * shifted
    return out


def _lru_kernel(gate_ref, lx_ref, buf_ref, h0_ref, cw_ref, cb_ref, wa_ref, ba_ref, wx_ref,
                bx_ref, lam_ref, out_ref, h1_ref, a_scr, b_scr, hin_scr, *, nseq, seqlen):
    rows = nseq * seqlen
    u = lx_ref[...]
    t_in_seq = _iota2((rows, LANES), 0) % seqlen
    if nseq == 1:
        buf = jnp.concatenate([buf_ref[0], jnp.zeros((rows - SUBLANES, LANES), F32)], axis=0) \
            if rows > SUBLANES else buf_ref[0]
    else:
        buf = buf_ref[0]
    nbuf = CONV_WIDTH - 1
    xc = _conv_taps(u, lambda j: pltpu.roll(buf, (j - nbuf) % rows, 0), cw_ref[...], cb_ref[...],
                    t_in_seq)

    r = _sigmoid(_dot1(xc, wa_ref[0]) + ba_ref[...])
    i = _sigmoid(_dot1(xc, wx_ref[0]) + bx_ref[...])
    log_a = (-LRU_C) * r * _softplus(-lam_ref[...])
    a = jnp.exp(log_a)
    b = jnp.sqrt(jnp.tanh(-log_a) * (a * a + 1.0)) * (i * xc)

    t8 = _iota2((rows, LANES), 0) % SUBLANES
    for s in (1, 2, 4):
        m = t8 >= s
        a_sh = pltpu.roll(a, s, 0)
        b_sh = pltpu.roll(b, s, 0)
        b = jnp.where(m, a * b_sh + b, b)
        a = jnp.where(m, a * a_sh, a)

    if seqlen == SUBLANES:
        h0 = h0_ref[0]
        hin = jnp.broadcast_to(h0[:, None, :], (nseq, SUBLANES, LANES)).reshape(rows, LANES)
        h = a * hin + b
        out_ref[...] = (h * _gelu_tanh(gate_ref[...])).astype(out_ref.dtype)
        a_scr[...] = h
        h1_ref[0] = a_scr[pl.ds(SUBLANES - 1, nseq, stride=SUBLANES), :]
    else:
        assert nseq == 1
        a_scr[...] = a
        b_scr[...] = b

        def carry_step(g, carry):
            base = pl.multiple_of(g * SUBLANES, SUBLANES)
            hin_scr[pl.ds(base, SUBLANES), :] = jnp.broadcast_to(carry, (SUBLANES, LANES))
            a7 = a_scr[pl.ds(base + SUBLANES - 1, 1), :]
            b7 = b_scr[pl.ds(base + SUBLANES - 1, 1), :]
            return a7 * carry + b7

        last = lax.fori_loop(0, rows // SUBLANES, carry_step, h0_ref[0])
        h = a_scr[...] * hin_scr[...] + b_scr[...]
        out_ref[...] = (h * _gelu_tanh(gate_ref[...])).astype(out_ref.dtype)
        h1_ref[0] = last


def _lru(proj, buf, h0, cw, cb, wa, ba, wx, bx, lam, *, nblk, nseq, seqlen):
    rows = nseq * seqlen
    gate_blk = C_GATE // LRU_BLOCK
    lx_blk = C_LX // LRU_BLOCK
    row = lambda s, h: (0, h)
    return pl.pallas_call(
        functools.partial(_lru_kernel, nseq=nseq, seqlen=seqlen),
        grid=(nblk, LRU_HEADS),
        in_specs=[pl.BlockSpec((rows, LRU_BLOCK), lambda s, h: (s, gate_blk + h)),
                  pl.BlockSpec((rows, LRU_BLOCK), lambda s, h: (s, lx_blk + h)),
                  pl.BlockSpec((1, nseq * SUBLANES, LRU_BLOCK), lambda s, h: (s, 0, h)),
                  pl.BlockSpec((1, nseq, LRU_BLOCK), lambda s, h: (s, 0, h)),
                  pl.BlockSpec((CONV_WIDTH, LRU_BLOCK), row),
                  pl.BlockSpec((1, LRU_BLOCK), row),
                  pl.BlockSpec((1, LRU_BLOCK, LRU_BLOCK), lambda s, h: (h, 0, 0)),
                  pl.BlockSpec((1, LRU_BLOCK), row),
                  pl.BlockSpec((1, LRU_BLOCK, LRU_BLOCK), lambda s, h: (h, 0, 0)),
                  pl.BlockSpec((1, LRU_BLOCK), row),
                  pl.BlockSpec((1, LRU_BLOCK), row)],
        out_specs=[pl.BlockSpec((rows, LRU_BLOCK), lambda s, h: (s, h)),
                   pl.BlockSpec((1, nseq, LRU_BLOCK), lambda s, h: (s, 0, h))],
        out_shape=[jax.ShapeDtypeStruct((nblk * rows, LRU_WIDTH), BF16),
                   jax.ShapeDtypeStruct((nblk, nseq, LRU_WIDTH), F32)],
        scratch_shapes=[pltpu.VMEM((rows, LANES), F32)] * 3,
        compiler_params=_cparams("arbitrary", "arbitrary"),
        name="lru",
    )(proj, proj, buf, h0, cw, cb, wa, ba, wx, bx, lam)


def _transpose_rows_to_lanes(x, t):
    if t < LANES:
        x = jnp.concatenate([x, jnp.zeros((LANES - t, LANES), F32)], axis=0)
    return x.T[:, :t]


def _ssd_kernel(z_ref, xbc_ref, dt_ref, buf_ref, s0_ref, cw_ref, cb_ref, dtb_ref, alog_ref,
                dch_ref, ng_ref, ehp_ref, out_ref, s1_ref, tail_ref, pad_ref, y_scr, *, T):
    c = pl.program_id(1)

    @pl.when(c == 0)
    def _():
        tail_ref[...] = buf_ref[0]
        s1_ref[...] = s0_ref[...]

    u = xbc_ref[...]
    pad_ref[0:SUBLANES, :] = tail_ref[...]
    pad_ref[SUBLANES:SUBLANES + T, :] = u
    cw = cw_ref[...]
    xc = cb_ref[...] + cw[CONV_WIDTH - 1:CONV_WIDTH, :] * u
    for j in range(1, CONV_WIDTH):
        xc = xc + cw[CONV_WIDTH - 1 - j:CONV_WIDTH - j, :] * pad_ref[SUBLANES - j:SUBLANES - j + T, :]
    tail_ref[...] = pad_ref[T:T + SUBLANES, :]

    xa = _silu(xc)
    xs = xa[:, :SSD_WIDTH]
    bm = xa[:, SSD_WIDTH:SSD_WIDTH + SSD_GROUPS * SSD_STATE]
    cm = xa[:, SSD_WIDTH + SSD_GROUPS * SSD_STATE:]
    dt = _softplus(dt_ref[...] + dtb_ref[...])
    da = dt * (-jnp.exp(alog_ref[...]))
    ii = _iota2((T, T), 0)
    jj = _iota2((T, T), 1)
    causal = ii >= jj
    cum = _dotx_left(causal.astype(BF16), da)
    cum_t = _transpose_rows_to_lanes(cum, T)
    ehp = ehp_ref[...]
    cumx = _dotx(cum, ehp)
    dtx = _dotx(dt, ehp)
    xdt = xs * dtx
    ecum = jnp.exp(cumx)
    xdtd = xdt * jnp.exp(cumx[T - 1:T, :] - cumx)

    gw = SSD_GROUP_WIDTH
    for g in range(SSD_GROUPS):
        h_lo = g * SSD_HEADS_PER_GROUP
        cg = cm[:, g * SSD_STATE:(g + 1) * SSD_STATE]
        bg = bm[:, g * SSD_STATE:(g + 1) * SSD_STATE]
        cb = _dot1(cg, bg, _NT)
        sg = s1_ref[0, h_lo:h_lo + SSD_HEADS_PER_GROUP].reshape(gw, SSD_STATE)
        y_off = _dot1(cg, sg, _NT) * ecum[:, g * gw:(g + 1) * gw]
        y_diag = []
        decay = []
        for e in range(SSD_HEADS_PER_GROUP):
            h = h_lo + e
            seg = cum[:, h:h + 1] - cum_t[h:h + 1, :]
            lm = jnp.where(causal, jnp.exp(seg), 0.0)
            y_diag.append(_dot1(cb * lm, xdt[:, h * HEAD_DIM:(h + 1) * HEAD_DIM]))
            decay.append(jnp.broadcast_to(jnp.exp(cum_t[h:h + 1, T - 1:T]), (HEAD_DIM, SSD_STATE)))
        y_scr[:, g * gw:(g + 1) * gw] = jnp.concatenate(y_diag, axis=1) + y_off
        st = _dot1(xdtd[:, g * gw:(g + 1) * gw], bg, _TN)
        s_new = sg * jnp.concatenate(decay, axis=0) + st
        s1_ref[0, h_lo:h_lo + SSD_HEADS_PER_GROUP] = s_new.reshape(
            SSD_HEADS_PER_GROUP, HEAD_DIM, SSD_STATE)

    y = (y_scr[...] + dch_ref[...] * xs) * _silu(z_ref[...])
    outs = []
    for g in range(SSD_GROUPS):
        yg = y[:, g * gw:(g + 1) * gw]
        ms = jnp.mean(yg * yg, axis=-1, keepdims=True)
        outs.append(yg * lax.rsqrt(ms + 1e-5))
    out_ref[...] = (jnp.concatenate(outs, axis=1) * ng_ref[...]).astype(out_ref.dtype)


def _ssd(proj, buf, s0, cw, cb, dtb, alog, dch, ng, ehp, *, nseq, seqlen):
    T = min(SSD_CHUNK, seqlen)
    nc = seqlen // T
    const = lambda b, c: (0, 0)
    state_spec = pl.BlockSpec((1, SSD_HEADS, HEAD_DIM, SSD_STATE), lambda b, c: (b, 0, 0, 0))
    return pl.pallas_call(
        functools.partial(_ssd_kernel, T=T),
        grid=(nseq, nc),
        in_specs=[pl.BlockSpec((T, SSD_WIDTH), lambda b, c: (b * nc + c, C_Z // SSD_WIDTH)),
                  pl.BlockSpec((T, SSD_CONV_DIM), lambda b, c: (b * nc + c, C_XBC // SSD_CONV_DIM)),
                  pl.BlockSpec((T, DT_PAD), lambda b, c: (b * nc + c, C_DT // DT_PAD)),
                  pl.BlockSpec((1, SUBLANES, SSD_CONV_DIM), lambda b, c: (b, 0, 0)),
                  state_spec,
                  pl.BlockSpec((CONV_WIDTH, SSD_CONV_DIM), const),
                  pl.BlockSpec((1, SSD_CONV_DIM), const),
                  pl.BlockSpec((1, DT_PAD), const),
                  pl.BlockSpec((1, DT_PAD), const),
                  pl.BlockSpec((1, SSD_WIDTH), const),
                  pl.BlockSpec((1, SSD_WIDTH), const),
                  pl.BlockSpec((DT_PAD, SSD_WIDTH), const)],
        out_specs=[pl.BlockSpec((T, SSD_WIDTH), lambda b, c: (b * nc + c, 0)), state_spec],
        out_shape=[jax.ShapeDtypeStruct((nseq * seqlen, SSD_WIDTH), BF16),
                   jax.ShapeDtypeStruct((nseq, SSD_HEADS, HEAD_DIM, SSD_STATE), F32)],
        scratch_shapes=[pltpu.VMEM((SUBLANES, SSD_CONV_DIM), F32),
                        pltpu.VMEM((T + SUBLANES, SSD_CONV_DIM), F32),
                        pltpu.VMEM((T, SSD_WIDTH), F32)],
        compiler_params=_cparams("arbitrary", "arbitrary"),
        name="ssd",
    )(proj, proj, proj, buf, s0, cw, cb, dtb, alog, dch, ng, ehp)


def _rwkv_kernel(r_ref, k_ref, v_ref, g_ref, wa_ref, sh_ref, s0_ref, mu_ref, w0_ref, wup_ref,
                 a0_ref, aup_ref, gup_ref, kkw_ref, kaw_ref, rkw_ref, lng_ref, lnb_ref, ones_ref,
                 out_ref, s1_ref, prev_scr, h_scr, al_scr, be_scr, kt_scr, rt_scr, bs_scr,
                 ks_scr, v_scr, gam_scr, o_scr, *, T, nchunks):
    c = pl.program_id(1)
    W = RWKV_WIDTH
    D = HEAD_DIM

    @pl.when(c == 0)
    def _():
        prev_scr[...] = sh_ref[0]
        for h in range(RWKV_HEADS):
            h_scr[h] = s0_ref[0, h].T

    def token_shift(p, lo, hi):
        first = _iota2(p.shape, 0) == 0
        prev = jnp.where(first, prev_scr[:, lo:hi], pltpu.roll(p, 1, 0))
        prev_scr[:, lo:hi] = p[T - 1:T, :]
        return p + (prev - p) * mu_ref[:, lo:hi]

    xr = token_shift(r_ref[...], 0, W)
    xk = token_shift(k_ref[...], W, 2 * W)
    xv = token_shift(v_ref[...], 2 * W, 3 * W)
    xg = token_shift(g_ref[...], 3 * W, 3 * W + G_PAD)
    xwa = token_shift(wa_ref[...], 3 * W + G_PAD, RW_PACK)

    w_lin = w0_ref[...] + _dot1(jnp.tanh(xwa), wup_ref[...])
    a = _sigmoid(a0_ref[...] + _dot1(xwa, aup_ref[...]))
    gate = _dot1(_sigmoid(xg), gup_ref[...])
    lw = -jnp.exp(-_softplus(-w_lin) - 0.5)

    ones = ones_ref[...]

    def head_sum(x):
        tw = ones.shape[0]
        return jnp.concatenate(
            [_dotx(x[:, i * tw:(i + 1) * tw], ones) for i in range(W // tw)], axis=1)

    kk = xk * kkw_ref[...]
    kk = kk / jnp.maximum(jnp.sqrt(head_sum(kk * kk)), 1e-12)
    kp = xk * (1.0 + (a - 1.0) * kaw_ref[...])

    ii = _iota2((T, T), 0)
    jj = _iota2((T, T), 1)
    incl = ii >= jj
    strict = ii > jj
    eye_t = jnp.where(ii == jj, 1.0, 0.0)
    eye_d = _iota2((D, D), 0) == _iota2((D, D), 1)

    cum = _dotx_left(incl.astype(BF16), lw)
    e_neg = jnp.exp(-cum)
    gam = jnp.exp(cum[T - 1:T, :])
    be = kk * a * e_neg
    kt = kp * e_neg
    al_scr[...] = -kk * jnp.exp(cum - lw)
    be_scr[...] = be
    kt_scr[...] = kt
    rt_scr[...] = xr * jnp.exp(cum)
    bs_scr[...] = be * gam
    ks_scr[...] = kt * gam
    v_scr[...] = xv
    gam_scr[...] = gam

    def head_pair(p, carry):
        off = pl.multiple_of(p * LANES, LANES)
        cols = pl.ds(off, LANES)
        al2, be2, kt2, rt2 = al_scr[:, cols], be_scr[:, cols], kt_scr[:, cols], rt_scr[:, cols]
        bs2, ks2, v2, gam2 = bs_scr[:, cols], ks_scr[:, cols], v_scr[:, cols], gam_scr[:, cols]
        outs = []
        for s in range(LANES // D):
            sl = slice(s * D, (s + 1) * D)
            al, be_h, kt_h, rt, vv = al2[:, sl], be2[:, sl], kt2[:, sl], rt2[:, sl], v2[:, sl]
            hidx = p * (LANES // D) + s
            hmat = h_scr[hidx]
            gram = _dot3(jnp.concatenate([al, rt], axis=0),
                         jnp.concatenate([be_h, kt_h], axis=0), _NT)
            a_ab = jnp.where(strict, gram[:T, :T], 0.0)
            a_ak = jnp.where(strict, gram[:T, T:], 0.0)
            r_b = jnp.where(incl, gram[T:, :T], 0.0)
            r_k = jnp.where(incl, gram[T:, T:], 0.0)
            inv = eye_t + a_ab
            power = a_ab
            for _ in range(int(math.log2(T)) - 1):
                power = _dot3(power, power)
                inv = inv + _dot3(inv, power)
            pw = _dot3(inv, jnp.concatenate([al, _dot3(a_ak, vv)], axis=1))
            qo = _dot3(r_b, pw)
            q = rt + qo[:, :D]
            o_intra = qo[:, D:] + _dot3(r_k, vv)
            gz = _dot3(bs2[:, sl], pw, _TN)
            gmat = jnp.where(eye_d, jnp.broadcast_to(gam2[:, sl], (D, D)), 0.0) + gz[:, :D]
            z = gz[:, D:] + _dot3(ks2[:, sl], vv, _TN)
            outs.append(_dot3(q, hmat) + o_intra)
            h_scr[hidx] = _dot3(gmat, hmat) + z
        o_scr[:, cols] = jnp.concatenate(outs, axis=1)
        return carry

    lax.fori_loop(0, RWKV_HEADS * D // LANES, head_pair, 0)

    o = o_scr[...]
    mean = head_sum(o) * (1.0 / D)
    d = o - mean
    var = head_sum(d * d) * (1.0 / D)
    on = d * lax.rsqrt(var + RWKV_GN_EPS) * lng_ref[...] + lnb_ref[...]
    bonus = head_sum(xr * kp * rkw_ref[...]) * xv
    out_ref[...] = ((on + bonus) * gate).astype(out_ref.dtype)

    @pl.when(c == nchunks - 1)
    def _():
        for h in range(RWKV_HEADS):
            s1_ref[0, h] = h_scr[h].T


def _rwkv(proj, sh, s0, mu, w0, wup, a0, aup, gup, kkw, kaw, rkw, lng, lnb, ones, *, nseq, seqlen):
    T = min(RWKV_CHUNK, seqlen)
    nc = seqlen // T
    W = RWKV_WIDTH
    const = lambda b, c: (0, 0)
    rowblk = lambda col, width: pl.BlockSpec((T, width), lambda b, c: (b * nc + c, col // width))
    state_spec = pl.BlockSpec((1, RWKV_HEADS, HEAD_DIM, HEAD_DIM), lambda b, c: (b, 0, 0, 0))
    vec = pl.BlockSpec((1, W), const)
    return pl.pallas_call(
        functools.partial(_rwkv_kernel, T=T, nchunks=nc),
        grid=(nseq, nc),
        in_specs=[rowblk(C_R, W), rowblk(C_K, W), rowblk(C_V, W), rowblk(C_G, G_PAD),
                  rowblk(C_WA, WA_PAD),
                  pl.BlockSpec((1, 1, RW_PACK), lambda b, c: (b, 0, 0)),
                  state_spec,
                  pl.BlockSpec((1, RW_PACK), const),
                  vec, pl.BlockSpec((WA_PAD, W), const),
                  vec, pl.BlockSpec((WA_PAD, W), const),
                  pl.BlockSpec((G_PAD, W), const),
                  vec, vec, vec, vec, vec,
                  pl.BlockSpec((2 * LANES, 2 * LANES), const)],
        out_specs=[pl.BlockSpec((T, W), lambda b, c: (b * nc + c, 0)), state_spec],
        out_shape=[jax.ShapeDtypeStruct((nseq * seqlen, W), BF16),
                   jax.ShapeDtypeStruct((nseq, RWKV_HEADS, HEAD_DIM, HEAD_DIM), F32)],
        scratch_shapes=[pltpu.VMEM((1, RW_PACK), F32),
                        pltpu.VMEM((RWKV_HEADS, HEAD_DIM, HEAD_DIM), F32)]
        + [pltpu.VMEM((T, W), F32)] * 7
        + [pltpu.VMEM((1, W), F32), pltpu.VMEM((T, W), F32)],
        compiler_params=_cparams("arbitrary", "arbitrary"),
        name="rwkv",
    )(proj, proj, proj, proj, proj, sh, s0, mu, w0, wup, a0, aup, gup, kkw, kaw, rkw, lng, lnb, ones)


_O_DT = 2 * LRU_WIDTH + SSD_WIDTH + SSD_CONV_DIM
_O_RW = _O_DT + SSD_HEADS
_O_XW = 3 * RWKV_WIDTH
_O_XG = _O_XW + RWKV_R_W + RWKV_R_A


def _zeros_like_cols(x, n):
    return jnp.zeros(x.shape[:-1] + (n,), x.dtype)


def _pack_rwkv_cols(x):
    return jnp.concatenate([x[..., :_O_XW], x[..., _O_XG:], _zeros_like_cols(x, G_PAD - RWKV_R_G),
                            x[..., _O_XW:_O_XG]], axis=-1)


def _pack_proj_cols(w):
    return jnp.concatenate([w[..., :_O_DT], _pack_rwkv_cols(w[..., _O_RW:]), w[..., _O_DT:_O_RW],
                            _zeros_like_cols(w, DT_PAD - SSD_HEADS)], axis=-1)


def _pad_rows(x, n_before, n_total):
    b, r, c = x.shape
    return jnp.concatenate([jnp.zeros((b, n_before, c), x.dtype), x,
                            jnp.zeros((b, n_total - n_before - r, c), x.dtype)], axis=1)


def _layer_params(l, p):
    row = lambda v: v.reshape(1, -1)
    pad_lanes = lambda v, n: jnp.concatenate([v, jnp.zeros((n - v.shape[0],), v.dtype)]).reshape(1, n)
    zeros_w = jnp.zeros((RWKV_R_W, RWKV_WIDTH), F32)
    lp = dict(
        w_in=_pack_proj_cols(p['w_in'][l]).astype(BF16),
        lru=(p['lru_conv_w'][l], row(p['lru_conv_b'][l]), p['lru_wa'][l].astype(BF16),
             row(p['lru_ba'][l]), p['lru_wx'][l].astype(BF16), row(p['lru_bx'][l]),
             row(p['lru_lambda'][l])),
        ssd=(p['ssd_conv_w'][l], row(p['ssd_conv_b'][l]), pad_lanes(p['ssd_dt_bias'][l], DT_PAD),
             pad_lanes(p['ssd_a_log'][l], DT_PAD), row(jnp.repeat(p['ssd_d'][l], HEAD_DIM)),
             row(p['ssd_norm_g'][l])),
        rwkv=(row(_pack_rwkv_cols(p['rwkv_mu'][l])), row(p['rwkv_w0'][l]),
              jnp.concatenate([p['rwkv_w_up'][l], zeros_w], axis=0).astype(BF16),
              row(p['rwkv_a0'][l]),
              jnp.concatenate([zeros_w, p['rwkv_a_up'][l]], axis=0).astype(BF16),
              jnp.concatenate([p['rwkv_g_up'][l],
                               jnp.zeros((G_PAD - RWKV_R_G, RWKV_WIDTH), F32)], axis=0).astype(BF16),
              row(p['rwkv_k_k'][l]), row(p['rwkv_k_a'][l]), row(p['rwkv_r_k'][l]),
              row(p['rwkv_ln_g'][l]), row(p['rwkv_ln_b'][l])),
        w_out=p['w_out'][l].astype(BF16),
        ln1=(row(p['ln1_g'][l]), row(p['ln1_b'][l])),
        w_gate=p['w_gate'][l].astype(BF16), w_up=p['w_up'][l].astype(BF16),
        w_down=p['w_down'][l].astype(BF16),
        ln2=(row(p['ln2_g'][l]), row(p['ln2_b'][l])),
    )
    return lp


def _constants():
    lane = jnp.arange(SSD_WIDTH) // HEAD_DIM
    ehp = (jnp.arange(DT_PAD)[:, None] == lane[None, :]).astype(BF16)
    blk = jnp.arange(2 * LANES) // HEAD_DIM
    ones = (blk[:, None] == blk[None, :]).astype(BF16)
    return ehp, ones


def _layer(x, state, lp, consts, *, nseq, seqlen, lru_nblk):
    lru_conv0, lru_h0, ssd_conv0, ssd_s0, rw_shift0, rw_s0 = state
    ehp, ones = consts
    nbuf = CONV_WIDTH - 1
    proj = _proj(x, lp['w_in'])

    lru_nseq = nseq // lru_nblk
    out_a, lru_h1 = _lru(
        proj, _pad_rows(lru_conv0, 0, SUBLANES).reshape(lru_nblk, lru_nseq * SUBLANES, LRU_WIDTH),
        lru_h0.reshape(lru_nblk, lru_nseq, LRU_WIDTH), *lp['lru'],
        nblk=lru_nblk, nseq=lru_nseq, seqlen=seqlen)
    out_b, ssd_s1 = _ssd(proj, _pad_rows(ssd_conv0, SUBLANES - nbuf, SUBLANES), ssd_s0,
                         *lp['ssd'], ehp, nseq=nseq, seqlen=seqlen)
    out_c, rw_s1 = _rwkv(proj, _pack_rwkv_cols(rw_shift0)[:, None, :], rw_s0, *lp['rwkv'], ones,
                         nseq=nseq, seqlen=seqlen)

    y = _outproj_ln(out_a, out_b, out_c, lp['w_out'], x, *lp['ln1'])
    y = _ffn_down_ln(_ffn_up(y, lp['w_gate'], lp['w_up']), lp['w_down'], y, *lp['ln2'])

    p3 = proj.reshape(nseq, seqlen, N_PROJ)
    tail = p3[:, seqlen - nbuf:, :]
    last = p3[:, seqlen - 1, :]
    rw_shift1 = jnp.concatenate([last[:, C_R:C_R + 3 * RWKV_WIDTH], last[:, C_WA:C_WA + WA_PAD],
                                 last[:, C_G:C_G + RWKV_R_G]], axis=-1)
    new_state = (tail[:, :, C_LX:C_LX + LRU_WIDTH], lru_h1.reshape(nseq, LRU_WIDTH),
                 tail[:, :, C_XBC:C_XBC + SSD_CONV_DIM], ssd_s1, rw_shift1, rw_s1)
    return y, new_state


def kernel(x_prompt, x_sample, state_lru_conv, state_lru_h, state_ssd_conv, state_ssd,
           state_rwkv_shift, state_rwkv, w_in, lru_conv_w, lru_conv_b, lru_wa, lru_ba, lru_wx,
           lru_bx, lru_lambda, ssd_conv_w, ssd_conv_b, ssd_dt_bias, ssd_a_log, ssd_d, ssd_norm_g,
           rwkv_mu, rwkv_w0, rwkv_w_up, rwkv_a0, rwkv_a_up, rwkv_g_up, rwkv_k_k, rwkv_k_a,
           rwkv_r_k, rwkv_ln_g, rwkv_ln_b, w_out, ln1_g, ln1_b, w_gate, w_up, w_down, ln2_g, ln2_b):
    params = dict(
        w_in=w_in, lru_conv_w=lru_conv_w, lru_conv_b=lru_conv_b, lru_wa=lru_wa, lru_ba=lru_ba,
        lru_wx=lru_wx, lru_bx=lru_bx, lru_lambda=lru_lambda, ssd_conv_w=ssd_conv_w,
        ssd_conv_b=ssd_conv_b, ssd_dt_bias=ssd_dt_bias, ssd_a_log=ssd_a_log, ssd_d=ssd_d,
        ssd_norm_g=ssd_norm_g, rwkv_mu=rwkv_mu, rwkv_w0=rwkv_w0, rwkv_w_up=rwkv_w_up,
        rwkv_a0=rwkv_a0, rwkv_a_up=rwkv_a_up, rwkv_g_up=rwkv_g_up, rwkv_k_k=rwkv_k_k,
        rwkv_k_a=rwkv_k_a, rwkv_r_k=rwkv_r_k.reshape(DEPTH, RWKV_WIDTH), rwkv_ln_g=rwkv_ln_g,
        rwkv_ln_b=rwkv_ln_b, w_out=w_out, ln1_g=ln1_g, ln1_b=ln1_b, w_gate=w_gate, w_up=w_up,
        w_down=w_down, ln2_g=ln2_g, ln2_b=ln2_b)
    bp, lp_len, _ = x_prompt.shape
    bs, ls_len, _ = x_sample.shape
    consts = _constants()
    nbuf = CONV_WIDTH - 1
    zero_state = (jnp.zeros((bp, nbuf, LRU_WIDTH), F32), jnp.zeros((bp, LRU_WIDTH), F32),
                  jnp.zeros((bp, nbuf, SSD_CONV_DIM), F32),
                  jnp.zeros((bp, SSD_HEADS, HEAD_DIM, SSD_STATE), F32),
                  jnp.zeros((bp, RWKV_SHIFT), F32),
                  jnp.zeros((bp, RWKV_HEADS, HEAD_DIM, HEAD_DIM), F32))
    yp = x_prompt.reshape(bp * lp_len, D_MODEL)
    ys = x_sample.reshape(bs * ls_len, D_MODEL)
    new_p = [[] for _ in range(6)]
    new_s = [[] for _ in range(6)]
    for l in range(DEPTH):
        lp = _layer_params(l, params)
        yp, sp = _layer(yp, zero_state, lp, consts, nseq=bp, seqlen=lp_len, lru_nblk=bp)
        ys, ss = _layer(ys, (state_lru_conv[l], state_lru_h[l], state_ssd_conv[l], state_ssd[l],
                             state_rwkv_shift[l], state_rwkv[l]), lp, consts,
                        nseq=bs, seqlen=ls_len, lru_nblk=1)
        for i in range(6):
            new_p[i].append(sp[i])
            new_s[i].append(ss[i])
    return (yp.reshape(bp, lp_len, D_MODEL), ys.reshape(bs, ls_len, D_MODEL),
            *[jnp.stack(v) for v in new_p], *[jnp.stack(v) for v in new_s])
```

```python
import functools
import math

import jax
import jax.numpy as jnp
from jax import lax
from jax.experimental import pallas as pl
from jax.experimental.pallas import tpu as pltpu

F32 = jnp.float32
BF16 = jnp.bfloat16

D_MODEL = 2048
DEPTH = 2
D_MIX = 2 * D_MODEL
HEAD_DIM = 64
CONV_WIDTH = 4
LRU_WIDTH = D_MIX // 4
LRU_HEADS = 8
LRU_BLOCK = LRU_WIDTH // LRU_HEADS
LRU_C = 8.0
SSD_WIDTH = D_MIX // 2
SSD_HEADS = SSD_WIDTH // HEAD_DIM
SSD_GROUPS = 8
SSD_STATE = 128
SSD_CHUNK = 128
SSD_CONV_DIM = SSD_WIDTH + 2 * SSD_GROUPS * SSD_STATE
SSD_GROUP_WIDTH = SSD_WIDTH // SSD_GROUPS
SSD_HEADS_PER_GROUP = SSD_HEADS // SSD_GROUPS
RWKV_WIDTH = D_MIX - LRU_WIDTH - SSD_WIDTH
RWKV_HEADS = RWKV_WIDTH // HEAD_DIM
RWKV_R_W = max(32, int(round(1.8 * RWKV_WIDTH ** 0.5 / 32)) * 32)
RWKV_R_A = max(32, int(round(1.8 * RWKV_WIDTH ** 0.5 / 32)) * 32)
RWKV_R_G = max(32, int(round(0.6 * RWKV_WIDTH ** 0.8 / 32)) * 32)
RWKV_SHIFT = 3 * RWKV_WIDTH + RWKV_R_W + RWKV_R_A + RWKV_R_G
RWKV_GN_EPS = 64e-5
RWKV_CHUNK = 64
D_FF = -(-(8 * D_MODEL) // (3 * 256)) * 256
ALPHA = (2 * DEPTH) ** 0.25
LN_EPS = 1e-5

LANES = 128
SUBLANES = 8
VMEM_LIMIT = 56 * 1024 * 1024

C_GATE = 0
C_LX = C_GATE + LRU_WIDTH
C_Z = C_LX + LRU_WIDTH
C_XBC = C_Z + SSD_WIDTH
C_R = C_XBC + SSD_CONV_DIM
C_K = C_R + RWKV_WIDTH
C_V = C_K + RWKV_WIDTH
C_G = C_V + RWKV_WIDTH
G_PAD = 2 * LANES
C_WA = C_G + G_PAD
WA_PAD = LANES
C_DT = C_WA + WA_PAD
DT_PAD = LANES
N_PROJ = C_DT + DT_PAD
assert RWKV_R_G <= G_PAD and RWKV_R_W + RWKV_R_A == WA_PAD and SSD_HEADS <= DT_PAD
RW_PACK = 3 * RWKV_WIDTH + G_PAD + WA_PAD


def _cparams(*sem):
    return pltpu.CompilerParams(dimension_semantics=sem, vmem_limit_bytes=VMEM_LIMIT)


_NN = (((1,), (0,)), ((), ()))
_NT = (((1,), (1,)), ((), ()))
_TN = (((0,), (0,)), ((), ()))


def _dg(a, b, dims):
    return lax.dot_general(a, b, dims, preferred_element_type=F32)


def _dot1(a, b, dims=_NN):
    return _dg(a.astype(BF16), b.astype(BF16), dims)


def _split2(x):
    hi = x.astype(BF16)
    lo = (x - hi.astype(F32)).astype(BF16)
    return hi, lo


def _split3(x):
    hi = x.astype(BF16)
    r1 = x - hi.astype(F32)
    mid = r1.astype(BF16)
    lo = (r1 - mid.astype(F32)).astype(BF16)
    return hi, mid, lo


def _dot3(a, b, dims=_NN):
    ah, al = _split2(a)
    bh, bl = _split2(b)
    return _dg(ah, bh, dims) + (_dg(ah, bl, dims) + _dg(al, bh, dims))


def _dotx(a, w_exact):
    hi, mid, lo = _split3(a)
    return _dg(hi, w_exact, _NN) + (_dg(mid, w_exact, _NN) + _dg(lo, w_exact, _NN))


def _dotx_left(w_exact, a):
    hi, mid, lo = _split3(a)
    return _dg(w_exact, hi, _NN) + (_dg(w_exact, mid, _NN) + _dg(w_exact, lo, _NN))


def _iota2(shape, dim):
    return lax.broadcasted_iota(jnp.int32, shape, dim)


def _softplus(x):
    return jnp.maximum(x, 0.0) + jnp.log1p(jnp.exp(-jnp.abs(x)))


def _sigmoid(x):
    return 1.0 / (1.0 + jnp.exp(-x))


def _silu(x):
    return x * _sigmoid(x)


def _gelu_tanh(x):
    c = math.sqrt(2.0 / math.pi)
    return 0.5 * x * (1.0 + jnp.tanh(c * (x + 0.044715 * (x * x * x))))


def _layer_norm(y, g, b):
    mu = jnp.mean(y, axis=-1, keepdims=True)
    d = y - mu
    var = jnp.mean(d * d, axis=-1, keepdims=True)
    return d * lax.rsqrt(var + LN_EPS) * g + b


def _proj_kernel(x_ref, w_ref, o_ref, xb_ref):
    @pl.when(pl.program_id(1) == 0)
    def _():
        xb_ref[...] = x_ref[...].astype(BF16)

    o_ref[...] = jnp.dot(xb_ref[...], w_ref[...], preferred_element_type=F32)


def _proj(x, w):
    m = x.shape[0]
    tm = min(m, 1024)
    tn = 512
    return pl.pallas_call(
        _proj_kernel,
        grid=(m // tm, N_PROJ // tn),
        in_specs=[pl.BlockSpec((tm, D_MODEL), lambda i, j: (i, 0)),
                  pl.BlockSpec((D_MODEL, tn), lambda i, j: (0, j))],
        out_specs=pl.BlockSpec((tm, tn), lambda i, j: (i, j)),
        out_shape=jax.ShapeDtypeStruct((m, N_PROJ), F32),
        scratch_shapes=[pltpu.VMEM((tm, D_MODEL), BF16)],
        compiler_params=_cparams("arbitrary", "arbitrary"),
        name="proj",
    )(x, w)


OUT_KSTEPS = D_MIX // LRU_WIDTH


def _outproj_kernel(a_ref, b_ref, c_ref, w_ref, x_ref, g_ref, beta_ref, o_ref, acc_ref):
    k = pl.program_id(1)

    @pl.when(k == 0)
    def _():
        acc_ref[...] = jnp.dot(a_ref[...], w_ref[...], preferred_element_type=F32)

    @pl.when(jnp.logical_and(k > 0, k < OUT_KSTEPS - 1))
    def _():
        acc_ref[...] += jnp.dot(b_ref[...], w_ref[...], preferred_element_type=F32)

    @pl.when(k == OUT_KSTEPS - 1)
    def _():
        mix = acc_ref[...] + jnp.dot(c_ref[...], w_ref[...], preferred_element_type=F32)
        o_ref[...] = _layer_norm(ALPHA * x_ref[...] + mix, g_ref[...], beta_ref[...])


def _outproj_ln(out_a, out_b, out_c, w_out, x, g, beta):
    m = x.shape[0]
    tm = min(m, 512)
    tk = LRU_WIDTH
    nb = SSD_WIDTH // tk
    return pl.pallas_call(
        _outproj_kernel,
        grid=(m // tm, OUT_KSTEPS),
        in_specs=[pl.BlockSpec((tm, tk), lambda i, k: (i, 0)),
                  pl.BlockSpec((tm, tk), lambda i, k: (i, jnp.clip(k - 1, 0, nb - 1))),
                  pl.BlockSpec((tm, tk), lambda i, k: (i, 0)),
                  pl.BlockSpec((tk, D_MODEL), lambda i, k: (k, 0)),
                  pl.BlockSpec((tm, D_MODEL), lambda i, k: (i, 0)),
                  pl.BlockSpec((1, D_MODEL), lambda i, k: (0, 0)),
                  pl.BlockSpec((1, D_MODEL), lambda i, k: (0, 0))],
        out_specs=pl.BlockSpec((tm, D_MODEL), lambda i, k: (i, 0)),
        out_shape=jax.ShapeDtypeStruct((m, D_MODEL), F32),
        scratch_shapes=[pltpu.VMEM((tm, D_MODEL), F32)],
        compiler_params=_cparams("arbitrary", "arbitrary"),
        name="outproj_ln",
    )(out_a, out_b, out_c, w_out, x, g, beta)


def _ffn_up_kernel(x_ref, wg_ref, wu_ref, o_ref, xb_ref):
    @pl.when(pl.program_id(1) == 0)
    def _():
        xb_ref[...] = x_ref[...].astype(BF16)

    xb = xb_ref[...]
    gate = jnp.dot(xb, wg_ref[...], preferred_element_type=F32)
    up = jnp.dot(xb, wu_ref[...], preferred_element_type=F32)
    o_ref[...] = (_silu(gate) * up).astype(BF16)


def _ffn_up(x, wg, wu):
    m = x.shape[0]
    tm = min(m, 1024)
    tn = 512
    return pl.pallas_call(
        _ffn_up_kernel,
        grid=(m // tm, D_FF // tn),
        in_specs=[pl.BlockSpec((tm, D_MODEL), lambda i, j: (i, 0)),
                  pl.BlockSpec((D_MODEL, tn), lambda i, j: (0, j)),
                  pl.BlockSpec((D_MODEL, tn), lambda i, j: (0, j))],
        out_specs=pl.BlockSpec((tm, tn), lambda i, j: (i, j)),
        out_shape=jax.ShapeDtypeStruct((m, D_FF), BF16),
        scratch_shapes=[pltpu.VMEM((tm, D_MODEL), BF16)],
        compiler_params=_cparams("arbitrary", "arbitrary"),
        name="ffn_up",
    )(x, wg, wu)


FFN_DOWN_KSTEPS = 4


def _ffn_down_kernel(h_ref, w_ref, x_ref, g_ref, beta_ref, o_ref, acc_ref):
    k = pl.program_id(1)

    @pl.when(k == 0)
    def _():
        acc_ref[...] = jnp.dot(h_ref[...], w_ref[...], preferred_element_type=F32)

    @pl.when(jnp.logical_and(k > 0, k < FFN_DOWN_KSTEPS - 1))
    def _():
        acc_ref[...] += jnp.dot(h_ref[...], w_ref[...], preferred_element_type=F32)

    @pl.when(k == FFN_DOWN_KSTEPS - 1)
    def _():
        ffn = acc_ref[...] + jnp.dot(h_ref[...], w_ref[...], preferred_element_type=F32)
        o_ref[...] = _layer_norm(ALPHA * x_ref[...] + ffn, g_ref[...], beta_ref[...])


def _ffn_down_ln(h, w_down, x, g, beta):
    m = x.shape[0]
    tm = min(m, 512)
    tk = D_FF // FFN_DOWN_KSTEPS
    assert tk * FFN_DOWN_KSTEPS == D_FF and tk % LANES == 0
    return pl.pallas_call(
        _ffn_down_kernel,
        grid=(m // tm, FFN_DOWN_KSTEPS),
        in_specs=[pl.BlockSpec((tm, tk), lambda i, k: (i, k)),
                  pl.BlockSpec((tk, D_MODEL), lambda i, k: (k, 0)),
                  pl.BlockSpec((tm, D_MODEL), lambda i, k: (i, 0)),
                  pl.BlockSpec((1, D_MODEL), lambda i, k: (0, 0)),
                  pl.BlockSpec((1, D_MODEL), lambda i, k: (0, 0))],
        out_specs=pl.BlockSpec((tm, D_MODEL), lambda i, k: (i, 0)),
        out_shape=jax.ShapeDtypeStruct((m, D_MODEL), F32),
        scratch_shapes=[pltpu.VMEM((tm, D_MODEL), F32)],
        compiler_params=_cparams("arbitrary", "arbitrary"),
        name="ffn_down_ln",
    )(h, w_down, x, g, beta)


def _conv_taps(u, first_rows, cw, cb, t_in_seq):
    out = cb + cw[CONV_WIDTH - 1:CONV_WIDTH, :] * u
    for j in range(1, CONV_WIDTH):
        shifted = jnp.where(t_in_seq < j, first_rows(j), pltpu.roll(u, j, 0))
        out = out + cw[CONV_WIDTH - 1 - j:CONV_WIDTH - j, :] * shifted
    return out


def _lru_kernel(gate_ref, lx_ref, buf_ref, h0_ref, cw_ref, cb_ref, wa_ref, ba_ref, wx_ref,
                bx_ref, lam_ref, out_ref, h1_ref, a_scr, b_scr, hin_scr, *, nseq, seqlen):
    rows = nseq * seqlen
    u = lx_ref[...]
    t_in_seq = _iota2((rows, LANES), 0) % seqlen
    if nseq == 1:
        buf = jnp.concatenate([buf_ref[0], jnp.zeros((rows - SUBLANES, LANES), F32)], axis=0) \
            if rows > SUBLANES else buf_ref[0]
    else:
        buf = buf_ref[0]
    nbuf = CONV_WIDTH - 1
    xc = _conv_taps(u, lambda j: pltpu.roll(buf, (j - nbuf) % rows, 0), cw_ref[...], cb_ref[...],
                    t_in_seq)

    r = _sigmoid(_dot1(xc, wa_ref[0]) + ba_ref[...])
    i = _sigmoid(_dot1(xc, wx_ref[0]) + bx_ref[...])
    log_a = (-LRU_C) * r * _softplus(-lam_ref[...])
    a = jnp.exp(log_a)
    b = jnp.sqrt(jnp.tanh(-log_a) * (a * a + 1.0)) * (i * xc)

    t8 = _iota2((rows, LANES), 0) % SUBLANES
    for s in (1, 2, 4):
        m = t8 >= s
        a_sh = pltpu.roll(a, s, 0)
        b_sh = pltpu.roll(b, s, 0)
        b = jnp.where(m, a * b_sh + b, b)
        a = jnp.where(m, a * a_sh, a)

    if seqlen == SUBLANES:
        h0 = h0_ref[0]
        hin = jnp.broadcast_to(h0[:, None, :], (nseq, SUBLANES, LANES)).reshape(rows, LANES)
        h = a * hin + b
        out_ref[...] = (h * _gelu_tanh(gate_ref[...])).astype(out_ref.dtype)
        a_scr[...] = h
        h1_ref[0] = a_scr[pl.ds(SUBLANES - 1, nseq, stride=SUBLANES), :]
    else:
        assert nseq == 1
        a_scr[...] = a
        b_scr[...] = b

        def carry_step(g, carry):
            base = pl.multiple_of(g * SUBLANES, SUBLANES)
            hin_scr[pl.ds(base, SUBLANES), :] = jnp.broadcast_to(carry, (SUBLANES, LANES))
            a7 = a_scr[pl.ds(base + SUBLANES - 1, 1), :]
            b7 = b_scr[pl.ds(base + SUBLANES - 1, 1), :]
            return a7 * carry + b7

        last = lax.fori_loop(0, rows // SUBLANES, carry_step, h0_ref[0])
        h = a_scr[...] * hin_scr[...] + b_scr[...]
        out_ref[...] = (h * _gelu_tanh(gate_ref[...])).astype(out_ref.dtype)
        h1_ref[0] = last


def _lru(proj, buf, h0, cw, cb, wa, ba, wx, bx, lam, *, nblk, nseq, seqlen):
    rows = nseq * seqlen
    gate_blk = C_GATE // LRU_BLOCK
    lx_blk = C_LX // LRU_BLOCK
    row = lambda s, h: (0, h)
    return pl.pallas_call(
        functools.partial(_lru_kernel, nseq=nseq, seqlen=seqlen),
        grid=(nblk, LRU_HEADS),
        in_specs=[pl.BlockSpec((rows, LRU_BLOCK), lambda s, h: (s, gate_blk + h)),
                  pl.BlockSpec((rows, LRU_BLOCK), lambda s, h: (s, lx_blk + h)),
                  pl.BlockSpec((1, nseq * SUBLANES, LRU_BLOCK), lambda s, h: (s, 0, h)),
                  pl.BlockSpec((1, nseq, LRU_BLOCK), lambda s, h: (s, 0, h)),
                  pl.BlockSpec((CONV_WIDTH, LRU_BLOCK), row),
                  pl.BlockSpec((1, LRU_BLOCK), row),
                  pl.BlockSpec((1, LRU_BLOCK, LRU_BLOCK), lambda s, h: (h, 0, 0)),
                  pl.BlockSpec((1, LRU_BLOCK), row),
                  pl.BlockSpec((1, LRU_BLOCK, LRU_BLOCK), lambda s, h: (h, 0, 0)),
                  pl.BlockSpec((1, LRU_BLOCK), row),
                  pl.BlockSpec((1, LRU_BLOCK), row)],
        out_specs=[pl.BlockSpec((rows, LRU_BLOCK), lambda s, h: (s, h)),
                   pl.BlockSpec((1, nseq, LRU_BLOCK), lambda s, h: (s, 0, h))],
        out_shape=[jax.ShapeDtypeStruct((nblk * rows, LRU_WIDTH), BF16),
                   jax.ShapeDtypeStruct((nblk, nseq, LRU_WIDTH), F32)],
        scratch_shapes=[pltpu.VMEM((rows, LANES), F32)] * 3,
        compiler_params=_cparams("arbitrary", "arbitrary"),
        name="lru",
    )(proj, proj, buf, h0, cw, cb, wa, ba, wx, bx, lam)


def _transpose_rows_to_lanes(x, t):
    if t < LANES:
        x = jnp.concatenate([x, jnp.zeros((LANES - t, LANES), F32)], axis=0)
    return x.T[:, :t]


def _ssd_kernel(z_ref, xbc_ref, dt_ref, buf_ref, s0_ref, cw_ref, cb_ref, dtb_ref, alog_ref,
                dch_ref, ng_ref, ehp_ref, out_ref, s1_ref, tail_ref, pad_ref, y_scr, *, T):
    c = pl.program_id(1)

    @pl.when(c == 0)
    def _():
        tail_ref[...] = buf_ref[0]
        s1_ref[...] = s0_ref[...]

    u = xbc_ref[...]
    pad_ref[0:SUBLANES, :] = tail_ref[...]
    pad_ref[SUBLANES:SUBLANES + T, :] = u
    cw = cw_ref[...]
    xc = cb_ref[...] + cw[CONV_WIDTH - 1:CONV_WIDTH, :] * u
    for j in range(1, CONV_WIDTH):
        xc = xc + cw[CONV_WIDTH - 1 - j:CONV_WIDTH - j, :] * pad_ref[SUBLANES - j:SUBLANES - j + T, :]
    tail_ref[...] = pad_ref[T:T + SUBLANES, :]

    xa = _silu(xc)
    xs = xa[:, :SSD_WIDTH]
    bm = xa[:, SSD_WIDTH:SSD_WIDTH + SSD_GROUPS * SSD_STATE]
    cm = xa[:, SSD_WIDTH + SSD_GROUPS * SSD_STATE:]
    dt = _softplus(dt_ref[...] + dtb_ref[...])
    da = dt * (-jnp.exp(alog_ref[...]))
    ii = _iota2((T, T), 0)
    jj = _iota2((T, T), 1)
    causal = ii >= jj
    cum = _dotx_left(causal.astype(BF16), da)
    cum_t = _transpose_rows_to_lanes(cum, T)
    ehp = ehp_ref[...]
    cumx = _dotx(cum, ehp)
    dtx = _dotx(dt, ehp)
    xdt = xs * dtx
    ecum = jnp.exp(cumx)
    xdtd = xdt * jnp.exp(cumx[T - 1:T, :] - cumx)

    gw = SSD_GROUP_WIDTH
    for g in range(SSD_GROUPS):
        h_lo = g * SSD_HEADS_PER_GROUP
        cg = cm[:, g * SSD_STATE:(g + 1) * SSD_STATE]
        bg = bm[:, g * SSD_STATE:(g + 1) * SSD_STATE]
        cb = _dot1(cg, bg, _NT)
        sg = s1_ref[0, h_lo:h_lo + SSD_HEADS_PER_GROUP].reshape(gw, SSD_STATE)
        y_off = _dot1(cg, sg, _NT) * ecum[:, g * gw:(g + 1) * gw]
        y_diag = []
        decay = []
        for e in range(SSD_HEADS_PER_GROUP):
            h = h_lo + e
            seg = cum[:, h:h + 1] - cum_t[h:h + 1, :]
            lm = jnp.where(causal, jnp.exp(seg), 0.0)
            y_diag.append(_dot1(cb * lm, xdt[:, h * HEAD_DIM:(h + 1) * HEAD_DIM]))
            decay.append(jnp.broadcast_to(jnp.exp(cum_t[h:h + 1, T - 1:T]), (HEAD_DIM, SSD_STATE)))
        y_scr[:, g * gw:(g + 1) * gw] = jnp.concatenate(y_diag, axis=1) + y_off
        st = _dot1(xdtd[:, g * gw:(g + 1) * gw], bg, _TN)
        s_new = sg * jnp.concatenate(decay, axis=0) + st
        s1_ref[0, h_lo:h_lo + SSD_HEADS_PER_GROUP] = s_new.reshape(
            SSD_HEADS_PER_GROUP, HEAD_DIM, SSD_STATE)

    y = (y_scr[...] + dch_ref[...] * xs) * _silu(z_ref[...])
    outs = []
    for g in range(SSD_GROUPS):
        yg = y[:, g * gw:(g + 1) * gw]
        ms = jnp.mean(yg * yg, axis=-1, keepdims=True)
        outs.append(yg * lax.rsqrt(ms + 1e-5))
    out_ref[...] = (jnp.concatenate(outs, axis=1) * ng_ref[...]).astype(out_ref.dtype)


def _ssd(proj, buf, s0, cw, cb, dtb, alog, dch, ng, ehp, *, nseq, seqlen):
    T = min(SSD_CHUNK, seqlen)
    nc = seqlen // T
    const = lambda b, c: (0, 0)
    state_spec = pl.BlockSpec((1, SSD_HEADS, HEAD_DIM, SSD_STATE), lambda b, c: (b, 0, 0, 0))
    return pl.pallas_call(
        functools.partial(_ssd_kernel, T=T),
        grid=(nseq, nc),
        in_specs=[pl.BlockSpec((T, SSD_WIDTH), lambda b, c: (b * nc + c, C_Z // SSD_WIDTH)),
                  pl.BlockSpec((T, SSD_CONV_DIM), lambda b, c: (b * nc + c, C_XBC // SSD_CONV_DIM)),
                  pl.BlockSpec((T, DT_PAD), lambda b, c: (b * nc + c, C_DT // DT_PAD)),
                  pl.BlockSpec((1, SUBLANES, SSD_CONV_DIM), lambda b, c: (b, 0, 0)),
                  state_spec,
                  pl.BlockSpec((CONV_WIDTH, SSD_CONV_DIM), const),
                  pl.BlockSpec((1, SSD_CONV_DIM), const),
                  pl.BlockSpec((1, DT_PAD), const),
                  pl.BlockSpec((1, DT_PAD), const),
                  pl.BlockSpec((1, SSD_WIDTH), const),
                  pl.BlockSpec((1, SSD_WIDTH), const),
                  pl.BlockSpec((DT_PAD, SSD_WIDTH), const)],
        out_specs=[pl.BlockSpec((T, SSD_WIDTH), lambda b, c: (b * nc + c, 0)), state_spec],
        out_shape=[jax.ShapeDtypeStruct((nseq * seqlen, SSD_WIDTH), BF16),
                   jax.ShapeDtypeStruct((nseq, SSD_HEADS, HEAD_DIM, SSD_STATE), F32)],
        scratch_shapes=[pltpu.VMEM((SUBLANES, SSD_CONV_DIM), F32),
                        pltpu.VMEM((T + SUBLANES, SSD_CONV_DIM), F32),
                        pltpu.VMEM((T, SSD_WIDTH), F32)],
        compiler_params=_cparams("arbitrary", "arbitrary"),
        name="ssd",
    )(proj, proj, proj, buf, s0, cw, cb, dtb, alog, dch, ng, ehp)


def _rwkv_kernel(r_ref, k_ref, v_ref, g_ref, wa_ref, sh_ref, s0_ref, mu_ref, w0_ref, wup_ref,
                 a0_ref, aup_ref, gup_ref, kkw_ref, kaw_ref, rkw_ref, lng_ref, lnb_ref, ones_ref,
                 out_ref, s1_ref, prev_scr, h_scr, al_scr, be_scr, kt_scr, rt_scr, bs_scr,
                 ks_scr, v_scr, gam_scr, o_scr, *, T, nchunks):
    c = pl.program_id(1)
    W = RWKV_WIDTH
    D = HEAD_DIM

    @pl.when(c == 0)
    def _():
        prev_scr[...] = sh_ref[0]
        for h in range(RWKV_HEADS):
            h_scr[h] = s0_ref[0, h].T

    def token_shift(p, lo, hi):
        first = _iota2(p.shape, 0) == 0
        prev = jnp.where(first, prev_scr[:, lo:hi], pltpu.roll(p, 1, 0))
        prev_scr[:, lo:hi] = p[T - 1:T, :]
        return p + (prev - p) * mu_ref[:, lo:hi]

    xr = token_shift(r_ref[...], 0, W)
    xk = token_shift(k_ref[...], W, 2 * W)
    xv = token_shift(v_ref[...], 2 * W, 3 * W)
    xg = token_shift(g_ref[...], 3 * W, 3 * W + G_PAD)
    xwa = token_shift(wa_ref[...], 3 * W + G_PAD, RW_PACK)

    w_lin = w0_ref[...] + _dot1(jnp.tanh(xwa), wup_ref[...])
    a = _sigmoid(a0_ref[...] + _dot1(xwa, aup_ref[...]))
    gate = _dot1(_sigmoid(xg), gup_ref[...])
    lw = -jnp.exp(-_softplus(-w_lin) - 0.5)

    ones = ones_ref[...]

    def head_sum(x):
        tw = ones.shape[0]
        return jnp.concatenate(
            [_dotx(x[:, i * tw:(i + 1) * tw], ones) for i in range(W // tw)], axis=1)

    kk = xk * kkw_ref[...]
    kk = kk / jnp.maximum(jnp.sqrt(head_sum(kk * kk)), 1e-12)
    kp = xk * (1.0 + (a - 1.0) * kaw_ref[...])

    ii = _iota2((T, T), 0)
    jj = _iota2((T, T), 1)
    incl = ii >= jj
    strict = ii > jj
    eye_t = jnp.where(ii == jj, 1.0, 0.0)
    eye_d = _iota2((D, D), 0) == _iota2((D, D), 1)

    cum = _dotx_left(incl.astype(BF16), lw)
    e_neg = jnp.exp(-cum)
    gam = jnp.exp(cum[T - 1:T, :])
    be = kk * a * e_neg
    kt = kp * e_neg
    al_scr[...] = -kk * jnp.exp(cum - lw)
    be_scr[...] = be
    kt_scr[...] = kt
    rt_scr[...] = xr * jnp.exp(cum)
    bs_scr[...] = be * gam
    ks_scr[...] = kt * gam
    v_scr[...] = xv
    gam_scr[...] = gam

    heads = range(RWKV_HEADS)

    def per_head(ref):
        return [ref[:, h * D:(h + 1) * D] for h in heads]

    al, be_h, kt_h, rt = per_head(al_scr), per_head(be_scr), per_head(kt_scr), per_head(rt_scr)
    bs_h, ks_h, vv, gam_h = per_head(bs_scr), per_head(ks_scr), per_head(v_scr), per_head(gam_scr)
    gram = [_dot3(jnp.concatenate([al[h], rt[h]], axis=0),
                  jnp.concatenate([be_h[h], kt_h[h]], axis=0), _NT) for h in heads]
    a_ab = [jnp.where(strict, gram[h][:T, :T], 0.0) for h in heads]
    a_ak = [jnp.where(strict, gram[h][:T, T:], 0.0) for h in heads]
    r_b = [jnp.where(incl, gram[h][T:, :T], 0.0) for h in heads]
    r_k = [jnp.where(incl, gram[h][T:, T:], 0.0) for h in heads]
    inv = [eye_t + a_ab[h] for h in heads]
    power = a_ab
    for _ in range(int(math.log2(T)) - 1):
        power = [_dot3(power[h], power[h]) for h in heads]
        inv = [inv[h] + _dot3(inv[h], power[h]) for h in heads]
    akv = [_dot3(a_ak[h], vv[h]) for h in heads]
    pw = [_dot3(inv[h], jnp.concatenate([al[h], akv[h]], axis=1)) for h in heads]
    qo = [_dot3(r_b[h], pw[h]) for h in heads]
    rkv = [_dot3(r_k[h], vv[h]) for h in heads]
    gz = [_dot3(bs_h[h], pw[h], _TN) for h in heads]
    ksv = [_dot3(ks_h[h], vv[h], _TN) for h in heads]
    hmat = [h_scr[h] for h in heads]
    qh = [_dot3(rt[h] + qo[h][:, :D], hmat[h]) for h in heads]
    gmat = [jnp.where(eye_d, jnp.broadcast_to(gam_h[h], (D, D)), 0.0) + gz[h][:, :D] for h in heads]
    gh = [_dot3(gmat[h], hmat[h]) for h in heads]
    for h in heads:
        h_scr[h] = gh[h] + gz[h][:, D:] + ksv[h]
    for p in range(RWKV_HEADS * D // LANES):
        o_scr[:, p * LANES:(p + 1) * LANES] = jnp.concatenate(
            [qh[h] + qo[h][:, D:] + rkv[h] for h in range(2 * p, 2 * p + 2)], axis=1)

    o = o_scr[...]
    mean = head_sum(o) * (1.0 / D)
    d = o - mean
    var = head_sum(d * d) * (1.0 / D)
    on = d * lax.rsqrt(var + RWKV_GN_EPS) * lng_ref[...] + lnb_ref[...]
    bonus = head_sum(xr * kp * rkw_ref[...]) * xv
    out_ref[...] = ((on + bonus) * gate).astype(out_ref.dtype)

    @pl.when(c == nchunks - 1)
    def _():
        for h in range(RWKV_HEADS):
            s1_ref[0, h] = h_scr[h].T


def _rwkv(proj, sh, s0, mu, w0, wup, a0, aup, gup, kkw, kaw, rkw, lng, lnb, ones, *, nseq, seqlen):
    T = min(RWKV_CHUNK, seqlen)
    nc = seqlen // T
    W = RWKV_WIDTH
    const = lambda b, c: (0, 0)
    rowblk = lambda col, width: pl.BlockSpec((T, width), lambda b, c: (b * nc + c, col // width))
    state_spec = pl.BlockSpec((1, RWKV_HEADS, HEAD_DIM, HEAD_DIM), lambda b, c: (b, 0, 0, 0))
    vec = pl.BlockSpec((1, W), const)
    return pl.pallas_call(
        functools.partial(_rwkv_kernel, T=T, nchunks=nc),
        grid=(nseq, nc),
        in_specs=[rowblk(C_R, W), rowblk(C_K, W), rowblk(C_V, W), rowblk(C_G, G_PAD),
                  rowblk(C_WA, WA_PAD),
                  pl.BlockSpec((1, 1, RW_PACK), lambda b, c: (b, 0, 0)),
                  state_spec,
                  pl.BlockSpec((1, RW_PACK), const),
                  vec, pl.BlockSpec((WA_PAD, W), const),
                  vec, pl.BlockSpec((WA_PAD, W), const),
                  pl.BlockSpec((G_PAD, W), const),
                  vec, vec, vec, vec, vec,
                  pl.BlockSpec((2 * LANES, 2 * LANES), const)],
        out_specs=[pl.BlockSpec((T, W), lambda b, c: (b * nc + c, 0)), state_spec],
        out_shape=[jax.ShapeDtypeStruct((nseq * seqlen, W), BF16),
                   jax.ShapeDtypeStruct((nseq, RWKV_HEADS, HEAD_DIM, HEAD_DIM), F32)],
        scratch_shapes=[pltpu.VMEM((1, RW_PACK), F32),
                        pltpu.VMEM((RWKV_HEADS, HEAD_DIM, HEAD_DIM), F32)]
        + [pltpu.VMEM((T, W), F32)] * 7
        + [pltpu.VMEM((1, W), F32), pltpu.VMEM((T, W), F32)],
        compiler_params=_cparams("arbitrary", "arbitrary"),
        name="rwkv",
    )(proj, proj, proj, proj, proj, sh, s0, mu, w0, wup, a0, aup, gup, kkw, kaw, rkw, lng, lnb, ones)


_O_DT = 2 * LRU_WIDTH + SSD_WIDTH + SSD_CONV_DIM
_O_RW = _O_DT + SSD_HEADS
_O_XW = 3 * RWKV_WIDTH
_O_XG = _O_XW + RWKV_R_W + RWKV_R_A


def _zeros_like_cols(x, n):
    return jnp.zeros(x.shape[:-1] + (n,), x.dtype)


def _pack_rwkv_cols(x):
    return jnp.concatenate([x[..., :_O_XW], x[..., _O_XG:], _zeros_like_cols(x, G_PAD - RWKV_R_G),
                            x[..., _O_XW:_O_XG]], axis=-1)


def _pack_proj_cols(w):
    return jnp.concatenate([w[..., :_O_DT], _pack_rwkv_cols(w[..., _O_RW:]), w[..., _O_DT:_O_RW],
                            _zeros_like_cols(w, DT_PAD - SSD_HEADS)], axis=-1)


def _pad_rows(x, n_before, n_total):
    b, r, c = x.shape
    return jnp.concatenate([jnp.zeros((b, n_before, c), x.dtype), x,
                            jnp.zeros((b, n_total - n_before - r, c), x.dtype)], axis=1)


def _layer_params(l, p):
    row = lambda v: v.reshape(1, -1)
    pad_lanes = lambda v, n: jnp.concatenate([v, jnp.zeros((n - v.shape[0],), v.dtype)]).reshape(1, n)
    zeros_w = jnp.zeros((RWKV_R_W, RWKV_WIDTH), F32)
    lp = dict(
        w_in=_pack_proj_cols(p['w_in'][l]).astype(BF16),
        lru=(p['lru_conv_w'][l], row(p['lru_conv_b'][l]), p['lru_wa'][l].astype(BF16),
             row(p['lru_ba'][l]), p['lru_wx'][l].astype(BF16), row(p['lru_bx'][l]),
             row(p['lru_lambda'][l])),
        ssd=(p['ssd_conv_w'][l], row(p['ssd_conv_b'][l]), pad_lanes(p['ssd_dt_bias'][l], DT_PAD),
             pad_lanes(p['ssd_a_log'][l], DT_PAD), row(jnp.repeat(p['ssd_d'][l], HEAD_DIM)),
             row(p['ssd_norm_g'][l])),
        rwkv=(row(_pack_rwkv_cols(p['rwkv_mu'][l])), row(p['rwkv_w0'][l]),
              jnp.concatenate([p['rwkv_w_up'][l], zeros_w], axis=0).astype(BF16),
              row(p['rwkv_a0'][l]),
              jnp.concatenate([zeros_w, p['rwkv_a_up'][l]], axis=0).astype(BF16),
              jnp.concatenate([p['rwkv_g_up'][l],
                               jnp.zeros((G_PAD - RWKV_R_G, RWKV_WIDTH), F32)], axis=0).astype(BF16),
              row(p['rwkv_k_k'][l]), row(p['rwkv_k_a'][l]), row(p['rwkv_r_k'][l]),
              row(p['rwkv_ln_g'][l]), row(p['rwkv_ln_b'][l])),
        w_out=p['w_out'][l].astype(BF16),
        ln1=(row(p['ln1_g'][l]), row(p['ln1_b'][l])),
        w_gate=p['w_gate'][l].astype(BF16), w_up=p['w_up'][l].astype(BF16),
        w_down=p['w_down'][l].astype(BF16),
        ln2=(row(p['ln2_g'][l]), row(p['ln2_b'][l])),
    )
    return lp


def _constants():
    lane = jnp.arange(SSD_WIDTH) // HEAD_DIM
    ehp = (jnp.arange(DT_PAD)[:, None] == lane[None, :]).astype(BF16)
    blk = jnp.arange(2 * LANES) // HEAD_DIM
    ones = (blk[:, None] == blk[None, :]).astype(BF16)
    return ehp, ones


def _layer(x, state, lp, consts, *, nseq, seqlen, lru_nblk):
    lru_conv0, lru_h0, ssd_conv0, ssd_s0, rw_shift0, rw_s0 = state
    ehp, ones = consts
    nbuf = CONV_WIDTH - 1
    proj = _proj(x, lp['w_in'])

    lru_nseq = nseq // lru_nblk
    out_a, lru_h1 = _lru(
        proj, _pad_rows(lru_conv0, 0, SUBLANES).reshape(lru_nblk, lru_nseq * SUBLANES, LRU_WIDTH),
        lru_h0.reshape(lru_nblk, lru_nseq, LRU_WIDTH), *lp['lru'],
        nblk=lru_nblk, nseq=lru_nseq, seqlen=seqlen)
    out_b, ssd_s1 = _ssd(proj, _pad_rows(ssd_conv0, SUBLANES - nbuf, SUBLANES), ssd_s0,
                         *lp['ssd'], ehp, nseq=nseq, seqlen=seqlen)
    out_c, rw_s1 = _rwkv(proj, _pack_rwkv_cols(rw_shift0)[:, None, :], rw_s0, *lp['rwkv'], ones,
                         nseq=nseq, seqlen=seqlen)

    y = _outproj_ln(out_a, out_b, out_c, lp['w_out'], x, *lp['ln1'])
    y = _ffn_down_ln(_ffn_up(y, lp['w_gate'], lp['w_up']), lp['w_down'], y, *lp['ln2'])

    p3 = proj.reshape(nseq, seqlen, N_PROJ)
    tail = p3[:, seqlen - nbuf:, :]
    last = p3[:, seqlen - 1, :]
    rw_shift1 = jnp.concatenate([last[:, C_R:C_R + 3 * RWKV_WIDTH], last[:, C_WA:C_WA + WA_PAD],
                                 last[:, C_G:C_G + RWKV_R_G]], axis=-1)
    new_state = (tail[:, :, C_LX:C_LX + LRU_WIDTH], lru_h1.reshape(nseq, LRU_WIDTH),
                 tail[:, :, C_XBC:C_XBC + SSD_CONV_DIM], ssd_s1, rw_shift1, rw_s1)
    return y, new_state


def kernel(x_prompt, x_sample, state_lru_conv, state_lru_h, state_ssd_conv, state_ssd,
           state_rwkv_shift, state_rwkv, w_in, lru_conv_w, lru_conv_b, lru_wa, lru_ba, lru_wx,
           lru_bx, lru_lambda, ssd_conv_w, ssd_conv_b, ssd_dt_bias, ssd_a_log, ssd_d, ssd_norm_g,
           rwkv_mu, rwkv_w0, rwkv_w_up, rwkv_a0, rwkv_a_up, rwkv_g_up, rwkv_k_k, rwkv_k_a,
           rwkv_r_k, rwkv_ln_g, rwkv_ln_b, w_out, ln1_g, ln1_b, w_gate, w_up, w_down, ln2_g, ln2_b):
    params = dict(
        w_in=w_in, lru_conv_w=lru_conv_w, lru_conv_b=lru_conv_b, lru_wa=lru_wa, lru_ba=lru_ba,
        lru_wx=lru_wx, lru_bx=lru_bx, lru_lambda=lru_lambda, ssd_conv_w=ssd_conv_w,
        ssd_conv_b=ssd_conv_b, ssd_dt_bias=ssd_dt_bias, ssd_a_log=ssd_a_log, ssd_d=ssd_d,
        ssd_norm_g=ssd_norm_g, rwkv_mu=rwkv_mu, rwkv_w0=rwkv_w0, rwkv_w_up=rwkv_w_up,
        rwkv_a0=rwkv_a0, rwkv_a_up=rwkv_a_up, rwkv_g_up=rwkv_g_up, rwkv_k_k=rwkv_k_k,
        rwkv_k_a=rwkv_k_a, rwkv_r_k=rwkv_r_k.reshape(DEPTH, RWKV_WIDTH), rwkv_ln_g=rwkv_ln_g,
        rwkv_ln_b=rwkv_ln_b, w_out=w_out, ln1_g=ln1_g, ln1_b=ln1_b, w_gate=w_gate, w_up=w_up,
        w_down=w_down, ln2_g=ln2_g, ln2_b=ln2_b)
    bp, lp_len, _ = x_prompt.shape
    bs, ls_len, _ = x_sample.shape
    consts = _constants()
    nbuf = CONV_WIDTH - 1
    zero_state = (jnp.zeros((bp, nbuf, LRU_WIDTH), F32), jnp.zeros((bp, LRU_WIDTH), F32),
                  jnp.zeros((bp, nbuf, SSD_CONV_DIM), F32),
                  jnp.zeros((bp, SSD_HEADS, HEAD_DIM, SSD_STATE), F32),
                  jnp.zeros((bp, RWKV_SHIFT), F32),
                  jnp.zeros((bp, RWKV_HEADS, HEAD_DIM, HEAD_DIM), F32))
    yp = x_prompt.reshape(bp * lp_len, D_MODEL)
    ys = x_sample.reshape(bs * ls_len, D_MODEL)
    new_p = [[] for _ in range(6)]
    new_s = [[] for _ in range(6)]
    for l in range(DEPTH):
        lp = _layer_params(l, params)
        yp, sp = _layer(yp, zero_state, lp, consts, nseq=bp, seqlen=lp_len, lru_nblk=bp)
        ys, ss = _layer(ys, (state_lru_conv[l], state_lru_h[l], state_ssd_conv[l], state_ssd[l],
                             state_rwkv_shift[l], state_rwkv[l]), lp, consts,
                        nseq=bs, seqlen=ls_len, lru_nblk=1)
        for i in range(6):
            new_p[i].append(sp[i])
            new_s[i].append(ss[i])
    return (yp.reshape(bp, lp_len, D_MODEL), ys.reshape(bs, ls_len, D_MODEL),
            *[jnp.stack(v) for v in new_p], *[jnp.stack(v) for v in new_s])
```

```python
import functools
import math

import jax
import jax.numpy as jnp
from jax import lax
from jax.experimental import pallas as pl
from jax.experimental.pallas import tpu as pltpu

F32 = jnp.float32
BF16 = jnp.bfloat16

D_MODEL = 2048
DEPTH = 2
D_MIX = 2 * D_MODEL
HEAD_DIM = 64
CONV_WIDTH = 4
LRU_WIDTH = D_MIX // 4
LRU_HEADS = 8
LRU_BLOCK = LRU_WIDTH // LRU_HEADS
LRU_C = 8.0
SSD_WIDTH = D_MIX // 2
SSD_HEADS = SSD_WIDTH // HEAD_DIM
SSD_GROUPS = 8
SSD_STATE = 128
SSD_CHUNK = 128
SSD_CONV_DIM = SSD_WIDTH + 2 * SSD_GROUPS * SSD_STATE
SSD_GROUP_WIDTH = SSD_WIDTH // SSD_GROUPS
SSD_HEADS_PER_GROUP = SSD_HEADS // SSD_GROUPS
RWKV_WIDTH = D_MIX - LRU_WIDTH - SSD_WIDTH
RWKV_HEADS = RWKV_WIDTH // HEAD_DIM
RWKV_R_W = max(32, int(round(1.8 * RWKV_WIDTH ** 0.5 / 32)) * 32)
RWKV_R_A = max(32, int(round(1.8 * RWKV_WIDTH ** 0.5 / 32)) * 32)
RWKV_R_G = max(32, int(round(0.6 * RWKV_WIDTH ** 0.8 / 32)) * 32)
RWKV_SHIFT = 3 * RWKV_WIDTH + RWKV_R_W + RWKV_R_A + RWKV_R_G
RWKV_GN_EPS = 64e-5
RWKV_CHUNK = 64
D_FF = -(-(8 * D_MODEL) // (3 * 256)) * 256
ALPHA = (2 * DEPTH) ** 0.25
LN_EPS = 1e-5

LANES = 128
SUBLANES = 8
VMEM_LIMIT = 56 * 1024 * 1024

C_GATE = 0
C_LX = C_GATE + LRU_WIDTH
C_Z = C_LX + LRU_WIDTH
C_XBC = C_Z + SSD_WIDTH
C_R = C_XBC + SSD_CONV_DIM
C_K = C_R + RWKV_WIDTH
C_V = C_K + RWKV_WIDTH
C_G = C_V + RWKV_WIDTH
G_PAD = 2 * LANES
C_WA = C_G + G_PAD
WA_PAD = LANES
C_DT = C_WA + WA_PAD
DT_PAD = LANES
N_PROJ = C_DT + DT_PAD
assert RWKV_R_G <= G_PAD and RWKV_R_W + RWKV_R_A == WA_PAD and SSD_HEADS <= DT_PAD
RW_PACK = 3 * RWKV_WIDTH + G_PAD + WA_PAD

_O_DT = 2 * LRU_WIDTH + SSD_WIDTH + SSD_CONV_DIM
_O_RW = _O_DT + SSD_HEADS
_O_XW = 3 * RWKV_WIDTH
_O_XG = _O_XW + RWKV_R_W + RWKV_R_A


def _cparams(*sem):
    return pltpu.CompilerParams(dimension_semantics=sem, vmem_limit_bytes=VMEM_LIMIT)


_NN = (((1,), (0,)), ((), ()))
_NT = (((1,), (1,)), ((), ()))
_TN = (((0,), (0,)), ((), ()))


def _dg(a, b, dims):
    return lax.dot_general(a, b, dims, preferred_element_type=F32)


def _dot1(a, b, dims=_NN):
    return _dg(a.astype(BF16), b.astype(BF16), dims)


def _split2(x):
    hi = x.astype(BF16)
    lo = (x - hi.astype(F32)).astype(BF16)
    return hi, lo


def _split3(x):
    hi = x.astype(BF16)
    r1 = x - hi.astype(F32)
    mid = r1.astype(BF16)
    lo = (r1 - mid.astype(F32)).astype(BF16)
    return hi, mid, lo


def _dot3(a, b, dims=_NN):
    ah, al = _split2(a)
    bh, bl = _split2(b)
    return _dg(ah, bh, dims) + (_dg(ah, bl, dims) + _dg(al, bh, dims))


def _dotx(a, w_exact):
    hi, mid, lo = _split3(a)
    return _dg(hi, w_exact, _NN) + (_dg(mid, w_exact, _NN) + _dg(lo, w_exact, _NN))


def _dotx2(a, w_exact):
    hi, lo = _split2(a)
    return _dg(hi, w_exact, _NN) + _dg(lo, w_exact, _NN)


def _dotx_left(w_exact, a):
    hi, mid, lo = _split3(a)
    return _dg(w_exact, hi, _NN) + (_dg(w_exact, mid, _NN) + _dg(w_exact, lo, _NN))


def _iota2(shape, dim):
    return lax.broadcasted_iota(jnp.int32, shape, dim)


def _softplus(x):
    return jnp.maximum(x, 0.0) + jnp.log1p(jnp.exp(-jnp.abs(x)))


def _sigmoid(x):
    return 1.0 / (1.0 + jnp.exp(-x))


def _silu(x):
    return x * _sigmoid(x)


def _gelu_tanh(x):
    c = math.sqrt(2.0 / math.pi)
    return 0.5 * x * (1.0 + jnp.tanh(c * (x + 0.044715 * (x * x * x))))


def _layer_norm(y, g, b):
    mu = jnp.mean(y, axis=-1, keepdims=True)
    d = y - mu
    var = jnp.mean(d * d, axis=-1, keepdims=True)
    return d * lax.rsqrt(var + LN_EPS) * g + b


PROJ_TN = 512
N_MAIN = C_R
N_TAIL = N_PROJ - N_MAIN
assert N_MAIN == _O_DT and N_MAIN % PROJ_TN == 0 and N_TAIL % PROJ_TN == 0


def _proj_kernel(x_ref, wm_ref, wt_ref, o_ref, xb_ref):
    j = pl.program_id(1)

    @pl.when(j == 0)
    def _():
        xb_ref[...] = x_ref[...].astype(BF16)

    @pl.when(j < N_MAIN // PROJ_TN)
    def _():
        o_ref[...] = jnp.dot(xb_ref[...], wm_ref[0], preferred_element_type=F32)

    @pl.when(j >= N_MAIN // PROJ_TN)
    def _():
        o_ref[...] = jnp.dot(xb_ref[...], wt_ref[0], preferred_element_type=F32)


def _proj(x, w_main, w_tail, l):
    m = x.shape[0]
    tm = min(m, 1024)
    tn = PROJ_TN
    nmain = N_MAIN // tn
    return pl.pallas_call(
        _proj_kernel,
        grid=(m // tm, N_PROJ // tn),
        in_specs=[pl.BlockSpec((tm, D_MODEL), lambda i, j: (i, 0)),
                  pl.BlockSpec((1, D_MODEL, tn), lambda i, j: (l, 0, jnp.minimum(j, nmain - 1))),
                  pl.BlockSpec((1, D_MODEL, tn), lambda i, j: (l, 0, jnp.maximum(j - nmain, 0)))],
        out_specs=pl.BlockSpec((tm, tn), lambda i, j: (i, j)),
        out_shape=jax.ShapeDtypeStruct((m, N_PROJ), F32),
        scratch_shapes=[pltpu.VMEM((tm, D_MODEL), BF16)],
        compiler_params=_cparams("arbitrary", "arbitrary"),
        name="proj",
    )(x, w_main, w_tail)


OUT_KSTEPS = D_MIX // LRU_WIDTH


def _outproj_kernel(a_ref, b_ref, c_ref, w_ref, x_ref, g_ref, beta_ref, o_ref, acc_ref):
    k = pl.program_id(1)

    @pl.when(k == 0)
    def _():
        acc_ref[...] = jnp.dot(a_ref[...], w_ref[0], preferred_element_type=F32)

    @pl.when(jnp.logical_and(k > 0, k < OUT_KSTEPS - 1))
    def _():
        acc_ref[...] += jnp.dot(b_ref[...], w_ref[0], preferred_element_type=F32)

    @pl.when(k == OUT_KSTEPS - 1)
    def _():
        mix = acc_ref[...] + jnp.dot(c_ref[...], w_ref[0], preferred_element_type=F32)
        o_ref[...] = _layer_norm(ALPHA * x_ref[...] + mix, g_ref[...], beta_ref[...])


def _outproj_ln(out_a, out_b, out_c, w_out, l, x, g, beta):
    m = x.shape[0]
    tm = min(m, 512)
    tk = LRU_WIDTH
    nb = SSD_WIDTH // tk
    return pl.pallas_call(
        _outproj_kernel,
        grid=(m // tm, OUT_KSTEPS),
        in_specs=[pl.BlockSpec((tm, tk), lambda i, k: (i, 0)),
                  pl.BlockSpec((tm, tk), lambda i, k: (i, jnp.clip(k - 1, 0, nb - 1))),
                  pl.BlockSpec((tm, tk), lambda i, k: (i, 0)),
                  pl.BlockSpec((1, tk, D_MODEL), lambda i, k: (l, k, 0)),
                  pl.BlockSpec((tm, D_MODEL), lambda i, k: (i, 0)),
                  pl.BlockSpec((1, D_MODEL), lambda i, k: (0, 0)),
                  pl.BlockSpec((1, D_MODEL), lambda i, k: (0, 0))],
        out_specs=pl.BlockSpec((tm, D_MODEL), lambda i, k: (i, 0)),
        out_shape=jax.ShapeDtypeStruct((m, D_MODEL), F32),
        scratch_shapes=[pltpu.VMEM((tm, D_MODEL), F32)],
        compiler_params=_cparams("arbitrary", "arbitrary"),
        name="outproj_ln",
    )(out_a, out_b, out_c, w_out, x, g, beta)


def _ffn_up_kernel(x_ref, wg_ref, wu_ref, o_ref, xb_ref):
    @pl.when(pl.program_id(1) == 0)
    def _():
        xb_ref[...] = x_ref[...].astype(BF16)

    xb = xb_ref[...]
    gate = jnp.dot(xb, wg_ref[0], preferred_element_type=F32)
    up = jnp.dot(xb, wu_ref[0], preferred_element_type=F32)
    o_ref[...] = (_silu(gate) * up).astype(BF16)


def _ffn_up(x, wg, wu, l):
    m = x.shape[0]
    tm = min(m, 1024)
    tn = 512
    return pl.pallas_call(
        _ffn_up_kernel,
        grid=(m // tm, D_FF // tn),
        in_specs=[pl.BlockSpec((tm, D_MODEL), lambda i, j: (i, 0)),
                  pl.BlockSpec((1, D_MODEL, tn), lambda i, j: (l, 0, j)),
                  pl.BlockSpec((1, D_MODEL, tn), lambda i, j: (l, 0, j))],
        out_specs=pl.BlockSpec((tm, tn), lambda i, j: (i, j)),
        out_shape=jax.ShapeDtypeStruct((m, D_FF), BF16),
        scratch_shapes=[pltpu.VMEM((tm, D_MODEL), BF16)],
        compiler_params=_cparams("arbitrary", "arbitrary"),
        name="ffn_up",
    )(x, wg, wu)


FFN_DOWN_KSTEPS = 4


def _ffn_down_kernel(h_ref, w_ref, x_ref, g_ref, beta_ref, o_ref, acc_ref):
    k = pl.program_id(1)

    @pl.when(k == 0)
    def _():
        acc_ref[...] = jnp.dot(h_ref[...], w_ref[0], preferred_element_type=F32)

    @pl.when(jnp.logical_and(k > 0, k < FFN_DOWN_KSTEPS - 1))
    def _():
        acc_ref[...] += jnp.dot(h_ref[...], w_ref[0], preferred_element_type=F32)

    @pl.when(k == FFN_DOWN_KSTEPS - 1)
    def _():
        ffn = acc_ref[...] + jnp.dot(h_ref[...], w_ref[0], preferred_element_type=F32)
        o_ref[...] = _layer_norm(ALPHA * x_ref[...] + ffn, g_ref[...], beta_ref[...])


def _ffn_down_ln(h, w_down, l, x, g, beta):
    m = x.shape[0]
    tm = min(m, 512)
    tk = D_FF // FFN_DOWN_KSTEPS
    assert tk * FFN_DOWN_KSTEPS == D_FF and tk % LANES == 0
    return pl.pallas_call(
        _ffn_down_kernel,
        grid=(m // tm, FFN_DOWN_KSTEPS),
        in_specs=[pl.BlockSpec((tm, tk), lambda i, k: (i, k)),
                  pl.BlockSpec((1, tk, D_MODEL), lambda i, k: (l, k, 0)),
                  pl.BlockSpec((tm, D_MODEL), lambda i, k: (i, 0)),
                  pl.BlockSpec((1, D_MODEL), lambda i, k: (0, 0)),
                  pl.BlockSpec((1, D_MODEL), lambda i, k: (0, 0))],
        out_specs=pl.BlockSpec((tm, D_MODEL), lambda i, k: (i, 0)),
        out_shape=jax.ShapeDtypeStruct((m, D_MODEL), F32),
        scratch_shapes=[pltpu.VMEM((tm, D_MODEL), F32)],
        compiler_params=_cparams("arbitrary", "arbitrary"),
        name="ffn_down_ln",
    )(h, w_down, x, g, beta)


def _conv_taps(u, first_rows, cw, cb, t_in_seq):
    out = cb + cw[CONV_WIDTH - 1:CONV_WIDTH, :] * u
    for j in range(1, CONV_WIDTH):
        shifted = jnp.where(t_in_seq < j, first_rows(j), pltpu.roll(u, j, 0))
        out = out + cw[CONV_WIDTH - 1 - j:CONV_WIDTH - j, :] * shifted
    return out


def _lru_kernel(gate_ref, lx_ref, buf_ref, h0_ref, cw_ref, cb_ref, wa_ref, ba_ref, wx_ref,
                bx_ref, lam_ref, out_ref, h1_ref, a_scr, b_scr, hin_scr, *, nseq, seqlen):
    rows = nseq * seqlen
    u = lx_ref[...]
    t_in_seq = _iota2((rows, LANES), 0) % seqlen
    if nseq == 1:
        buf = jnp.concatenate([buf_ref[0], jnp.zeros((rows - SUBLANES, LANES), F32)], axis=0) \
            if rows > SUBLANES else buf_ref[0]
    else:
        buf = buf_ref[0]
    nbuf = CONV_WIDTH - 1
    xc = _conv_taps(u, lambda j: pltpu.roll(buf, (j - nbuf) % rows, 0), cw_ref[...], cb_ref[...],
                    t_in_seq)

    r = _sigmoid(_dot1(xc, wa_ref[0]) + ba_ref[...])
    i = _sigmoid(_dot1(xc, wx_ref[0]) + bx_ref[...])
    log_a = (-LRU_C) * r * _softplus(-lam_ref[...])
    a = jnp.exp(log_a)
    b = jnp.sqrt(jnp.tanh(-log_a) * (a * a + 1.0)) * (i * xc)

    t8 = _iota2((rows, LANES), 0) % SUBLANES
    for s in (1, 2, 4):
        m = t8 >= s
        a_sh = pltpu.roll(a, s, 0)
        b_sh = pltpu.roll(b, s, 0)
        b = jnp.where(m, a * b_sh + b, b)
        a = jnp.where(m, a * a_sh, a)

    if seqlen == SUBLANES:
        h0 = h0_ref[0]
        hin = jnp.broadcast_to(h0[:, None, :], (nseq, SUBLANES, LANES)).reshape(rows, LANES)
        h = a * hin + b
        out_ref[...] = (h * _gelu_tanh(gate_ref[...])).astype(out_ref.dtype)
        a_scr[...] = h
        h1_ref[0] = a_scr[pl.ds(SUBLANES - 1, nseq, stride=SUBLANES), :]
    else:
        assert nseq == 1
        a_scr[...] = a
        b_scr[...] = b

        def carry_step(g, carry):
            base = pl.multiple_of(g * SUBLANES, SUBLANES)
            hin_scr[pl.ds(base, SUBLANES), :] = jnp.broadcast_to(carry, (SUBLANES, LANES))
            a7 = a_scr[pl.ds(base + SUBLANES - 1, 1), :]
            b7 = b_scr[pl.ds(base + SUBLANES - 1, 1), :]
            return a7 * carry + b7

        last = lax.fori_loop(0, rows // SUBLANES, carry_step, h0_ref[0])
        h = a_scr[...] * hin_scr[...] + b_scr[...]
        out_ref[...] = (h * _gelu_tanh(gate_ref[...])).astype(out_ref.dtype)
        h1_ref[0] = last


def _lru(proj, buf, h0, cw, cb, wa, ba, wx, bx, lam, *, nblk, nseq, seqlen):
    rows = nseq * seqlen
    gate_blk = C_GATE // LRU_BLOCK
    lx_blk = C_LX // LRU_BLOCK
    row = lambda s, h: (0, h)
    return pl.pallas_call(
        functools.partial(_lru_kernel, nseq=nseq, seqlen=seqlen),
        grid=(nblk, LRU_HEADS),
        in_specs=[pl.BlockSpec((rows, LRU_BLOCK), lambda s, h: (s, gate_blk + h)),
                  pl.BlockSpec((rows, LRU_BLOCK), lambda s, h: (s, lx_blk + h)),
                  pl.BlockSpec((1, nseq * SUBLANES, LRU_BLOCK), lambda s, h: (s, 0, h)),
                  pl.BlockSpec((1, nseq, LRU_BLOCK), lambda s, h: (s, 0, h)),
                  pl.BlockSpec((CONV_WIDTH, LRU_BLOCK), row),
                  pl.BlockSpec((1, LRU_BLOCK), row),
                  pl.BlockSpec((1, LRU_BLOCK, LRU_BLOCK), lambda s, h: (h, 0, 0)),
                  pl.BlockSpec((1, LRU_BLOCK), row),
                  pl.BlockSpec((1, LRU_BLOCK, LRU_BLOCK), lambda s, h: (h, 0, 0)),
                  pl.BlockSpec((1, LRU_BLOCK), row),
                  pl.BlockSpec((1, LRU_BLOCK), row)],
        out_specs=[pl.BlockSpec((rows, LRU_BLOCK), lambda s, h: (s, h)),
                   pl.BlockSpec((1, nseq, LRU_BLOCK), lambda s, h: (s, 0, h))],
        out_shape=[jax.ShapeDtypeStruct((nblk * rows, LRU_WIDTH), BF16),
                   jax.ShapeDtypeStruct((nblk, nseq, LRU_WIDTH), F32)],
        scratch_shapes=[pltpu.VMEM((rows, LANES), F32)] * 3,
        compiler_params=_cparams("arbitrary", "arbitrary"),
        name="lru",
    )(proj, proj, buf, h0, cw, cb, wa, ba, wx, bx, lam)


def _stacked_state_call(kernel_fn, grid, in_specs, operands, row_spec, row_shape, state_dims,
                        l_out, prev_out, scratch_shapes, name):
    nseq = grid[0]
    zeros = (0,) * len(state_dims)
    state_shape = jax.ShapeDtypeStruct((DEPTH, nseq) + tuple(state_dims), F32)
    aliases = {}
    if prev_out is None:
        assert l_out == 0
        state_spec = pl.BlockSpec((DEPTH, 1) + tuple(state_dims), lambda b, c: (0, b) + zeros)
    else:
        state_spec = pl.BlockSpec((1, 1) + tuple(state_dims), lambda b, c: (l_out, b) + zeros)
        in_specs = in_specs + [pl.BlockSpec(memory_space=pl.ANY)]
        operands = operands + [prev_out]
        aliases = {len(operands) - 1: 1}
        kernel_fn = functools.partial(_drop_alias_ref, kernel_fn, len(operands) - 1)
    return pl.pallas_call(
        kernel_fn, grid=grid, in_specs=in_specs, out_specs=[row_spec, state_spec],
        out_shape=[row_shape, state_shape], scratch_shapes=scratch_shapes,
        input_output_aliases=aliases, compiler_params=_cparams("arbitrary", "arbitrary"),
        name=name)(*operands)


def _drop_alias_ref(kernel_fn, pos, *refs):
    return kernel_fn(*refs[:pos], *refs[pos + 1:])


def _zero_other_slabs(state_ref):
    if state_ref.shape[0] > 1:
        state_ref[1:] = jnp.zeros((state_ref.shape[0] - 1,) + state_ref.shape[1:], state_ref.dtype)


def _transpose_rows_to_lanes(x, t):
    if t < LANES:
        x = jnp.concatenate([x, jnp.zeros((LANES - t, LANES), F32)], axis=0)
    return x.T[:, :t]


def _ssd_kernel(z_ref, xbc_ref, dt_ref, buf_ref, s0_ref, cw_ref, cb_ref, dtb_ref, alog_ref,
                dch_ref, ng_ref, ehp_ref, out_ref, s1_ref, tail_ref, pad_ref, y_scr, *, T):
    c = pl.program_id(1)
    n_state = SSD_STATE
    gw = SSD_GROUP_WIDTH
    hpg = SSD_HEADS_PER_GROUP

    @pl.when(c == 0)
    def _():
        tail_ref[...] = buf_ref[0]
        s1_ref[0:1] = s0_ref[...]
        _zero_other_slabs(s1_ref)

    u = xbc_ref[...]
    pad_ref[0:SUBLANES, :] = tail_ref[...]
    pad_ref[SUBLANES:SUBLANES + T, :] = u
    cw = cw_ref[...]
    xc = cb_ref[...] + cw[CONV_WIDTH - 1:CONV_WIDTH, :] * u
    for j in range(1, CONV_WIDTH):
        xc = xc + cw[CONV_WIDTH - 1 - j:CONV_WIDTH - j, :] * pad_ref[SUBLANES - j:SUBLANES - j + T, :]
    tail_ref[...] = pad_ref[T:T + SUBLANES, :]

    xa = _silu(xc)
    xs = xa[:, :SSD_WIDTH]
    bm = xa[:, SSD_WIDTH:SSD_WIDTH + SSD_GROUPS * n_state]
    cm = xa[:, SSD_WIDTH + SSD_GROUPS * n_state:]
    dt = _softplus(dt_ref[...] + dtb_ref[...])
    da = dt * (-jnp.exp(alog_ref[...]))
    ii = _iota2((T, T), 0)
    jj = _iota2((T, T), 1)
    causal = ii >= jj
    cum = _dotx_left(causal.astype(BF16), da)
    cum_t = _transpose_rows_to_lanes(cum, T)
    ehp = ehp_ref[...]
    cumx = _dotx2(cum, ehp)
    dtx = _dotx2(dt, ehp)
    xdt = xs * dtx
    ecum = jnp.exp(cumx)
    xdtd = xdt * jnp.exp(cumx[T - 1:T, :] - cumx)

    groups = range(SSD_GROUPS)
    heads = range(SSD_HEADS)
    cg = [cm[:, g * n_state:(g + 1) * n_state] for g in groups]
    bg = [bm[:, g * n_state:(g + 1) * n_state] for g in groups]
    sg = [s1_ref[0, 0, g * hpg:(g + 1) * hpg].reshape(gw, n_state) for g in groups]
    cb = [_dot1(cg[g], bg[g], _NT) for g in groups]
    y_off = [_dot1(cg[g], sg[g], _NT) for g in groups]
    st = [_dot1(xdtd[:, g * gw:(g + 1) * gw], bg[g], _TN) for g in groups]
    lm = [jnp.where(causal, jnp.exp(cum[:, h:h + 1] - cum_t[h:h + 1, :]), 0.0) for h in heads]
    y_diag = [_dot1(cb[h // hpg] * lm[h], xdt[:, h * HEAD_DIM:(h + 1) * HEAD_DIM]) for h in heads]
    for g in groups:
        y_scr[:, g * gw:(g + 1) * gw] = (jnp.concatenate(y_diag[g * hpg:(g + 1) * hpg], axis=1)
                                         + y_off[g] * ecum[:, g * gw:(g + 1) * gw])
        decay = jnp.concatenate(
            [jnp.broadcast_to(jnp.exp(cum_t[h:h + 1, T - 1:T]), (HEAD_DIM, n_state))
             for h in range(g * hpg, (g + 1) * hpg)], axis=0)
        s1_ref[0, 0, g * hpg:(g + 1) * hpg] = (sg[g] * decay + st[g]).reshape(hpg, HEAD_DIM, n_state)

    y = (y_scr[...] + dch_ref[...] * xs) * _silu(z_ref[...])
    outs = []
    for g in groups:
        yg = y[:, g * gw:(g + 1) * gw]
        ms = jnp.mean(yg * yg, axis=-1, keepdims=True)
        outs.append(yg * lax.rsqrt(ms + 1e-5))
    out_ref[...] = (jnp.concatenate(outs, axis=1) * ng_ref[...]).astype(out_ref.dtype)


def _ssd(proj, buf, s0, l_in, l_out, prev_out, cw, cb, dtb, alog, dch, ng, ehp, *, nseq, seqlen):
    T = min(SSD_CHUNK, seqlen)
    nc = seqlen // T
    const = lambda b, c: (0, 0)
    sblk = (1, 1, SSD_HEADS, HEAD_DIM, SSD_STATE)
    in_specs = [pl.BlockSpec((T, SSD_WIDTH), lambda b, c: (b * nc + c, C_Z // SSD_WIDTH)),
                pl.BlockSpec((T, SSD_CONV_DIM), lambda b, c: (b * nc + c, C_XBC // SSD_CONV_DIM)),
                pl.BlockSpec((T, DT_PAD), lambda b, c: (b * nc + c, C_DT // DT_PAD)),
                pl.BlockSpec((1, SUBLANES, SSD_CONV_DIM), lambda b, c: (b, 0, 0)),
                pl.BlockSpec(sblk, lambda b, c: (l_in, b, 0, 0, 0)),
                pl.BlockSpec((CONV_WIDTH, SSD_CONV_DIM), const),
                pl.BlockSpec((1, SSD_CONV_DIM), const),
                pl.BlockSpec((1, DT_PAD), const),
                pl.BlockSpec((1, DT_PAD), const),
                pl.BlockSpec((1, SSD_WIDTH), const),
                pl.BlockSpec((1, SSD_WIDTH), const),
                pl.BlockSpec((DT_PAD, SSD_WIDTH), const)]
    return _stacked_state_call(
        functools.partial(_ssd_kernel, T=T), (nseq, nc), in_specs,
        [proj, proj, proj, buf, s0, cw, cb, dtb, alog, dch, ng, ehp],
        pl.BlockSpec((T, SSD_WIDTH), lambda b, c: (b * nc + c, 0)),
        jax.ShapeDtypeStruct((nseq * seqlen, SSD_WIDTH), BF16),
        sblk[2:], l_out, prev_out,
        [pltpu.VMEM((SUBLANES, SSD_CONV_DIM), F32),
         pltpu.VMEM((T + SUBLANES, SSD_CONV_DIM), F32),
         pltpu.VMEM((T, SSD_WIDTH), F32)],
        "ssd")


def _unit_lower_inverse(a_list, ii, jj, T):
    n = len(a_list)
    pair = (ii >> 1) == (jj >> 1)
    inv = [jnp.where(ii == jj, 1.0, 0.0) + jnp.where(pair, a_list[h], 0.0) for h in range(n)]
    shift = 1
    while (2 << shift) <= T:
        band = jnp.logical_and((ii >> (shift + 1)) == (jj >> (shift + 1)),
                               (ii >> shift) != (jj >> shift))
        x = [_dot1(jnp.where(band, a_list[h], 0.0), inv[h]) for h in range(n)]
        inv = [inv[h] + _dot1(inv[h], x[h]) for h in range(n)]
        shift += 1
    return inv


def _rwkv_kernel(r_ref, k_ref, v_ref, g_ref, wa_ref, sh_ref, s0_ref, mu_ref, w0_ref, wup_ref,
                 a0_ref, aup_ref, gup_ref, kkw_ref, kaw_ref, rkw_ref, lng_ref, lnb_ref, ones_ref,
                 out_ref, s1_ref, prev_scr, h_scr, al_scr, be_scr, kt_scr, rt_scr, bs_scr,
                 ks_scr, v_scr, gam_scr, o_scr, *, T, nchunks):
    c = pl.program_id(1)
    W = RWKV_WIDTH
    D = HEAD_DIM

    @pl.when(c == 0)
    def _():
        prev_scr[...] = sh_ref[0]
        for h in range(RWKV_HEADS):
            h_scr[h] = s0_ref[0, 0, h].T

    def token_shift(p, lo, hi):
        first = _iota2(p.shape, 0) == 0
        prev = jnp.where(first, prev_scr[:, lo:hi], pltpu.roll(p, 1, 0))
        prev_scr[:, lo:hi] = p[T - 1:T, :]
        return p + (prev - p) * mu_ref[:, lo:hi]

    xr = token_shift(r_ref[...], 0, W)
    xk = token_shift(k_ref[...], W, 2 * W)
    xv = token_shift(v_ref[...], 2 * W, 3 * W)
    xg = token_shift(g_ref[...], 3 * W, 3 * W + G_PAD)
    xwa = token_shift(wa_ref[...], 3 * W + G_PAD, RW_PACK)

    w_lin = w0_ref[...] + _dot1(jnp.tanh(xwa), wup_ref[...])
    a = _sigmoid(a0_ref[...] + _dot1(xwa, aup_ref[...]))
    gate = _dot1(_sigmoid(xg), gup_ref[...])
    lw = -jnp.exp(-_softplus(-w_lin) - 0.5)

    ones = ones_ref[...]

    def head_sum(x):
        tw = ones.shape[0]
        return jnp.concatenate(
            [_dotx(x[:, i * tw:(i + 1) * tw], ones) for i in range(W // tw)], axis=1)

    kk = xk * kkw_ref[...]
    kk = kk / jnp.maximum(jnp.sqrt(head_sum(kk * kk)), 1e-12)
    kp = xk * (1.0 + (a - 1.0) * kaw_ref[...])

    ii = _iota2((T, T), 0)
    jj = _iota2((T, T), 1)
    incl = ii >= jj
    strict = ii > jj
    eye_d = _iota2((D, D), 0) == _iota2((D, D), 1)

    cum = _dotx_left(incl.astype(BF16), lw)
    e_neg = jnp.exp(-cum)
    gam = jnp.exp(cum[T - 1:T, :])
    be = kk * a * e_neg
    kt = kp * e_neg
    al_scr[...] = -kk * jnp.exp(cum - lw)
    be_scr[...] = be
    kt_scr[...] = kt
    rt_scr[...] = xr * jnp.exp(cum)
    bs_scr[...] = be * gam
    ks_scr[...] = kt * gam
    v_scr[...] = xv
    gam_scr[...] = gam

    heads = range(RWKV_HEADS)

    def per_head(ref):
        return [ref[:, h * D:(h + 1) * D] for h in heads]

    al, be_h, kt_h, rt = per_head(al_scr), per_head(be_scr), per_head(kt_scr), per_head(rt_scr)
    bs_h, ks_h, vv, gam_h = per_head(bs_scr), per_head(ks_scr), per_head(v_scr), per_head(gam_scr)
    gram = [_dot1(jnp.concatenate([al[h], rt[h]], axis=0),
                  jnp.concatenate([be_h[h], kt_h[h]], axis=0), _NT) for h in heads]
    a_ab = [jnp.where(strict, gram[h][:T, :T], 0.0) for h in heads]
    a_ak = [jnp.where(strict, gram[h][:T, T:], 0.0) for h in heads]
    r_b = [jnp.where(incl, gram[h][T:, :T], 0.0) for h in heads]
    r_k = [jnp.where(incl, gram[h][T:, T:], 0.0) for h in heads]
    inv = _unit_lower_inverse(a_ab, ii, jj, T)
    akv = [_dot1(a_ak[h], vv[h]) for h in heads]
    pw = [_dot1(inv[h], jnp.concatenate([al[h], akv[h]], axis=1)) for h in heads]
    qo = [_dot1(r_b[h], pw[h]) for h in heads]
    rkv = [_dot1(r_k[h], vv[h]) for h in heads]
    gz = [_dot3(bs_h[h], pw[h], _TN) for h in heads]
    ksv = [_dot3(ks_h[h], vv[h], _TN) for h in heads]
    hmat = [h_scr[h] for h in heads]
    qh = [_dot1(rt[h] + qo[h][:, :D], hmat[h]) for h in heads]
    gmat = [jnp.where(eye_d, jnp.broadcast_to(gam_h[h], (D, D)), 0.0) + gz[h][:, :D] for h in heads]
    gh = [_dot3(gmat[h], hmat[h]) for h in heads]
    for h in heads:
        h_scr[h] = gh[h] + gz[h][:, D:] + ksv[h]
    for p in range(RWKV_HEADS * D // LANES):
        o_scr[:, p * LANES:(p + 1) * LANES] = jnp.concatenate(
            [qh[h] + qo[h][:, D:] + rkv[h] for h in range(2 * p, 2 * p + 2)], axis=1)

    o = o_scr[...]
    mean = head_sum(o) * (1.0 / D)
    d = o - mean
    var = head_sum(d * d) * (1.0 / D)
    on = d * lax.rsqrt(var + RWKV_GN_EPS) * lng_ref[...] + lnb_ref[...]
    bonus = head_sum(xr * kp * rkw_ref[...]) * xv
    out_ref[...] = ((on + bonus) * gate).astype(out_ref.dtype)

    @pl.when(c == nchunks - 1)
    def _():
        for h in range(RWKV_HEADS):
            s1_ref[0, 0, h] = h_scr[h].T
        _zero_other_slabs(s1_ref)


def _rwkv(proj, sh, s0, l_in, l_out, prev_out, mu, w0, wup, a0, aup, gup, kkw, kaw, rkw, lng, lnb,
          ones, *, nseq, seqlen):
    T = min(RWKV_CHUNK, seqlen)
    nc = seqlen // T
    W = RWKV_WIDTH
    const = lambda b, c: (0, 0)
    rowblk = lambda col, width: pl.BlockSpec((T, width), lambda b, c: (b * nc + c, col // width))
    sblk = (1, 1, RWKV_HEADS, HEAD_DIM, HEAD_DIM)
    vec = pl.BlockSpec((1, W), const)
    in_specs = [rowblk(C_R, W), rowblk(C_K, W), rowblk(C_V, W), rowblk(C_G, G_PAD),
                rowblk(C_WA, WA_PAD),
                pl.BlockSpec((1, 1, RW_PACK), lambda b, c: (b, 0, 0)),
                pl.BlockSpec(sblk, lambda b, c: (l_in, b, 0, 0, 0)),
                pl.BlockSpec((1, RW_PACK), const),
                vec, pl.BlockSpec((WA_PAD, W), const),
                vec, pl.BlockSpec((WA_PAD, W), const),
                pl.BlockSpec((G_PAD, W), const),
                vec, vec, vec, vec, vec,
                pl.BlockSpec((2 * LANES, 2 * LANES), const)]
    return _stacked_state_call(
        functools.partial(_rwkv_kernel, T=T, nchunks=nc), (nseq, nc), in_specs,
        [proj, proj, proj, proj, proj, sh, s0, mu, w0, wup, a0, aup, gup, kkw, kaw, rkw, lng, lnb,
         ones],
        pl.BlockSpec((T, W), lambda b, c: (b * nc + c, 0)),
        jax.ShapeDtypeStruct((nseq * seqlen, W), BF16),
        sblk[2:], l_out, prev_out,
        [pltpu.VMEM((1, RW_PACK), F32), pltpu.VMEM((RWKV_HEADS, HEAD_DIM, HEAD_DIM), F32)]
        + [pltpu.VMEM((T, W), F32)] * 7
        + [pltpu.VMEM((1, W), F32), pltpu.VMEM((T, W), F32)],
        "rwkv")


def _zeros_like_cols(x, n):
    return jnp.zeros(x.shape[:-1] + (n,), x.dtype)


def _pack_rwkv_cols(x):
    return jnp.concatenate([x[..., :_O_XW], x[..., _O_XG:], _zeros_like_cols(x, G_PAD - RWKV_R_G),
                            x[..., _O_XW:_O_XG]], axis=-1)


def _pack_tail_cols(w):
    return jnp.concatenate([_pack_rwkv_cols(w[..., _O_RW:]), w[..., _O_DT:_O_RW],
                            _zeros_like_cols(w, DT_PAD - SSD_HEADS)], axis=-1)


def _pad_rows(x, n_before, n_total):
    b, r, c = x.shape
    return jnp.concatenate([jnp.zeros((b, n_before, c), x.dtype), x,
                            jnp.zeros((b, n_total - n_before - r, c), x.dtype)], axis=1)


def _small_params(l, p):
    row = lambda v: v.reshape(1, -1)
    pad_lanes = lambda v, n: jnp.concatenate([v, jnp.zeros((n - v.shape[0],), v.dtype)]).reshape(1, n)
    zeros_w = jnp.zeros((RWKV_R_W, RWKV_WIDTH), F32)
    return dict(
        lru=(p['lru_conv_w'][l], row(p['lru_conv_b'][l]), p['lru_wa'][l].astype(BF16),
             row(p['lru_ba'][l]), p['lru_wx'][l].astype(BF16), row(p['lru_bx'][l]),
             row(p['lru_lambda'][l])),
        ssd=(p['ssd_conv_w'][l], row(p['ssd_conv_b'][l]), pad_lanes(p['ssd_dt_bias'][l], DT_PAD),
             pad_lanes(p['ssd_a_log'][l], DT_PAD), row(jnp.repeat(p['ssd_d'][l], HEAD_DIM)),
             row(p['ssd_norm_g'][l])),
        rwkv=(row(_pack_rwkv_cols(p['rwkv_mu'][l])), row(p['rwkv_w0'][l]),
              jnp.concatenate([p['rwkv_w_up'][l], zeros_w], axis=0).astype(BF16),
              row(p['rwkv_a0'][l]),
              jnp.concatenate([zeros_w, p['rwkv_a_up'][l]], axis=0).astype(BF16),
              jnp.concatenate([p['rwkv_g_up'][l],
                               jnp.zeros((G_PAD - RWKV_R_G, RWKV_WIDTH), F32)], axis=0).astype(BF16),
              row(p['rwkv_k_k'][l]), row(p['rwkv_k_a'][l]), row(p['rwkv_r_k'][l]),
              row(p['rwkv_ln_g'][l]), row(p['rwkv_ln_b'][l])),
        ln1=(row(p['ln1_g'][l]), row(p['ln1_b'][l])),
        ln2=(row(p['ln2_g'][l]), row(p['ln2_b'][l])),
    )


def _constants():
    lane = jnp.arange(SSD_WIDTH) // HEAD_DIM
    ehp = (jnp.arange(DT_PAD)[:, None] == lane[None, :]).astype(BF16)
    blk = jnp.arange(2 * LANES) // HEAD_DIM
    ones = (blk[:, None] == blk[None, :]).astype(BF16)
    return ehp, ones


def _layer(x, l, small_state, ssd_in, rw_in, prev_outs, sp, big, consts, *, nseq, seqlen, lru_nblk):
    lru_conv0, lru_h0, ssd_conv0, rw_shift0 = small_state
    ehp, ones = consts
    nbuf = CONV_WIDTH - 1
    proj = _proj(x, big['w_main'], big['w_tail'], l)

    lru_nseq = nseq // lru_nblk
    out_a, lru_h1 = _lru(
        proj, _pad_rows(lru_conv0, 0, SUBLANES).reshape(lru_nblk, lru_nseq * SUBLANES, LRU_WIDTH),
        lru_h0.reshape(lru_nblk, lru_nseq, LRU_WIDTH), *sp['lru'],
        nblk=lru_nblk, nseq=lru_nseq, seqlen=seqlen)
    out_b, ssd_out = _ssd(proj, _pad_rows(ssd_conv0, SUBLANES - nbuf, SUBLANES), ssd_in[0],
                          ssd_in[1], l, prev_outs[0], *sp['ssd'], ehp, nseq=nseq, seqlen=seqlen)
    out_c, rw_out = _rwkv(proj, _pack_rwkv_cols(rw_shift0)[:, None, :], rw_in[0], rw_in[1], l,
                          prev_outs[1], *sp['rwkv'], ones, nseq=nseq, seqlen=seqlen)

    y = _outproj_ln(out_a, out_b, out_c, big['w_out'], l, x, *sp['ln1'])
    y = _ffn_down_ln(_ffn_up(y, big['w_gate'], big['w_up'], l), big['w_down'], l, y, *sp['ln2'])

    p3 = proj.reshape(nseq, seqlen, N_PROJ)
    tail = p3[:, seqlen - nbuf:, :]
    last = p3[:, seqlen - 1, :]
    rw_shift1 = jnp.concatenate([last[:, C_R:C_R + 3 * RWKV_WIDTH], last[:, C_WA:C_WA + WA_PAD],
                                 last[:, C_G:C_G + RWKV_R_G]], axis=-1)
    small_new = (tail[:, :, C_LX:C_LX + LRU_WIDTH], lru_h1.reshape(nseq, LRU_WIDTH),
                 tail[:, :, C_XBC:C_XBC + SSD_CONV_DIM], rw_shift1)
    return y, small_new, (ssd_out, rw_out)


def kernel(x_prompt, x_sample, state_lru_conv, state_lru_h, state_ssd_conv, state_ssd,
           state_rwkv_shift, state_rwkv, w_in, lru_conv_w, lru_conv_b, lru_wa, lru_ba, lru_wx,
           lru_bx, lru_lambda, ssd_conv_w, ssd_conv_b, ssd_dt_bias, ssd_a_log, ssd_d, ssd_norm_g,
           rwkv_mu, rwkv_w0, rwkv_w_up, rwkv_a0, rwkv_a_up, rwkv_g_up, rwkv_k_k, rwkv_k_a,
           rwkv_r_k, rwkv_ln_g, rwkv_ln_b, w_out, ln1_g, ln1_b, w_gate, w_up, w_down, ln2_g, ln2_b):
    params = dict(
        lru_conv_w=lru_conv_w, lru_conv_b=lru_conv_b, lru_wa=lru_wa, lru_ba=lru_ba,
        lru_wx=lru_wx, lru_bx=lru_bx, lru_lambda=lru_lambda, ssd_conv_w=ssd_conv_w,
        ssd_conv_b=ssd_conv_b, ssd_dt_bias=ssd_dt_bias, ssd_a_log=ssd_a_log, ssd_d=ssd_d,
        ssd_norm_g=ssd_norm_g, rwkv_mu=rwkv_mu, rwkv_w0=rwkv_w0, rwkv_w_up=rwkv_w_up,
        rwkv_a0=rwkv_a0, rwkv_a_up=rwkv_a_up, rwkv_g_up=rwkv_g_up, rwkv_k_k=rwkv_k_k,
        rwkv_k_a=rwkv_k_a, rwkv_r_k=rwkv_r_k.reshape(DEPTH, RWKV_WIDTH), rwkv_ln_g=rwkv_ln_g,
        rwkv_ln_b=rwkv_ln_b, ln1_g=ln1_g, ln1_b=ln1_b, ln2_g=ln2_g, ln2_b=ln2_b)
    w_main = w_in.astype(BF16)
    big = dict(w_main=w_main, w_tail=_pack_tail_cols(w_main), w_out=w_out.astype(BF16),
               w_gate=w_gate.astype(BF16), w_up=w_up.astype(BF16), w_down=w_down.astype(BF16))
    bp, lp_len, _ = x_prompt.shape
    bs, ls_len, _ = x_sample.shape
    consts = _constants()
    nbuf = CONV_WIDTH - 1
    zero_small = (jnp.zeros((bp, nbuf, LRU_WIDTH), F32), jnp.zeros((bp, LRU_WIDTH), F32),
                  jnp.zeros((bp, nbuf, SSD_CONV_DIM), F32), jnp.zeros((bp, RWKV_SHIFT), F32))
    zero_ssd = jnp.zeros((1, bp, SSD_HEADS, HEAD_DIM, SSD_STATE), F32)
    zero_rw = jnp.zeros((1, bp, RWKV_HEADS, HEAD_DIM, HEAD_DIM), F32)
    yp = x_prompt.reshape(bp * lp_len, D_MODEL)
    ys = x_sample.reshape(bs * ls_len, D_MODEL)
    new_p = [[] for _ in range(4)]
    new_s = [[] for _ in range(4)]
    outs_p = (None, None)
    outs_s = (None, None)
    for l in range(DEPTH):
        sp = _small_params(l, params)
        yp, small_p, outs_p = _layer(yp, l, zero_small, (zero_ssd, 0), (zero_rw, 0), outs_p, sp,
                                     big, consts, nseq=bp, seqlen=lp_len, lru_nblk=bp)
        ys, small_s, outs_s = _layer(
            ys, l, (state_lru_conv[l], state_lru_h[l], state_ssd_conv[l], state_rwkv_shift[l]),
            (state_ssd, l), (state_rwkv, l), outs_s, sp, big, consts,
            nseq=bs, seqlen=ls_len, lru_nblk=1)
        for i in range(4):
            new_p[i].append(small_p[i])
            new_s[i].append(small_s[i])
    p_lru_conv, p_lru_h, p_ssd_conv, p_rw_shift = [jnp.stack(v) for v in new_p]
    s_lru_conv, s_lru_h, s_ssd_conv, s_rw_shift = [jnp.stack(v) for v in new_s]
    return (yp.reshape(bp, lp_len, D_MODEL), ys.reshape(bs, ls_len, D_MODEL),
            p_lru_conv, p_lru_h, p_ssd_conv, outs_p[0], p_rw_shift, outs_p[1],
            s_lru_conv, s_lru_h, s_ssd_conv, outs_s[0], s_rw_shift, outs_s[1])
```

```python
import functools
import math

import jax
import jax.numpy as jnp
from jax import lax
from jax.experimental import pallas as pl
from jax.experimental.pallas import tpu as pltpu

F32 = jnp.float32
BF16 = jnp.bfloat16

D_MODEL = 2048
DEPTH = 2
D_MIX = 2 * D_MODEL
HEAD_DIM = 64
CONV_WIDTH = 4
LRU_WIDTH = D_MIX // 4
LRU_HEADS = 8
LRU_BLOCK = LRU_WIDTH // LRU_HEADS
LRU_C = 8.0
SSD_WIDTH = D_MIX // 2
SSD_HEADS = SSD_WIDTH // HEAD_DIM
SSD_GROUPS = 8
SSD_STATE = 128
SSD_CHUNK = 128
SSD_CONV_DIM = SSD_WIDTH + 2 * SSD_GROUPS * SSD_STATE
SSD_GROUP_WIDTH = SSD_WIDTH // SSD_GROUPS
SSD_HEADS_PER_GROUP = SSD_HEADS // SSD_GROUPS
RWKV_WIDTH = D_MIX - LRU_WIDTH - SSD_WIDTH
RWKV_HEADS = RWKV_WIDTH // HEAD_DIM
RWKV_R_W = max(32, int(round(1.8 * RWKV_WIDTH ** 0.5 / 32)) * 32)
RWKV_R_A = max(32, int(round(1.8 * RWKV_WIDTH ** 0.5 / 32)) * 32)
RWKV_R_G = max(32, int(round(0.6 * RWKV_WIDTH ** 0.8 / 32)) * 32)
RWKV_SHIFT = 3 * RWKV_WIDTH + RWKV_R_W + RWKV_R_A + RWKV_R_G
RWKV_GN_EPS = 64e-5
RWKV_CHUNK = 64
RWKV_PROMPT_NB = 2
RWKV_SAMPLE_NB = 8
D_FF = -(-(8 * D_MODEL) // (3 * 256)) * 256
ALPHA = (2 * DEPTH) ** 0.25
LN_EPS = 1e-5

LANES = 128
SUBLANES = 8
VMEM_LIMIT = 56 * 1024 * 1024

C_GATE = 0
C_LX = C_GATE + LRU_WIDTH
C_Z = C_LX + LRU_WIDTH
C_XBC = C_Z + SSD_WIDTH
C_R = C_XBC + SSD_CONV_DIM
C_K = C_R + RWKV_WIDTH
C_V = C_K + RWKV_WIDTH
C_G = C_V + RWKV_WIDTH
G_PAD = 2 * LANES
C_WA = C_G + G_PAD
WA_PAD = LANES
C_DT = C_WA + WA_PAD
DT_PAD = LANES
N_PROJ = C_DT + DT_PAD
assert RWKV_R_G <= G_PAD and RWKV_R_W + RWKV_R_A == WA_PAD and SSD_HEADS <= DT_PAD
RW_PACK = 3 * RWKV_WIDTH + G_PAD + WA_PAD

_O_DT = 2 * LRU_WIDTH + SSD_WIDTH + SSD_CONV_DIM
_O_RW = _O_DT + SSD_HEADS
_O_XW = 3 * RWKV_WIDTH
_O_XG = _O_XW + RWKV_R_W + RWKV_R_A


def _cparams(*sem):
    return pltpu.CompilerParams(dimension_semantics=sem, vmem_limit_bytes=VMEM_LIMIT)


_NN = (((1,), (0,)), ((), ()))
_NT = (((1,), (1,)), ((), ()))
_TN = (((0,), (0,)), ((), ()))


def _dg(a, b, dims):
    return lax.dot_general(a, b, dims, preferred_element_type=F32)


def _dot1(a, b, dims=_NN):
    return _dg(a.astype(BF16), b.astype(BF16), dims)


def _split2(x):
    hi = x.astype(BF16)
    lo = (x - hi.astype(F32)).astype(BF16)
    return hi, lo


def _split3(x):
    hi = x.astype(BF16)
    r1 = x - hi.astype(F32)
    mid = r1.astype(BF16)
    lo = (r1 - mid.astype(F32)).astype(BF16)
    return hi, mid, lo


def _dot3(a, b, dims=_NN):
    ah, al = _split2(a)
    bh, bl = _split2(b)
    free_axis = 1 if dims == _TN else 0
    m = a.shape[free_axis]
    both = _dg(jnp.concatenate([ah, al], axis=free_axis), bh, dims)
    return (both[:m] + both[m:]) + _dg(ah, bl, dims)


def _dot_shared_rhs(parts, w_exact, npieces):
    pieces = []
    for p in parts:
        rest = p
        for _ in range(npieces):
            piece = rest.astype(BF16).astype(F32)
            pieces.append(piece)
            rest = rest - piece
    prod = _dg(jnp.concatenate(pieces, axis=0).astype(BF16), w_exact, _NN)
    outs = []
    off = 0
    for p in parts:
        r = p.shape[0]
        acc = prod[off:off + r]
        for i in range(1, npieces):
            acc = acc + prod[off + i * r:off + (i + 1) * r]
        outs.append(acc)
        off += npieces * r
    return outs


def _dotx_left(w_exact, a):
    hi, mid, lo = _split3(a)
    return _dg(w_exact, hi, _NN) + (_dg(w_exact, mid, _NN) + _dg(w_exact, lo, _NN))


def _iota2(shape, dim):
    return lax.broadcasted_iota(jnp.int32, shape, dim)


def _softplus(x):
    return jnp.maximum(x, 0.0) + jnp.log1p(jnp.exp(-jnp.abs(x)))


def _sigmoid(x):
    return 1.0 / (1.0 + jnp.exp(-x))


def _silu(x):
    return x * _sigmoid(x)


def _gelu_tanh(x):
    c = math.sqrt(2.0 / math.pi)
    return 0.5 * x * (1.0 + jnp.tanh(c * (x + 0.044715 * (x * x * x))))


def _layer_norm(y, g, b):
    mu = jnp.mean(y, axis=-1, keepdims=True)
    d = y - mu
    var = jnp.mean(d * d, axis=-1, keepdims=True)
    return d * lax.rsqrt(var + LN_EPS) * g + b


PROJ_TN = 512
N_MAIN = C_R
N_TAIL = N_PROJ - N_MAIN
assert N_MAIN == _O_DT and N_MAIN % PROJ_TN == 0 and N_TAIL % PROJ_TN == 0


def _proj_kernel(x_ref, wm_ref, wt_ref, o_ref, xb_ref):
    j = pl.program_id(1)

    @pl.when(j == 0)
    def _():
        xb_ref[...] = x_ref[...].astype(BF16)

    @pl.when(j < N_MAIN // PROJ_TN)
    def _():
        o_ref[...] = jnp.dot(xb_ref[...], wm_ref[0], preferred_element_type=F32)

    @pl.when(j >= N_MAIN // PROJ_TN)
    def _():
        o_ref[...] = jnp.dot(xb_ref[...], wt_ref[0], preferred_element_type=F32)


def _proj(x, w_main, w_tail, l):
    m = x.shape[0]
    tm = min(m, 1024)
    tn = PROJ_TN
    nmain = N_MAIN // tn
    return pl.pallas_call(
        _proj_kernel,
        grid=(m // tm, N_PROJ // tn),
        in_specs=[pl.BlockSpec((tm, D_MODEL), lambda i, j: (i, 0)),
                  pl.BlockSpec((1, D_MODEL, tn), lambda i, j: (l, 0, jnp.minimum(j, nmain - 1))),
                  pl.BlockSpec((1, D_MODEL, tn), lambda i, j: (l, 0, jnp.maximum(j - nmain, 0)))],
        out_specs=pl.BlockSpec((tm, tn), lambda i, j: (i, j)),
        out_shape=jax.ShapeDtypeStruct((m, N_PROJ), F32),
        scratch_shapes=[pltpu.VMEM((tm, D_MODEL), BF16)],
        compiler_params=_cparams("arbitrary", "arbitrary"),
        name="proj",
    )(x, w_main, w_tail)


OUT_KSTEPS = D_MIX // LRU_WIDTH


def _outproj_kernel(a_ref, b_ref, c_ref, w_ref, x_ref, g_ref, beta_ref, o_ref, acc_ref):
    k = pl.program_id(1)

    @pl.when(k == 0)
    def _():
        acc_ref[...] = jnp.dot(a_ref[...], w_ref[0], preferred_element_type=F32)

    @pl.when(jnp.logical_and(k > 0, k < OUT_KSTEPS - 1))
    def _():
        acc_ref[...] += jnp.dot(b_ref[...], w_ref[0], preferred_element_type=F32)

    @pl.when(k == OUT_KSTEPS - 1)
    def _():
        mix = acc_ref[...] + jnp.dot(c_ref[...], w_ref[0], preferred_element_type=F32)
        o_ref[...] = _layer_norm(ALPHA * x_ref[...] + mix, g_ref[...], beta_ref[...])


def _outproj_ln(out_a, out_b, out_c, w_out, l, x, g, beta):
    m = x.shape[0]
    tm = min(m, 512)
    tk = LRU_WIDTH
    nb = SSD_WIDTH // tk
    return pl.pallas_call(
        _outproj_kernel,
        grid=(m // tm, OUT_KSTEPS),
        in_specs=[pl.BlockSpec((tm, tk), lambda i, k: (i, 0)),
                  pl.BlockSpec((tm, tk), lambda i, k: (i, jnp.clip(k - 1, 0, nb - 1))),
                  pl.BlockSpec((tm, tk), lambda i, k: (i, 0)),
                  pl.BlockSpec((1, tk, D_MODEL), lambda i, k: (l, k, 0)),
                  pl.BlockSpec((tm, D_MODEL), lambda i, k: (i, 0)),
                  pl.BlockSpec((1, D_MODEL), lambda i, k: (0, 0)),
                  pl.BlockSpec((1, D_MODEL), lambda i, k: (0, 0))],
        out_specs=pl.BlockSpec((tm, D_MODEL), lambda i, k: (i, 0)),
        out_shape=jax.ShapeDtypeStruct((m, D_MODEL), F32),
        scratch_shapes=[pltpu.VMEM((tm, D_MODEL), F32)],
        compiler_params=_cparams("arbitrary", "arbitrary"),
        name="outproj_ln",
    )(out_a, out_b, out_c, w_out, x, g, beta)


def _ffn_up_kernel(x_ref, wg_ref, wu_ref, o_ref, xb_ref):
    @pl.when(pl.program_id(1) == 0)
    def _():
        xb_ref[...] = x_ref[...].astype(BF16)

    xb = xb_ref[...]
    gate = jnp.dot(xb, wg_ref[0], preferred_element_type=F32)
    up = jnp.dot(xb, wu_ref[0], preferred_element_type=F32)
    o_ref[...] = (_silu(gate) * up).astype(BF16)


def _ffn_up(x, wg, wu, l):
    m = x.shape[0]
    tm = min(m, 1024)
    tn = 512
    return pl.pallas_call(
        _ffn_up_kernel,
        grid=(m // tm, D_FF // tn),
        in_specs=[pl.BlockSpec((tm, D_MODEL), lambda i, j: (i, 0)),
                  pl.BlockSpec((1, D_MODEL, tn), lambda i, j: (l, 0, j)),
                  pl.BlockSpec((1, D_MODEL, tn), lambda i, j: (l, 0, j))],
        out_specs=pl.BlockSpec((tm, tn), lambda i, j: (i, j)),
        out_shape=jax.ShapeDtypeStruct((m, D_FF), BF16),
        scratch_shapes=[pltpu.VMEM((tm, D_MODEL), BF16)],
        compiler_params=_cparams("arbitrary", "arbitrary"),
        name="ffn_up",
    )(x, wg, wu)


FFN_DOWN_KSTEPS = 4


def _ffn_down_kernel(h_ref, w_ref, x_ref, g_ref, beta_ref, o_ref, acc_ref):
    k = pl.program_id(1)

    @pl.when(k == 0)
    def _():
        acc_ref[...] = jnp.dot(h_ref[...], w_ref[0], preferred_element_type=F32)

    @pl.when(jnp.logical_and(k > 0, k < FFN_DOWN_KSTEPS - 1))
    def _():
        acc_ref[...] += jnp.dot(h_ref[...], w_ref[0], preferred_element_type=F32)

    @pl.when(k == FFN_DOWN_KSTEPS - 1)
    def _():
        ffn = acc_ref[...] + jnp.dot(h_ref[...], w_ref[0], preferred_element_type=F32)
        o_ref[...] = _layer_norm(ALPHA * x_ref[...] + ffn, g_ref[...], beta_ref[...])


def _ffn_down_ln(h, w_down, l, x, g, beta):
    m = x.shape[0]
    tm = min(m, 512)
    tk = D_FF // FFN_DOWN_KSTEPS
    assert tk * FFN_DOWN_KSTEPS == D_FF and tk % LANES == 0
    return pl.pallas_call(
        _ffn_down_kernel,
        grid=(m // tm, FFN_DOWN_KSTEPS),
        in_specs=[pl.BlockSpec((tm, tk), lambda i, k: (i, k)),
                  pl.BlockSpec((1, tk, D_MODEL), lambda i, k: (l, k, 0)),
                  pl.BlockSpec((tm, D_MODEL), lambda i, k: (i, 0)),
                  pl.BlockSpec((1, D_MODEL), lambda i, k: (0, 0)),
                  pl.BlockSpec((1, D_MODEL), lambda i, k: (0, 0))],
        out_specs=pl.BlockSpec((tm, D_MODEL), lambda i, k: (i, 0)),
        out_shape=jax.ShapeDtypeStruct((m, D_MODEL), F32),
        scratch_shapes=[pltpu.VMEM((tm, D_MODEL), F32)],
        compiler_params=_cparams("arbitrary", "arbitrary"),
        name="ffn_down_ln",
    )(h, w_down, x, g, beta)


def _conv_taps(u, first_rows, cw, cb, t_in_seq):
    out = cb + cw[CONV_WIDTH - 1:CONV_WIDTH, :] * u
    for j in range(1, CONV_WIDTH):
        shifted = jnp.where(t_in_seq < j, first_rows(j), pltpu.roll(u, j, 0))
        out = out + cw[CONV_WIDTH - 1 - j:CONV_WIDTH - j, :] * shifted
    return out


def _lru_kernel(gate_ref, lx_ref, buf_ref, h0_ref, cw_ref, cb_ref, wa_ref, ba_ref, wx_ref,
                bx_ref, lam_ref, out_ref, h1_ref, a_scr, b_scr, hin_scr, *, nseq, seqlen):
    rows = nseq * seqlen
    u = lx_ref[...]
    t_in_seq = _iota2((rows, LANES), 0) % seqlen
    if nseq == 1:
        buf = jnp.concatenate([buf_ref[0], jnp.zeros((rows - SUBLANES, LANES), F32)], axis=0) \
            if rows > SUBLANES else buf_ref[0]
    else:
        buf = buf_ref[0]
    nbuf = CONV_WIDTH - 1
    xc = _conv_taps(u, lambda j: pltpu.roll(buf, (j - nbuf) % rows, 0), cw_ref[...], cb_ref[...],
                    t_in_seq)

    r = _sigmoid(_dot1(xc, wa_ref[0]) + ba_ref[...])
    i = _sigmoid(_dot1(xc, wx_ref[0]) + bx_ref[...])
    log_a = (-LRU_C) * r * _softplus(-lam_ref[...])
    a = jnp.exp(log_a)
    b = jnp.sqrt(jnp.tanh(-log_a) * (a * a + 1.0)) * (i * xc)

    t8 = _iota2((rows, LANES), 0) % SUBLANES
    for s in (1, 2, 4):
        m = t8 >= s
        a_sh = pltpu.roll(a, s, 0)
        b_sh = pltpu.roll(b, s, 0)
        b = jnp.where(m, a * b_sh + b, b)
        a = jnp.where(m, a * a_sh, a)

    if seqlen == SUBLANES:
        h0 = h0_ref[0]
        hin = jnp.broadcast_to(h0[:, None, :], (nseq, SUBLANES, LANES)).reshape(rows, LANES)
        h = a * hin + b
        out_ref[...] = (h * _gelu_tanh(gate_ref[...])).astype(out_ref.dtype)
        a_scr[...] = h
        h1_ref[0] = a_scr[pl.ds(SUBLANES - 1, nseq, stride=SUBLANES), :]
    else:
        assert nseq == 1
        a_scr[...] = a
        b_scr[...] = b

        def carry_step(g, carry):
            base = pl.multiple_of(g * SUBLANES, SUBLANES)
            hin_scr[pl.ds(base, SUBLANES), :] = jnp.broadcast_to(carry, (SUBLANES, LANES))
            a7 = a_scr[pl.ds(base + SUBLANES - 1, 1), :]
            b7 = b_scr[pl.ds(base + SUBLANES - 1, 1), :]
            return a7 * carry + b7

        last = lax.fori_loop(0, rows // SUBLANES, carry_step, h0_ref[0])
        h = a_scr[...] * hin_scr[...] + b_scr[...]
        out_ref[...] = (h * _gelu_tanh(gate_ref[...])).astype(out_ref.dtype)
        h1_ref[0] = last


def _lru(proj, buf, h0, cw, cb, wa, ba, wx, bx, lam, *, nblk, nseq, seqlen):
    rows = nseq * seqlen
    gate_blk = C_GATE // LRU_BLOCK
    lx_blk = C_LX // LRU_BLOCK
    row = lambda s, h: (0, h)
    return pl.pallas_call(
        functools.partial(_lru_kernel, nseq=nseq, seqlen=seqlen),
        grid=(nblk, LRU_HEADS),
        in_specs=[pl.BlockSpec((rows, LRU_BLOCK), lambda s, h: (s, gate_blk + h)),
                  pl.BlockSpec((rows, LRU_BLOCK), lambda s, h: (s, lx_blk + h)),
                  pl.BlockSpec((1, nseq * SUBLANES, LRU_BLOCK), lambda s, h: (s, 0, h)),
                  pl.BlockSpec((1, nseq, LRU_BLOCK), lambda s, h: (s, 0, h)),
                  pl.BlockSpec((CONV_WIDTH, LRU_BLOCK), row),
                  pl.BlockSpec((1, LRU_BLOCK), row),
                  pl.BlockSpec((1, LRU_BLOCK, LRU_BLOCK), lambda s, h: (h, 0, 0)),
                  pl.BlockSpec((1, LRU_BLOCK), row),
                  pl.BlockSpec((1, LRU_BLOCK, LRU_BLOCK), lambda s, h: (h, 0, 0)),
                  pl.BlockSpec((1, LRU_BLOCK), row),
                  pl.BlockSpec((1, LRU_BLOCK), row)],
        out_specs=[pl.BlockSpec((rows, LRU_BLOCK), lambda s, h: (s, h)),
                   pl.BlockSpec((1, nseq, LRU_BLOCK), lambda s, h: (s, 0, h))],
        out_shape=[jax.ShapeDtypeStruct((nblk * rows, LRU_WIDTH), BF16),
                   jax.ShapeDtypeStruct((nblk, nseq, LRU_WIDTH), F32)],
        scratch_shapes=[pltpu.VMEM((rows, LANES), F32)] * 3,
        compiler_params=_cparams("arbitrary", "arbitrary"),
        name="lru",
    )(proj, proj, buf, h0, cw, cb, wa, ba, wx, bx, lam)


def _stacked_state_call(kernel_fn, grid, in_specs, operands, row_spec, row_shape, state_dims,
                        l_out, prev_out, scratch_shapes, name, nb=1):
    nseq = grid[0] * nb
    zeros = (0,) * len(state_dims)
    state_shape = jax.ShapeDtypeStruct((DEPTH, nseq) + tuple(state_dims), F32)
    aliases = {}
    if prev_out is None:
        assert l_out == 0
        state_spec = pl.BlockSpec((DEPTH, nb) + tuple(state_dims), lambda b, c: (0, b) + zeros)
    else:
        state_spec = pl.BlockSpec((1, nb) + tuple(state_dims), lambda b, c: (l_out, b) + zeros)
        in_specs = in_specs + [pl.BlockSpec(memory_space=pl.ANY)]
        operands = operands + [prev_out]
        aliases = {len(operands) - 1: 1}
        kernel_fn = functools.partial(_drop_alias_ref, kernel_fn, len(operands) - 1)
    return pl.pallas_call(
        kernel_fn, grid=grid, in_specs=in_specs, out_specs=[row_spec, state_spec],
        out_shape=[row_shape, state_shape], scratch_shapes=scratch_shapes,
        input_output_aliases=aliases, compiler_params=_cparams("arbitrary", "arbitrary"),
        name=name)(*operands)


def _drop_alias_ref(kernel_fn, pos, *refs):
    return kernel_fn(*refs[:pos], *refs[pos + 1:])


def _zero_other_slabs(state_ref):
    if state_ref.shape[0] > 1:
        state_ref[1:] = jnp.zeros((state_ref.shape[0] - 1,) + state_ref.shape[1:], state_ref.dtype)


def _transpose_rows_to_lanes(x, t):
    if t < LANES:
        x = jnp.concatenate([x, jnp.zeros((LANES - t, LANES), F32)], axis=0)
    return x.T[:, :t]


def _ssd_kernel(z_ref, xbc_ref, dt_ref, buf_ref, s0_ref, cw_ref, cb_ref, dtb_ref, alog_ref,
                dch_ref, ng_ref, ehp_ref, out_ref, s1_ref, tail_ref, pad_ref, y_scr, *, T):
    c = pl.program_id(1)
    n_state = SSD_STATE
    gw = SSD_GROUP_WIDTH
    hpg = SSD_HEADS_PER_GROUP

    @pl.when(c == 0)
    def _():
        tail_ref[...] = buf_ref[0]
        s1_ref[0:1] = s0_ref[...]
        _zero_other_slabs(s1_ref)

    u = xbc_ref[...]
    pad_ref[0:SUBLANES, :] = tail_ref[...]
    pad_ref[SUBLANES:SUBLANES + T, :] = u
    cw = cw_ref[...]
    xc = cb_ref[...] + cw[CONV_WIDTH - 1:CONV_WIDTH, :] * u
    for j in range(1, CONV_WIDTH):
        xc = xc + cw[CONV_WIDTH - 1 - j:CONV_WIDTH - j, :] * pad_ref[SUBLANES - j:SUBLANES - j + T, :]
    tail_ref[...] = pad_ref[T:T + SUBLANES, :]

    xa = _silu(xc)
    xs = xa[:, :SSD_WIDTH]
    bm = xa[:, SSD_WIDTH:SSD_WIDTH + SSD_GROUPS * n_state]
    cm = xa[:, SSD_WIDTH + SSD_GROUPS * n_state:]
    dt = _softplus(dt_ref[...] + dtb_ref[...])
    da = dt * (-jnp.exp(alog_ref[...]))
    ii = _iota2((T, T), 0)
    jj = _iota2((T, T), 1)
    causal = ii >= jj
    cum = _dotx_left(causal.astype(BF16), da)
    cum_t = _transpose_rows_to_lanes(cum, T)
    ehp = ehp_ref[...]
    cumx, dtx = _dot_shared_rhs([cum, dt], ehp, 2)
    xdt = xs * dtx
    ecum = jnp.exp(cumx)
    xdtd = xdt * jnp.exp(cumx[T - 1:T, :] - cumx)

    groups = range(SSD_GROUPS)
    heads = range(SSD_HEADS)
    cg = [cm[:, g * n_state:(g + 1) * n_state] for g in groups]
    bg = [bm[:, g * n_state:(g + 1) * n_state] for g in groups]
    sg = [s1_ref[0, 0, g * hpg:(g + 1) * hpg].reshape(gw, n_state) for g in groups]
    cb = [_dot1(cg[g], bg[g], _NT) for g in groups]
    y_off = [_dot1(cg[g], sg[g], _NT) for g in groups]
    st = [_dot1(xdtd[:, g * gw:(g + 1) * gw], bg[g], _TN) for g in groups]
    lm = [jnp.where(causal, jnp.exp(cum[:, h:h + 1] - cum_t[h:h + 1, :]), 0.0) for h in heads]
    y_diag = [_dot1(cb[h // hpg] * lm[h], xdt[:, h * HEAD_DIM:(h + 1) * HEAD_DIM]) for h in heads]
    for g in groups:
        y_scr[:, g * gw:(g + 1) * gw] = (jnp.concatenate(y_diag[g * hpg:(g + 1) * hpg], axis=1)
                                         + y_off[g] * ecum[:, g * gw:(g + 1) * gw])
        decay = jnp.concatenate(
            [jnp.broadcast_to(jnp.exp(cum_t[h:h + 1, T - 1:T]), (HEAD_DIM, n_state))
             for h in range(g * hpg, (g + 1) * hpg)], axis=0)
        s1_ref[0, 0, g * hpg:(g + 1) * hpg] = (sg[g] * decay + st[g]).reshape(hpg, HEAD_DIM, n_state)

    y = (y_scr[...] + dch_ref[...] * xs) * _silu(z_ref[...])
    outs = []
    for g in groups:
        yg = y[:, g * gw:(g + 1) * gw]
        ms = jnp.mean(yg * yg, axis=-1, keepdims=True)
        outs.append(yg * lax.rsqrt(ms + 1e-5))
    out_ref[...] = (jnp.concatenate(outs, axis=1) * ng_ref[...]).astype(out_ref.dtype)


def _ssd(proj, buf, s0, l_in, l_out, prev_out, cw, cb, dtb, alog, dch, ng, ehp, *, nseq, seqlen):
    T = min(SSD_CHUNK, seqlen)
    nc = seqlen // T
    const = lambda b, c: (0, 0)
    sblk = (1, 1, SSD_HEADS, HEAD_DIM, SSD_STATE)
    in_specs = [pl.BlockSpec((T, SSD_WIDTH), lambda b, c: (b * nc + c, C_Z // SSD_WIDTH)),
                pl.BlockSpec((T, SSD_CONV_DIM), lambda b, c: (b * nc + c, C_XBC // SSD_CONV_DIM)),
                pl.BlockSpec((T, DT_PAD), lambda b, c: (b * nc + c, C_DT // DT_PAD)),
                pl.BlockSpec((1, SUBLANES, SSD_CONV_DIM), lambda b, c: (b, 0, 0)),
                pl.BlockSpec(sblk, lambda b, c: (l_in, b, 0, 0, 0)),
                pl.BlockSpec((CONV_WIDTH, SSD_CONV_DIM), const),
                pl.BlockSpec((1, SSD_CONV_DIM), const),
                pl.BlockSpec((1, DT_PAD), const),
                pl.BlockSpec((1, DT_PAD), const),
                pl.BlockSpec((1, SSD_WIDTH), const),
                pl.BlockSpec((1, SSD_WIDTH), const),
                pl.BlockSpec((DT_PAD, SSD_WIDTH), const)]
    return _stacked_state_call(
        functools.partial(_ssd_kernel, T=T), (nseq, nc), in_specs,
        [proj, proj, proj, buf, s0, cw, cb, dtb, alog, dch, ng, ehp],
        pl.BlockSpec((T, SSD_WIDTH), lambda b, c: (b * nc + c, 0)),
        jax.ShapeDtypeStruct((nseq * seqlen, SSD_WIDTH), BF16),
        sblk[2:], l_out, prev_out,
        [pltpu.VMEM((SUBLANES, SSD_CONV_DIM), F32),
         pltpu.VMEM((T + SUBLANES, SSD_CONV_DIM), F32),
         pltpu.VMEM((T, SSD_WIDTH), F32)],
        "ssd")


def _unit_lower_inverse(a_list, ii, jj, T):
    n = len(a_list)
    pair = (ii >> 1) == (jj >> 1)
    inv = [jnp.where(ii == jj, 1.0, 0.0) + jnp.where(pair, a_list[h], 0.0) for h in range(n)]
    shift = 1
    while (2 << shift) <= T:
        band = jnp.logical_and((ii >> (shift + 1)) == (jj >> (shift + 1)),
                               (ii >> shift) != (jj >> shift))
        x = [_dot1(jnp.where(band, a_list[h], 0.0), inv[h]) for h in range(n)]
        inv = [inv[h] + _dot1(inv[h], x[h]) for h in range(n)]
        shift += 1
    return inv


def _rwkv_kernel(r_ref, k_ref, v_ref, g_ref, wa_ref, sh_ref, s0_ref, mu_ref, w0_ref, wup_ref,
                 a0_ref, aup_ref, gup_ref, kkw_ref, kaw_ref, rkw_ref, lng_ref, lnb_ref, ones_ref,
                 out_ref, s1_ref, prev_scr, s_scr, al_scr, be_scr, kt_scr, rt_scr, bs_scr,
                 ks_scr, v_scr, gam_scr, o_scr, *, nb, T, nchunks):
    c = pl.program_id(1)
    W = RWKV_WIDTH
    D = HEAD_DIM
    R = nb * T
    nh = RWKV_HEADS

    @pl.when(c == 0)
    def _():
        prev_scr[...] = sh_ref[...].reshape(nb, RW_PACK)
        s_scr[...] = s0_ref[0].reshape(nb * nh, D, D)

    def rows_of(ref3):
        return ref3[...].reshape(R, ref3.shape[-1])

    def per_seq_rows(x):
        return jnp.broadcast_to(x[:, None, :], (nb, T, x.shape[-1])).reshape(R, x.shape[-1])

    def last_rows(x):
        return jnp.concatenate([x[(b + 1) * T - 1:(b + 1) * T, :] for b in range(nb)], axis=0)

    def token_shift(p, lo, hi):
        first = _iota2(p.shape, 0) % T == 0
        prev = jnp.where(first, per_seq_rows(prev_scr[:, lo:hi]), pltpu.roll(p, 1, 0))
        prev_scr[:, lo:hi] = last_rows(p)
        return p + (prev - p) * mu_ref[:, lo:hi]

    xr = token_shift(rows_of(r_ref), 0, W)
    xk = token_shift(rows_of(k_ref), W, 2 * W)
    xv = token_shift(rows_of(v_ref), 2 * W, 3 * W)
    xg = token_shift(rows_of(g_ref), 3 * W, 3 * W + G_PAD)
    xwa = token_shift(rows_of(wa_ref), 3 * W + G_PAD, RW_PACK)

    w_lin = w0_ref[...] + _dot1(jnp.tanh(xwa), wup_ref[...])
    a = _sigmoid(a0_ref[...] + _dot1(xwa, aup_ref[...]))
    gate = _dot1(_sigmoid(xg), gup_ref[...])
    lw = -jnp.exp(-_softplus(-w_lin) - 0.5)

    ones = ones_ref[...]

    def head_sums(xs):
        tw = ones.shape[0]
        nt = W // tw
        tiles = [x[:, i * tw:(i + 1) * tw] for x in xs for i in range(nt)]
        sums = _dot_shared_rhs(tiles, ones, 3)
        return [jnp.concatenate(sums[n * nt:(n + 1) * nt], axis=1) for n in range(len(xs))]

    kk = xk * kkw_ref[...]
    kp = xk * (1.0 + (a - 1.0) * kaw_ref[...])
    kk_sq, rk_sum = head_sums([kk * kk, xr * kp * rkw_ref[...]])
    kk = kk / jnp.maximum(jnp.sqrt(kk_sq), 1e-12)

    ri = _iota2((R, R), 0)
    rj = _iota2((R, R), 1)
    same_seq_causal = jnp.logical_and(ri >= rj, ri // T == rj // T)
    cum = _dotx_left(same_seq_causal.astype(BF16), lw)
    e_neg = jnp.exp(-cum)
    gam = jnp.exp(per_seq_rows(last_rows(cum)))
    be = kk * a * e_neg
    kt = kp * e_neg
    al_scr[...] = -kk * jnp.exp(cum - lw)
    be_scr[...] = be
    kt_scr[...] = kt
    rt_scr[...] = xr * jnp.exp(cum)
    bs_scr[...] = be * gam
    ks_scr[...] = kt * gam
    v_scr[...] = xv
    gam_scr[...] = gam

    ii = _iota2((T, T), 0)
    jj = _iota2((T, T), 1)
    incl = ii >= jj
    strict = ii > jj
    eye_d = _iota2((D, D), 0) == _iota2((D, D), 1)

    units = [(b, h) for b in range(nb) for h in range(nh)]
    nu = range(len(units))

    def per_unit(ref):
        return [ref[b * T:(b + 1) * T, h * D:(h + 1) * D] for b, h in units]

    al, be_u, kt_u, rt = per_unit(al_scr), per_unit(be_scr), per_unit(kt_scr), per_unit(rt_scr)
    bs_u, ks_u, vv, gam_u = per_unit(bs_scr), per_unit(ks_scr), per_unit(v_scr), per_unit(gam_scr)
    gram = [_dot1(jnp.concatenate([al[u], rt[u]], axis=0),
                  jnp.concatenate([be_u[u], kt_u[u]], axis=0), _NT) for u in nu]
    a_ab = [jnp.where(strict, gram[u][:T, :T], 0.0) for u in nu]
    a_ak = [jnp.where(strict, gram[u][:T, T:], 0.0) for u in nu]
    r_b = [jnp.where(incl, gram[u][T:, :T], 0.0) for u in nu]
    r_k = [jnp.where(incl, gram[u][T:, T:], 0.0) for u in nu]
    inv = _unit_lower_inverse(a_ab, ii, jj, T)
    akv = [_dot1(a_ak[u], vv[u]) for u in nu]
    pw = [_dot1(inv[u], jnp.concatenate([al[u], akv[u]], axis=1)) for u in nu]
    qo = [_dot1(r_b[u], pw[u]) for u in nu]
    rkv = [_dot1(r_k[u], vv[u]) for u in nu]
    smat = [s_scr[u] for u in nu]
    qs = [_dot1(rt[u] + qo[u][:, :D], smat[u], _NT) for u in nu]
    gp = [_dot3(bs_u[u], pw[u][:, :D], _TN) for u in nu]
    zt = [_dot3(jnp.concatenate([pw[u][:, D:], vv[u]], axis=0),
                jnp.concatenate([bs_u[u], ks_u[u]], axis=0), _TN) for u in nu]
    gmat = [jnp.where(eye_d, jnp.broadcast_to(gam_u[u][0:1, :], (D, D)), 0.0) + gp[u] for u in nu]
    sg = [_dot3(smat[u], gmat[u], _NT) for u in nu]
    for u in nu:
        s_scr[u] = sg[u] + zt[u]
    pairs = LANES // D
    for b in range(nb):
        for p in range(nh // pairs):
            us = [b * nh + p * pairs + s for s in range(pairs)]
            o_scr[b * T:(b + 1) * T, p * LANES:(p + 1) * LANES] = jnp.concatenate(
                [qs[u] + qo[u][:, D:] + rkv[u] for u in us], axis=1)

    o = o_scr[...]
    mean = head_sums([o])[0] * (1.0 / D)
    d = o - mean
    var = head_sums([d * d])[0] * (1.0 / D)
    on = d * lax.rsqrt(var + RWKV_GN_EPS) * lng_ref[...] + lnb_ref[...]
    bonus = rk_sum * xv
    out_ref[...] = ((on + bonus) * gate).astype(out_ref.dtype).reshape(nb, T, W)

    @pl.when(c == nchunks - 1)
    def _():
        s1_ref[0] = s_scr[...].reshape(nb, nh, D, D)
        _zero_other_slabs(s1_ref)


def _rwkv(proj3, sh, s0, l_in, l_out, prev_out, mu, w0, wup, a0, aup, gup, kkw, kaw, rkw, lng, lnb,
          ones, *, nb):
    nseq, seqlen, _ = proj3.shape
    T = min(RWKV_CHUNK, seqlen)
    nc = seqlen // T
    W = RWKV_WIDTH
    const = lambda b, c: (0, 0)
    rowblk = lambda col, width: pl.BlockSpec((nb, T, width), lambda b, c: (b, c, col // width))
    sdims = (RWKV_HEADS, HEAD_DIM, HEAD_DIM)
    vec = pl.BlockSpec((1, W), const)
    in_specs = [rowblk(C_R, W), rowblk(C_K, W), rowblk(C_V, W), rowblk(C_G, G_PAD),
                rowblk(C_WA, WA_PAD),
                pl.BlockSpec((nb, 1, RW_PACK), lambda b, c: (b, 0, 0)),
                pl.BlockSpec((1, nb) + sdims, lambda b, c: (l_in, b, 0, 0, 0)),
                pl.BlockSpec((1, RW_PACK), const),
                vec, pl.BlockSpec((WA_PAD, W), const),
                vec, pl.BlockSpec((WA_PAD, W), const),
                pl.BlockSpec((G_PAD, W), const),
                vec, vec, vec, vec, vec,
                pl.BlockSpec((2 * LANES, 2 * LANES), const)]
    rows = nb * T
    return _stacked_state_call(
        functools.partial(_rwkv_kernel, nb=nb, T=T, nchunks=nc), (nseq // nb, nc), in_specs,
        [proj3, proj3, proj3, proj3, proj3, sh, s0, mu, w0, wup, a0, aup, gup, kkw, kaw, rkw, lng,
         lnb, ones],
        pl.BlockSpec((nb, T, W), lambda b, c: (b, c, 0)),
        jax.ShapeDtypeStruct((nseq, seqlen, W), BF16),
        sdims, l_out, prev_out,
        [pltpu.VMEM((nb, RW_PACK), F32), pltpu.VMEM((nb * RWKV_HEADS, HEAD_DIM, HEAD_DIM), F32)]
        + [pltpu.VMEM((rows, W), F32)] * 7
        + [pltpu.VMEM((rows, W), F32), pltpu.VMEM((rows, W), F32)],
        "rwkv", nb=nb)


def _zeros_like_cols(x, n):
    return jnp.zeros(x.shape[:-1] + (n,), x.dtype)


def _pack_rwkv_cols(x):
    return jnp.concatenate([x[..., :_O_XW], x[..., _O_XG:], _zeros_like_cols(x, G_PAD - RWKV_R_G),
                            x[..., _O_XW:_O_XG]], axis=-1)


def _pack_tail_cols(w):
    return jnp.concatenate([_pack_rwkv_cols(w[..., _O_RW:]), w[..., _O_DT:_O_RW],
                            _zeros_like_cols(w, DT_PAD - SSD_HEADS)], axis=-1)


def _pad_rows(x, n_before, n_total):
    b, r, c = x.shape
    return jnp.concatenate([jnp.zeros((b, n_before, c), x.dtype), x,
                            jnp.zeros((b, n_total - n_before - r, c), x.dtype)], axis=1)


def _small_params(l, p):
    row = lambda v: v.reshape(1, -1)
    pad_lanes = lambda v, n: jnp.concatenate([v, jnp.zeros((n - v.shape[0],), v.dtype)]).reshape(1, n)
    zeros_w = jnp.zeros((RWKV_R_W, RWKV_WIDTH), F32)
    return dict(
        lru=(p['lru_conv_w'][l], row(p['lru_conv_b'][l]), p['lru_wa'][l].astype(BF16),
             row(p['lru_ba'][l]), p['lru_wx'][l].astype(BF16), row(p['lru_bx'][l]),
             row(p['lru_lambda'][l])),
        ssd=(p['ssd_conv_w'][l], row(p['ssd_conv_b'][l]), pad_lanes(p['ssd_dt_bias'][l], DT_PAD),
             pad_lanes(p['ssd_a_log'][l], DT_PAD), row(jnp.repeat(p['ssd_d'][l], HEAD_DIM)),
             row(p['ssd_norm_g'][l])),
        rwkv=(row(_pack_rwkv_cols(p['rwkv_mu'][l])), row(p['rwkv_w0'][l]),
              jnp.concatenate([p['rwkv_w_up'][l], zeros_w], axis=0).astype(BF16),
              row(p['rwkv_a0'][l]),
              jnp.concatenate([zeros_w, p['rwkv_a_up'][l]], axis=0).astype(BF16),
              jnp.concatenate([p['rwkv_g_up'][l],
                               jnp.zeros((G_PAD - RWKV_R_G, RWKV_WIDTH), F32)], axis=0).astype(BF16),
              row(p['rwkv_k_k'][l]), row(p['rwkv_k_a'][l]), row(p['rwkv_r_k'][l]),
              row(p['rwkv_ln_g'][l]), row(p['rwkv_ln_b'][l])),
        ln1=(row(p['ln1_g'][l]), row(p['ln1_b'][l])),
        ln2=(row(p['ln2_g'][l]), row(p['ln2_b'][l])),
    )


def _constants():
    lane = jnp.arange(SSD_WIDTH) // HEAD_DIM
    ehp = (jnp.arange(DT_PAD)[:, None] == lane[None, :]).astype(BF16)
    blk = jnp.arange(2 * LANES) // HEAD_DIM
    ones = (blk[:, None] == blk[None, :]).astype(BF16)
    return ehp, ones


def _layer(x, l, small_state, ssd_in, rw_in, prev_outs, sp, big, consts, *, nseq, seqlen, lru_nblk,
           rwkv_nb):
    lru_conv0, lru_h0, ssd_conv0, rw_shift0 = small_state
    ehp, ones = consts
    nbuf = CONV_WIDTH - 1
    proj = _proj(x, big['w_main'], big['w_tail'], l)

    lru_nseq = nseq // lru_nblk
    out_a, lru_h1 = _lru(
        proj, _pad_rows(lru_conv0, 0, SUBLANES).reshape(lru_nblk, lru_nseq * SUBLANES, LRU_WIDTH),
        lru_h0.reshape(lru_nblk, lru_nseq, LRU_WIDTH), *sp['lru'],
        nblk=lru_nblk, nseq=lru_nseq, seqlen=seqlen)
    out_b, ssd_out = _ssd(proj, _pad_rows(ssd_conv0, SUBLANES - nbuf, SUBLANES), ssd_in[0],
                          ssd_in[1], l, prev_outs[0], *sp['ssd'], ehp, nseq=nseq, seqlen=seqlen)
    p3 = proj.reshape(nseq, seqlen, N_PROJ)
    out_c, rw_out = _rwkv(p3, _pack_rwkv_cols(rw_shift0)[:, None, :], rw_in[0], rw_in[1], l,
                          prev_outs[1], *sp['rwkv'], ones, nb=rwkv_nb)
    out_c = out_c.reshape(nseq * seqlen, RWKV_WIDTH)

    y = _outproj_ln(out_a, out_b, out_c, big['w_out'], l, x, *sp['ln1'])
    y = _ffn_down_ln(_ffn_up(y, big['w_gate'], big['w_up'], l), big['w_down'], l, y, *sp['ln2'])

    tail = p3[:, seqlen - nbuf:, :]
    last = p3[:, seqlen - 1, :]
    rw_shift1 = jnp.concatenate([last[:, C_R:C_R + 3 * RWKV_WIDTH], last[:, C_WA:C_WA + WA_PAD],
                                 last[:, C_G:C_G + RWKV_R_G]], axis=-1)
    small_new = (tail[:, :, C_LX:C_LX + LRU_WIDTH], lru_h1.reshape(nseq, LRU_WIDTH),
                 tail[:, :, C_XBC:C_XBC + SSD_CONV_DIM], rw_shift1)
    return y, small_new, (ssd_out, rw_out)


def kernel(x_prompt, x_sample, state_lru_conv, state_lru_h, state_ssd_conv, state_ssd,
           state_rwkv_shift, state_rwkv, w_in, lru_conv_w, lru_conv_b, lru_wa, lru_ba, lru_wx,
           lru_bx, lru_lambda, ssd_conv_w, ssd_conv_b, ssd_dt_bias, ssd_a_log, ssd_d, ssd_norm_g,
           rwkv_mu, rwkv_w0, rwkv_w_up, rwkv_a0, rwkv_a_up, rwkv_g_up, rwkv_k_k, rwkv_k_a,
           rwkv_r_k, rwkv_ln_g, rwkv_ln_b, w_out, ln1_g, ln1_b, w_gate, w_up, w_down, ln2_g, ln2_b):
    params = dict(
        lru_conv_w=lru_conv_w, lru_conv_b=lru_conv_b, lru_wa=lru_wa, lru_ba=lru_ba,
        lru_wx=lru_wx, lru_bx=lru_bx, lru_lambda=lru_lambda, ssd_conv_w=ssd_conv_w,
        ssd_conv_b=ssd_conv_b, ssd_dt_bias=ssd_dt_bias, ssd_a_log=ssd_a_log, ssd_d=ssd_d,
        ssd_norm_g=ssd_norm_g, rwkv_mu=rwkv_mu, rwkv_w0=rwkv_w0, rwkv_w_up=rwkv_w_up,
        rwkv_a0=rwkv_a0, rwkv_a_up=rwkv_a_up, rwkv_g_up=rwkv_g_up, rwkv_k_k=rwkv_k_k,
        rwkv_k_a=rwkv_k_a, rwkv_r_k=rwkv_r_k.reshape(DEPTH, RWKV_WIDTH), rwkv_ln_g=rwkv_ln_g,
        rwkv_ln_b=rwkv_ln_b, ln1_g=ln1_g, ln1_b=ln1_b, ln2_g=ln2_g, ln2_b=ln2_b)
    w_main = w_in.astype(BF16)
    big = dict(w_main=w_main, w_tail=_pack_tail_cols(w_main), w_out=w_out.astype(BF16),
               w_gate=w_gate.astype(BF16), w_up=w_up.astype(BF16), w_down=w_down.astype(BF16))
    bp, lp_len, _ = x_prompt.shape
    bs, ls_len, _ = x_sample.shape
    consts = _constants()
    nbuf = CONV_WIDTH - 1
    zero_small = (jnp.zeros((bp, nbuf, LRU_WIDTH), F32), jnp.zeros((bp, LRU_WIDTH), F32),
                  jnp.zeros((bp, nbuf, SSD_CONV_DIM), F32), jnp.zeros((bp, RWKV_SHIFT), F32))
    zero_ssd = jnp.zeros((1, bp, SSD_HEADS, HEAD_DIM, SSD_STATE), F32)
    zero_rw = jnp.zeros((1, bp, RWKV_HEADS, HEAD_DIM, HEAD_DIM), F32)
    yp = x_prompt.reshape(bp * lp_len, D_MODEL)
    ys = x_sample.reshape(bs * ls_len, D_MODEL)
    new_p = [[] for _ in range(4)]
    new_s = [[] for _ in range(4)]
    outs_p = (None, None)
    outs_s = (None, None)
    for l in range(DEPTH):
        sp = _small_params(l, params)
        yp, small_p, outs_p = _layer(yp, l, zero_small, (zero_ssd, 0), (zero_rw, 0), outs_p, sp,
                                     big, consts, nseq=bp, seqlen=lp_len, lru_nblk=bp,
                                     rwkv_nb=RWKV_PROMPT_NB)
        ys, small_s, outs_s = _layer(
            ys, l, (state_lru_conv[l], state_lru_h[l], state_ssd_conv[l], state_rwkv_shift[l]),
            (state_ssd, l), (state_rwkv, l), outs_s, sp, big, consts,
            nseq=bs, seqlen=ls_len, lru_nblk=1, rwkv_nb=RWKV_SAMPLE_NB)
        for i in range(4):
            new_p[i].append(small_p[i])
            new_s[i].append(small_s[i])
    p_lru_conv, p_lru_h, p_ssd_conv, p_rw_shift = [jnp.stack(v) for v in new_p]
    s_lru_conv, s_lru_h, s_ssd_conv, s_rw_shift = [jnp.stack(v) for v in new_s]
    return (yp.reshape(bp, lp_len, D_MODEL), ys.reshape(bs, ls_len, D_MODEL),
            p_lru_conv, p_lru_h, p_ssd_conv, outs_p[0], p_rw_shift, outs_p[1],
            s_lru_conv, s_lru_h, s_ssd_conv, outs_s[0], s_rw_shift, outs_s[1])
```

```python
import functools
import math

import jax
import jax.numpy as jnp
from jax import lax
from jax.experimental import pallas as pl
from jax.experimental.pallas import tpu as pltpu

F32 = jnp.float32
BF16 = jnp.bfloat16

D_MODEL = 2048
DEPTH = 2
D_MIX = 2 * D_MODEL
HEAD_DIM = 64
CONV_WIDTH = 4
LRU_WIDTH = D_MIX // 4
LRU_HEADS = 8
LRU_BLOCK = LRU_WIDTH // LRU_HEADS
LRU_C = 8.0
SSD_WIDTH = D_MIX // 2
SSD_HEADS = SSD_WIDTH // HEAD_DIM
SSD_GROUPS = 8
SSD_STATE = 128
SSD_CHUNK = 128
SSD_CONV_DIM = SSD_WIDTH + 2 * SSD_GROUPS * SSD_STATE
SSD_GROUP_WIDTH = SSD_WIDTH // SSD_GROUPS
SSD_HEADS_PER_GROUP = SSD_HEADS // SSD_GROUPS
RWKV_WIDTH = D_MIX - LRU_WIDTH - SSD_WIDTH
RWKV_HEADS = RWKV_WIDTH // HEAD_DIM
RWKV_R_W = max(32, int(round(1.8 * RWKV_WIDTH ** 0.5 / 32)) * 32)
RWKV_R_A = max(32, int(round(1.8 * RWKV_WIDTH ** 0.5 / 32)) * 32)
RWKV_R_G = max(32, int(round(0.6 * RWKV_WIDTH ** 0.8 / 32)) * 32)
RWKV_SHIFT = 3 * RWKV_WIDTH + RWKV_R_W + RWKV_R_A + RWKV_R_G
RWKV_GN_EPS = 64e-5
RWKV_CHUNK = 64
RWKV_PROMPT_NB = 2
RWKV_SAMPLE_NB = 8
D_FF = -(-(8 * D_MODEL) // (3 * 256)) * 256
ALPHA = (2 * DEPTH) ** 0.25
LN_EPS = 1e-5

LANES = 128
SUBLANES = 8
VMEM_LIMIT = 56 * 1024 * 1024

C_GATE = 0
C_LX = C_GATE + LRU_WIDTH
C_Z = C_LX + LRU_WIDTH
C_XBC = C_Z + SSD_WIDTH
C_R = C_XBC + SSD_CONV_DIM
C_K = C_R + RWKV_WIDTH
C_V = C_K + RWKV_WIDTH
C_G = C_V + RWKV_WIDTH
G_PAD = 2 * LANES
C_WA = C_G + G_PAD
WA_PAD = LANES
C_DT = C_WA + WA_PAD
DT_PAD = LANES
N_PROJ = C_DT + DT_PAD
assert RWKV_R_G <= G_PAD and RWKV_R_W + RWKV_R_A == WA_PAD and SSD_HEADS <= DT_PAD
RW_PACK = 3 * RWKV_WIDTH + G_PAD + WA_PAD

_O_DT = 2 * LRU_WIDTH + SSD_WIDTH + SSD_CONV_DIM
_O_RW = _O_DT + SSD_HEADS
_O_XW = 3 * RWKV_WIDTH
_O_XG = _O_XW + RWKV_R_W + RWKV_R_A


def _cparams(*sem):
    return pltpu.CompilerParams(dimension_semantics=sem, vmem_limit_bytes=VMEM_LIMIT)


_NN = (((1,), (0,)), ((), ()))
_NT = (((1,), (1,)), ((), ()))
_TN = (((0,), (0,)), ((), ()))


def _dg(a, b, dims):
    return lax.dot_general(a, b, dims, preferred_element_type=F32)


def _dot1(a, b, dims=_NN):
    return _dg(a.astype(BF16), b.astype(BF16), dims)


def _split2(x):
    hi = x.astype(BF16)
    lo = (x - hi.astype(F32)).astype(BF16)
    return hi, lo


def _split3(x):
    hi = x.astype(BF16)
    r1 = x - hi.astype(F32)
    mid = r1.astype(BF16)
    lo = (r1 - mid.astype(F32)).astype(BF16)
    return hi, mid, lo


def _dot3(a, b, dims=_NN):
    ah, al = _split2(a)
    bh, bl = _split2(b)
    free_axis = 1 if dims == _TN else 0
    m = a.shape[free_axis]
    both = _dg(jnp.concatenate([ah, al], axis=free_axis), bh, dims)
    return (both[:m] + both[m:]) + _dg(ah, bl, dims)


def _dot_shared_rhs(parts, w_exact, npieces):
    pieces = []
    for p in parts:
        rest = p
        for _ in range(npieces):
            piece = rest.astype(BF16).astype(F32)
            pieces.append(piece)
            rest = rest - piece
    prod = _dg(jnp.concatenate(pieces, axis=0).astype(BF16), w_exact, _NN)
    outs = []
    off = 0
    for p in parts:
        r = p.shape[0]
        acc = prod[off:off + r]
        for i in range(1, npieces):
            acc = acc + prod[off + i * r:off + (i + 1) * r]
        outs.append(acc)
        off += npieces * r
    return outs


def _dotx_left(w_exact, a):
    hi, mid, lo = _split3(a)
    return _dg(w_exact, hi, _NN) + (_dg(w_exact, mid, _NN) + _dg(w_exact, lo, _NN))


def _iota2(shape, dim):
    return lax.broadcasted_iota(jnp.int32, shape, dim)


def _softplus(x):
    return jnp.maximum(x, 0.0) + jnp.log1p(jnp.exp(-jnp.abs(x)))


def _sigmoid(x):
    return 1.0 / (1.0 + jnp.exp(-x))


def _silu(x):
    return x * _sigmoid(x)


def _gelu_tanh(x):
    c = math.sqrt(2.0 / math.pi)
    return 0.5 * x * (1.0 + jnp.tanh(c * (x + 0.044715 * (x * x * x))))


def _layer_norm(y, g, b):
    mu = jnp.mean(y, axis=-1, keepdims=True)
    d = y - mu
    var = jnp.mean(d * d, axis=-1, keepdims=True)
    return d * lax.rsqrt(var + LN_EPS) * g + b


DENSE_TM = 2048
PROJ_TN = 256
N_MAIN = C_R
N_TAIL = N_PROJ - N_MAIN
assert N_MAIN == _O_DT and N_MAIN % PROJ_TN == 0 and N_TAIL % PROJ_TN == 0


def _proj_kernel(x_ref, wm_ref, wt_ref, o_ref, xb_ref):
    j = pl.program_id(1)

    @pl.when(j == 0)
    def _():
        xb_ref[...] = x_ref[...].astype(BF16)

    @pl.when(j < N_MAIN // PROJ_TN)
    def _():
        o_ref[...] = _dg(xb_ref[...], wm_ref[0].astype(BF16), _NT)

    @pl.when(j >= N_MAIN // PROJ_TN)
    def _():
        o_ref[...] = _dg(xb_ref[...], wt_ref[0].astype(BF16), _NT)


def _proj(x, w_main_t, w_tail_t, l):
    m = x.shape[0]
    tm = min(m, DENSE_TM)
    tn = PROJ_TN
    nmain = N_MAIN // tn
    return pl.pallas_call(
        _proj_kernel,
        grid=(m // tm, N_PROJ // tn),
        in_specs=[pl.BlockSpec((tm, D_MODEL), lambda i, j: (i, 0), pipeline_mode=pl.Buffered(1)),
                  pl.BlockSpec((1, tn, D_MODEL), lambda i, j: (l, jnp.minimum(j, nmain - 1), 0)),
                  pl.BlockSpec((1, tn, D_MODEL), lambda i, j: (l, jnp.maximum(j - nmain, 0), 0))],
        out_specs=pl.BlockSpec((tm, tn), lambda i, j: (i, j)),
        out_shape=jax.ShapeDtypeStruct((m, N_PROJ), F32),
        scratch_shapes=[pltpu.VMEM((tm, D_MODEL), BF16)],
        compiler_params=_cparams("arbitrary", "arbitrary"),
        name="proj",
    )(x, w_main_t, w_tail_t)


OUT_KSTEPS = D_MIX // LRU_WIDTH


def _outproj_kernel(a_ref, b_ref, c_ref, w_ref, x_ref, g_ref, beta_ref, o_ref, acc_ref):
    k = pl.program_id(1)

    @pl.when(k == 0)
    def _():
        acc_ref[...] = jnp.dot(a_ref[...], w_ref[0], preferred_element_type=F32)

    @pl.when(jnp.logical_and(k > 0, k < OUT_KSTEPS - 1))
    def _():
        acc_ref[...] += jnp.dot(b_ref[...], w_ref[0], preferred_element_type=F32)

    @pl.when(k == OUT_KSTEPS - 1)
    def _():
        mix = acc_ref[...] + jnp.dot(c_ref[...], w_ref[0], preferred_element_type=F32)
        o_ref[...] = _layer_norm(ALPHA * x_ref[...] + mix, g_ref[...], beta_ref[...])


def _outproj_ln(out_a, out_b, out_c, w_out, l, x, g, beta):
    m = x.shape[0]
    tm = min(m, 512)
    tk = LRU_WIDTH
    nb = SSD_WIDTH // tk
    return pl.pallas_call(
        _outproj_kernel,
        grid=(m // tm, OUT_KSTEPS),
        in_specs=[pl.BlockSpec((tm, tk), lambda i, k: (i, 0)),
                  pl.BlockSpec((tm, tk), lambda i, k: (i, jnp.clip(k - 1, 0, nb - 1))),
                  pl.BlockSpec((tm, tk), lambda i, k: (i, 0)),
                  pl.BlockSpec((1, tk, D_MODEL), lambda i, k: (l, k, 0)),
                  pl.BlockSpec((tm, D_MODEL), lambda i, k: (i, 0)),
                  pl.BlockSpec((1, D_MODEL), lambda i, k: (0, 0)),
                  pl.BlockSpec((1, D_MODEL), lambda i, k: (0, 0))],
        out_specs=pl.BlockSpec((tm, D_MODEL), lambda i, k: (i, 0)),
        out_shape=jax.ShapeDtypeStruct((m, D_MODEL), F32),
        scratch_shapes=[pltpu.VMEM((tm, D_MODEL), F32)],
        compiler_params=_cparams("arbitrary", "arbitrary"),
        name="outproj_ln",
    )(out_a, out_b, out_c, w_out, x, g, beta)


def _ffn_up_kernel(x_ref, wg_ref, wu_ref, o_ref, xb_ref):
    @pl.when(pl.program_id(1) == 0)
    def _():
        xb_ref[...] = x_ref[...].astype(BF16)

    xb = xb_ref[...]
    gate = jnp.dot(xb, wg_ref[0].astype(BF16), preferred_element_type=F32)
    up = jnp.dot(xb, wu_ref[0].astype(BF16), preferred_element_type=F32)
    o_ref[...] = (_silu(gate) * up).astype(BF16)


def _ffn_up(x, wg, wu, l):
    m = x.shape[0]
    tm = min(m, DENSE_TM)
    tn = 256
    return pl.pallas_call(
        _ffn_up_kernel,
        grid=(m // tm, D_FF // tn),
        in_specs=[pl.BlockSpec((tm, D_MODEL), lambda i, j: (i, 0)),
                  pl.BlockSpec((1, D_MODEL, tn), lambda i, j: (l, 0, j)),
                  pl.BlockSpec((1, D_MODEL, tn), lambda i, j: (l, 0, j))],
        out_specs=pl.BlockSpec((tm, tn), lambda i, j: (i, j)),
        out_shape=jax.ShapeDtypeStruct((m, D_FF), BF16),
        scratch_shapes=[pltpu.VMEM((tm, D_MODEL), BF16)],
        compiler_params=_cparams("arbitrary", "arbitrary"),
        name="ffn_up",
    )(x, wg, wu)


FFN_DOWN_KSTEPS = 4


def _ffn_down_kernel(h_ref, w_ref, x_ref, g_ref, beta_ref, o_ref, acc_ref):
    k = pl.program_id(1)

    @pl.when(k == 0)
    def _():
        acc_ref[...] = jnp.dot(h_ref[...], w_ref[0], preferred_element_type=F32)

    @pl.when(jnp.logical_and(k > 0, k < FFN_DOWN_KSTEPS - 1))
    def _():
        acc_ref[...] += jnp.dot(h_ref[...], w_ref[0], preferred_element_type=F32)

    @pl.when(k == FFN_DOWN_KSTEPS - 1)
    def _():
        ffn = acc_ref[...] + jnp.dot(h_ref[...], w_ref[0], preferred_element_type=F32)
        o_ref[...] = _layer_norm(ALPHA * x_ref[...] + ffn, g_ref[...], beta_ref[...])


def _ffn_down_ln(h, w_down, l, x, g, beta):
    m = x.shape[0]
    tm = min(m, 512)
    tk = D_FF // FFN_DOWN_KSTEPS
    assert tk * FFN_DOWN_KSTEPS == D_FF and tk % LANES == 0
    return pl.pallas_call(
        _ffn_down_kernel,
        grid=(m // tm, FFN_DOWN_KSTEPS),
        in_specs=[pl.BlockSpec((tm, tk), lambda i, k: (i, k)),
                  pl.BlockSpec((1, tk, D_MODEL), lambda i, k: (l, k, 0)),
                  pl.BlockSpec((tm, D_MODEL), lambda i, k: (i, 0)),
                  pl.BlockSpec((1, D_MODEL), lambda i, k: (0, 0)),
                  pl.BlockSpec((1, D_MODEL), lambda i, k: (0, 0))],
        out_specs=pl.BlockSpec((tm, D_MODEL), lambda i, k: (i, 0)),
        out_shape=jax.ShapeDtypeStruct((m, D_MODEL), F32),
        scratch_shapes=[pltpu.VMEM((tm, D_MODEL), F32)],
        compiler_params=_cparams("arbitrary", "arbitrary"),
        name="ffn_down_ln",
    )(h, w_down, x, g, beta)


def _conv_taps(u, first_rows, cw, cb, t_in_seq):
    out = cb + cw[CONV_WIDTH - 1:CONV_WIDTH, :] * u
    for j in range(1, CONV_WIDTH):
        shifted = jnp.where(t_in_seq < j, first_rows(j), pltpu.roll(u, j, 0))
        out = out + cw[CONV_WIDTH - 1 - j:CONV_WIDTH - j, :] * shifted
    return out


def _lru_kernel(gate_ref, lx_ref, buf_ref, h0_ref, cw_ref, cb_ref, wa_ref, ba_ref, wx_ref,
                bx_ref, lam_ref, out_ref, h1_ref, a_scr, b_scr, hin_scr, *, nseq, seqlen):
    rows = nseq * seqlen
    u = lx_ref[...]
    t_in_seq = _iota2((rows, LANES), 0) % seqlen
    if nseq == 1:
        buf = jnp.concatenate([buf_ref[0], jnp.zeros((rows - SUBLANES, LANES), F32)], axis=0) \
            if rows > SUBLANES else buf_ref[0]
    else:
        buf = buf_ref[0]
    nbuf = CONV_WIDTH - 1
    xc = _conv_taps(u, lambda j: pltpu.roll(buf, (j - nbuf) % rows, 0), cw_ref[...], cb_ref[...],
                    t_in_seq)

    r = _sigmoid(_dot1(xc, wa_ref[0]) + ba_ref[...])
    i = _sigmoid(_dot1(xc, wx_ref[0]) + bx_ref[...])
    log_a = (-LRU_C) * r * _softplus(-lam_ref[...])
    a = jnp.exp(log_a)
    b = jnp.sqrt(jnp.tanh(-log_a) * (a * a + 1.0)) * (i * xc)

    t8 = _iota2((rows, LANES), 0) % SUBLANES
    for s in (1, 2, 4):
        m = t8 >= s
        a_sh = pltpu.roll(a, s, 0)
        b_sh = pltpu.roll(b, s, 0)
        b = jnp.where(m, a * b_sh + b, b)
        a = jnp.where(m, a * a_sh, a)

    if seqlen == SUBLANES:
        h0 = h0_ref[0]
        hin = jnp.broadcast_to(h0[:, None, :], (nseq, SUBLANES, LANES)).reshape(rows, LANES)
        h = a * hin + b
        out_ref[...] = (h * _gelu_tanh(gate_ref[...])).astype(out_ref.dtype)
        a_scr[...] = h
        h1_ref[0] = a_scr[pl.ds(SUBLANES - 1, nseq, stride=SUBLANES), :]
    else:
        assert nseq == 1
        a_scr[...] = a
        b_scr[...] = b

        def carry_step(g, carry):
            base = pl.multiple_of(g * SUBLANES, SUBLANES)
            hin_scr[pl.ds(base, SUBLANES), :] = jnp.broadcast_to(carry, (SUBLANES, LANES))
            a7 = a_scr[pl.ds(base + SUBLANES - 1, 1), :]
            b7 = b_scr[pl.ds(base + SUBLANES - 1, 1), :]
            return a7 * carry + b7

        last = lax.fori_loop(0, rows // SUBLANES, carry_step, h0_ref[0])
        h = a_scr[...] * hin_scr[...] + b_scr[...]
        out_ref[...] = (h * _gelu_tanh(gate_ref[...])).astype(out_ref.dtype)
        h1_ref[0] = last


def _lru(proj, buf, h0, cw, cb, wa, ba, wx, bx, lam, *, nblk, nseq, seqlen):
    rows = nseq * seqlen
    gate_blk = C_GATE // LRU_BLOCK
    lx_blk = C_LX // LRU_BLOCK
    row = lambda s, h: (0, h)
    return pl.pallas_call(
        functools.partial(_lru_kernel, nseq=nseq, seqlen=seqlen),
        grid=(nblk, LRU_HEADS),
        in_specs=[pl.BlockSpec((rows, LRU_BLOCK), lambda s, h: (s, gate_blk + h)),
                  pl.BlockSpec((rows, LRU_BLOCK), lambda s, h: (s, lx_blk + h)),
                  pl.BlockSpec((1, nseq * SUBLANES, LRU_BLOCK), lambda s, h: (s, 0, h)),
                  pl.BlockSpec((1, nseq, LRU_BLOCK), lambda s, h: (s, 0, h)),
                  pl.BlockSpec((CONV_WIDTH, LRU_BLOCK), row),
                  pl.BlockSpec((1, LRU_BLOCK), row),
                  pl.BlockSpec((1, LRU_BLOCK, LRU_BLOCK), lambda s, h: (h, 0, 0)),
                  pl.BlockSpec((1, LRU_BLOCK), row),
                  pl.BlockSpec((1, LRU_BLOCK, LRU_BLOCK), lambda s, h: (h, 0, 0)),
                  pl.BlockSpec((1, LRU_BLOCK), row),
                  pl.BlockSpec((1, LRU_BLOCK), row)],
        out_specs=[pl.BlockSpec((rows, LRU_BLOCK), lambda s, h: (s, h)),
                   pl.BlockSpec((1, nseq, LRU_BLOCK), lambda s, h: (s, 0, h))],
        out_shape=[jax.ShapeDtypeStruct((nblk * rows, LRU_WIDTH), BF16),
                   jax.ShapeDtypeStruct((nblk, nseq, LRU_WIDTH), F32)],
        scratch_shapes=[pltpu.VMEM((rows, LANES), F32)] * 3,
        compiler_params=_cparams("arbitrary", "arbitrary"),
        name="lru",
    )(proj, proj, buf, h0, cw, cb, wa, ba, wx, bx, lam)


def _stacked_state_call(kernel_fn, grid, in_specs, operands, row_spec, row_shape, state_dims,
                        l_out, prev_out, scratch_shapes, name, nb=1):
    nseq = grid[0] * nb
    zeros = (0,) * len(state_dims)
    state_shape = jax.ShapeDtypeStruct((DEPTH, nseq) + tuple(state_dims), F32)
    aliases = {}
    if prev_out is None:
        assert l_out == 0
        state_spec = pl.BlockSpec((DEPTH, nb) + tuple(state_dims), lambda b, c: (0, b) + zeros)
    else:
        state_spec = pl.BlockSpec((1, nb) + tuple(state_dims), lambda b, c: (l_out, b) + zeros)
        in_specs = in_specs + [pl.BlockSpec(memory_space=pl.ANY)]
        operands = operands + [prev_out]
        aliases = {len(operands) - 1: 1}
        kernel_fn = functools.partial(_drop_alias_ref, kernel_fn, len(operands) - 1)
    return pl.pallas_call(
        kernel_fn, grid=grid, in_specs=in_specs, out_specs=[row_spec, state_spec],
        out_shape=[row_shape, state_shape], scratch_shapes=scratch_shapes,
        input_output_aliases=aliases, compiler_params=_cparams("arbitrary", "arbitrary"),
        name=name)(*operands)


def _drop_alias_ref(kernel_fn, pos, *refs):
    return kernel_fn(*refs[:pos], *refs[pos + 1:])


def _zero_other_slabs(state_ref):
    if state_ref.shape[0] > 1:
        state_ref[1:] = jnp.zeros((state_ref.shape[0] - 1,) + state_ref.shape[1:], state_ref.dtype)


def _transpose_rows_to_lanes(x, t):
    if t < LANES:
        x = jnp.concatenate([x, jnp.zeros((LANES - t, LANES), F32)], axis=0)
    return x.T[:, :t]


def _ssd_kernel(z_ref, xbc_ref, dt_ref, buf_ref, s0_ref, cw_ref, cb_ref, dtb_ref, alog_ref,
                dch_ref, ng_ref, ehp_ref, out_ref, s1_ref, tail_ref, pad_ref, y_scr, *, T):
    c = pl.program_id(1)
    n_state = SSD_STATE
    gw = SSD_GROUP_WIDTH
    hpg = SSD_HEADS_PER_GROUP

    @pl.when(c == 0)
    def _():
        tail_ref[...] = buf_ref[0]
        s1_ref[0:1] = s0_ref[...]
        _zero_other_slabs(s1_ref)

    u = xbc_ref[...]
    pad_ref[0:SUBLANES, :] = tail_ref[...]
    pad_ref[SUBLANES:SUBLANES + T, :] = u
    cw = cw_ref[...]
    xc = cb_ref[...] + cw[CONV_WIDTH - 1:CONV_WIDTH, :] * u
    for j in range(1, CONV_WIDTH):
        xc = xc + cw[CONV_WIDTH - 1 - j:CONV_WIDTH - j, :] * pad_ref[SUBLANES - j:SUBLANES - j + T, :]
    tail_ref[...] = pad_ref[T:T + SUBLANES, :]

    xa = _silu(xc)
    xs = xa[:, :SSD_WIDTH]
    bm = xa[:, SSD_WIDTH:SSD_WIDTH + SSD_GROUPS * n_state]
    cm = xa[:, SSD_WIDTH + SSD_GROUPS * n_state:]
    dt = _softplus(dt_ref[...] + dtb_ref[...])
    da = dt * (-jnp.exp(alog_ref[...]))
    ii = _iota2((T, T), 0)
    jj = _iota2((T, T), 1)
    causal = ii >= jj
    cum = _dotx_left(causal.astype(BF16), da)
    cum_t = _transpose_rows_to_lanes(cum, T)
    ehp = ehp_ref[...]
    cumx, dtx = _dot_shared_rhs([cum, dt], ehp, 2)
    xdt = xs * dtx
    ecum = jnp.exp(cumx)
    xdtd = xdt * jnp.exp(cumx[T - 1:T, :] - cumx)

    groups = range(SSD_GROUPS)
    heads = range(SSD_HEADS)
    cg = [cm[:, g * n_state:(g + 1) * n_state] for g in groups]
    bg = [bm[:, g * n_state:(g + 1) * n_state] for g in groups]
    sg = [s1_ref[0, 0, g * hpg:(g + 1) * hpg].reshape(gw, n_state) for g in groups]
    cb = [_dot1(cg[g], bg[g], _NT) for g in groups]
    y_off = [_dot1(cg[g], sg[g], _NT) for g in groups]
    st = [_dot1(xdtd[:, g * gw:(g + 1) * gw], bg[g], _TN) for g in groups]
    lm = [jnp.where(causal, jnp.exp(cum[:, h:h + 1] - cum_t[h:h + 1, :]), 0.0) for h in heads]
    y_diag = [_dot1(cb[h // hpg] * lm[h], xdt[:, h * HEAD_DIM:(h + 1) * HEAD_DIM]) for h in heads]
    for g in groups:
        y_scr[:, g * gw:(g + 1) * gw] = (jnp.concatenate(y_diag[g * hpg:(g + 1) * hpg], axis=1)
                                         + y_off[g] * ecum[:, g * gw:(g + 1) * gw])
        decay = jnp.concatenate(
            [jnp.broadcast_to(jnp.exp(cum_t[h:h + 1, T - 1:T]), (HEAD_DIM, n_state))
             for h in range(g * hpg, (g + 1) * hpg)], axis=0)
        s1_ref[0, 0, g * hpg:(g + 1) * hpg] = (sg[g] * decay + st[g]).reshape(hpg, HEAD_DIM, n_state)

    y = (y_scr[...] + dch_ref[...] * xs) * _silu(z_ref[...])
    outs = []
    for g in groups:
        yg = y[:, g * gw:(g + 1) * gw]
        ms = jnp.mean(yg * yg, axis=-1, keepdims=True)
        outs.append(yg * lax.rsqrt(ms + 1e-5))
    out_ref[...] = (jnp.concatenate(outs, axis=1) * ng_ref[...]).astype(out_ref.dtype)


def _ssd(proj, buf, s0, l_in, l_out, prev_out, cw, cb, dtb, alog, dch, ng, ehp, *, nseq, seqlen):
    T = min(SSD_CHUNK, seqlen)
    nc = seqlen // T
    const = lambda b, c: (0, 0)
    sblk = (1, 1, SSD_HEADS, HEAD_DIM, SSD_STATE)
    in_specs = [pl.BlockSpec((T, SSD_WIDTH), lambda b, c: (b * nc + c, C_Z // SSD_WIDTH)),
                pl.BlockSpec((T, SSD_CONV_DIM), lambda b, c: (b * nc + c, C_XBC // SSD_CONV_DIM)),
                pl.BlockSpec((T, DT_PAD), lambda b, c: (b * nc + c, C_DT // DT_PAD)),
                pl.BlockSpec((1, SUBLANES, SSD_CONV_DIM), lambda b, c: (b, 0, 0)),
                pl.BlockSpec(sblk, lambda b, c: (l_in, b, 0, 0, 0)),
                pl.BlockSpec((CONV_WIDTH, SSD_CONV_DIM), const),
                pl.BlockSpec((1, SSD_CONV_DIM), const),
                pl.BlockSpec((1, DT_PAD), const),
                pl.BlockSpec((1, DT_PAD), const),
                pl.BlockSpec((1, SSD_WIDTH), const),
                pl.BlockSpec((1, SSD_WIDTH), const),
                pl.BlockSpec((DT_PAD, SSD_WIDTH), const)]
    return _stacked_state_call(
        functools.partial(_ssd_kernel, T=T), (nseq, nc), in_specs,
        [proj, proj, proj, buf, s0, cw, cb, dtb, alog, dch, ng, ehp],
        pl.BlockSpec((T, SSD_WIDTH), lambda b, c: (b * nc + c, 0)),
        jax.ShapeDtypeStruct((nseq * seqlen, SSD_WIDTH), BF16),
        sblk[2:], l_out, prev_out,
        [pltpu.VMEM((SUBLANES, SSD_CONV_DIM), F32),
         pltpu.VMEM((T + SUBLANES, SSD_CONV_DIM), F32),
         pltpu.VMEM((T, SSD_WIDTH), F32)],
        "ssd")


def _unit_lower_inverse(a_list, ii, jj, T):
    n = len(a_list)
    pair = (ii >> 1) == (jj >> 1)
    inv = [jnp.where(ii == jj, 1.0, 0.0) + jnp.where(pair, a_list[h], 0.0) for h in range(n)]
    shift = 1
    while (2 << shift) <= T:
        band = jnp.logical_and((ii >> (shift + 1)) == (jj >> (shift + 1)),
                               (ii >> shift) != (jj >> shift))
        x = [_dot1(jnp.where(band, a_list[h], 0.0), inv[h]) for h in range(n)]
        inv = [inv[h] + _dot1(inv[h], x[h]) for h in range(n)]
        shift += 1
    return inv


def _rwkv_kernel(r_ref, k_ref, v_ref, g_ref, wa_ref, sh_ref, s0_ref, mu_ref, w0_ref, wup_ref,
                 a0_ref, aup_ref, gup_ref, kkw_ref, kaw_ref, rkw_ref, lng_ref, lnb_ref, ones_ref,
                 out_ref, s1_ref, prev_scr, s_scr, al_scr, be_scr, kt_scr, rt_scr, bs_scr,
                 ks_scr, v_scr, gam_scr, o_scr, *, nb, T, nchunks):
    c = pl.program_id(1)
    W = RWKV_WIDTH
    D = HEAD_DIM
    R = nb * T
    nh = RWKV_HEADS

    @pl.when(c == 0)
    def _():
        prev_scr[...] = sh_ref[...].reshape(nb, RW_PACK)
        s_scr[...] = s0_ref[0].reshape(nb * nh, D, D)

    def rows_of(ref3):
        return ref3[...].reshape(R, ref3.shape[-1])

    def per_seq_rows(x):
        return jnp.broadcast_to(x[:, None, :], (nb, T, x.shape[-1])).reshape(R, x.shape[-1])

    def last_rows(x):
        return jnp.concatenate([x[(b + 1) * T - 1:(b + 1) * T, :] for b in range(nb)], axis=0)

    def token_shift(p, lo, hi):
        first = _iota2(p.shape, 0) % T == 0
        prev = jnp.where(first, per_seq_rows(prev_scr[:, lo:hi]), pltpu.roll(p, 1, 0))
        prev_scr[:, lo:hi] = last_rows(p)
        return p + (prev - p) * mu_ref[:, lo:hi]

    xr = token_shift(rows_of(r_ref), 0, W)
    xk = token_shift(rows_of(k_ref), W, 2 * W)
    xv = token_shift(rows_of(v_ref), 2 * W, 3 * W)
    xg = token_shift(rows_of(g_ref), 3 * W, 3 * W + G_PAD)
    xwa = token_shift(rows_of(wa_ref), 3 * W + G_PAD, RW_PACK)

    w_lin = w0_ref[...] + _dot1(jnp.tanh(xwa), wup_ref[...])
    a = _sigmoid(a0_ref[...] + _dot1(xwa, aup_ref[...]))
    gate = _dot1(_sigmoid(xg), gup_ref[...])
    lw = -jnp.exp(-_softplus(-w_lin) - 0.5)

    ones = ones_ref[...]

    def head_sums(xs):
        tw = ones.shape[0]
        nt = W // tw
        tiles = [x[:, i * tw:(i + 1) * tw] for x in xs for i in range(nt)]
        sums = _dot_shared_rhs(tiles, ones, 3)
        return [jnp.concatenate(sums[n * nt:(n + 1) * nt], axis=1) for n in range(len(xs))]

    kk = xk * kkw_ref[...]
    kp = xk * (1.0 + (a - 1.0) * kaw_ref[...])
    kk_sq, rk_sum = head_sums([kk * kk, xr * kp * rkw_ref[...]])
    kk = kk / jnp.maximum(jnp.sqrt(kk_sq), 1e-12)

    ri = _iota2((R, R), 0)
    rj = _iota2((R, R), 1)
    same_seq_causal = jnp.logical_and(ri >= rj, ri // T == rj // T)
    cum = _dotx_left(same_seq_causal.astype(BF16), lw)
    e_neg = jnp.exp(-cum)
    gam = jnp.exp(per_seq_rows(last_rows(cum)))
    be = kk * a * e_neg
    kt = kp * e_neg
    al_scr[...] = -kk * jnp.exp(cum - lw)
    be_scr[...] = be
    kt_scr[...] = kt
    rt_scr[...] = xr * jnp.exp(cum)
    bs_scr[...] = be * gam
    ks_scr[...] = kt * gam
    v_scr[...] = xv
    gam_scr[...] = gam

    ii = _iota2((T, T), 0)
    jj = _iota2((T, T), 1)
    incl = ii >= jj
    strict = ii > jj
    eye_d = _iota2((D, D), 0) == _iota2((D, D), 1)

    units = [(b, h) for b in range(nb) for h in range(nh)]
    nu = range(len(units))

    def per_unit(ref):
        return [ref[b * T:(b + 1) * T, h * D:(h + 1) * D] for b, h in units]

    al, be_u, kt_u, rt = per_unit(al_scr), per_unit(be_scr), per_unit(kt_scr), per_unit(rt_scr)
    bs_u, ks_u, vv, gam_u = per_unit(bs_scr), per_unit(ks_scr), per_unit(v_scr), per_unit(gam_scr)
    gram = [_dot1(jnp.concatenate([al[u], rt[u]], axis=0),
                  jnp.concatenate([be_u[u], kt_u[u]], axis=0), _NT) for u in nu]
    a_ab = [jnp.where(strict, gram[u][:T, :T], 0.0) for u in nu]
    a_ak = [jnp.where(strict, gram[u][:T, T:], 0.0) for u in nu]
    r_b = [jnp.where(incl, gram[u][T:, :T], 0.0) for u in nu]
    r_k = [jnp.where(incl, gram[u][T:, T:], 0.0) for u in nu]
    inv = _unit_lower_inverse(a_ab, ii, jj, T)
    akv = [_dot1(a_ak[u], vv[u]) for u in nu]
    pw = [_dot1(inv[u], jnp.concatenate([al[u], akv[u]], axis=1)) for u in nu]
    qo = [_dot1(r_b[u], pw[u]) for u in nu]
    rkv = [_dot1(r_k[u], vv[u]) for u in nu]
    smat = [s_scr[u] for u in nu]
    qs = [_dot1(rt[u] + qo[u][:, :D], smat[u], _NT) for u in nu]
    gp = [_dot3(bs_u[u], pw[u][:, :D], _TN) for u in nu]
    zt = [_dot3(jnp.concatenate([pw[u][:, D:], vv[u]], axis=0),
                jnp.concatenate([bs_u[u], ks_u[u]], axis=0), _TN) for u in nu]
    gmat = [jnp.where(eye_d, jnp.broadcast_to(gam_u[u][0:1, :], (D, D)), 0.0) + gp[u] for u in nu]
    sg = [_dot3(smat[u], gmat[u], _NT) for u in nu]
    for u in nu:
        s_scr[u] = sg[u] + zt[u]
    pairs = LANES // D
    for b in range(nb):
        for p in range(nh // pairs):
            us = [b * nh + p * pairs + s for s in range(pairs)]
            o_scr[b * T:(b + 1) * T, p * LANES:(p + 1) * LANES] = jnp.concatenate(
                [qs[u] + qo[u][:, D:] + rkv[u] for u in us], axis=1)

    o = o_scr[...]
    mean = head_sums([o])[0] * (1.0 / D)
    d = o - mean
    var = head_sums([d * d])[0] * (1.0 / D)
    on = d * lax.rsqrt(var + RWKV_GN_EPS) * lng_ref[...] + lnb_ref[...]
    bonus = rk_sum * xv
    out_ref[...] = ((on + bonus) * gate).astype(out_ref.dtype).reshape(nb, T, W)

    @pl.when(c == nchunks - 1)
    def _():
        s1_ref[0] = s_scr[...].reshape(nb, nh, D, D)
        _zero_other_slabs(s1_ref)


def _rwkv(proj3, sh, s0, l_in, l_out, prev_out, mu, w0, wup, a0, aup, gup, kkw, kaw, rkw, lng, lnb,
          ones, *, nb):
    nseq, seqlen, _ = proj3.shape
    T = min(RWKV_CHUNK, seqlen)
    nc = seqlen // T
    W = RWKV_WIDTH
    const = lambda b, c: (0, 0)
    rowblk = lambda col, width: pl.BlockSpec((nb, T, width), lambda b, c: (b, c, col // width))
    sdims = (RWKV_HEADS, HEAD_DIM, HEAD_DIM)
    vec = pl.BlockSpec((1, W), const)
    in_specs = [rowblk(C_R, W), rowblk(C_K, W), rowblk(C_V, W), rowblk(C_G, G_PAD),
                rowblk(C_WA, WA_PAD),
                pl.BlockSpec((nb, 1, RW_PACK), lambda b, c: (b, 0, 0)),
                pl.BlockSpec((1, nb) + sdims, lambda b, c: (l_in, b, 0, 0, 0)),
                pl.BlockSpec((1, RW_PACK), const),
                vec, pl.BlockSpec((WA_PAD, W), const),
                vec, pl.BlockSpec((WA_PAD, W), const),
                pl.BlockSpec((G_PAD, W), const),
                vec, vec, vec, vec, vec,
                pl.BlockSpec((2 * LANES, 2 * LANES), const)]
    rows = nb * T
    return _stacked_state_call(
        functools.partial(_rwkv_kernel, nb=nb, T=T, nchunks=nc), (nseq // nb, nc), in_specs,
        [proj3, proj3, proj3, proj3, proj3, sh, s0, mu, w0, wup, a0, aup, gup, kkw, kaw, rkw, lng,
         lnb, ones],
        pl.BlockSpec((nb, T, W), lambda b, c: (b, c, 0)),
        jax.ShapeDtypeStruct((nseq, seqlen, W), BF16),
        sdims, l_out, prev_out,
        [pltpu.VMEM((nb, RW_PACK), F32), pltpu.VMEM((nb * RWKV_HEADS, HEAD_DIM, HEAD_DIM), F32)]
        + [pltpu.VMEM((rows, W), F32)] * 7
        + [pltpu.VMEM((rows, W), F32), pltpu.VMEM((rows, W), F32)],
        "rwkv", nb=nb)


def _zeros_like_cols(x, n):
    return jnp.zeros(x.shape[:-1] + (n,), x.dtype)


def _pack_rwkv_cols(x):
    return jnp.concatenate([x[..., :_O_XW], x[..., _O_XG:], _zeros_like_cols(x, G_PAD - RWKV_R_G),
                            x[..., _O_XW:_O_XG]], axis=-1)


def _pack_tail_cols(w):
    return jnp.concatenate([_pack_rwkv_cols(w[..., _O_RW:]), w[..., _O_DT:_O_RW],
                            _zeros_like_cols(w, DT_PAD - SSD_HEADS)], axis=-1)


def _pad_rows(x, n_before, n_total):
    b, r, c = x.shape
    return jnp.concatenate([jnp.zeros((b, n_before, c), x.dtype), x,
                            jnp.zeros((b, n_total - n_before - r, c), x.dtype)], axis=1)


def _small_params(l, p):
    row = lambda v: v.reshape(1, -1)
    pad_lanes = lambda v, n: jnp.concatenate([v, jnp.zeros((n - v.shape[0],), v.dtype)]).reshape(1, n)
    zeros_w = jnp.zeros((RWKV_R_W, RWKV_WIDTH), F32)
    return dict(
        lru=(p['lru_conv_w'][l], row(p['lru_conv_b'][l]), p['lru_wa'][l].astype(BF16),
             row(p['lru_ba'][l]), p['lru_wx'][l].astype(BF16), row(p['lru_bx'][l]),
             row(p['lru_lambda'][l])),
        ssd=(p['ssd_conv_w'][l], row(p['ssd_conv_b'][l]), pad_lanes(p['ssd_dt_bias'][l], DT_PAD),
             pad_lanes(p['ssd_a_log'][l], DT_PAD), row(jnp.repeat(p['ssd_d'][l], HEAD_DIM)),
             row(p['ssd_norm_g'][l])),
        rwkv=(row(_pack_rwkv_cols(p['rwkv_mu'][l])), row(p['rwkv_w0'][l]),
              jnp.concatenate([p['rwkv_w_up'][l], zeros_w], axis=0).astype(BF16),
              row(p['rwkv_a0'][l]),
              jnp.concatenate([zeros_w, p['rwkv_a_up'][l]], axis=0).astype(BF16),
              jnp.concatenate([p['rwkv_g_up'][l],
                               jnp.zeros((G_PAD - RWKV_R_G, RWKV_WIDTH), F32)], axis=0).astype(BF16),
              row(p['rwkv_k_k'][l]), row(p['rwkv_k_a'][l]), row(p['rwkv_r_k'][l]),
              row(p['rwkv_ln_g'][l]), row(p['rwkv_ln_b'][l])),
        ln1=(row(p['ln1_g'][l]), row(p['ln1_b'][l])),
        ln2=(row(p['ln2_g'][l]), row(p['ln2_b'][l])),
    )


def _constants():
    lane = jnp.arange(SSD_WIDTH) // HEAD_DIM
    ehp = (jnp.arange(DT_PAD)[:, None] == lane[None, :]).astype(BF16)
    blk = jnp.arange(2 * LANES) // HEAD_DIM
    ones = (blk[:, None] == blk[None, :]).astype(BF16)
    return ehp, ones


def _layer(x, l, small_state, ssd_in, rw_in, prev_outs, sp, big, consts, *, nseq, seqlen, lru_nblk,
           rwkv_nb):
    lru_conv0, lru_h0, ssd_conv0, rw_shift0 = small_state
    ehp, ones = consts
    nbuf = CONV_WIDTH - 1
    proj = _proj(x, big['w_main'], big['w_tail'], l)

    lru_nseq = nseq // lru_nblk
    out_a, lru_h1 = _lru(
        proj, _pad_rows(lru_conv0, 0, SUBLANES).reshape(lru_nblk, lru_nseq * SUBLANES, LRU_WIDTH),
        lru_h0.reshape(lru_nblk, lru_nseq, LRU_WIDTH), *sp['lru'],
        nblk=lru_nblk, nseq=lru_nseq, seqlen=seqlen)
    out_b, ssd_out = _ssd(proj, _pad_rows(ssd_conv0, SUBLANES - nbuf, SUBLANES), ssd_in[0],
                          ssd_in[1], l, prev_outs[0], *sp['ssd'], ehp, nseq=nseq, seqlen=seqlen)
    p3 = proj.reshape(nseq, seqlen, N_PROJ)
    out_c, rw_out = _rwkv(p3, _pack_rwkv_cols(rw_shift0)[:, None, :], rw_in[0], rw_in[1], l,
                          prev_outs[1], *sp['rwkv'], ones, nb=rwkv_nb)
    out_c = out_c.reshape(nseq * seqlen, RWKV_WIDTH)

    y = _outproj_ln(out_a, out_b, out_c, big['w_out'], l, x, *sp['ln1'])
    y = _ffn_down_ln(_ffn_up(y, big['w_gate'], big['w_up'], l), big['w_down'], l, y, *sp['ln2'])

    tail = p3[:, seqlen - nbuf:, :]
    last = p3[:, seqlen - 1, :]
    rw_shift1 = jnp.concatenate([last[:, C_R:C_R + 3 * RWKV_WIDTH], last[:, C_WA:C_WA + WA_PAD],
                                 last[:, C_G:C_G + RWKV_R_G]], axis=-1)
    small_new = (tail[:, :, C_LX:C_LX + LRU_WIDTH], lru_h1.reshape(nseq, LRU_WIDTH),
                 tail[:, :, C_XBC:C_XBC + SSD_CONV_DIM], rw_shift1)
    return y, small_new, (ssd_out, rw_out)


def kernel(x_prompt, x_sample, state_lru_conv, state_lru_h, state_ssd_conv, state_ssd,
           state_rwkv_shift, state_rwkv, w_in, lru_conv_w, lru_conv_b, lru_wa, lru_ba, lru_wx,
           lru_bx, lru_lambda, ssd_conv_w, ssd_conv_b, ssd_dt_bias, ssd_a_log, ssd_d, ssd_norm_g,
           rwkv_mu, rwkv_w0, rwkv_w_up, rwkv_a0, rwkv_a_up, rwkv_g_up, rwkv_k_k, rwkv_k_a,
           rwkv_r_k, rwkv_ln_g, rwkv_ln_b, w_out, ln1_g, ln1_b, w_gate, w_up, w_down, ln2_g, ln2_b):
    params = dict(
        lru_conv_w=lru_conv_w, lru_conv_b=lru_conv_b, lru_wa=lru_wa, lru_ba=lru_ba,
        lru_wx=lru_wx, lru_bx=lru_bx, lru_lambda=lru_lambda, ssd_conv_w=ssd_conv_w,
        ssd_conv_b=ssd_conv_b, ssd_dt_bias=ssd_dt_bias, ssd_a_log=ssd_a_log, ssd_d=ssd_d,
        ssd_norm_g=ssd_norm_g, rwkv_mu=rwkv_mu, rwkv_w0=rwkv_w0, rwkv_w_up=rwkv_w_up,
        rwkv_a0=rwkv_a0, rwkv_a_up=rwkv_a_up, rwkv_g_up=rwkv_g_up, rwkv_k_k=rwkv_k_k,
        rwkv_k_a=rwkv_k_a, rwkv_r_k=rwkv_r_k.reshape(DEPTH, RWKV_WIDTH), rwkv_ln_g=rwkv_ln_g,
        rwkv_ln_b=rwkv_ln_b, ln1_g=ln1_g, ln1_b=ln1_b, ln2_g=ln2_g, ln2_b=ln2_b)
    big = dict(w_main=jnp.swapaxes(w_in, 1, 2),
               w_tail=jnp.swapaxes(_pack_tail_cols(w_in), 1, 2),
               w_out=w_out.astype(BF16),
               w_gate=w_gate, w_up=w_up, w_down=w_down.astype(BF16))
    bp, lp_len, _ = x_prompt.shape
    bs, ls_len, _ = x_sample.shape
    consts = _constants()
    nbuf = CONV_WIDTH - 1
    zero_small = (jnp.zeros((bp, nbuf, LRU_WIDTH), F32), jnp.zeros((bp, LRU_WIDTH), F32),
                  jnp.zeros((bp, nbuf, SSD_CONV_DIM), F32), jnp.zeros((bp, RWKV_SHIFT), F32))
    zero_ssd = jnp.zeros((1, bp, SSD_HEADS, HEAD_DIM, SSD_STATE), F32)
    zero_rw = jnp.zeros((1, bp, RWKV_HEADS, HEAD_DIM, HEAD_DIM), F32)
    yp = x_prompt.reshape(bp * lp_len, D_MODEL)
    ys = x_sample.reshape(bs * ls_len, D_MODEL)
    new_p = [[] for _ in range(4)]
    new_s = [[] for _ in range(4)]
    outs_p = (None, None)
    outs_s = (None, None)
    for l in range(DEPTH):
        sp = _small_params(l, params)
        yp, small_p, outs_p = _layer(yp, l, zero_small, (zero_ssd, 0), (zero_rw, 0), outs_p, sp,
                                     big, consts, nseq=bp, seqlen=lp_len, lru_nblk=bp,
                                     rwkv_nb=RWKV_PROMPT_NB)
        ys, small_s, outs_s = _layer(
            ys, l, (state_lru_conv[l], state_lru_h[l], state_ssd_conv[l], state_rwkv_shift[l]),
            (state_ssd, l), (state_rwkv, l), outs_s, sp, big, consts,
            nseq=bs, seqlen=ls_len, lru_nblk=1, rwkv_nb=RWKV_SAMPLE_NB)
        for i in range(4):
            new_p[i].append(small_p[i])
            new_s[i].append(small_s[i])
    p_lru_conv, p_lru_h, p_ssd_conv, p_rw_shift = [jnp.stack(v) for v in new_p]
    s_lru_conv, s_lru_h, s_ssd_conv, s_rw_shift = [jnp.stack(v) for v in new_s]
    return (yp.reshape(bp, lp_len, D_MODEL), ys.reshape(bs, ls_len, D_MODEL),
            p_lru_conv, p_lru_h, p_ssd_conv, outs_p[0], p_rw_shift, outs_p[1],
            s_lru_conv, s_lru_h, s_ssd_conv, outs_s[0], s_rw_shift, outs_s[1])
```

```python
import functools
import math

import jax
import jax.numpy as jnp
from jax import lax
from jax.experimental import pallas as pl
from jax.experimental.pallas import tpu as pltpu

F32 = jnp.float32
BF16 = jnp.bfloat16

D_MODEL = 2048
DEPTH = 2
D_MIX = 2 * D_MODEL
HEAD_DIM = 64
CONV_WIDTH = 4
LRU_WIDTH = D_MIX // 4
LRU_HEADS = 8
LRU_BLOCK = LRU_WIDTH // LRU_HEADS
LRU_C = 8.0
SSD_WIDTH = D_MIX // 2
SSD_HEADS = SSD_WIDTH // HEAD_DIM
SSD_GROUPS = 8
SSD_STATE = 128
SSD_CHUNK = 128
SSD_CONV_DIM = SSD_WIDTH + 2 * SSD_GROUPS * SSD_STATE
SSD_GROUP_WIDTH = SSD_WIDTH // SSD_GROUPS
SSD_HEADS_PER_GROUP = SSD_HEADS // SSD_GROUPS
RWKV_WIDTH = D_MIX - LRU_WIDTH - SSD_WIDTH
RWKV_HEADS = RWKV_WIDTH // HEAD_DIM
RWKV_R_W = max(32, int(round(1.8 * RWKV_WIDTH ** 0.5 / 32)) * 32)
RWKV_R_A = max(32, int(round(1.8 * RWKV_WIDTH ** 0.5 / 32)) * 32)
RWKV_R_G = max(32, int(round(0.6 * RWKV_WIDTH ** 0.8 / 32)) * 32)
RWKV_SHIFT = 3 * RWKV_WIDTH + RWKV_R_W + RWKV_R_A + RWKV_R_G
RWKV_GN_EPS = 64e-5
RWKV_CHUNK = 64
RWKV_PROMPT_NB = 2
RWKV_SAMPLE_NB = 8
D_FF = -(-(8 * D_MODEL) // (3 * 256)) * 256
ALPHA = (2 * DEPTH) ** 0.25
LN_EPS = 1e-5

LANES = 128
SUBLANES = 8
VMEM_LIMIT = 56 * 1024 * 1024

C_GATE = 0
C_LX = C_GATE + LRU_WIDTH
C_Z = C_LX + LRU_WIDTH
C_XBC = C_Z + SSD_WIDTH
C_R = C_XBC + SSD_CONV_DIM
C_K = C_R + RWKV_WIDTH
C_V = C_K + RWKV_WIDTH
C_G = C_V + RWKV_WIDTH
G_PAD = 2 * LANES
C_WA = C_G + G_PAD
WA_PAD = LANES
C_DT = C_WA + WA_PAD
DT_PAD = LANES
N_PROJ = C_DT + DT_PAD
assert RWKV_R_G <= G_PAD and RWKV_R_W + RWKV_R_A == WA_PAD and SSD_HEADS <= DT_PAD
RW_PACK = 3 * RWKV_WIDTH + G_PAD + WA_PAD

_O_DT = 2 * LRU_WIDTH + SSD_WIDTH + SSD_CONV_DIM
_O_RW = _O_DT + SSD_HEADS
_O_XW = 3 * RWKV_WIDTH
_O_XG = _O_XW + RWKV_R_W + RWKV_R_A


def _cparams(*sem):
    return pltpu.CompilerParams(dimension_semantics=sem, vmem_limit_bytes=VMEM_LIMIT)


_NN = (((1,), (0,)), ((), ()))
_NT = (((1,), (1,)), ((), ()))
_TN = (((0,), (0,)), ((), ()))


def _dg(a, b, dims):
    return lax.dot_general(a, b, dims, preferred_element_type=F32)


def _dot1(a, b, dims=_NN):
    return _dg(a.astype(BF16), b.astype(BF16), dims)


def _split2(x):
    hi = x.astype(BF16)
    lo = (x - hi.astype(F32)).astype(BF16)
    return hi, lo


def _split3(x):
    hi = x.astype(BF16)
    r1 = x - hi.astype(F32)
    mid = r1.astype(BF16)
    lo = (r1 - mid.astype(F32)).astype(BF16)
    return hi, mid, lo


def _dot3(a, b, dims=_NN):
    ah, al = _split2(a)
    bh, bl = _split2(b)
    free_axis = 1 if dims == _TN else 0
    m = a.shape[free_axis]
    both = _dg(jnp.concatenate([ah, al], axis=free_axis), bh, dims)
    return (both[:m] + both[m:]) + _dg(ah, bl, dims)


def _dot_shared_rhs(parts, w_exact, npieces):
    pieces = []
    for p in parts:
        rest = p
        for _ in range(npieces):
            piece = rest.astype(BF16).astype(F32)
            pieces.append(piece)
            rest = rest - piece
    prod = _dg(jnp.concatenate(pieces, axis=0).astype(BF16), w_exact, _NN)
    outs = []
    off = 0
    for p in parts:
        r = p.shape[0]
        acc = prod[off:off + r]
        for i in range(1, npieces):
            acc = acc + prod[off + i * r:off + (i + 1) * r]
        outs.append(acc)
        off += npieces * r
    return outs


def _dotx_left(w_exact, a, npieces=3):
    if npieces == 2:
        hi, lo = _split2(a)
        return _dg(w_exact, hi, _NN) + _dg(w_exact, lo, _NN)
    hi, mid, lo = _split3(a)
    return _dg(w_exact, hi, _NN) + (_dg(w_exact, mid, _NN) + _dg(w_exact, lo, _NN))


def _iota2(shape, dim):
    return lax.broadcasted_iota(jnp.int32, shape, dim)


def _softplus(x):
    return jnp.maximum(x, 0.0) + jnp.log1p(jnp.exp(-jnp.abs(x)))


def _sigmoid(x):
    return 1.0 / (1.0 + jnp.exp(-x))


def _silu(x):
    return x * _sigmoid(x)


def _gelu_tanh(x):
    c = math.sqrt(2.0 / math.pi)
    return 0.5 * x * (1.0 + jnp.tanh(c * (x + 0.044715 * (x * x * x))))


def _layer_norm(y, g, b):
    mu = jnp.mean(y, axis=-1, keepdims=True)
    d = y - mu
    var = jnp.mean(d * d, axis=-1, keepdims=True)
    return d * lax.rsqrt(var + LN_EPS) * g + b


DENSE_TM = 2048
PROJ_TN = 256
N_MAIN = C_R
N_TAIL = N_PROJ - N_MAIN
assert N_MAIN == _O_DT and N_MAIN % PROJ_TN == 0 and N_TAIL % PROJ_TN == 0


def _proj_kernel(x_ref, wm_ref, wt_ref, o_ref, xb_ref):
    j = pl.program_id(1)

    @pl.when(j == 0)
    def _():
        xb_ref[...] = x_ref[...].astype(BF16)

    @pl.when(j < N_MAIN // PROJ_TN)
    def _():
        o_ref[...] = _dg(xb_ref[...], wm_ref[0].astype(BF16), _NT)

    @pl.when(j >= N_MAIN // PROJ_TN)
    def _():
        o_ref[...] = _dg(xb_ref[...], wt_ref[0].astype(BF16), _NT)


def _proj(x, w_main_t, w_tail_t, l):
    m = x.shape[0]
    tm = min(m, DENSE_TM)
    tn = PROJ_TN
    nmain = N_MAIN // tn
    return pl.pallas_call(
        _proj_kernel,
        grid=(m // tm, N_PROJ // tn),
        in_specs=[pl.BlockSpec((tm, D_MODEL), lambda i, j: (i, 0), pipeline_mode=pl.Buffered(1)),
                  pl.BlockSpec((1, tn, D_MODEL), lambda i, j: (l, jnp.minimum(j, nmain - 1), 0)),
                  pl.BlockSpec((1, tn, D_MODEL), lambda i, j: (l, jnp.maximum(j - nmain, 0), 0))],
        out_specs=pl.BlockSpec((tm, tn), lambda i, j: (i, j)),
        out_shape=jax.ShapeDtypeStruct((m, N_PROJ), F32),
        scratch_shapes=[pltpu.VMEM((tm, D_MODEL), BF16)],
        compiler_params=_cparams("arbitrary", "arbitrary"),
        name="proj",
    )(x, w_main_t, w_tail_t)


OUT_KSTEPS = D_MIX // LRU_WIDTH


def _outproj_kernel(a_ref, b_ref, c_ref, w_ref, x_ref, g_ref, beta_ref, o_ref, acc_ref):
    k = pl.program_id(1)

    @pl.when(k == 0)
    def _():
        acc_ref[...] = jnp.dot(a_ref[...], w_ref[0], preferred_element_type=F32)

    @pl.when(jnp.logical_and(k > 0, k < OUT_KSTEPS - 1))
    def _():
        acc_ref[...] += jnp.dot(b_ref[...], w_ref[0], preferred_element_type=F32)

    @pl.when(k == OUT_KSTEPS - 1)
    def _():
        mix = acc_ref[...] + jnp.dot(c_ref[...], w_ref[0], preferred_element_type=F32)
        o_ref[...] = _layer_norm(ALPHA * x_ref[...] + mix, g_ref[...], beta_ref[...])


def _outproj_ln(out_a, out_b, out_c, w_out, l, x, g, beta):
    m = x.shape[0]
    tm = min(m, 512)
    tk = LRU_WIDTH
    nb = SSD_WIDTH // tk
    return pl.pallas_call(
        _outproj_kernel,
        grid=(m // tm, OUT_KSTEPS),
        in_specs=[pl.BlockSpec((tm, tk), lambda i, k: (i, 0)),
                  pl.BlockSpec((tm, tk), lambda i, k: (i, jnp.clip(k - 1, 0, nb - 1))),
                  pl.BlockSpec((tm, tk), lambda i, k: (i, 0)),
                  pl.BlockSpec((1, tk, D_MODEL), lambda i, k: (l, k, 0)),
                  pl.BlockSpec((tm, D_MODEL), lambda i, k: (i, 0)),
                  pl.BlockSpec((1, D_MODEL), lambda i, k: (0, 0)),
                  pl.BlockSpec((1, D_MODEL), lambda i, k: (0, 0))],
        out_specs=pl.BlockSpec((tm, D_MODEL), lambda i, k: (i, 0)),
        out_shape=jax.ShapeDtypeStruct((m, D_MODEL), F32),
        scratch_shapes=[pltpu.VMEM((tm, D_MODEL), F32)],
        compiler_params=_cparams("arbitrary", "arbitrary"),
        name="outproj_ln",
    )(out_a, out_b, out_c, w_out, x, g, beta)


def _ffn_up_kernel(x_ref, wg_ref, wu_ref, o_ref, xb_ref):
    @pl.when(pl.program_id(1) == 0)
    def _():
        xb_ref[...] = x_ref[...].astype(BF16)

    xb = xb_ref[...]
    gate = jnp.dot(xb, wg_ref[0].astype(BF16), preferred_element_type=F32)
    up = jnp.dot(xb, wu_ref[0].astype(BF16), preferred_element_type=F32)
    o_ref[...] = (_silu(gate) * up).astype(BF16)


def _ffn_up(x, wg, wu, l):
    m = x.shape[0]
    tm = min(m, DENSE_TM)
    tn = 256
    return pl.pallas_call(
        _ffn_up_kernel,
        grid=(m // tm, D_FF // tn),
        in_specs=[pl.BlockSpec((tm, D_MODEL), lambda i, j: (i, 0)),
                  pl.BlockSpec((1, D_MODEL, tn), lambda i, j: (l, 0, j)),
                  pl.BlockSpec((1, D_MODEL, tn), lambda i, j: (l, 0, j))],
        out_specs=pl.BlockSpec((tm, tn), lambda i, j: (i, j)),
        out_shape=jax.ShapeDtypeStruct((m, D_FF), BF16),
        scratch_shapes=[pltpu.VMEM((tm, D_MODEL), BF16)],
        compiler_params=_cparams("arbitrary", "arbitrary"),
        name="ffn_up",
    )(x, wg, wu)


FFN_DOWN_KSTEPS = 2


LN_ROW_CHUNK = 128


def _ffn_down_kernel(h_ref, w_ref, x_ref, g_ref, beta_ref, o_ref):
    k = pl.program_id(1)

    @pl.when(k == 0)
    def _():
        o_ref[...] = jnp.dot(h_ref[...], w_ref[0], preferred_element_type=F32)

    @pl.when(k > 0)
    def _():
        o_ref[...] += jnp.dot(h_ref[...], w_ref[0], preferred_element_type=F32)

    @pl.when(k == FFN_DOWN_KSTEPS - 1)
    def _():
        chunk = min(LN_ROW_CHUNK, o_ref.shape[0])

        def ln_rows(r, carry):
            rows = pl.ds(pl.multiple_of(r * chunk, chunk), chunk)
            o_ref[rows, :] = _layer_norm(ALPHA * x_ref[rows, :] + o_ref[rows, :], g_ref[...],
                                         beta_ref[...])
            return carry

        lax.fori_loop(0, o_ref.shape[0] // chunk, ln_rows, 0)


def _ffn_down_ln(h, w_down, l, x, g, beta):
    m = x.shape[0]
    tm = min(m, 512)
    tk = D_FF // FFN_DOWN_KSTEPS
    assert tk * FFN_DOWN_KSTEPS == D_FF and tk % LANES == 0 and tm % min(LN_ROW_CHUNK, tm) == 0
    return pl.pallas_call(
        _ffn_down_kernel,
        grid=(m // tm, FFN_DOWN_KSTEPS),
        in_specs=[pl.BlockSpec((tm, tk), lambda i, k: (i, k)),
                  pl.BlockSpec((1, tk, D_MODEL), lambda i, k: (l, k, 0)),
                  pl.BlockSpec((tm, D_MODEL), lambda i, k: (i, 0)),
                  pl.BlockSpec((1, D_MODEL), lambda i, k: (0, 0)),
                  pl.BlockSpec((1, D_MODEL), lambda i, k: (0, 0))],
        out_specs=pl.BlockSpec((tm, D_MODEL), lambda i, k: (i, 0)),
        out_shape=jax.ShapeDtypeStruct((m, D_MODEL), F32),
        compiler_params=_cparams("arbitrary", "arbitrary"),
        name="ffn_down_ln",
    )(h, w_down, x, g, beta)


def _conv_taps(u, first_rows, cw, cb, t_in_seq):
    out = cb + cw[CONV_WIDTH - 1:CONV_WIDTH, :] * u
    for j in range(1, CONV_WIDTH):
        shifted = jnp.where(t_in_seq < j, first_rows(j), pltpu.roll(u, j, 0))
        out = out + cw[CONV_WIDTH - 1 - j:CONV_WIDTH - j, :] * shifted
    return out


def _lru_kernel(gate_ref, lx_ref, buf_ref, h0_ref, cw_ref, cb_ref, wa_ref, ba_ref, wx_ref,
                bx_ref, lam_ref, out_ref, h1_ref, a_scr, b_scr, hin_scr, *, nseq, seqlen):
    rows = nseq * seqlen
    u = lx_ref[...]
    t_in_seq = _iota2((rows, LANES), 0) % seqlen
    if nseq == 1:
        buf = jnp.concatenate([buf_ref[0], jnp.zeros((rows - SUBLANES, LANES), F32)], axis=0) \
            if rows > SUBLANES else buf_ref[0]
    else:
        buf = buf_ref[0]
    nbuf = CONV_WIDTH - 1
    xc = _conv_taps(u, lambda j: pltpu.roll(buf, (j - nbuf) % rows, 0), cw_ref[...], cb_ref[...],
                    t_in_seq)

    r = _sigmoid(_dot1(xc, wa_ref[0]) + ba_ref[...])
    i = _sigmoid(_dot1(xc, wx_ref[0]) + bx_ref[...])
    log_a = (-LRU_C) * r * _softplus(-lam_ref[...])
    a = jnp.exp(log_a)
    b = jnp.sqrt(jnp.tanh(-log_a) * (a * a + 1.0)) * (i * xc)

    t8 = _iota2((rows, LANES), 0) % SUBLANES
    for s in (1, 2, 4):
        m = t8 >= s
        a_sh = pltpu.roll(a, s, 0)
        b_sh = pltpu.roll(b, s, 0)
        b = jnp.where(m, a * b_sh + b, b)
        a = jnp.where(m, a * a_sh, a)

    if seqlen == SUBLANES:
        h0 = h0_ref[0]
        hin = jnp.broadcast_to(h0[:, None, :], (nseq, SUBLANES, LANES)).reshape(rows, LANES)
        h = a * hin + b
        out_ref[...] = (h * _gelu_tanh(gate_ref[...])).astype(out_ref.dtype)
        a_scr[...] = h
        h1_ref[0] = a_scr[pl.ds(SUBLANES - 1, nseq, stride=SUBLANES), :]
    else:
        assert nseq == 1
        a_scr[...] = a
        b_scr[...] = b

        def carry_step(g, carry):
            base = pl.multiple_of(g * SUBLANES, SUBLANES)
            hin_scr[pl.ds(base, SUBLANES), :] = jnp.broadcast_to(carry, (SUBLANES, LANES))
            a7 = a_scr[pl.ds(base + SUBLANES - 1, 1), :]
            b7 = b_scr[pl.ds(base + SUBLANES - 1, 1), :]
            return a7 * carry + b7

        last = lax.fori_loop(0, rows // SUBLANES, carry_step, h0_ref[0])
        h = a_scr[...] * hin_scr[...] + b_scr[...]
        out_ref[...] = (h * _gelu_tanh(gate_ref[...])).astype(out_ref.dtype)
        h1_ref[0] = last


def _lru(proj, buf, h0, cw, cb, wa, ba, wx, bx, lam, *, nblk, nseq, seqlen):
    rows = nseq * seqlen
    gate_blk = C_GATE // LRU_BLOCK
    lx_blk = C_LX // LRU_BLOCK
    row = lambda s, h: (0, h)
    return pl.pallas_call(
        functools.partial(_lru_kernel, nseq=nseq, seqlen=seqlen),
        grid=(nblk, LRU_HEADS),
        in_specs=[pl.BlockSpec((rows, LRU_BLOCK), lambda s, h: (s, gate_blk + h)),
                  pl.BlockSpec((rows, LRU_BLOCK), lambda s, h: (s, lx_blk + h)),
                  pl.BlockSpec((1, nseq * SUBLANES, LRU_BLOCK), lambda s, h: (s, 0, h)),
                  pl.BlockSpec((1, nseq, LRU_BLOCK), lambda s, h: (s, 0, h)),
                  pl.BlockSpec((CONV_WIDTH, LRU_BLOCK), row),
                  pl.BlockSpec((1, LRU_BLOCK), row),
                  pl.BlockSpec((1, LRU_BLOCK, LRU_BLOCK), lambda s, h: (h, 0, 0)),
                  pl.BlockSpec((1, LRU_BLOCK), row),
                  pl.BlockSpec((1, LRU_BLOCK, LRU_BLOCK), lambda s, h: (h, 0, 0)),
                  pl.BlockSpec((1, LRU_BLOCK), row),
                  pl.BlockSpec((1, LRU_BLOCK), row)],
        out_specs=[pl.BlockSpec((rows, LRU_BLOCK), lambda s, h: (s, h)),
                   pl.BlockSpec((1, nseq, LRU_BLOCK), lambda s, h: (s, 0, h))],
        out_shape=[jax.ShapeDtypeStruct((nblk * rows, LRU_WIDTH), BF16),
                   jax.ShapeDtypeStruct((nblk, nseq, LRU_WIDTH), F32)],
        scratch_shapes=[pltpu.VMEM((rows, LANES), F32)] * 3,
        compiler_params=_cparams("arbitrary", "arbitrary"),
        name="lru",
    )(proj, proj, buf, h0, cw, cb, wa, ba, wx, bx, lam)


def _stacked_state_call(kernel_fn, grid, in_specs, operands, row_spec, row_shape, state_dims,
                        l_out, prev_out, scratch_shapes, name, nb=1):
    nseq = grid[0] * nb
    zeros = (0,) * len(state_dims)
    state_shape = jax.ShapeDtypeStruct((DEPTH, nseq) + tuple(state_dims), F32)
    aliases = {}
    if prev_out is None:
        assert l_out == 0
        state_spec = pl.BlockSpec((DEPTH, nb) + tuple(state_dims), lambda b, c: (0, b) + zeros)
    else:
        state_spec = pl.BlockSpec((1, nb) + tuple(state_dims), lambda b, c: (l_out, b) + zeros)
        in_specs = in_specs + [pl.BlockSpec(memory_space=pl.ANY)]
        operands = operands + [prev_out]
        aliases = {len(operands) - 1: 1}
        kernel_fn = functools.partial(_drop_alias_ref, kernel_fn, len(operands) - 1)
    return pl.pallas_call(
        kernel_fn, grid=grid, in_specs=in_specs, out_specs=[row_spec, state_spec],
        out_shape=[row_shape, state_shape], scratch_shapes=scratch_shapes,
        input_output_aliases=aliases, compiler_params=_cparams("arbitrary", "arbitrary"),
        name=name)(*operands)


def _drop_alias_ref(kernel_fn, pos, *refs):
    return kernel_fn(*refs[:pos], *refs[pos + 1:])


def _zero_other_slabs(state_ref):
    if state_ref.shape[0] > 1:
        state_ref[1:] = jnp.zeros((state_ref.shape[0] - 1,) + state_ref.shape[1:], state_ref.dtype)


def _transpose_rows_to_lanes(x, t):
    if t < LANES:
        x = jnp.concatenate([x, jnp.zeros((LANES - t, LANES), F32)], axis=0)
    return x.T[:, :t]


def _ssd_kernel(z_ref, xbc_ref, dt_ref, buf_ref, s0_ref, cw_ref, cb_ref, dtb_ref, alog_ref,
                dch_ref, ng_ref, ehp_ref, out_ref, s1_ref, tail_ref, pad_ref, y_scr, *, T):
    c = pl.program_id(1)
    n_state = SSD_STATE
    gw = SSD_GROUP_WIDTH
    hpg = SSD_HEADS_PER_GROUP

    @pl.when(c == 0)
    def _():
        tail_ref[...] = buf_ref[0]
        s1_ref[0:1] = s0_ref[...]
        _zero_other_slabs(s1_ref)

    u = xbc_ref[...]
    pad_ref[0:SUBLANES, :] = tail_ref[...]
    pad_ref[SUBLANES:SUBLANES + T, :] = u
    cw = cw_ref[...]
    xc = cb_ref[...] + cw[CONV_WIDTH - 1:CONV_WIDTH, :] * u
    for j in range(1, CONV_WIDTH):
        xc = xc + cw[CONV_WIDTH - 1 - j:CONV_WIDTH - j, :] * pad_ref[SUBLANES - j:SUBLANES - j + T, :]
    tail_ref[...] = pad_ref[T:T + SUBLANES, :]

    xa = _silu(xc)
    xs = xa[:, :SSD_WIDTH]
    bm = xa[:, SSD_WIDTH:SSD_WIDTH + SSD_GROUPS * n_state]
    cm = xa[:, SSD_WIDTH + SSD_GROUPS * n_state:]
    dt = _softplus(dt_ref[...] + dtb_ref[...])
    da = dt * (-jnp.exp(alog_ref[...]))
    ii = _iota2((T, T), 0)
    jj = _iota2((T, T), 1)
    causal = ii >= jj
    cum = _dotx_left(causal.astype(BF16), da)
    cum_t = _transpose_rows_to_lanes(cum, T)
    ehp = ehp_ref[...]
    cumx, dtx = _dot_shared_rhs([cum, dt], ehp, 2)
    xdt = xs * dtx
    ecum = jnp.exp(cumx)
    xdtd = xdt * jnp.exp(cumx[T - 1:T, :] - cumx)

    groups = range(SSD_GROUPS)
    heads = range(SSD_HEADS)
    cg = [cm[:, g * n_state:(g + 1) * n_state] for g in groups]
    bg = [bm[:, g * n_state:(g + 1) * n_state] for g in groups]
    sg = [s1_ref[0, 0, g * hpg:(g + 1) * hpg].reshape(gw, n_state) for g in groups]
    cb = [_dot1(cg[g], bg[g], _NT) for g in groups]
    y_off = [_dot1(cg[g], sg[g], _NT) for g in groups]
    st = [_dot1(xdtd[:, g * gw:(g + 1) * gw], bg[g], _TN) for g in groups]
    lm = [jnp.where(causal, jnp.exp(cum[:, h:h + 1] - cum_t[h:h + 1, :]), 0.0) for h in heads]
    y_diag = [_dot1(cb[h // hpg] * lm[h], xdt[:, h * HEAD_DIM:(h + 1) * HEAD_DIM]) for h in heads]
    for g in groups:
        y_scr[:, g * gw:(g + 1) * gw] = (jnp.concatenate(y_diag[g * hpg:(g + 1) * hpg], axis=1)
                                         + y_off[g] * ecum[:, g * gw:(g + 1) * gw])
        decay = jnp.concatenate(
            [jnp.broadcast_to(jnp.exp(cum_t[h:h + 1, T - 1:T]), (HEAD_DIM, n_state))
             for h in range(g * hpg, (g + 1) * hpg)], axis=0)
        s1_ref[0, 0, g * hpg:(g + 1) * hpg] = (sg[g] * decay + st[g]).reshape(hpg, HEAD_DIM, n_state)

    y = (y_scr[...] + dch_ref[...] * xs) * _silu(z_ref[...])
    outs = []
    for g in groups:
        yg = y[:, g * gw:(g + 1) * gw]
        ms = jnp.mean(yg * yg, axis=-1, keepdims=True)
        outs.append(yg * lax.rsqrt(ms + 1e-5))
    out_ref[...] = (jnp.concatenate(outs, axis=1) * ng_ref[...]).astype(out_ref.dtype)


def _ssd(proj, buf, s0, l_in, l_out, prev_out, cw, cb, dtb, alog, dch, ng, ehp, *, nseq, seqlen):
    T = min(SSD_CHUNK, seqlen)
    nc = seqlen // T
    const = lambda b, c: (0, 0)
    sblk = (1, 1, SSD_HEADS, HEAD_DIM, SSD_STATE)
    in_specs = [pl.BlockSpec((T, SSD_WIDTH), lambda b, c: (b * nc + c, C_Z // SSD_WIDTH)),
                pl.BlockSpec((T, SSD_CONV_DIM), lambda b, c: (b * nc + c, C_XBC // SSD_CONV_DIM)),
                pl.BlockSpec((T, DT_PAD), lambda b, c: (b * nc + c, C_DT // DT_PAD)),
                pl.BlockSpec((1, SUBLANES, SSD_CONV_DIM), lambda b, c: (b, 0, 0)),
                pl.BlockSpec(sblk, lambda b, c: (l_in, b, 0, 0, 0)),
                pl.BlockSpec((CONV_WIDTH, SSD_CONV_DIM), const),
                pl.BlockSpec((1, SSD_CONV_DIM), const),
                pl.BlockSpec((1, DT_PAD), const),
                pl.BlockSpec((1, DT_PAD), const),
                pl.BlockSpec((1, SSD_WIDTH), const),
                pl.BlockSpec((1, SSD_WIDTH), const),
                pl.BlockSpec((DT_PAD, SSD_WIDTH), const)]
    return _stacked_state_call(
        functools.partial(_ssd_kernel, T=T), (nseq, nc), in_specs,
        [proj, proj, proj, buf, s0, cw, cb, dtb, alog, dch, ng, ehp],
        pl.BlockSpec((T, SSD_WIDTH), lambda b, c: (b * nc + c, 0)),
        jax.ShapeDtypeStruct((nseq * seqlen, SSD_WIDTH), BF16),
        sblk[2:], l_out, prev_out,
        [pltpu.VMEM((SUBLANES, SSD_CONV_DIM), F32),
         pltpu.VMEM((T + SUBLANES, SSD_CONV_DIM), F32),
         pltpu.VMEM((T, SSD_WIDTH), F32)],
        "ssd")


def _unit_lower_inverse(a_list, ii, jj, T):
    n = len(a_list)
    pair = (ii >> 1) == (jj >> 1)
    inv = [jnp.where(ii == jj, 1.0, 0.0) + jnp.where(pair, a_list[h], 0.0) for h in range(n)]
    shift = 1
    while (2 << shift) <= T:
        band = jnp.logical_and((ii >> (shift + 1)) == (jj >> (shift + 1)),
                               (ii >> shift) != (jj >> shift))
        x = [_dot1(jnp.where(band, a_list[h], 0.0), inv[h]) for h in range(n)]
        inv = [inv[h] + _dot1(inv[h], x[h]) for h in range(n)]
        shift += 1
    return inv


def _rwkv_kernel(r_ref, k_ref, v_ref, g_ref, wa_ref, sh_ref, s0_ref, mu_ref, w0_ref, wup_ref,
                 a0_ref, aup_ref, gup_ref, kkw_ref, kaw_ref, rkw_ref, lng_ref, lnb_ref, ones_ref,
                 out_ref, s1_ref, prev_scr, s_scr, al_scr, be_scr, kt_scr, rt_scr, bs_scr,
                 ks_scr, v_scr, gam_scr, o_scr, *, nb, T, nchunks):
    c = pl.program_id(1)
    W = RWKV_WIDTH
    D = HEAD_DIM
    R = nb * T
    nh = RWKV_HEADS

    @pl.when(c == 0)
    def _():
        prev_scr[...] = sh_ref[...].reshape(nb, RW_PACK)
        s_scr[...] = s0_ref[0].reshape(nb * nh, D, D)

    def rows_of(ref3):
        return ref3[...].reshape(R, ref3.shape[-1])

    def per_seq_rows(x):
        return jnp.broadcast_to(x[:, None, :], (nb, T, x.shape[-1])).reshape(R, x.shape[-1])

    def last_rows(x):
        return jnp.concatenate([x[(b + 1) * T - 1:(b + 1) * T, :] for b in range(nb)], axis=0)

    def token_shift(p, lo, hi):
        first = _iota2(p.shape, 0) % T == 0
        prev = jnp.where(first, per_seq_rows(prev_scr[:, lo:hi]), pltpu.roll(p, 1, 0))
        prev_scr[:, lo:hi] = last_rows(p)
        return p + (prev - p) * mu_ref[:, lo:hi]

    xr = token_shift(rows_of(r_ref), 0, W)
    xk = token_shift(rows_of(k_ref), W, 2 * W)
    xv = token_shift(rows_of(v_ref), 2 * W, 3 * W)
    xg = token_shift(rows_of(g_ref), 3 * W, 3 * W + G_PAD)
    xwa = token_shift(rows_of(wa_ref), 3 * W + G_PAD, RW_PACK)

    w_lin = w0_ref[...] + _dot1(jnp.tanh(xwa), wup_ref[...])
    a = _sigmoid(a0_ref[...] + _dot1(xwa, aup_ref[...]))
    gate = _dot1(_sigmoid(xg), gup_ref[...])
    lw = -jnp.exp(-_softplus(-w_lin) - 0.5)

    ones = ones_ref[...]

    def head_sums(xs):
        tw = ones.shape[0]
        nt = W // tw
        tiles = [x[:, i * tw:(i + 1) * tw] for x in xs for i in range(nt)]
        sums = _dot_shared_rhs(tiles, ones, 2)
        return [jnp.concatenate(sums[n * nt:(n + 1) * nt], axis=1) for n in range(len(xs))]

    kk = xk * kkw_ref[...]
    kp = xk * (1.0 + (a - 1.0) * kaw_ref[...])
    kk_sq, rk_sum = head_sums([kk * kk, xr * kp * rkw_ref[...]])
    kk = kk / jnp.maximum(jnp.sqrt(kk_sq), 1e-12)

    ri = _iota2((R, R), 0)
    rj = _iota2((R, R), 1)
    same_seq_causal = jnp.logical_and(ri >= rj, ri // T == rj // T)
    cum = _dotx_left(same_seq_causal.astype(BF16), lw, 2)
    e_neg = jnp.exp(-cum)
    gam = jnp.exp(per_seq_rows(last_rows(cum)))
    be = kk * a * e_neg
    kt = kp * e_neg
    al_scr[...] = -kk * jnp.exp(cum - lw)
    be_scr[...] = be
    kt_scr[...] = kt
    rt_scr[...] = xr * jnp.exp(cum)
    bs_scr[...] = be * gam
    ks_scr[...] = kt * gam
    v_scr[...] = xv
    gam_scr[...] = gam

    ii = _iota2((T, T), 0)
    jj = _iota2((T, T), 1)
    incl = ii >= jj
    strict = ii > jj
    eye_d = _iota2((D, D), 0) == _iota2((D, D), 1)

    units = [(b, h) for b in range(nb) for h in range(nh)]
    nu = range(len(units))

    def per_unit(ref):
        return [ref[b * T:(b + 1) * T, h * D:(h + 1) * D] for b, h in units]

    al, be_u, kt_u, rt = per_unit(al_scr), per_unit(be_scr), per_unit(kt_scr), per_unit(rt_scr)
    bs_u, ks_u, vv, gam_u = per_unit(bs_scr), per_unit(ks_scr), per_unit(v_scr), per_unit(gam_scr)
    gram = [_dot1(jnp.concatenate([al[u], rt[u]], axis=0),
                  jnp.concatenate([be_u[u], kt_u[u]], axis=0), _NT) for u in nu]
    a_ab = [jnp.where(strict, gram[u][:T, :T], 0.0) for u in nu]
    a_ak = [jnp.where(strict, gram[u][:T, T:], 0.0) for u in nu]
    r_b = [jnp.where(incl, gram[u][T:, :T], 0.0) for u in nu]
    r_k = [jnp.where(incl, gram[u][T:, T:], 0.0) for u in nu]
    inv = _unit_lower_inverse(a_ab, ii, jj, T)
    akv = [_dot1(a_ak[u], vv[u]) for u in nu]
    pw = [_dot1(inv[u], jnp.concatenate([al[u], akv[u]], axis=1)) for u in nu]
    qo = [_dot1(r_b[u], pw[u]) for u in nu]
    rkv = [_dot1(r_k[u], vv[u]) for u in nu]
    smat = [s_scr[u] for u in nu]
    qs = [_dot1(rt[u] + qo[u][:, :D], smat[u], _NT) for u in nu]
    gp = [_dot3(bs_u[u], pw[u][:, :D], _TN) for u in nu]
    zt = [_dot3(jnp.concatenate([pw[u][:, D:], vv[u]], axis=0),
                jnp.concatenate([bs_u[u], ks_u[u]], axis=0), _TN) for u in nu]
    gmat = [jnp.where(eye_d, jnp.broadcast_to(gam_u[u][0:1, :], (D, D)), 0.0) + gp[u] for u in nu]
    sg = [_dot3(smat[u], gmat[u], _NT) for u in nu]
    for u in nu:
        s_scr[u] = sg[u] + zt[u]
    pairs = LANES // D
    for b in range(nb):
        for p in range(nh // pairs):
            us = [b * nh + p * pairs + s for s in range(pairs)]
            o_scr[b * T:(b + 1) * T, p * LANES:(p + 1) * LANES] = jnp.concatenate(
                [qs[u] + qo[u][:, D:] + rkv[u] for u in us], axis=1)

    o = o_scr[...]
    mean = head_sums([o])[0] * (1.0 / D)
    d = o - mean
    var = head_sums([d * d])[0] * (1.0 / D)
    on = d * lax.rsqrt(var + RWKV_GN_EPS) * lng_ref[...] + lnb_ref[...]
    bonus = rk_sum * xv
    out_ref[...] = ((on + bonus) * gate).astype(out_ref.dtype).reshape(nb, T, W)

    @pl.when(c == nchunks - 1)
    def _():
        s1_ref[0] = s_scr[...].reshape(nb, nh, D, D)
        _zero_other_slabs(s1_ref)


def _rwkv(proj3, sh, s0, l_in, l_out, prev_out, mu, w0, wup, a0, aup, gup, kkw, kaw, rkw, lng, lnb,
          ones, *, nb):
    nseq, seqlen, _ = proj3.shape
    T = min(RWKV_CHUNK, seqlen)
    nc = seqlen // T
    W = RWKV_WIDTH
    const = lambda b, c: (0, 0)
    rowblk = lambda col, width: pl.BlockSpec((nb, T, width), lambda b, c: (b, c, col // width))
    sdims = (RWKV_HEADS, HEAD_DIM, HEAD_DIM)
    vec = pl.BlockSpec((1, W), const)
    in_specs = [rowblk(C_R, W), rowblk(C_K, W), rowblk(C_V, W), rowblk(C_G, G_PAD),
                rowblk(C_WA, WA_PAD),
                pl.BlockSpec((nb, 1, RW_PACK), lambda b, c: (b, 0, 0)),
                pl.BlockSpec((1, nb) + sdims, lambda b, c: (l_in, b, 0, 0, 0)),
                pl.BlockSpec((1, RW_PACK), const),
                vec, pl.BlockSpec((WA_PAD, W), const),
                vec, pl.BlockSpec((WA_PAD, W), const),
                pl.BlockSpec((G_PAD, W), const),
                vec, vec, vec, vec, vec,
                pl.BlockSpec((2 * LANES, 2 * LANES), const)]
    rows = nb * T
    return _stacked_state_call(
        functools.partial(_rwkv_kernel, nb=nb, T=T, nchunks=nc), (nseq // nb, nc), in_specs,
        [proj3, proj3, proj3, proj3, proj3, sh, s0, mu, w0, wup, a0, aup, gup, kkw, kaw, rkw, lng,
         lnb, ones],
        pl.BlockSpec((nb, T, W), lambda b, c: (b, c, 0)),
        jax.ShapeDtypeStruct((nseq, seqlen, W), BF16),
        sdims, l_out, prev_out,
        [pltpu.VMEM((nb, RW_PACK), F32), pltpu.VMEM((nb * RWKV_HEADS, HEAD_DIM, HEAD_DIM), F32)]
        + [pltpu.VMEM((rows, W), F32)] * 7
        + [pltpu.VMEM((rows, W), F32), pltpu.VMEM((rows, W), F32)],
        "rwkv", nb=nb)


def _zeros_like_cols(x, n):
    return jnp.zeros(x.shape[:-1] + (n,), x.dtype)


def _pack_rwkv_cols(x):
    return jnp.concatenate([x[..., :_O_XW], x[..., _O_XG:], _zeros_like_cols(x, G_PAD - RWKV_R_G),
                            x[..., _O_XW:_O_XG]], axis=-1)


def _pack_tail_cols(w):
    return jnp.concatenate([_pack_rwkv_cols(w[..., _O_RW:]), w[..., _O_DT:_O_RW],
                            _zeros_like_cols(w, DT_PAD - SSD_HEADS)], axis=-1)


def _pad_rows(x, n_before, n_total):
    b, r, c = x.shape
    return jnp.concatenate([jnp.zeros((b, n_before, c), x.dtype), x,
                            jnp.zeros((b, n_total - n_before - r, c), x.dtype)], axis=1)


def _small_params(l, p):
    row = lambda v: v.reshape(1, -1)
    pad_lanes = lambda v, n: jnp.concatenate([v, jnp.zeros((n - v.shape[0],), v.dtype)]).reshape(1, n)
    zeros_w = jnp.zeros((RWKV_R_W, RWKV_WIDTH), F32)
    return dict(
        lru=(p['lru_conv_w'][l], row(p['lru_conv_b'][l]), p['lru_wa'][l].astype(BF16),
             row(p['lru_ba'][l]), p['lru_wx'][l].astype(BF16), row(p['lru_bx'][l]),
             row(p['lru_lambda'][l])),
        ssd=(p['ssd_conv_w'][l], row(p['ssd_conv_b'][l]), pad_lanes(p['ssd_dt_bias'][l], DT_PAD),
             pad_lanes(p['ssd_a_log'][l], DT_PAD), row(jnp.repeat(p['ssd_d'][l], HEAD_DIM)),
             row(p['ssd_norm_g'][l])),
        rwkv=(row(_pack_rwkv_cols(p['rwkv_mu'][l])), row(p['rwkv_w0'][l]),
              jnp.concatenate([p['rwkv_w_up'][l], zeros_w], axis=0).astype(BF16),
              row(p['rwkv_a0'][l]),
              jnp.concatenate([zeros_w, p['rwkv_a_up'][l]], axis=0).astype(BF16),
              jnp.concatenate([p['rwkv_g_up'][l],
                               jnp.zeros((G_PAD - RWKV_R_G, RWKV_WIDTH), F32)], axis=0).astype(BF16),
              row(p['rwkv_k_k'][l]), row(p['rwkv_k_a'][l]), row(p['rwkv_r_k'][l]),
              row(p['rwkv_ln_g'][l]), row(p['rwkv_ln_b'][l])),
        ln1=(row(p['ln1_g'][l]), row(p['ln1_b'][l])),
        ln2=(row(p['ln2_g'][l]), row(p['ln2_b'][l])),
    )


def _constants():
    lane = jnp.arange(SSD_WIDTH) // HEAD_DIM
    ehp = (jnp.arange(DT_PAD)[:, None] == lane[None, :]).astype(BF16)
    blk = jnp.arange(2 * LANES) // HEAD_DIM
    ones = (blk[:, None] == blk[None, :]).astype(BF16)
    return ehp, ones


def _layer(x, l, small_state, ssd_in, rw_in, prev_outs, sp, big, consts, *, nseq, seqlen, lru_nblk,
           rwkv_nb):
    lru_conv0, lru_h0, ssd_conv0, rw_shift0 = small_state
    ehp, ones = consts
    nbuf = CONV_WIDTH - 1
    proj = _proj(x, big['w_main'], big['w_tail'], l)

    lru_nseq = nseq // lru_nblk
    out_a, lru_h1 = _lru(
        proj, _pad_rows(lru_conv0, 0, SUBLANES).reshape(lru_nblk, lru_nseq * SUBLANES, LRU_WIDTH),
        lru_h0.reshape(lru_nblk, lru_nseq, LRU_WIDTH), *sp['lru'],
        nblk=lru_nblk, nseq=lru_nseq, seqlen=seqlen)
    out_b, ssd_out = _ssd(proj, _pad_rows(ssd_conv0, SUBLANES - nbuf, SUBLANES), ssd_in[0],
                          ssd_in[1], l, prev_outs[0], *sp['ssd'], ehp, nseq=nseq, seqlen=seqlen)
    p3 = proj.reshape(nseq, seqlen, N_PROJ)
    out_c, rw_out = _rwkv(p3, _pack_rwkv_cols(rw_shift0)[:, None, :], rw_in[0], rw_in[1], l,
                          prev_outs[1], *sp['rwkv'], ones, nb=rwkv_nb)
    out_c = out_c.reshape(nseq * seqlen, RWKV_WIDTH)

    y = _outproj_ln(out_a, out_b, out_c, big['w_out'], l, x, *sp['ln1'])
    y = _ffn_down_ln(_ffn_up(y, big['w_gate'], big['w_up'], l), big['w_down'], l, y, *sp['ln2'])

    tail = lambda col, width: p3[:, seqlen - nbuf:, col:col + width]
    last = lambda col, width: p3[:, seqlen - 1, col:col + width]
    rw_shift1 = jnp.concatenate([last(C_R, 3 * RWKV_WIDTH), last(C_WA, WA_PAD),
                                 last(C_G, RWKV_R_G)], axis=-1)
    small_new = (tail(C_LX, LRU_WIDTH), lru_h1.reshape(nseq, LRU_WIDTH),
                 tail(C_XBC, SSD_CONV_DIM), rw_shift1)
    return y, small_new, (ssd_out, rw_out)


def kernel(x_prompt, x_sample, state_lru_conv, state_lru_h, state_ssd_conv, state_ssd,
           state_rwkv_shift, state_rwkv, w_in, lru_conv_w, lru_conv_b, lru_wa, lru_ba, lru_wx,
           lru_bx, lru_lambda, ssd_conv_w, ssd_conv_b, ssd_dt_bias, ssd_a_log, ssd_d, ssd_norm_g,
           rwkv_mu, rwkv_w0, rwkv_w_up, rwkv_a0, rwkv_a_up, rwkv_g_up, rwkv_k_k, rwkv_k_a,
           rwkv_r_k, rwkv_ln_g, rwkv_ln_b, w_out, ln1_g, ln1_b, w_gate, w_up, w_down, ln2_g, ln2_b):
    params = dict(
        lru_conv_w=lru_conv_w, lru_conv_b=lru_conv_b, lru_wa=lru_wa, lru_ba=lru_ba,
        lru_wx=lru_wx, lru_bx=lru_bx, lru_lambda=lru_lambda, ssd_conv_w=ssd_conv_w,
        ssd_conv_b=ssd_conv_b, ssd_dt_bias=ssd_dt_bias, ssd_a_log=ssd_a_log, ssd_d=ssd_d,
        ssd_norm_g=ssd_norm_g, rwkv_mu=rwkv_mu, rwkv_w0=rwkv_w0, rwkv_w_up=rwkv_w_up,
        rwkv_a0=rwkv_a0, rwkv_a_up=rwkv_a_up, rwkv_g_up=rwkv_g_up, rwkv_k_k=rwkv_k_k,
        rwkv_k_a=rwkv_k_a, rwkv_r_k=rwkv_r_k.reshape(DEPTH, RWKV_WIDTH), rwkv_ln_g=rwkv_ln_g,
        rwkv_ln_b=rwkv_ln_b, ln1_g=ln1_g, ln1_b=ln1_b, ln2_g=ln2_g, ln2_b=ln2_b)
    big = dict(w_main=jnp.swapaxes(w_in, 1, 2),
               w_tail=jnp.swapaxes(_pack_tail_cols(w_in), 1, 2),
               w_out=w_out.astype(BF16),
               w_gate=w_gate, w_up=w_up, w_down=w_down.astype(BF16))
    bp, lp_len, _ = x_prompt.shape
    bs, ls_len, _ = x_sample.shape
    consts = _constants()
    nbuf = CONV_WIDTH - 1
    zero_small = (jnp.zeros((bp, nbuf, LRU_WIDTH), F32), jnp.zeros((bp, LRU_WIDTH), F32),
                  jnp.zeros((bp, nbuf, SSD_CONV_DIM), F32), jnp.zeros((bp, RWKV_SHIFT), F32))
    zero_ssd = jnp.zeros((1, bp, SSD_HEADS, HEAD_DIM, SSD_STATE), F32)
    zero_rw = jnp.zeros((1, bp, RWKV_HEADS, HEAD_DIM, HEAD_DIM), F32)
    yp = x_prompt.reshape(bp * lp_len, D_MODEL)
    ys = x_sample.reshape(bs * ls_len, D_MODEL)
    new_p = [[] for _ in range(4)]
    new_s = [[] for _ in range(4)]
    outs_p = (None, None)
    outs_s = (None, None)
    for l in range(DEPTH):
        sp = _small_params(l, params)
        yp, small_p, outs_p = _layer(yp, l, zero_small, (zero_ssd, 0), (zero_rw, 0), outs_p, sp,
                                     big, consts, nseq=bp, seqlen=lp_len, lru_nblk=bp,
                                     rwkv_nb=RWKV_PROMPT_NB)
        ys, small_s, outs_s = _layer(
            ys, l, (state_lru_conv[l], state_lru_h[l], state_ssd_conv[l], state_rwkv_shift[l]),
            (state_ssd, l), (state_rwkv, l), outs_s, sp, big, consts,
            nseq=bs, seqlen=ls_len, lru_nblk=1, rwkv_nb=RWKV_SAMPLE_NB)
        for i in range(4):
            new_p[i].append(small_p[i])
            new_s[i].append(small_s[i])
    p_lru_conv, p_lru_h, p_ssd_conv, p_rw_shift = [jnp.stack(v) for v in new_p]
    s_lru_conv, s_lru_h, s_ssd_conv, s_rw_shift = [jnp.stack(v) for v in new_s]
    return (yp.reshape(bp, lp_len, D_MODEL), ys.reshape(bs, ls_len, D_MODEL),
            p_lru_conv, p_lru_h, p_ssd_conv, outs_p[0], p_rw_shift, outs_p[1],
            s_lru_conv, s_lru_h, s_ssd_conv, outs_s[0], s_rw_shift, outs_s[1])
```

```python
import functools
import math

import jax
import jax.numpy as jnp
from jax import lax
from jax.experimental import pallas as pl
from jax.experimental.pallas import tpu as pltpu

F32 = jnp.float32
BF16 = jnp.bfloat16

D_MODEL = 2048
DEPTH = 2
D_MIX = 2 * D_MODEL
HEAD_DIM = 64
CONV_WIDTH = 4
LRU_WIDTH = D_MIX // 4
LRU_HEADS = 8
LRU_BLOCK = LRU_WIDTH // LRU_HEADS
LRU_C = 8.0
SSD_WIDTH = D_MIX // 2
SSD_HEADS = SSD_WIDTH // HEAD_DIM
SSD_GROUPS = 8
SSD_STATE = 128
SSD_CHUNK = 128
SSD_CONV_DIM = SSD_WIDTH + 2 * SSD_GROUPS * SSD_STATE
SSD_GROUP_WIDTH = SSD_WIDTH // SSD_GROUPS
SSD_HEADS_PER_GROUP = SSD_HEADS // SSD_GROUPS
RWKV_WIDTH = D_MIX - LRU_WIDTH - SSD_WIDTH
RWKV_HEADS = RWKV_WIDTH // HEAD_DIM
RWKV_R_W = max(32, int(round(1.8 * RWKV_WIDTH ** 0.5 / 32)) * 32)
RWKV_R_A = max(32, int(round(1.8 * RWKV_WIDTH ** 0.5 / 32)) * 32)
RWKV_R_G = max(32, int(round(0.6 * RWKV_WIDTH ** 0.8 / 32)) * 32)
RWKV_SHIFT = 3 * RWKV_WIDTH + RWKV_R_W + RWKV_R_A + RWKV_R_G
RWKV_GN_EPS = 64e-5
RWKV_CHUNK = 64
RWKV_PROMPT_NB = 2
RWKV_SAMPLE_NB = 8
D_FF = -(-(8 * D_MODEL) // (3 * 256)) * 256
ALPHA = (2 * DEPTH) ** 0.25
LN_EPS = 1e-5

LANES = 128
SUBLANES = 8
VMEM_LIMIT = 56 * 1024 * 1024

C_GATE = 0
C_LX = C_GATE + LRU_WIDTH
C_Z = C_LX + LRU_WIDTH
C_XBC = C_Z + SSD_WIDTH
C_R = C_XBC + SSD_CONV_DIM
C_K = C_R + RWKV_WIDTH
C_V = C_K + RWKV_WIDTH
C_G = C_V + RWKV_WIDTH
G_PAD = 2 * LANES
C_WA = C_G + G_PAD
WA_PAD = LANES
C_DT = C_WA + WA_PAD
DT_PAD = LANES
N_PROJ = C_DT + DT_PAD
assert RWKV_R_G <= G_PAD and RWKV_R_W + RWKV_R_A == WA_PAD and SSD_HEADS <= DT_PAD
RW_PACK = 3 * RWKV_WIDTH + G_PAD + WA_PAD

_O_DT = 2 * LRU_WIDTH + SSD_WIDTH + SSD_CONV_DIM
_O_RW = _O_DT + SSD_HEADS
_O_XW = 3 * RWKV_WIDTH
_O_XG = _O_XW + RWKV_R_W + RWKV_R_A


def _cparams(*sem):
    return pltpu.CompilerParams(dimension_semantics=sem, vmem_limit_bytes=VMEM_LIMIT)


_NN = (((1,), (0,)), ((), ()))
_NT = (((1,), (1,)), ((), ()))
_TN = (((0,), (0,)), ((), ()))


def _dg(a, b, dims):
    return lax.dot_general(a, b, dims, preferred_element_type=F32)


def _dot1(a, b, dims=_NN):
    return _dg(a.astype(BF16), b.astype(BF16), dims)


def _split2(x):
    hi = x.astype(BF16)
    lo = (x - hi.astype(F32)).astype(BF16)
    return hi, lo


def _split3(x):
    hi = x.astype(BF16)
    r1 = x - hi.astype(F32)
    mid = r1.astype(BF16)
    lo = (r1 - mid.astype(F32)).astype(BF16)
    return hi, mid, lo


def _dot3(a, b, dims=_NN):
    ah, al = _split2(a)
    bh, bl = _split2(b)
    free_axis = 1 if dims == _TN else 0
    m = a.shape[free_axis]
    both = _dg(jnp.concatenate([ah, al], axis=free_axis), bh, dims)
    return (both[:m] + both[m:]) + _dg(ah, bl, dims)


def _dot_shared_rhs(parts, w_exact, npieces):
    pieces = []
    for p in parts:
        rest = p
        for _ in range(npieces):
            piece = rest.astype(BF16).astype(F32)
            pieces.append(piece)
            rest = rest - piece
    prod = _dg(jnp.concatenate(pieces, axis=0).astype(BF16), w_exact, _NN)
    outs = []
    off = 0
    for p in parts:
        r = p.shape[0]
        acc = prod[off:off + r]
        for i in range(1, npieces):
            acc = acc + prod[off + i * r:off + (i + 1) * r]
        outs.append(acc)
        off += npieces * r
    return outs


def _dotx_left(w_exact, a, npieces=3):
    if npieces == 2:
        hi, lo = _split2(a)
        return _dg(w_exact, hi, _NN) + _dg(w_exact, lo, _NN)
    hi, mid, lo = _split3(a)
    return _dg(w_exact, hi, _NN) + (_dg(w_exact, mid, _NN) + _dg(w_exact, lo, _NN))


def _iota2(shape, dim):
    return lax.broadcasted_iota(jnp.int32, shape, dim)


def _softplus(x):
    return jnp.maximum(x, 0.0) + jnp.log1p(jnp.exp(-jnp.abs(x)))


def _sigmoid(x):
    return 1.0 / (1.0 + jnp.exp(-x))


def _silu(x):
    return x * _sigmoid(x)


def _gelu_tanh(x):
    c = math.sqrt(2.0 / math.pi)
    return 0.5 * x * (1.0 + jnp.tanh(c * (x + 0.044715 * (x * x * x))))


def _layer_norm(y, g, b):
    mu = jnp.mean(y, axis=-1, keepdims=True)
    d = y - mu
    var = jnp.mean(d * d, axis=-1, keepdims=True)
    return d * lax.rsqrt(var + LN_EPS) * g + b


DENSE_TM = 2048
DENSE_TN = 512
N_MAIN = C_R
N_TAIL = N_PROJ - N_MAIN
assert N_MAIN == _O_DT and N_MAIN % DENSE_TN == 0 and N_TAIL % DENSE_TN == 0 and D_FF % DENSE_TN == 0


def _proj_kernel(x_ref, wm_ref, wt_ref, o_ref):
    j = pl.program_id(1)

    @pl.when(j < N_MAIN // DENSE_TN)
    def _():
        o_ref[...] = _dg(x_ref[...], wm_ref[0].astype(BF16), _NT)

    @pl.when(j >= N_MAIN // DENSE_TN)
    def _():
        o_ref[...] = _dg(x_ref[...], wt_ref[0].astype(BF16), _NT)


def _proj(x, w_main_t, w_tail_t, l):
    m = x.shape[0]
    tm = min(m, DENSE_TM)
    tn = DENSE_TN
    nmain = N_MAIN // tn
    return pl.pallas_call(
        _proj_kernel,
        grid=(m // tm, N_PROJ // tn),
        in_specs=[pl.BlockSpec((tm, D_MODEL), lambda i, j: (i, 0)),
                  pl.BlockSpec((1, tn, D_MODEL), lambda i, j: (l, jnp.minimum(j, nmain - 1), 0)),
                  pl.BlockSpec((1, tn, D_MODEL), lambda i, j: (l, jnp.maximum(j - nmain, 0), 0))],
        out_specs=pl.BlockSpec((tm, tn), lambda i, j: (i, j)),
        out_shape=jax.ShapeDtypeStruct((m, N_PROJ), F32),
        compiler_params=_cparams("arbitrary", "arbitrary"),
        name="proj",
    )(x, w_main_t, w_tail_t)


OUT_KSTEPS = D_MIX // LRU_WIDTH


def _outproj_kernel(a_ref, b_ref, c_ref, w_ref, x_ref, g_ref, beta_ref, o_ref, ob_ref, acc_ref):
    k = pl.program_id(1)

    @pl.when(k == 0)
    def _():
        acc_ref[...] = jnp.dot(a_ref[...], w_ref[0], preferred_element_type=F32)

    @pl.when(jnp.logical_and(k > 0, k < OUT_KSTEPS - 1))
    def _():
        acc_ref[...] += jnp.dot(b_ref[...], w_ref[0], preferred_element_type=F32)

    @pl.when(k == OUT_KSTEPS - 1)
    def _():
        mix = acc_ref[...] + jnp.dot(c_ref[...], w_ref[0], preferred_element_type=F32)
        y = _layer_norm(ALPHA * x_ref[...] + mix, g_ref[...], beta_ref[...])
        o_ref[...] = y
        ob_ref[...] = y.astype(BF16)


def _outproj_ln(out_a, out_b, out_c, w_out, l, x, g, beta):
    m = x.shape[0]
    tm = min(m, 512)
    tk = LRU_WIDTH
    nb = SSD_WIDTH // tk
    return pl.pallas_call(
        _outproj_kernel,
        grid=(m // tm, OUT_KSTEPS),
        in_specs=[pl.BlockSpec((tm, tk), lambda i, k: (i, 0)),
                  pl.BlockSpec((tm, tk), lambda i, k: (i, jnp.clip(k - 1, 0, nb - 1))),
                  pl.BlockSpec((tm, tk), lambda i, k: (i, 0)),
                  pl.BlockSpec((1, tk, D_MODEL), lambda i, k: (l, k, 0)),
                  pl.BlockSpec((tm, D_MODEL), lambda i, k: (i, 0)),
                  pl.BlockSpec((1, D_MODEL), lambda i, k: (0, 0)),
                  pl.BlockSpec((1, D_MODEL), lambda i, k: (0, 0))],
        out_specs=[pl.BlockSpec((tm, D_MODEL), lambda i, k: (i, 0))] * 2,
        out_shape=[jax.ShapeDtypeStruct((m, D_MODEL), F32), jax.ShapeDtypeStruct((m, D_MODEL), BF16)],
        scratch_shapes=[pltpu.VMEM((tm, D_MODEL), F32)],
        compiler_params=_cparams("arbitrary", "arbitrary"),
        name="outproj_ln",
    )(out_a, out_b, out_c, w_out, x, g, beta)


def _ffn_up_kernel(x_ref, wg_ref, wu_ref, o_ref):
    xb = x_ref[...]
    gate = jnp.dot(xb, wg_ref[0].astype(BF16), preferred_element_type=F32)
    up = jnp.dot(xb, wu_ref[0].astype(BF16), preferred_element_type=F32)
    o_ref[...] = (_silu(gate) * up).astype(BF16)


def _ffn_up(x, wg, wu, l):
    m = x.shape[0]
    tm = min(m, DENSE_TM)
    tn = DENSE_TN
    return pl.pallas_call(
        _ffn_up_kernel,
        grid=(m // tm, D_FF // tn),
        in_specs=[pl.BlockSpec((tm, D_MODEL), lambda i, j: (i, 0)),
                  pl.BlockSpec((1, D_MODEL, tn), lambda i, j: (l, 0, j)),
                  pl.BlockSpec((1, D_MODEL, tn), lambda i, j: (l, 0, j))],
        out_specs=pl.BlockSpec((tm, tn), lambda i, j: (i, j)),
        out_shape=jax.ShapeDtypeStruct((m, D_FF), BF16),
        compiler_params=_cparams("arbitrary", "arbitrary"),
        name="ffn_up",
    )(x, wg, wu)


FFN_DOWN_KSTEPS = 2


LN_ROW_CHUNK = 128


def _ffn_down_kernel(h_ref, w_ref, x_ref, g_ref, beta_ref, o_ref, ob_ref):
    k = pl.program_id(1)

    @pl.when(k == 0)
    def _():
        o_ref[...] = jnp.dot(h_ref[...], w_ref[0], preferred_element_type=F32)

    @pl.when(k > 0)
    def _():
        o_ref[...] += jnp.dot(h_ref[...], w_ref[0], preferred_element_type=F32)

    @pl.when(k == FFN_DOWN_KSTEPS - 1)
    def _():
        chunk = min(LN_ROW_CHUNK, o_ref.shape[0])

        def ln_rows(r, carry):
            rows = pl.ds(pl.multiple_of(r * chunk, chunk), chunk)
            y = _layer_norm(ALPHA * x_ref[rows, :] + o_ref[rows, :], g_ref[...], beta_ref[...])
            o_ref[rows, :] = y
            ob_ref[rows, :] = y.astype(BF16)
            return carry

        lax.fori_loop(0, o_ref.shape[0] // chunk, ln_rows, 0)


def _ffn_down_ln(h, w_down, l, x, g, beta):
    m = x.shape[0]
    tm = min(m, 512)
    tk = D_FF // FFN_DOWN_KSTEPS
    assert tk * FFN_DOWN_KSTEPS == D_FF and tk % LANES == 0 and tm % min(LN_ROW_CHUNK, tm) == 0
    return pl.pallas_call(
        _ffn_down_kernel,
        grid=(m // tm, FFN_DOWN_KSTEPS),
        in_specs=[pl.BlockSpec((tm, tk), lambda i, k: (i, k)),
                  pl.BlockSpec((1, tk, D_MODEL), lambda i, k: (l, k, 0)),
                  pl.BlockSpec((tm, D_MODEL), lambda i, k: (i, 0)),
                  pl.BlockSpec((1, D_MODEL), lambda i, k: (0, 0)),
                  pl.BlockSpec((1, D_MODEL), lambda i, k: (0, 0))],
        out_specs=[pl.BlockSpec((tm, D_MODEL), lambda i, k: (i, 0))] * 2,
        out_shape=[jax.ShapeDtypeStruct((m, D_MODEL), F32), jax.ShapeDtypeStruct((m, D_MODEL), BF16)],
        compiler_params=_cparams("arbitrary", "arbitrary"),
        name="ffn_down_ln",
    )(h, w_down, x, g, beta)


def _conv_taps(u, first_rows, cw, cb, t_in_seq):
    out = cb + cw[CONV_WIDTH - 1:CONV_WIDTH, :] * u
    for j in range(1, CONV_WIDTH):
        shifted = jnp.where(t_in_seq < j, first_rows(j), pltpu.roll(u, j, 0))
        out = out + cw[CONV_WIDTH - 1 - j:CONV_WIDTH - j, :] * shifted
    return out


def _lru_kernel(gate_ref, lx_ref, buf_ref, h0_ref, cw_ref, cb_ref, wa_ref, ba_ref, wx_ref,
                bx_ref, lam_ref, out_ref, h1_ref, a_scr, b_scr, hin_scr, *, nseq, seqlen):
    rows = nseq * seqlen
    u = lx_ref[...]
    t_in_seq = _iota2((rows, LANES), 0) % seqlen
    if nseq == 1:
        buf = jnp.concatenate([buf_ref[0], jnp.zeros((rows - SUBLANES, LANES), F32)], axis=0) \
            if rows > SUBLANES else buf_ref[0]
    else:
        buf = buf_ref[0]
    nbuf = CONV_WIDTH - 1
    xc = _conv_taps(u, lambda j: pltpu.roll(buf, (j - nbuf) % rows, 0), cw_ref[...], cb_ref[...],
                    t_in_seq)

    r = _sigmoid(_dot1(xc, wa_ref[0]) + ba_ref[...])
    i = _sigmoid(_dot1(xc, wx_ref[0]) + bx_ref[...])
    log_a = (-LRU_C) * r * _softplus(-lam_ref[...])
    a = jnp.exp(log_a)
    b = jnp.sqrt(jnp.tanh(-log_a) * (a * a + 1.0)) * (i * xc)

    t8 = _iota2((rows, LANES), 0) % SUBLANES
    for s in (1, 2, 4):
        m = t8 >= s
        a_sh = pltpu.roll(a, s, 0)
        b_sh = pltpu.roll(b, s, 0)
        b = jnp.where(m, a * b_sh + b, b)
        a = jnp.where(m, a * a_sh, a)

    if seqlen == SUBLANES:
        h0 = h0_ref[0]
        hin = jnp.broadcast_to(h0[:, None, :], (nseq, SUBLANES, LANES)).reshape(rows, LANES)
        h = a * hin + b
        out_ref[...] = (h * _gelu_tanh(gate_ref[...])).astype(out_ref.dtype)
        a_scr[...] = h
        h1_ref[0] = a_scr[pl.ds(SUBLANES - 1, nseq, stride=SUBLANES), :]
    else:
        assert nseq == 1
        a_scr[...] = a
        b_scr[...] = b

        def carry_step(g, carry):
            base = pl.multiple_of(g * SUBLANES, SUBLANES)
            hin_scr[pl.ds(base, SUBLANES), :] = jnp.broadcast_to(carry, (SUBLANES, LANES))
            a7 = a_scr[pl.ds(base + SUBLANES - 1, 1), :]
            b7 = b_scr[pl.ds(base + SUBLANES - 1, 1), :]
            return a7 * carry + b7

        last = lax.fori_loop(0, rows // SUBLANES, carry_step, h0_ref[0])
        h = a_scr[...] * hin_scr[...] + b_scr[...]
        out_ref[...] = (h * _gelu_tanh(gate_ref[...])).astype(out_ref.dtype)
        h1_ref[0] = last


def _lru(proj, buf, h0, cw, cb, wa, ba, wx, bx, lam, *, nblk, nseq, seqlen):
    rows = nseq * seqlen
    gate_blk = C_GATE // LRU_BLOCK
    lx_blk = C_LX // LRU_BLOCK
    row = lambda s, h: (0, h)
    return pl.pallas_call(
        functools.partial(_lru_kernel, nseq=nseq, seqlen=seqlen),
        grid=(nblk, LRU_HEADS),
        in_specs=[pl.BlockSpec((rows, LRU_BLOCK), lambda s, h: (s, gate_blk + h)),
                  pl.BlockSpec((rows, LRU_BLOCK), lambda s, h: (s, lx_blk + h)),
                  pl.BlockSpec((1, nseq * SUBLANES, LRU_BLOCK), lambda s, h: (s, 0, h)),
                  pl.BlockSpec((1, nseq, LRU_BLOCK), lambda s, h: (s, 0, h)),
                  pl.BlockSpec((CONV_WIDTH, LRU_BLOCK), row),
                  pl.BlockSpec((1, LRU_BLOCK), row),
                  pl.BlockSpec((1, LRU_BLOCK, LRU_BLOCK), lambda s, h: (h, 0, 0)),
                  pl.BlockSpec((1, LRU_BLOCK), row),
                  pl.BlockSpec((1, LRU_BLOCK, LRU_BLOCK), lambda s, h: (h, 0, 0)),
                  pl.BlockSpec((1, LRU_BLOCK), row),
                  pl.BlockSpec((1, LRU_BLOCK), row)],
        out_specs=[pl.BlockSpec((rows, LRU_BLOCK), lambda s, h: (s, h)),
                   pl.BlockSpec((1, nseq, LRU_BLOCK), lambda s, h: (s, 0, h))],
        out_shape=[jax.ShapeDtypeStruct((nblk * rows, LRU_WIDTH), BF16),
                   jax.ShapeDtypeStruct((nblk, nseq, LRU_WIDTH), F32)],
        scratch_shapes=[pltpu.VMEM((rows, LANES), F32)] * 3,
        compiler_params=_cparams("arbitrary", "arbitrary"),
        name="lru",
    )(proj, proj, buf, h0, cw, cb, wa, ba, wx, bx, lam)


def _stacked_state_call(kernel_fn, grid, in_specs, operands, row_spec, row_shape, state_dims,
                        l_out, prev_out, scratch_shapes, name, nb=1):
    nseq = grid[0] * nb
    zeros = (0,) * len(state_dims)
    state_shape = jax.ShapeDtypeStruct((DEPTH, nseq) + tuple(state_dims), F32)
    aliases = {}
    if prev_out is None:
        assert l_out == 0
        state_spec = pl.BlockSpec((DEPTH, nb) + tuple(state_dims), lambda b, c: (0, b) + zeros)
    else:
        state_spec = pl.BlockSpec((1, nb) + tuple(state_dims), lambda b, c: (l_out, b) + zeros)
        in_specs = in_specs + [pl.BlockSpec(memory_space=pl.ANY)]
        operands = operands + [prev_out]
        aliases = {len(operands) - 1: 1}
        kernel_fn = functools.partial(_drop_alias_ref, kernel_fn, len(operands) - 1)
    return pl.pallas_call(
        kernel_fn, grid=grid, in_specs=in_specs, out_specs=[row_spec, state_spec],
        out_shape=[row_shape, state_shape], scratch_shapes=scratch_shapes,
        input_output_aliases=aliases, compiler_params=_cparams("arbitrary", "arbitrary"),
        name=name)(*operands)


def _drop_alias_ref(kernel_fn, pos, *refs):
    return kernel_fn(*refs[:pos], *refs[pos + 1:])


def _zero_other_slabs(state_ref):
    if state_ref.shape[0] > 1:
        state_ref[1:] = jnp.zeros((state_ref.shape[0] - 1,) + state_ref.shape[1:], state_ref.dtype)


def _transpose_rows_to_lanes(x, t):
    if t < LANES:
        x = jnp.concatenate([x, jnp.zeros((LANES - t, LANES), F32)], axis=0)
    return x.T[:, :t]


def _ssd_kernel(z_ref, xbc_ref, dt_ref, buf_ref, s0_ref, cw_ref, cb_ref, dtb_ref, alog_ref,
                dch_ref, ng_ref, ehp_ref, out_ref, s1_ref, tail_ref, pad_ref, y_scr, *, T):
    c = pl.program_id(1)
    n_state = SSD_STATE
    gw = SSD_GROUP_WIDTH
    hpg = SSD_HEADS_PER_GROUP

    @pl.when(c == 0)
    def _():
        tail_ref[...] = buf_ref[0]
        s1_ref[0:1] = s0_ref[...]
        _zero_other_slabs(s1_ref)

    u = xbc_ref[...]
    pad_ref[0:SUBLANES, :] = tail_ref[...]
    pad_ref[SUBLANES:SUBLANES + T, :] = u
    cw = cw_ref[...]
    xc = cb_ref[...] + cw[CONV_WIDTH - 1:CONV_WIDTH, :] * u
    for j in range(1, CONV_WIDTH):
        xc = xc + cw[CONV_WIDTH - 1 - j:CONV_WIDTH - j, :] * pad_ref[SUBLANES - j:SUBLANES - j + T, :]
    tail_ref[...] = pad_ref[T:T + SUBLANES, :]

    xa = _silu(xc)
    xs = xa[:, :SSD_WIDTH]
    bm = xa[:, SSD_WIDTH:SSD_WIDTH + SSD_GROUPS * n_state]
    cm = xa[:, SSD_WIDTH + SSD_GROUPS * n_state:]
    dt = _softplus(dt_ref[...] + dtb_ref[...])
    da = dt * (-jnp.exp(alog_ref[...]))
    ii = _iota2((T, T), 0)
    jj = _iota2((T, T), 1)
    causal = ii >= jj
    cum = _dotx_left(causal.astype(BF16), da)
    cum_t = _transpose_rows_to_lanes(cum, T)
    ehp = ehp_ref[...]
    cumx, dtx = _dot_shared_rhs([cum, dt], ehp, 2)
    xdt = xs * dtx
    ecum = jnp.exp(cumx)
    xdtd = xdt * jnp.exp(cumx[T - 1:T, :] - cumx)

    groups = range(SSD_GROUPS)
    heads = range(SSD_HEADS)
    cg = [cm[:, g * n_state:(g + 1) * n_state] for g in groups]
    bg = [bm[:, g * n_state:(g + 1) * n_state] for g in groups]
    sg = [s1_ref[0, 0, g * hpg:(g + 1) * hpg].reshape(gw, n_state) for g in groups]
    cb = [_dot1(cg[g], bg[g], _NT) for g in groups]
    y_off = [_dot1(cg[g], sg[g], _NT) for g in groups]
    st = [_dot1(xdtd[:, g * gw:(g + 1) * gw], bg[g], _TN) for g in groups]
    lm = [jnp.where(causal, jnp.exp(cum[:, h:h + 1] - cum_t[h:h + 1, :]), 0.0) for h in heads]
    y_diag = [_dot1(cb[h // hpg] * lm[h], xdt[:, h * HEAD_DIM:(h + 1) * HEAD_DIM]) for h in heads]
    for g in groups:
        y_scr[:, g * gw:(g + 1) * gw] = (jnp.concatenate(y_diag[g * hpg:(g + 1) * hpg], axis=1)
                                         + y_off[g] * ecum[:, g * gw:(g + 1) * gw])
        decay = jnp.concatenate(
            [jnp.broadcast_to(jnp.exp(cum_t[h:h + 1, T - 1:T]), (HEAD_DIM, n_state))
             for h in range(g * hpg, (g + 1) * hpg)], axis=0)
        s1_ref[0, 0, g * hpg:(g + 1) * hpg] = (sg[g] * decay + st[g]).reshape(hpg, HEAD_DIM, n_state)

    y = (y_scr[...] + dch_ref[...] * xs) * _silu(z_ref[...])
    outs = []
    for g in groups:
        yg = y[:, g * gw:(g + 1) * gw]
        ms = jnp.mean(yg * yg, axis=-1, keepdims=True)
        outs.append(yg * lax.rsqrt(ms + 1e-5))
    out_ref[...] = (jnp.concatenate(outs, axis=1) * ng_ref[...]).astype(out_ref.dtype)


def _ssd(proj, buf, s0, l_in, l_out, prev_out, cw, cb, dtb, alog, dch, ng, ehp, *, nseq, seqlen):
    T = min(SSD_CHUNK, seqlen)
    nc = seqlen // T
    const = lambda b, c: (0, 0)
    sblk = (1, 1, SSD_HEADS, HEAD_DIM, SSD_STATE)
    in_specs = [pl.BlockSpec((T, SSD_WIDTH), lambda b, c: (b * nc + c, C_Z // SSD_WIDTH)),
                pl.BlockSpec((T, SSD_CONV_DIM), lambda b, c: (b * nc + c, C_XBC // SSD_CONV_DIM)),
                pl.BlockSpec((T, DT_PAD), lambda b, c: (b * nc + c, C_DT // DT_PAD)),
                pl.BlockSpec((1, SUBLANES, SSD_CONV_DIM), lambda b, c: (b, 0, 0)),
                pl.BlockSpec(sblk, lambda b, c: (l_in, b, 0, 0, 0)),
                pl.BlockSpec((CONV_WIDTH, SSD_CONV_DIM), const),
                pl.BlockSpec((1, SSD_CONV_DIM), const),
                pl.BlockSpec((1, DT_PAD), const),
                pl.BlockSpec((1, DT_PAD), const),
                pl.BlockSpec((1, SSD_WIDTH), const),
                pl.BlockSpec((1, SSD_WIDTH), const),
                pl.BlockSpec((DT_PAD, SSD_WIDTH), const)]
    return _stacked_state_call(
        functools.partial(_ssd_kernel, T=T), (nseq, nc), in_specs,
        [proj, proj, proj, buf, s0, cw, cb, dtb, alog, dch, ng, ehp],
        pl.BlockSpec((T, SSD_WIDTH), lambda b, c: (b * nc + c, 0)),
        jax.ShapeDtypeStruct((nseq * seqlen, SSD_WIDTH), BF16),
        sblk[2:], l_out, prev_out,
        [pltpu.VMEM((SUBLANES, SSD_CONV_DIM), F32),
         pltpu.VMEM((T + SUBLANES, SSD_CONV_DIM), F32),
         pltpu.VMEM((T, SSD_WIDTH), F32)],
        "ssd")


def _unit_lower_inverse(a_list, ii, jj, T):
    n = len(a_list)
    pair = (ii >> 1) == (jj >> 1)
    inv = [jnp.where(ii == jj, 1.0, 0.0) + jnp.where(pair, a_list[h], 0.0) for h in range(n)]
    shift = 1
    while (2 << shift) <= T:
        band = jnp.logical_and((ii >> (shift + 1)) == (jj >> (shift + 1)),
                               (ii >> shift) != (jj >> shift))
        x = [_dot1(jnp.where(band, a_list[h], 0.0), inv[h]) for h in range(n)]
        inv = [inv[h] + _dot1(inv[h], x[h]) for h in range(n)]
        shift += 1
    return inv


def _rwkv_kernel(r_ref, k_ref, v_ref, g_ref, wa_ref, sh_ref, s0_ref, mu_ref, w0_ref, wup_ref,
                 a0_ref, aup_ref, gup_ref, kkw_ref, kaw_ref, rkw_ref, lng_ref, lnb_ref, ones_ref,
                 out_ref, s1_ref, prev_scr, s_scr, al_scr, be_scr, kt_scr, rt_scr, bs_scr,
                 ks_scr, v_scr, gam_scr, o_scr, *, nb, T, nchunks):
    c = pl.program_id(1)
    W = RWKV_WIDTH
    D = HEAD_DIM
    R = nb * T
    nh = RWKV_HEADS

    @pl.when(c == 0)
    def _():
        prev_scr[...] = sh_ref[...].reshape(nb, RW_PACK)
        s_scr[...] = s0_ref[0].reshape(nb * nh, D, D)

    def rows_of(ref3):
        return ref3[...].reshape(R, ref3.shape[-1])

    def per_seq_rows(x):
        return jnp.broadcast_to(x[:, None, :], (nb, T, x.shape[-1])).reshape(R, x.shape[-1])

    def last_rows(x):
        return jnp.concatenate([x[(b + 1) * T - 1:(b + 1) * T, :] for b in range(nb)], axis=0)

    def token_shift(p, lo, hi):
        first = _iota2(p.shape, 0) % T == 0
        prev = jnp.where(first, per_seq_rows(prev_scr[:, lo:hi]), pltpu.roll(p, 1, 0))
        prev_scr[:, lo:hi] = last_rows(p)
        return p + (prev - p) * mu_ref[:, lo:hi]

    xr = token_shift(rows_of(r_ref), 0, W)
    xk = token_shift(rows_of(k_ref), W, 2 * W)
    xv = token_shift(rows_of(v_ref), 2 * W, 3 * W)
    xg = token_shift(rows_of(g_ref), 3 * W, 3 * W + G_PAD)
    xwa = token_shift(rows_of(wa_ref), 3 * W + G_PAD, RW_PACK)

    w_lin = w0_ref[...] + _dot1(jnp.tanh(xwa), wup_ref[...])
    a = _sigmoid(a0_ref[...] + _dot1(xwa, aup_ref[...]))
    gate = _dot1(_sigmoid(xg), gup_ref[...])
    lw = -jnp.exp(-_softplus(-w_lin) - 0.5)

    ones = ones_ref[...]

    def head_sums(xs, npieces=3):
        tw = ones.shape[0]
        nt = W // tw
        tiles = [x[:, i * tw:(i + 1) * tw] for x in xs for i in range(nt)]
        sums = _dot_shared_rhs(tiles, ones, npieces)
        return [jnp.concatenate(sums[n * nt:(n + 1) * nt], axis=1) for n in range(len(xs))]

    kk = xk * kkw_ref[...]
    kp = xk * (1.0 + (a - 1.0) * kaw_ref[...])
    kk_sq, rk_sum = head_sums([kk * kk, xr * kp * rkw_ref[...]], 1)
    kk = kk / jnp.maximum(jnp.sqrt(kk_sq), 1e-12)

    ri = _iota2((R, R), 0)
    rj = _iota2((R, R), 1)
    same_seq_causal = jnp.logical_and(ri >= rj, ri // T == rj // T)
    cum = _dotx_left(same_seq_causal.astype(BF16), lw, 3)
    e_neg = jnp.exp(-cum)
    gam = jnp.exp(per_seq_rows(last_rows(cum)))
    be = kk * a * e_neg
    kt = kp * e_neg
    al_scr[...] = -kk * jnp.exp(cum - lw)
    be_scr[...] = be
    kt_scr[...] = kt
    rt_scr[...] = xr * jnp.exp(cum)
    bs_scr[...] = be * gam
    ks_scr[...] = kt * gam
    v_scr[...] = xv
    gam_scr[...] = gam

    ii = _iota2((T, T), 0)
    jj = _iota2((T, T), 1)
    incl = ii >= jj
    strict = ii > jj
    eye_d = _iota2((D, D), 0) == _iota2((D, D), 1)

    units = [(b, h) for b in range(nb) for h in range(nh)]
    nu = range(len(units))

    def per_unit(ref):
        return [ref[b * T:(b + 1) * T, h * D:(h + 1) * D] for b, h in units]

    al, be_u, kt_u, rt = per_unit(al_scr), per_unit(be_scr), per_unit(kt_scr), per_unit(rt_scr)
    bs_u, ks_u, vv, gam_u = per_unit(bs_scr), per_unit(ks_scr), per_unit(v_scr), per_unit(gam_scr)
    gram = [_dot1(jnp.concatenate([al[u], rt[u]], axis=0),
                  jnp.concatenate([be_u[u], kt_u[u]], axis=0), _NT) for u in nu]
    a_ab = [jnp.where(strict, gram[u][:T, :T], 0.0) for u in nu]
    a_ak = [jnp.where(strict, gram[u][:T, T:], 0.0) for u in nu]
    r_b = [jnp.where(incl, gram[u][T:, :T], 0.0) for u in nu]
    r_k = [jnp.where(incl, gram[u][T:, T:], 0.0) for u in nu]
    inv = _unit_lower_inverse(a_ab, ii, jj, T)
    akv = [_dot1(a_ak[u], vv[u]) for u in nu]
    pw = [_dot1(inv[u], jnp.concatenate([al[u], akv[u]], axis=1)) for u in nu]
    qo = [_dot1(r_b[u], pw[u]) for u in nu]
    rkv = [_dot1(r_k[u], vv[u]) for u in nu]
    smat = [s_scr[u] for u in nu]
    qs = [_dot1(rt[u] + qo[u][:, :D], smat[u], _NT) for u in nu]
    gp = [_dot3(bs_u[u], pw[u][:, :D], _TN) for u in nu]
    zt = [_dot3(jnp.concatenate([pw[u][:, D:], vv[u]], axis=0),
                jnp.concatenate([bs_u[u], ks_u[u]], axis=0), _TN) for u in nu]
    gmat = [jnp.where(eye_d, jnp.broadcast_to(gam_u[u][0:1, :], (D, D)), 0.0) + gp[u] for u in nu]
    sg = [_dot3(smat[u], gmat[u], _NT) for u in nu]
    for u in nu:
        s_scr[u] = sg[u] + zt[u]
    pairs = LANES // D
    for b in range(nb):
        for p in range(nh // pairs):
            us = [b * nh + p * pairs + s for s in range(pairs)]
            o_scr[b * T:(b + 1) * T, p * LANES:(p + 1) * LANES] = jnp.concatenate(
                [qs[u] + qo[u][:, D:] + rkv[u] for u in us], axis=1)

    o = o_scr[...]
    mean = head_sums([o])[0] * (1.0 / D)
    d = o - mean
    var = head_sums([d * d])[0] * (1.0 / D)
    on = d * lax.rsqrt(var + RWKV_GN_EPS) * lng_ref[...] + lnb_ref[...]
    bonus = rk_sum * xv
    out_ref[...] = ((on + bonus) * gate).astype(out_ref.dtype).reshape(nb, T, W)

    @pl.when(c == nchunks - 1)
    def _():
        s1_ref[0] = s_scr[...].reshape(nb, nh, D, D)
        _zero_other_slabs(s1_ref)


def _rwkv(proj3, sh, s0, l_in, l_out, prev_out, mu, w0, wup, a0, aup, gup, kkw, kaw, rkw, lng, lnb,
          ones, *, nb):
    nseq, seqlen, _ = proj3.shape
    T = min(RWKV_CHUNK, seqlen)
    nc = seqlen // T
    W = RWKV_WIDTH
    const = lambda b, c: (0, 0)
    rowblk = lambda col, width: pl.BlockSpec((nb, T, width), lambda b, c: (b, c, col // width))
    sdims = (RWKV_HEADS, HEAD_DIM, HEAD_DIM)
    vec = pl.BlockSpec((1, W), const)
    in_specs = [rowblk(C_R, W), rowblk(C_K, W), rowblk(C_V, W), rowblk(C_G, G_PAD),
                rowblk(C_WA, WA_PAD),
                pl.BlockSpec((nb, 1, RW_PACK), lambda b, c: (b, 0, 0)),
                pl.BlockSpec((1, nb) + sdims, lambda b, c: (l_in, b, 0, 0, 0)),
                pl.BlockSpec((1, RW_PACK), const),
                vec, pl.BlockSpec((WA_PAD, W), const),
                vec, pl.BlockSpec((WA_PAD, W), const),
                pl.BlockSpec((G_PAD, W), const),
                vec, vec, vec, vec, vec,
                pl.BlockSpec((2 * LANES, 2 * LANES), const)]
    rows = nb * T
    return _stacked_state_call(
        functools.partial(_rwkv_kernel, nb=nb, T=T, nchunks=nc), (nseq // nb, nc), in_specs,
        [proj3, proj3, proj3, proj3, proj3, sh, s0, mu, w0, wup, a0, aup, gup, kkw, kaw, rkw, lng,
         lnb, ones],
        pl.BlockSpec((nb, T, W), lambda b, c: (b, c, 0)),
        jax.ShapeDtypeStruct((nseq, seqlen, W), BF16),
        sdims, l_out, prev_out,
        [pltpu.VMEM((nb, RW_PACK), F32), pltpu.VMEM((nb * RWKV_HEADS, HEAD_DIM, HEAD_DIM), F32)]
        + [pltpu.VMEM((rows, W), F32)] * 7
        + [pltpu.VMEM((rows, W), F32), pltpu.VMEM((rows, W), F32)],
        "rwkv", nb=nb)


def _zeros_like_cols(x, n):
    return jnp.zeros(x.shape[:-1] + (n,), x.dtype)


def _pack_rwkv_cols(x):
    return jnp.concatenate([x[..., :_O_XW], x[..., _O_XG:], _zeros_like_cols(x, G_PAD - RWKV_R_G),
                            x[..., _O_XW:_O_XG]], axis=-1)


def _pack_tail_cols(w):
    return jnp.concatenate([_pack_rwkv_cols(w[..., _O_RW:]), w[..., _O_DT:_O_RW],
                            _zeros_like_cols(w, DT_PAD - SSD_HEADS)], axis=-1)


def _pad_rows(x, n_before, n_total):
    b, r, c = x.shape
    return jnp.concatenate([jnp.zeros((b, n_before, c), x.dtype), x,
                            jnp.zeros((b, n_total - n_before - r, c), x.dtype)], axis=1)


def _small_params(l, p):
    row = lambda v: v.reshape(1, -1)
    pad_lanes = lambda v, n: jnp.concatenate([v, jnp.zeros((n - v.shape[0],), v.dtype)]).reshape(1, n)
    zeros_w = jnp.zeros((RWKV_R_W, RWKV_WIDTH), F32)
    return dict(
        lru=(p['lru_conv_w'][l], row(p['lru_conv_b'][l]), p['lru_wa'][l].astype(BF16),
             row(p['lru_ba'][l]), p['lru_wx'][l].astype(BF16), row(p['lru_bx'][l]),
             row(p['lru_lambda'][l])),
        ssd=(p['ssd_conv_w'][l], row(p['ssd_conv_b'][l]), pad_lanes(p['ssd_dt_bias'][l], DT_PAD),
             pad_lanes(p['ssd_a_log'][l], DT_PAD), row(jnp.repeat(p['ssd_d'][l], HEAD_DIM)),
             row(p['ssd_norm_g'][l])),
        rwkv=(row(_pack_rwkv_cols(p['rwkv_mu'][l])), row(p['rwkv_w0'][l]),
              jnp.concatenate([p['rwkv_w_up'][l], zeros_w], axis=0).astype(BF16),
              row(p['rwkv_a0'][l]),
              jnp.concatenate([zeros_w, p['rwkv_a_up'][l]], axis=0).astype(BF16),
              jnp.concatenate([p['rwkv_g_up'][l],
                               jnp.zeros((G_PAD - RWKV_R_G, RWKV_WIDTH), F32)], axis=0).astype(BF16),
              row(p['rwkv_k_k'][l]), row(p['rwkv_k_a'][l]), row(p['rwkv_r_k'][l]),
              row(p['rwkv_ln_g'][l]), row(p['rwkv_ln_b'][l])),
        ln1=(row(p['ln1_g'][l]), row(p['ln1_b'][l])),
        ln2=(row(p['ln2_g'][l]), row(p['ln2_b'][l])),
    )


def _constants():
    lane = jnp.arange(SSD_WIDTH) // HEAD_DIM
    ehp = (jnp.arange(DT_PAD)[:, None] == lane[None, :]).astype(BF16)
    blk = jnp.arange(2 * LANES) // HEAD_DIM
    ones = (blk[:, None] == blk[None, :]).astype(BF16)
    return ehp, ones


def _layer(x, xb, l, small_state, ssd_in, rw_in, prev_outs, sp, big, consts, *, nseq, seqlen,
           lru_nblk, rwkv_nb):
    lru_conv0, lru_h0, ssd_conv0, rw_shift0 = small_state
    ehp, ones = consts
    nbuf = CONV_WIDTH - 1
    proj = _proj(xb, big['w_main'], big['w_tail'], l)

    lru_nseq = nseq // lru_nblk
    out_a, lru_h1 = _lru(
        proj, _pad_rows(lru_conv0, 0, SUBLANES).reshape(lru_nblk, lru_nseq * SUBLANES, LRU_WIDTH),
        lru_h0.reshape(lru_nblk, lru_nseq, LRU_WIDTH), *sp['lru'],
        nblk=lru_nblk, nseq=lru_nseq, seqlen=seqlen)
    out_b, ssd_out = _ssd(proj, _pad_rows(ssd_conv0, SUBLANES - nbuf, SUBLANES), ssd_in[0],
                          ssd_in[1], l, prev_outs[0], *sp['ssd'], ehp, nseq=nseq, seqlen=seqlen)
    p3 = proj.reshape(nseq, seqlen, N_PROJ)
    out_c, rw_out = _rwkv(p3, _pack_rwkv_cols(rw_shift0)[:, None, :], rw_in[0], rw_in[1], l,
                          prev_outs[1], *sp['rwkv'], ones, nb=rwkv_nb)
    out_c = out_c.reshape(nseq * seqlen, RWKV_WIDTH)

    y, yb = _outproj_ln(out_a, out_b, out_c, big['w_out'], l, x, *sp['ln1'])
    y, yb = _ffn_down_ln(_ffn_up(yb, big['w_gate'], big['w_up'], l), big['w_down'], l, y, *sp['ln2'])

    tail = lambda col, width: p3[:, seqlen - nbuf:, col:col + width]
    last = lambda col, width: p3[:, seqlen - 1, col:col + width]
    rw_shift1 = jnp.concatenate([last(C_R, 3 * RWKV_WIDTH), last(C_WA, WA_PAD),
                                 last(C_G, RWKV_R_G)], axis=-1)
    small_new = (tail(C_LX, LRU_WIDTH), lru_h1.reshape(nseq, LRU_WIDTH),
                 tail(C_XBC, SSD_CONV_DIM), rw_shift1)
    return y, yb, small_new, (ssd_out, rw_out)


def kernel(x_prompt, x_sample, state_lru_conv, state_lru_h, state_ssd_conv, state_ssd,
           state_rwkv_shift, state_rwkv, w_in, lru_conv_w, lru_conv_b, lru_wa, lru_ba, lru_wx,
           lru_bx, lru_lambda, ssd_conv_w, ssd_conv_b, ssd_dt_bias, ssd_a_log, ssd_d, ssd_norm_g,
           rwkv_mu, rwkv_w0, rwkv_w_up, rwkv_a0, rwkv_a_up, rwkv_g_up, rwkv_k_k, rwkv_k_a,
           rwkv_r_k, rwkv_ln_g, rwkv_ln_b, w_out, ln1_g, ln1_b, w_gate, w_up, w_down, ln2_g, ln2_b):
    params = dict(
        lru_conv_w=lru_conv_w, lru_conv_b=lru_conv_b, lru_wa=lru_wa, lru_ba=lru_ba,
        lru_wx=lru_wx, lru_bx=lru_bx, lru_lambda=lru_lambda, ssd_conv_w=ssd_conv_w,
        ssd_conv_b=ssd_conv_b, ssd_dt_bias=ssd_dt_bias, ssd_a_log=ssd_a_log, ssd_d=ssd_d,
        ssd_norm_g=ssd_norm_g, rwkv_mu=rwkv_mu, rwkv_w0=rwkv_w0, rwkv_w_up=rwkv_w_up,
        rwkv_a0=rwkv_a0, rwkv_a_up=rwkv_a_up, rwkv_g_up=rwkv_g_up, rwkv_k_k=rwkv_k_k,
        rwkv_k_a=rwkv_k_a, rwkv_r_k=rwkv_r_k.reshape(DEPTH, RWKV_WIDTH), rwkv_ln_g=rwkv_ln_g,
        rwkv_ln_b=rwkv_ln_b, ln1_g=ln1_g, ln1_b=ln1_b, ln2_g=ln2_g, ln2_b=ln2_b)
    big = dict(w_main=jnp.swapaxes(w_in, 1, 2),
               w_tail=jnp.swapaxes(_pack_tail_cols(w_in), 1, 2),
               w_out=w_out.astype(BF16),
               w_gate=w_gate, w_up=w_up, w_down=w_down.astype(BF16))
    bp, lp_len, _ = x_prompt.shape
    bs, ls_len, _ = x_sample.shape
    consts = _constants()
    nbuf = CONV_WIDTH - 1
    zero_small = (jnp.zeros((bp, nbuf, LRU_WIDTH), F32), jnp.zeros((bp, LRU_WIDTH), F32),
                  jnp.zeros((bp, nbuf, SSD_CONV_DIM), F32), jnp.zeros((bp, RWKV_SHIFT), F32))
    zero_ssd = jnp.zeros((1, bp, SSD_HEADS, HEAD_DIM, SSD_STATE), F32)
    zero_rw = jnp.zeros((1, bp, RWKV_HEADS, HEAD_DIM, HEAD_DIM), F32)
    yp = x_prompt.reshape(bp * lp_len, D_MODEL)
    ys = x_sample.reshape(bs * ls_len, D_MODEL)
    ypb = yp.astype(BF16)
    ysb = ys.astype(BF16)
    new_p = [[] for _ in range(4)]
    new_s = [[] for _ in range(4)]
    outs_p = (None, None)
    outs_s = (None, None)
    for l in range(DEPTH):
        sp = _small_params(l, params)
        yp, ypb, small_p, outs_p = _layer(yp, ypb, l, zero_small, (zero_ssd, 0), (zero_rw, 0), outs_p, sp,
                                     big, consts, nseq=bp, seqlen=lp_len, lru_nblk=bp,
                                     rwkv_nb=RWKV_PROMPT_NB)
        ys, ysb, small_s, outs_s = _layer(
            ys, ysb, l, (state_lru_conv[l], state_lru_h[l], state_ssd_conv[l], state_rwkv_shift[l]),
            (state_ssd, l), (state_rwkv, l), outs_s, sp, big, consts,
            nseq=bs, seqlen=ls_len, lru_nblk=1, rwkv_nb=RWKV_SAMPLE_NB)
        for i in range(4):
            new_p[i].append(small_p[i])
            new_s[i].append(small_s[i])
    p_lru_conv, p_lru_h, p_ssd_conv, p_rw_shift = [jnp.stack(v) for v in new_p]
    s_lru_conv, s_lru_h, s_ssd_conv, s_rw_shift = [jnp.stack(v) for v in new_s]
    return (yp.reshape(bp, lp_len, D_MODEL), ys.reshape(bs, ls_len, D_MODEL),
            p_lru_conv, p_lru_h, p_ssd_conv, outs_p[0], p_rw_shift, outs_p[1],
            s_lru_conv, s_lru_h, s_ssd_conv, outs_s[0], s_rw_shift, outs_s[1])
```

```python
import functools
import math

import jax
import jax.numpy as jnp
from jax import lax
from jax.experimental import pallas as pl
from jax.experimental.pallas import tpu as pltpu

F32 = jnp.float32
BF16 = jnp.bfloat16

D_MODEL = 2048
DEPTH = 2
D_MIX = 2 * D_MODEL
HEAD_DIM = 64
CONV_WIDTH = 4
LRU_WIDTH = D_MIX // 4
LRU_HEADS = 8
LRU_BLOCK = LRU_WIDTH // LRU_HEADS
LRU_C = 8.0
SSD_WIDTH = D_MIX // 2
SSD_HEADS = SSD_WIDTH // HEAD_DIM
SSD_GROUPS = 8
SSD_STATE = 128
SSD_CHUNK = 128
SSD_CONV_DIM = SSD_WIDTH + 2 * SSD_GROUPS * SSD_STATE
SSD_GROUP_WIDTH = SSD_WIDTH // SSD_GROUPS
SSD_HEADS_PER_GROUP = SSD_HEADS // SSD_GROUPS
RWKV_WIDTH = D_MIX - LRU_WIDTH - SSD_WIDTH
RWKV_HEADS = RWKV_WIDTH // HEAD_DIM
RWKV_R_W = max(32, int(round(1.8 * RWKV_WIDTH ** 0.5 / 32)) * 32)
RWKV_R_A = max(32, int(round(1.8 * RWKV_WIDTH ** 0.5 / 32)) * 32)
RWKV_R_G = max(32, int(round(0.6 * RWKV_WIDTH ** 0.8 / 32)) * 32)
RWKV_SHIFT = 3 * RWKV_WIDTH + RWKV_R_W + RWKV_R_A + RWKV_R_G
RWKV_GN_EPS = 64e-5
RWKV_CHUNK = 64
RWKV_PROMPT_NB = 2
D_FF = -(-(8 * D_MODEL) // (3 * 256)) * 256
ALPHA = (2 * DEPTH) ** 0.25
LN_EPS = 1e-5

LANES = 128
SUBLANES = 8
VMEM_LIMIT = 56 * 1024 * 1024

C_GATE = 0
C_LX = C_GATE + LRU_WIDTH
C_Z = C_LX + LRU_WIDTH
C_XBC = C_Z + SSD_WIDTH
C_R = C_XBC + SSD_CONV_DIM
C_K = C_R + RWKV_WIDTH
C_V = C_K + RWKV_WIDTH
C_G = C_V + RWKV_WIDTH
G_PAD = 2 * LANES
C_WA = C_G + G_PAD
WA_PAD = LANES
C_DT = C_WA + WA_PAD
DT_PAD = LANES
N_PROJ = C_DT + DT_PAD
assert RWKV_R_G <= G_PAD and RWKV_R_W + RWKV_R_A == WA_PAD and SSD_HEADS <= DT_PAD
RW_PACK = 3 * RWKV_WIDTH + G_PAD + WA_PAD

_O_DT = 2 * LRU_WIDTH + SSD_WIDTH + SSD_CONV_DIM
_O_RW = _O_DT + SSD_HEADS
_O_XW = 3 * RWKV_WIDTH
_O_XG = _O_XW + RWKV_R_W + RWKV_R_A


def _cparams(*sem):
    return pltpu.CompilerParams(dimension_semantics=sem, vmem_limit_bytes=VMEM_LIMIT)


_NN = (((1,), (0,)), ((), ()))
_NT = (((1,), (1,)), ((), ()))
_TN = (((0,), (0,)), ((), ()))


def _dg(a, b, dims):
    return lax.dot_general(a, b, dims, preferred_element_type=F32)


def _dot1(a, b, dims=_NN):
    return _dg(a.astype(BF16), b.astype(BF16), dims)


def _split2(x):
    hi = x.astype(BF16)
    lo = (x - hi.astype(F32)).astype(BF16)
    return hi, lo


def _split3(x):
    hi = x.astype(BF16)
    r1 = x - hi.astype(F32)
    mid = r1.astype(BF16)
    lo = (r1 - mid.astype(F32)).astype(BF16)
    return hi, mid, lo


def _dot3(a, b, dims=_NN):
    ah, al = _split2(a)
    bh, bl = _split2(b)
    free_axis = 1 if dims == _TN else 0
    m = a.shape[free_axis]
    both = _dg(jnp.concatenate([ah, al], axis=free_axis), bh, dims)
    return (both[:m] + both[m:]) + _dg(ah, bl, dims)


def _dot_shared_rhs(parts, w_exact, npieces):
    pieces = []
    for p in parts:
        rest = p
        for _ in range(npieces):
            piece = rest.astype(BF16).astype(F32)
            pieces.append(piece)
            rest = rest - piece
    prod = _dg(jnp.concatenate(pieces, axis=0).astype(BF16), w_exact, _NN)
    outs = []
    off = 0
    for p in parts:
        r = p.shape[0]
        acc = prod[off:off + r]
        for i in range(1, npieces):
            acc = acc + prod[off + i * r:off + (i + 1) * r]
        outs.append(acc)
        off += npieces * r
    return outs


def _dotx_left(w_exact, a, npieces=3):
    if npieces == 2:
        hi, lo = _split2(a)
        return _dg(w_exact, hi, _NN) + _dg(w_exact, lo, _NN)
    hi, mid, lo = _split3(a)
    return _dg(w_exact, hi, _NN) + (_dg(w_exact, mid, _NN) + _dg(w_exact, lo, _NN))


def _iota2(shape, dim):
    return lax.broadcasted_iota(jnp.int32, shape, dim)


def _softplus(x):
    return jnp.maximum(x, 0.0) + jnp.log1p(jnp.exp(-jnp.abs(x)))


def _sigmoid(x):
    return 1.0 / (1.0 + jnp.exp(-x))


def _silu(x):
    return x * _sigmoid(x)


def _gelu_tanh(x):
    c = math.sqrt(2.0 / math.pi)
    return 0.5 * x * (1.0 + jnp.tanh(c * (x + 0.044715 * (x * x * x))))


def _layer_norm(y, g, b):
    mu = jnp.mean(y, axis=-1, keepdims=True)
    d = y - mu
    var = jnp.mean(d * d, axis=-1, keepdims=True)
    return d * lax.rsqrt(var + LN_EPS) * g + b


DENSE_TM = 2048
DENSE_TN = 512
N_MAIN = C_R
N_TAIL = N_PROJ - N_MAIN
assert N_MAIN == _O_DT and N_MAIN % DENSE_TN == 0 and N_TAIL % DENSE_TN == 0 and D_FF % DENSE_TN == 0


def _proj_kernel(x_ref, wm_ref, wt_ref, o_ref):
    j = pl.program_id(1)

    @pl.when(j < N_MAIN // DENSE_TN)
    def _():
        o_ref[...] = _dg(x_ref[...], wm_ref[0].astype(BF16), _NT)

    @pl.when(j >= N_MAIN // DENSE_TN)
    def _():
        o_ref[...] = _dg(x_ref[...], wt_ref[0].astype(BF16), _NT)


def _proj(x, w_main_t, w_tail_t, l):
    m = x.shape[0]
    tm = min(m, DENSE_TM)
    tn = DENSE_TN
    nmain = N_MAIN // tn
    return pl.pallas_call(
        _proj_kernel,
        grid=(m // tm, N_PROJ // tn),
        in_specs=[pl.BlockSpec((tm, D_MODEL), lambda i, j: (i, 0)),
                  pl.BlockSpec((1, tn, D_MODEL), lambda i, j: (l, jnp.minimum(j, nmain - 1), 0)),
                  pl.BlockSpec((1, tn, D_MODEL), lambda i, j: (l, jnp.maximum(j - nmain, 0), 0))],
        out_specs=pl.BlockSpec((tm, tn), lambda i, j: (i, j)),
        out_shape=jax.ShapeDtypeStruct((m, N_PROJ), F32),
        compiler_params=_cparams("arbitrary", "arbitrary"),
        name="proj",
    )(x, w_main_t, w_tail_t)


OUT_KSTEPS = D_MIX // LRU_WIDTH


def _outproj_kernel(a_ref, b_ref, c_ref, w_ref, x_ref, g_ref, beta_ref, o_ref, ob_ref, acc_ref):
    k = pl.program_id(1)

    @pl.when(k == 0)
    def _():
        acc_ref[...] = jnp.dot(a_ref[...], w_ref[0], preferred_element_type=F32)

    @pl.when(jnp.logical_and(k > 0, k < OUT_KSTEPS - 1))
    def _():
        acc_ref[...] += jnp.dot(b_ref[...], w_ref[0], preferred_element_type=F32)

    @pl.when(k == OUT_KSTEPS - 1)
    def _():
        mix = acc_ref[...] + jnp.dot(c_ref[...], w_ref[0], preferred_element_type=F32)
        y = _layer_norm(ALPHA * x_ref[...] + mix, g_ref[...], beta_ref[...])
        o_ref[...] = y
        ob_ref[...] = y.astype(BF16)


def _outproj_ln(out_a, out_b, out_c, w_out, l, x, g, beta):
    m = x.shape[0]
    tm = min(m, 512)
    tk = LRU_WIDTH
    nb = SSD_WIDTH // tk
    return pl.pallas_call(
        _outproj_kernel,
        grid=(m // tm, OUT_KSTEPS),
        in_specs=[pl.BlockSpec((tm, tk), lambda i, k: (i, 0)),
                  pl.BlockSpec((tm, tk), lambda i, k: (i, jnp.clip(k - 1, 0, nb - 1))),
                  pl.BlockSpec((tm, tk), lambda i, k: (i, 0)),
                  pl.BlockSpec((1, tk, D_MODEL), lambda i, k: (l, k, 0)),
                  pl.BlockSpec((tm, D_MODEL), lambda i, k: (i, 0)),
                  pl.BlockSpec((1, D_MODEL), lambda i, k: (0, 0)),
                  pl.BlockSpec((1, D_MODEL), lambda i, k: (0, 0))],
        out_specs=[pl.BlockSpec((tm, D_MODEL), lambda i, k: (i, 0))] * 2,
        out_shape=[jax.ShapeDtypeStruct((m, D_MODEL), F32), jax.ShapeDtypeStruct((m, D_MODEL), BF16)],
        scratch_shapes=[pltpu.VMEM((tm, D_MODEL), F32)],
        compiler_params=_cparams("arbitrary", "arbitrary"),
        name="outproj_ln",
    )(out_a, out_b, out_c, w_out, x, g, beta)


def _ffn_up_kernel(x_ref, wg_ref, wu_ref, o_ref):
    xb = x_ref[...]
    gate = jnp.dot(xb, wg_ref[0].astype(BF16), preferred_element_type=F32)
    up = jnp.dot(xb, wu_ref[0].astype(BF16), preferred_element_type=F32)
    o_ref[...] = (_silu(gate) * up).astype(BF16)


def _ffn_up(x, wg, wu, l):
    m = x.shape[0]
    tm = min(m, DENSE_TM)
    tn = DENSE_TN
    return pl.pallas_call(
        _ffn_up_kernel,
        grid=(m // tm, D_FF // tn),
        in_specs=[pl.BlockSpec((tm, D_MODEL), lambda i, j: (i, 0)),
                  pl.BlockSpec((1, D_MODEL, tn), lambda i, j: (l, 0, j)),
                  pl.BlockSpec((1, D_MODEL, tn), lambda i, j: (l, 0, j))],
        out_specs=pl.BlockSpec((tm, tn), lambda i, j: (i, j)),
        out_shape=jax.ShapeDtypeStruct((m, D_FF), BF16),
        compiler_params=_cparams("arbitrary", "arbitrary"),
        name="ffn_up",
    )(x, wg, wu)


FFN_DOWN_KSTEPS = 2


LN_ROW_CHUNK = 128


def _ffn_down_kernel(h_ref, w_ref, x_ref, g_ref, beta_ref, o_ref, ob_ref):
    k = pl.program_id(1)

    @pl.when(k == 0)
    def _():
        o_ref[...] = jnp.dot(h_ref[...], w_ref[0], preferred_element_type=F32)

    @pl.when(k > 0)
    def _():
        o_ref[...] += jnp.dot(h_ref[...], w_ref[0], preferred_element_type=F32)

    @pl.when(k == FFN_DOWN_KSTEPS - 1)
    def _():
        chunk = min(LN_ROW_CHUNK, o_ref.shape[0])

        def ln_rows(r, carry):
            rows = pl.ds(pl.multiple_of(r * chunk, chunk), chunk)
            y = _layer_norm(ALPHA * x_ref[rows, :] + o_ref[rows, :], g_ref[...], beta_ref[...])
            o_ref[rows, :] = y
            ob_ref[rows, :] = y.astype(BF16)
            return carry

        lax.fori_loop(0, o_ref.shape[0] // chunk, ln_rows, 0)


def _ffn_down_ln(h, w_down, l, x, g, beta):
    m = x.shape[0]
    tm = min(m, 512)
    tk = D_FF // FFN_DOWN_KSTEPS
    assert tk * FFN_DOWN_KSTEPS == D_FF and tk % LANES == 0 and tm % min(LN_ROW_CHUNK, tm) == 0
    return pl.pallas_call(
        _ffn_down_kernel,
        grid=(m // tm, FFN_DOWN_KSTEPS),
        in_specs=[pl.BlockSpec((tm, tk), lambda i, k: (i, k)),
                  pl.BlockSpec((1, tk, D_MODEL), lambda i, k: (l, k, 0)),
                  pl.BlockSpec((tm, D_MODEL), lambda i, k: (i, 0)),
                  pl.BlockSpec((1, D_MODEL), lambda i, k: (0, 0)),
                  pl.BlockSpec((1, D_MODEL), lambda i, k: (0, 0))],
        out_specs=[pl.BlockSpec((tm, D_MODEL), lambda i, k: (i, 0))] * 2,
        out_shape=[jax.ShapeDtypeStruct((m, D_MODEL), F32), jax.ShapeDtypeStruct((m, D_MODEL), BF16)],
        compiler_params=_cparams("arbitrary", "arbitrary"),
        name="ffn_down_ln",
    )(h, w_down, x, g, beta)


def _conv_taps(u, first_rows, cw, cb, t_in_seq):
    out = cb + cw[CONV_WIDTH - 1:CONV_WIDTH, :] * u
    for j in range(1, CONV_WIDTH):
        shifted = jnp.where(t_in_seq < j, first_rows(j), pltpu.roll(u, j, 0))
        out = out + cw[CONV_WIDTH - 1 - j:CONV_WIDTH - j, :] * shifted
    return out


def _lru_kernel(gate_ref, lx_ref, buf_ref, h0_ref, cw_ref, cb_ref, wa_ref, ba_ref, wx_ref,
                bx_ref, lam_ref, out_ref, h1_ref, a_scr, b_scr, hin_scr, *, nseq, seqlen):
    rows = nseq * seqlen
    u = lx_ref[...]
    t_in_seq = _iota2((rows, LANES), 0) % seqlen
    if nseq == 1:
        buf = jnp.concatenate([buf_ref[0], jnp.zeros((rows - SUBLANES, LANES), F32)], axis=0) \
            if rows > SUBLANES else buf_ref[0]
    else:
        buf = buf_ref[0]
    nbuf = CONV_WIDTH - 1
    xc = _conv_taps(u, lambda j: pltpu.roll(buf, (j - nbuf) % rows, 0), cw_ref[...], cb_ref[...],
                    t_in_seq)

    r = _sigmoid(_dot1(xc, wa_ref[0]) + ba_ref[...])
    i = _sigmoid(_dot1(xc, wx_ref[0]) + bx_ref[...])
    log_a = (-LRU_C) * r * _softplus(-lam_ref[...])
    a = jnp.exp(log_a)
    b = jnp.sqrt(jnp.tanh(-log_a) * (a * a + 1.0)) * (i * xc)

    t8 = _iota2((rows, LANES), 0) % SUBLANES
    for s in (1, 2, 4):
        m = t8 >= s
        a_sh = pltpu.roll(a, s, 0)
        b_sh = pltpu.roll(b, s, 0)
        b = jnp.where(m, a * b_sh + b, b)
        a = jnp.where(m, a * a_sh, a)

    if seqlen == SUBLANES:
        h0 = h0_ref[0]
        hin = jnp.broadcast_to(h0[:, None, :], (nseq, SUBLANES, LANES)).reshape(rows, LANES)
        h = a * hin + b
        out_ref[...] = (h * _gelu_tanh(gate_ref[...])).astype(out_ref.dtype)
        a_scr[...] = h
        h1_ref[0] = a_scr[pl.ds(SUBLANES - 1, nseq, stride=SUBLANES), :]
    else:
        assert nseq == 1
        a_scr[...] = a
        b_scr[...] = b

        def carry_step(g, carry):
            base = pl.multiple_of(g * SUBLANES, SUBLANES)
            hin_scr[pl.ds(base, SUBLANES), :] = jnp.broadcast_to(carry, (SUBLANES, LANES))
            a7 = a_scr[pl.ds(base + SUBLANES - 1, 1), :]
            b7 = b_scr[pl.ds(base + SUBLANES - 1, 1), :]
            return a7 * carry + b7

        last = lax.fori_loop(0, rows // SUBLANES, carry_step, h0_ref[0])
        h = a_scr[...] * hin_scr[...] + b_scr[...]
        out_ref[...] = (h * _gelu_tanh(gate_ref[...])).astype(out_ref.dtype)
        h1_ref[0] = last


def _lru(proj, buf, h0, cw, cb, wa, ba, wx, bx, lam, *, nblk, nseq, seqlen):
    rows = nseq * seqlen
    gate_blk = C_GATE // LRU_BLOCK
    lx_blk = C_LX // LRU_BLOCK
    row = lambda s, h: (0, h)
    return pl.pallas_call(
        functools.partial(_lru_kernel, nseq=nseq, seqlen=seqlen),
        grid=(nblk, LRU_HEADS),
        in_specs=[pl.BlockSpec((rows, LRU_BLOCK), lambda s, h: (s, gate_blk + h)),
                  pl.BlockSpec((rows, LRU_BLOCK), lambda s, h: (s, lx_blk + h)),
                  pl.BlockSpec((1, nseq * SUBLANES, LRU_BLOCK), lambda s, h: (s, 0, h)),
                  pl.BlockSpec((1, nseq, LRU_BLOCK), lambda s, h: (s, 0, h)),
                  pl.BlockSpec((CONV_WIDTH, LRU_BLOCK), row),
                  pl.BlockSpec((1, LRU_BLOCK), row),
                  pl.BlockSpec((1, LRU_BLOCK, LRU_BLOCK), lambda s, h: (h, 0, 0)),
                  pl.BlockSpec((1, LRU_BLOCK), row),
                  pl.BlockSpec((1, LRU_BLOCK, LRU_BLOCK), lambda s, h: (h, 0, 0)),
                  pl.BlockSpec((1, LRU_BLOCK), row),
                  pl.BlockSpec((1, LRU_BLOCK), row)],
        out_specs=[pl.BlockSpec((rows, LRU_BLOCK), lambda s, h: (s, h)),
                   pl.BlockSpec((1, nseq, LRU_BLOCK), lambda s, h: (s, 0, h))],
        out_shape=[jax.ShapeDtypeStruct((nblk * rows, LRU_WIDTH), BF16),
                   jax.ShapeDtypeStruct((nblk, nseq, LRU_WIDTH), F32)],
        scratch_shapes=[pltpu.VMEM((rows, LANES), F32)] * 3,
        compiler_params=_cparams("arbitrary", "arbitrary"),
        name="lru",
    )(proj, proj, buf, h0, cw, cb, wa, ba, wx, bx, lam)


def _stacked_state_call(kernel_fn, grid, in_specs, operands, row_spec, row_shape, state_dims,
                        l_out, prev_out, scratch_shapes, name, nb=1):
    nseq = grid[0] * nb
    zeros = (0,) * len(state_dims)
    state_shape = jax.ShapeDtypeStruct((DEPTH, nseq) + tuple(state_dims), F32)
    aliases = {}
    if prev_out is None:
        assert l_out == 0
        state_spec = pl.BlockSpec((DEPTH, nb) + tuple(state_dims), lambda b, c: (0, b) + zeros)
    else:
        state_spec = pl.BlockSpec((1, nb) + tuple(state_dims), lambda b, c: (l_out, b) + zeros)
        in_specs = in_specs + [pl.BlockSpec(memory_space=pl.ANY)]
        operands = operands + [prev_out]
        aliases = {len(operands) - 1: 1}
        kernel_fn = functools.partial(_drop_alias_ref, kernel_fn, len(operands) - 1)
    return pl.pallas_call(
        kernel_fn, grid=grid, in_specs=in_specs, out_specs=[row_spec, state_spec],
        out_shape=[row_shape, state_shape], scratch_shapes=scratch_shapes,
        input_output_aliases=aliases, compiler_params=_cparams("arbitrary", "arbitrary"),
        name=name)(*operands)


def _drop_alias_ref(kernel_fn, pos, *refs):
    return kernel_fn(*refs[:pos], *refs[pos + 1:])


def _zero_other_slabs(state_ref):
    if state_ref.shape[0] > 1:
        state_ref[1:] = jnp.zeros((state_ref.shape[0] - 1,) + state_ref.shape[1:], state_ref.dtype)


def _transpose_rows_to_lanes(x, t):
    if t < LANES:
        x = jnp.concatenate([x, jnp.zeros((LANES - t, LANES), F32)], axis=0)
    return x.T[:, :t]


def _ssd_kernel(z_ref, xbc_ref, dt_ref, buf_ref, s0_ref, cw_ref, cb_ref, dtb_ref, alog_ref,
                dch_ref, ng_ref, ehp_ref, out_ref, s1_ref, tail_ref, pad_ref, y_scr, *, T):
    c = pl.program_id(1)
    n_state = SSD_STATE
    gw = SSD_GROUP_WIDTH
    hpg = SSD_HEADS_PER_GROUP

    @pl.when(c == 0)
    def _():
        tail_ref[...] = buf_ref[0]
        s1_ref[0:1] = s0_ref[...]
        _zero_other_slabs(s1_ref)

    u = xbc_ref[...]
    pad_ref[0:SUBLANES, :] = tail_ref[...]
    pad_ref[SUBLANES:SUBLANES + T, :] = u
    cw = cw_ref[...]
    xc = cb_ref[...] + cw[CONV_WIDTH - 1:CONV_WIDTH, :] * u
    for j in range(1, CONV_WIDTH):
        xc = xc + cw[CONV_WIDTH - 1 - j:CONV_WIDTH - j, :] * pad_ref[SUBLANES - j:SUBLANES - j + T, :]
    tail_ref[...] = pad_ref[T:T + SUBLANES, :]

    xa = _silu(xc)
    xs = xa[:, :SSD_WIDTH]
    bm = xa[:, SSD_WIDTH:SSD_WIDTH + SSD_GROUPS * n_state]
    cm = xa[:, SSD_WIDTH + SSD_GROUPS * n_state:]
    dt = _softplus(dt_ref[...] + dtb_ref[...])
    da = dt * (-jnp.exp(alog_ref[...]))
    ii = _iota2((T, T), 0)
    jj = _iota2((T, T), 1)
    causal = ii >= jj
    cum = _dotx_left(causal.astype(BF16), da)
    cum_t = _transpose_rows_to_lanes(cum, T)
    ehp = ehp_ref[...]
    cumx, dtx = _dot_shared_rhs([cum, dt], ehp, 2)
    xdt = xs * dtx
    ecum = jnp.exp(cumx)
    xdtd = xdt * jnp.exp(cumx[T - 1:T, :] - cumx)

    groups = range(SSD_GROUPS)
    heads = range(SSD_HEADS)
    cg = [cm[:, g * n_state:(g + 1) * n_state] for g in groups]
    bg = [bm[:, g * n_state:(g + 1) * n_state] for g in groups]
    sg = [s1_ref[0, 0, g * hpg:(g + 1) * hpg].reshape(gw, n_state) for g in groups]
    cb = [_dot1(cg[g], bg[g], _NT) for g in groups]
    y_off = [_dot1(cg[g], sg[g], _NT) for g in groups]
    st = [_dot1(xdtd[:, g * gw:(g + 1) * gw], bg[g], _TN) for g in groups]
    lm = [jnp.where(causal, jnp.exp(cum[:, h:h + 1] - cum_t[h:h + 1, :]), 0.0) for h in heads]
    y_diag = [_dot1(cb[h // hpg] * lm[h], xdt[:, h * HEAD_DIM:(h + 1) * HEAD_DIM]) for h in heads]
    for g in groups:
        y_scr[:, g * gw:(g + 1) * gw] = (jnp.concatenate(y_diag[g * hpg:(g + 1) * hpg], axis=1)
                                         + y_off[g] * ecum[:, g * gw:(g + 1) * gw])
        decay = jnp.concatenate(
            [jnp.broadcast_to(jnp.exp(cum_t[h:h + 1, T - 1:T]), (HEAD_DIM, n_state))
             for h in range(g * hpg, (g + 1) * hpg)], axis=0)
        s1_ref[0, 0, g * hpg:(g + 1) * hpg] = (sg[g] * decay + st[g]).reshape(hpg, HEAD_DIM, n_state)

    y = (y_scr[...] + dch_ref[...] * xs) * _silu(z_ref[...])
    outs = []
    for g in groups:
        yg = y[:, g * gw:(g + 1) * gw]
        ms = jnp.mean(yg * yg, axis=-1, keepdims=True)
        outs.append(yg * lax.rsqrt(ms + 1e-5))
    out_ref[...] = (jnp.concatenate(outs, axis=1) * ng_ref[...]).astype(out_ref.dtype)


def _ssd(proj, buf, s0, l_in, l_out, prev_out, cw, cb, dtb, alog, dch, ng, ehp, *, nseq, seqlen):
    T = min(SSD_CHUNK, seqlen)
    nc = seqlen // T
    const = lambda b, c: (0, 0)
    sblk = (1, 1, SSD_HEADS, HEAD_DIM, SSD_STATE)
    in_specs = [pl.BlockSpec((T, SSD_WIDTH), lambda b, c: (b * nc + c, C_Z // SSD_WIDTH)),
                pl.BlockSpec((T, SSD_CONV_DIM), lambda b, c: (b * nc + c, C_XBC // SSD_CONV_DIM)),
                pl.BlockSpec((T, DT_PAD), lambda b, c: (b * nc + c, C_DT // DT_PAD)),
                pl.BlockSpec((1, SUBLANES, SSD_CONV_DIM), lambda b, c: (b, 0, 0)),
                pl.BlockSpec(sblk, lambda b, c: (l_in, b, 0, 0, 0)),
                pl.BlockSpec((CONV_WIDTH, SSD_CONV_DIM), const),
                pl.BlockSpec((1, SSD_CONV_DIM), const),
                pl.BlockSpec((1, DT_PAD), const),
                pl.BlockSpec((1, DT_PAD), const),
                pl.BlockSpec((1, SSD_WIDTH), const),
                pl.BlockSpec((1, SSD_WIDTH), const),
                pl.BlockSpec((DT_PAD, SSD_WIDTH), const)]
    return _stacked_state_call(
        functools.partial(_ssd_kernel, T=T), (nseq, nc), in_specs,
        [proj, proj, proj, buf, s0, cw, cb, dtb, alog, dch, ng, ehp],
        pl.BlockSpec((T, SSD_WIDTH), lambda b, c: (b * nc + c, 0)),
        jax.ShapeDtypeStruct((nseq * seqlen, SSD_WIDTH), BF16),
        sblk[2:], l_out, prev_out,
        [pltpu.VMEM((SUBLANES, SSD_CONV_DIM), F32),
         pltpu.VMEM((T + SUBLANES, SSD_CONV_DIM), F32),
         pltpu.VMEM((T, SSD_WIDTH), F32)],
        "ssd")


def _unit_lower_inverse(a_list, ii, jj, T):
    n = len(a_list)
    pair = (ii >> 1) == (jj >> 1)
    inv = [jnp.where(ii == jj, 1.0, 0.0) + jnp.where(pair, a_list[h], 0.0) for h in range(n)]
    shift = 1
    while (2 << shift) <= T:
        band = jnp.logical_and((ii >> (shift + 1)) == (jj >> (shift + 1)),
                               (ii >> shift) != (jj >> shift))
        x = [_dot1(jnp.where(band, a_list[h], 0.0), inv[h]) for h in range(n)]
        inv = [inv[h] + _dot1(inv[h], x[h]) for h in range(n)]
        shift += 1
    return inv


def _rwkv_kernel(r_ref, k_ref, v_ref, g_ref, wa_ref, sh_ref, s0_ref, mu_ref, w0_ref, wup_ref,
                 a0_ref, aup_ref, gup_ref, kkw_ref, kaw_ref, rkw_ref, lng_ref, lnb_ref, ones_ref,
                 out_ref, s1_ref, prev_scr, s_scr, al_scr, be_scr, kt_scr, rt_scr, bs_scr,
                 ks_scr, v_scr, gam_scr, o_scr, *, nb, T, nchunks):
    c = pl.program_id(1)
    W = RWKV_WIDTH
    D = HEAD_DIM
    R = nb * T
    nh = RWKV_HEADS

    @pl.when(c == 0)
    def _():
        prev_scr[...] = sh_ref[...].reshape(nb, RW_PACK)
        s_scr[...] = s0_ref[0].reshape(nb * nh, D, D)

    def rows_of(ref3):
        return ref3[...].reshape(R, ref3.shape[-1])

    def per_seq_rows(x):
        return jnp.broadcast_to(x[:, None, :], (nb, T, x.shape[-1])).reshape(R, x.shape[-1])

    def last_rows(x):
        return jnp.concatenate([x[(b + 1) * T - 1:(b + 1) * T, :] for b in range(nb)], axis=0)

    def token_shift(p, lo, hi):
        first = _iota2(p.shape, 0) % T == 0
        prev = jnp.where(first, per_seq_rows(prev_scr[:, lo:hi]), pltpu.roll(p, 1, 0))
        prev_scr[:, lo:hi] = last_rows(p)
        return p + (prev - p) * mu_ref[:, lo:hi]

    xr = token_shift(rows_of(r_ref), 0, W)
    xk = token_shift(rows_of(k_ref), W, 2 * W)
    xv = token_shift(rows_of(v_ref), 2 * W, 3 * W)
    xg = token_shift(rows_of(g_ref), 3 * W, 3 * W + G_PAD)
    xwa = token_shift(rows_of(wa_ref), 3 * W + G_PAD, RW_PACK)

    w_lin = w0_ref[...] + _dot1(jnp.tanh(xwa), wup_ref[...])
    a = _sigmoid(a0_ref[...] + _dot1(xwa, aup_ref[...]))
    gate = _dot1(_sigmoid(xg), gup_ref[...])
    lw = -jnp.exp(-_softplus(-w_lin) - 0.5)

    ones = ones_ref[...]

    def head_sums(xs, npieces=3):
        tw = ones.shape[0]
        nt = W // tw
        tiles = [x[:, i * tw:(i + 1) * tw] for x in xs for i in range(nt)]
        sums = _dot_shared_rhs(tiles, ones, npieces)
        return [jnp.concatenate(sums[n * nt:(n + 1) * nt], axis=1) for n in range(len(xs))]

    kk = xk * kkw_ref[...]
    kp = xk * (1.0 + (a - 1.0) * kaw_ref[...])
    kk_sq, rk_sum = head_sums([kk * kk, xr * kp * rkw_ref[...]], 1)
    kk = kk / jnp.maximum(jnp.sqrt(kk_sq), 1e-12)

    ri = _iota2((R, R), 0)
    rj = _iota2((R, R), 1)
    same_seq_causal = jnp.logical_and(ri >= rj, ri // T == rj // T)
    cum = _dotx_left(same_seq_causal.astype(BF16), lw, 3)
    e_neg = jnp.exp(-cum)
    gam = jnp.exp(per_seq_rows(last_rows(cum)))
    be = kk * a * e_neg
    kt = kp * e_neg
    al_scr[...] = -kk * jnp.exp(cum - lw)
    be_scr[...] = be
    kt_scr[...] = kt
    rt_scr[...] = xr * jnp.exp(cum)
    bs_scr[...] = be * gam
    ks_scr[...] = kt * gam
    v_scr[...] = xv
    gam_scr[...] = gam

    ii = _iota2((T, T), 0)
    jj = _iota2((T, T), 1)
    incl = ii >= jj
    strict = ii > jj
    eye_d = _iota2((D, D), 0) == _iota2((D, D), 1)

    units = [(b, h) for b in range(nb) for h in range(nh)]
    nu = range(len(units))

    def per_unit(ref):
        return [ref[b * T:(b + 1) * T, h * D:(h + 1) * D] for b, h in units]

    al, be_u, kt_u, rt = per_unit(al_scr), per_unit(be_scr), per_unit(kt_scr), per_unit(rt_scr)
    bs_u, ks_u, vv, gam_u = per_unit(bs_scr), per_unit(ks_scr), per_unit(v_scr), per_unit(gam_scr)
    gram = [_dot1(jnp.concatenate([al[u], rt[u]], axis=0),
                  jnp.concatenate([be_u[u], kt_u[u]], axis=0), _NT) for u in nu]
    a_ab = [jnp.where(strict, gram[u][:T, :T], 0.0) for u in nu]
    a_ak = [jnp.where(strict, gram[u][:T, T:], 0.0) for u in nu]
    r_b = [jnp.where(incl, gram[u][T:, :T], 0.0) for u in nu]
    r_k = [jnp.where(incl, gram[u][T:, T:], 0.0) for u in nu]
    inv = _unit_lower_inverse(a_ab, ii, jj, T)
    akv = [_dot1(a_ak[u], vv[u]) for u in nu]
    pw = [_dot1(inv[u], jnp.concatenate([al[u], akv[u]], axis=1)) for u in nu]
    qo = [_dot1(r_b[u], pw[u]) for u in nu]
    rkv = [_dot1(r_k[u], vv[u]) for u in nu]
    smat = [s_scr[u] for u in nu]
    qs = [_dot1(rt[u] + qo[u][:, :D], smat[u], _NT) for u in nu]
    gp = [_dot3(bs_u[u], pw[u][:, :D], _TN) for u in nu]
    zt = [_dot3(jnp.concatenate([pw[u][:, D:], vv[u]], axis=0),
                jnp.concatenate([bs_u[u], ks_u[u]], axis=0), _TN) for u in nu]
    gmat = [jnp.where(eye_d, jnp.broadcast_to(gam_u[u][0:1, :], (D, D)), 0.0) + gp[u] for u in nu]
    sg = [_dot3(smat[u], gmat[u], _NT) for u in nu]
    for u in nu:
        s_scr[u] = sg[u] + zt[u]
    pairs = LANES // D
    for b in range(nb):
        for p in range(nh // pairs):
            us = [b * nh + p * pairs + s for s in range(pairs)]
            o_scr[b * T:(b + 1) * T, p * LANES:(p + 1) * LANES] = jnp.concatenate(
                [qs[u] + qo[u][:, D:] + rkv[u] for u in us], axis=1)

    o = o_scr[...]
    mean = head_sums([o])[0] * (1.0 / D)
    d = o - mean
    var = head_sums([d * d])[0] * (1.0 / D)
    on = d * lax.rsqrt(var + RWKV_GN_EPS) * lng_ref[...] + lnb_ref[...]
    bonus = rk_sum * xv
    out_ref[...] = ((on + bonus) * gate).astype(out_ref.dtype).reshape(nb, T, W)

    @pl.when(c == nchunks - 1)
    def _():
        s1_ref[0] = s_scr[...].reshape(nb, nh, D, D)
        _zero_other_slabs(s1_ref)


def _rwkv(proj3, sh, s0, l_in, l_out, prev_out, mu, w0, wup, a0, aup, gup, kkw, kaw, rkw, lng, lnb,
          ones, *, nb):
    nseq, seqlen, _ = proj3.shape
    T = min(RWKV_CHUNK, seqlen)
    nc = seqlen // T
    W = RWKV_WIDTH
    const = lambda b, c: (0, 0)
    rowblk = lambda col, width: pl.BlockSpec((nb, T, width), lambda b, c: (b, c, col // width))
    sdims = (RWKV_HEADS, HEAD_DIM, HEAD_DIM)
    vec = pl.BlockSpec((1, W), const)
    in_specs = [rowblk(C_R, W), rowblk(C_K, W), rowblk(C_V, W), rowblk(C_G, G_PAD),
                rowblk(C_WA, WA_PAD),
                pl.BlockSpec((nb, 1, RW_PACK), lambda b, c: (b, 0, 0)),
                pl.BlockSpec((1, nb) + sdims, lambda b, c: (l_in, b, 0, 0, 0)),
                pl.BlockSpec((1, RW_PACK), const),
                vec, pl.BlockSpec((WA_PAD, W), const),
                vec, pl.BlockSpec((WA_PAD, W), const),
                pl.BlockSpec((G_PAD, W), const),
                vec, vec, vec, vec, vec,
                pl.BlockSpec((2 * LANES, 2 * LANES), const)]
    rows = nb * T
    return _stacked_state_call(
        functools.partial(_rwkv_kernel, nb=nb, T=T, nchunks=nc), (nseq // nb, nc), in_specs,
        [proj3, proj3, proj3, proj3, proj3, sh, s0, mu, w0, wup, a0, aup, gup, kkw, kaw, rkw, lng,
         lnb, ones],
        pl.BlockSpec((nb, T, W), lambda b, c: (b, c, 0)),
        jax.ShapeDtypeStruct((nseq, seqlen, W), BF16),
        sdims, l_out, prev_out,
        [pltpu.VMEM((nb, RW_PACK), F32), pltpu.VMEM((nb * RWKV_HEADS, HEAD_DIM, HEAD_DIM), F32)]
        + [pltpu.VMEM((rows, W), F32)] * 7
        + [pltpu.VMEM((rows, W), F32), pltpu.VMEM((rows, W), F32)],
        "rwkv", nb=nb)


def _rws_pre_kernel(r_ref, k_ref, v_ref, g_ref, wa_ref, shr_ref, shk_ref, shv_ref, shg_ref,
                    shwa_ref, mur_ref, muk_ref, muv_ref, mug_ref, muwa_ref, w0_ref, wup_ref,
                    a0_ref, aup_ref, gup_ref, kkw_ref, kaw_ref, rkw_ref, ones_ref,
                    rt_ref, wt_ref, knt_ref, kat_ref, kpt_ref, vt_ref, gate_ref, bonus_ref):
    steps, nseq, _ = r_ref.shape

    def shifted(ref, sh_ref, mu_ref):
        x = ref[...]
        prev = jnp.concatenate([sh_ref[...][None], x[:steps - 1]], axis=0)
        return (x + (prev - x) * mu_ref[...]).reshape(steps * nseq, x.shape[-1])

    xr = shifted(r_ref, shr_ref, mur_ref)
    xk = shifted(k_ref, shk_ref, muk_ref)
    xv = shifted(v_ref, shv_ref, muv_ref)
    xg = shifted(g_ref, shg_ref, mug_ref)
    xwa = shifted(wa_ref, shwa_ref, muwa_ref)

    w_lin = w0_ref[...] + _dot1(jnp.tanh(xwa), wup_ref[...])
    a = _sigmoid(a0_ref[...] + _dot1(xwa, aup_ref[...]))
    gate = _dot1(_sigmoid(xg), gup_ref[...])
    decay = jnp.exp(-jnp.exp(-_softplus(-w_lin) - 0.5))
    kk = xk * kkw_ref[...]
    kp = xk * (1.0 + (a - 1.0) * kaw_ref[...])
    kk_sq, rk_sum = _dot_shared_rhs([kk * kk, xr * kp * rkw_ref[...]], ones_ref[...], 3)
    kk = kk / jnp.maximum(jnp.sqrt(kk_sq), 1e-12)

    for t in range(steps):
        rows = slice(t * nseq, (t + 1) * nseq)
        rt_ref[t] = xr[rows].T
        wt_ref[t] = decay[rows].T
        knt_ref[t] = (-kk[rows]).T
        kat_ref[t] = (kk[rows] * a[rows]).T
        kpt_ref[t] = kp[rows].T
        vt_ref[t] = xv[rows].T
    gate_ref[...] = gate.reshape(gate_ref.shape)
    bonus_ref[...] = (rk_sum * xv).reshape(bonus_ref.shape)


def _rws_rec_kernel(r_ref, w_ref, kn_ref, ka_ref, kp_ref, v_ref, s0_ref, o_ref, s1_ref):
    steps = r_ref.shape[0]

    def value_rows(vc, carry):
        v0 = pl.multiple_of(vc * SUBLANES, SUBLANES)
        s = [s0_ref[0, 0, v0 + i] for i in range(SUBLANES)]
        for t in range(steps):
            kn, ka, kp, w, r = kn_ref[t], ka_ref[t], kp_ref[t], w_ref[t], r_ref[t]
            vv = v_ref[t, pl.ds(v0, SUBLANES), :]
            outs = []
            for i in range(SUBLANES):
                sa = jnp.sum(s[i] * kn, axis=0, keepdims=True)
                s[i] = s[i] * w + (sa * ka + vv[i:i + 1, :] * kp)
                outs.append(jnp.sum(s[i] * r, axis=0, keepdims=True))
            o_ref[t, pl.ds(v0, SUBLANES), :] = jnp.concatenate(outs, axis=0)
        for i in range(SUBLANES):
            s1_ref[0, 0, v0 + i] = s[i]
        return carry

    lax.fori_loop(0, HEAD_DIM // SUBLANES, value_rows, 0)
    _zero_other_slabs(s1_ref)


def _rws_post_kernel(o_ref, gate_ref, bonus_ref, lng_ref, lnb_ref, ones_ref, out_ref):
    steps, _, nseq = o_ref.shape
    o = jnp.concatenate([o_ref[t].T for t in range(steps)], axis=0)
    ones = ones_ref[...]
    mean = _dot_shared_rhs([o], ones, 3)[0] * (1.0 / HEAD_DIM)
    d = o - mean
    var = _dot_shared_rhs([d * d], ones, 3)[0] * (1.0 / HEAD_DIM)
    on = d * lax.rsqrt(var + RWKV_GN_EPS) * lng_ref[...] + lnb_ref[...]
    rows = steps * nseq
    res = (on + bonus_ref[...].reshape(rows, LANES)) * gate_ref[...].reshape(rows, LANES)
    out_ref[...] = res.astype(out_ref.dtype).reshape(out_ref.shape)


def _rwkv_steps(p3, sh, s0_t, l_in, l_out, prev_out, mu, w0, wup, a0, aup, gup, kkw, kaw, rkw, lng,
                lnb, ones):
    nseq, steps, _ = p3.shape
    W, D, H = RWKV_WIDTH, HEAD_DIM, RWKV_HEADS
    pt = jnp.transpose(p3[:, :, C_R:], (1, 0, 2))
    nblk = W // LANES
    ones_blk = ones[:LANES, :LANES]
    g_col, wa_col = 3 * W, 3 * W + G_PAD

    def cols(rows, width, col):
        return pl.BlockSpec(rows + (width,), lambda p: (0,) * len(rows) + (col // width,))

    def cols_p(rows, base):
        return pl.BlockSpec(rows + (LANES,), lambda p: (0,) * len(rows) + (base // LANES + p,))

    tb = (steps, nseq)
    in_specs = ([cols_p(tb, 0), cols_p(tb, W), cols_p(tb, 2 * W), cols(tb, G_PAD, g_col),
                 cols(tb, WA_PAD, wa_col)]
                + [cols_p((nseq,), 0), cols_p((nseq,), W), cols_p((nseq,), 2 * W),
                   cols((nseq,), G_PAD, g_col), cols((nseq,), WA_PAD, wa_col)]
                + [cols_p((1,), 0), cols_p((1,), W), cols_p((1,), 2 * W), cols((1,), G_PAD, g_col),
                   cols((1,), WA_PAD, wa_col)]
                + [cols_p((1,), 0), cols_p((WA_PAD,), 0), cols_p((1,), 0), cols_p((WA_PAD,), 0),
                   cols_p((G_PAD,), 0), cols_p((1,), 0), cols_p((1,), 0), cols_p((1,), 0),
                   pl.BlockSpec((LANES, LANES), lambda p: (0, 0))])
    t_spec = pl.BlockSpec((steps, LANES, nseq), lambda p: (0, p, 0))
    n_spec = pl.BlockSpec((steps, nseq, LANES), lambda p: (0, 0, p))
    t_shape = jax.ShapeDtypeStruct((steps, W, nseq), F32)
    n_shape = jax.ShapeDtypeStruct((steps, nseq, W), F32)
    pre = pl.pallas_call(
        _rws_pre_kernel, grid=(nblk,), in_specs=in_specs,
        out_specs=[t_spec] * 6 + [n_spec] * 2, out_shape=[t_shape] * 6 + [n_shape] * 2,
        compiler_params=_cparams("arbitrary"), name="rwkv_pre",
    )(pt, pt, pt, pt, pt, sh, sh, sh, sh, sh, mu, mu, mu, mu, mu, w0, wup, a0, aup, gup, kkw, kaw,
      rkw, ones_blk)
    rt, wt, knt, kat, kpt, vt, gate, bonus = pre

    head = pl.BlockSpec((steps, D, nseq), lambda h, c: (0, h, 0))
    o_t, stack = _stacked_state_call(
        _rws_rec_kernel, (H, 1),
        [head] * 6 + [pl.BlockSpec((1, 1, D, D, nseq), lambda h, c: (l_in, h, 0, 0, 0))],
        [rt, wt, knt, kat, kpt, vt, s0_t],
        head, t_shape, (D, D, nseq), l_out, prev_out, [], "rwkv_rec")

    out = pl.pallas_call(
        _rws_post_kernel, grid=(nblk,),
        in_specs=[t_spec, n_spec, n_spec, pl.BlockSpec((1, LANES), lambda p: (0, p)),
                  pl.BlockSpec((1, LANES), lambda p: (0, p)),
                  pl.BlockSpec((LANES, LANES), lambda p: (0, 0))],
        out_specs=n_spec, out_shape=jax.ShapeDtypeStruct((steps, nseq, W), BF16),
        compiler_params=_cparams("arbitrary"), name="rwkv_post",
    )(o_t, gate, bonus, lng, lnb, ones_blk)
    return jnp.transpose(out, (1, 0, 2)).reshape(nseq * steps, W), stack


def _zeros_like_cols(x, n):
    return jnp.zeros(x.shape[:-1] + (n,), x.dtype)


def _pack_rwkv_cols(x):
    return jnp.concatenate([x[..., :_O_XW], x[..., _O_XG:], _zeros_like_cols(x, G_PAD - RWKV_R_G),
                            x[..., _O_XW:_O_XG]], axis=-1)


def _pack_tail_cols(w):
    return jnp.concatenate([_pack_rwkv_cols(w[..., _O_RW:]), w[..., _O_DT:_O_RW],
                            _zeros_like_cols(w, DT_PAD - SSD_HEADS)], axis=-1)


def _pad_rows(x, n_before, n_total):
    b, r, c = x.shape
    return jnp.concatenate([jnp.zeros((b, n_before, c), x.dtype), x,
                            jnp.zeros((b, n_total - n_before - r, c), x.dtype)], axis=1)


def _small_params(l, p):
    row = lambda v: v.reshape(1, -1)
    pad_lanes = lambda v, n: jnp.concatenate([v, jnp.zeros((n - v.shape[0],), v.dtype)]).reshape(1, n)
    zeros_w = jnp.zeros((RWKV_R_W, RWKV_WIDTH), F32)
    return dict(
        lru=(p['lru_conv_w'][l], row(p['lru_conv_b'][l]), p['lru_wa'][l].astype(BF16),
             row(p['lru_ba'][l]), p['lru_wx'][l].astype(BF16), row(p['lru_bx'][l]),
             row(p['lru_lambda'][l])),
        ssd=(p['ssd_conv_w'][l], row(p['ssd_conv_b'][l]), pad_lanes(p['ssd_dt_bias'][l], DT_PAD),
             pad_lanes(p['ssd_a_log'][l], DT_PAD), row(jnp.repeat(p['ssd_d'][l], HEAD_DIM)),
             row(p['ssd_norm_g'][l])),
        rwkv=(row(_pack_rwkv_cols(p['rwkv_mu'][l])), row(p['rwkv_w0'][l]),
              jnp.concatenate([p['rwkv_w_up'][l], zeros_w], axis=0).astype(BF16),
              row(p['rwkv_a0'][l]),
              jnp.concatenate([zeros_w, p['rwkv_a_up'][l]], axis=0).astype(BF16),
              jnp.concatenate([p['rwkv_g_up'][l],
                               jnp.zeros((G_PAD - RWKV_R_G, RWKV_WIDTH), F32)], axis=0).astype(BF16),
              row(p['rwkv_k_k'][l]), row(p['rwkv_k_a'][l]), row(p['rwkv_r_k'][l]),
              row(p['rwkv_ln_g'][l]), row(p['rwkv_ln_b'][l])),
        ln1=(row(p['ln1_g'][l]), row(p['ln1_b'][l])),
        ln2=(row(p['ln2_g'][l]), row(p['ln2_b'][l])),
    )


def _constants():
    lane = jnp.arange(SSD_WIDTH) // HEAD_DIM
    ehp = (jnp.arange(DT_PAD)[:, None] == lane[None, :]).astype(BF16)
    blk = jnp.arange(2 * LANES) // HEAD_DIM
    ones = (blk[:, None] == blk[None, :]).astype(BF16)
    return ehp, ones


def _layer(x, xb, l, small_state, ssd_in, rw_in, prev_outs, sp, big, consts, *, nseq, seqlen,
           lru_nblk, rwkv_nb):
    lru_conv0, lru_h0, ssd_conv0, rw_shift0 = small_state
    ehp, ones = consts
    nbuf = CONV_WIDTH - 1
    proj = _proj(xb, big['w_main'], big['w_tail'], l)

    lru_nseq = nseq // lru_nblk
    out_a, lru_h1 = _lru(
        proj, _pad_rows(lru_conv0, 0, SUBLANES).reshape(lru_nblk, lru_nseq * SUBLANES, LRU_WIDTH),
        lru_h0.reshape(lru_nblk, lru_nseq, LRU_WIDTH), *sp['lru'],
        nblk=lru_nblk, nseq=lru_nseq, seqlen=seqlen)
    out_b, ssd_out = _ssd(proj, _pad_rows(ssd_conv0, SUBLANES - nbuf, SUBLANES), ssd_in[0],
                          ssd_in[1], l, prev_outs[0], *sp['ssd'], ehp, nseq=nseq, seqlen=seqlen)
    p3 = proj.reshape(nseq, seqlen, N_PROJ)
    if rwkv_nb is None:
        out_c, rw_out = _rwkv_steps(p3, _pack_rwkv_cols(rw_shift0), rw_in[0], rw_in[1], l,
                                    prev_outs[1], *sp['rwkv'], ones)
    else:
        out_c, rw_out = _rwkv(p3, _pack_rwkv_cols(rw_shift0)[:, None, :], rw_in[0], rw_in[1], l,
                              prev_outs[1], *sp['rwkv'], ones, nb=rwkv_nb)
        out_c = out_c.reshape(nseq * seqlen, RWKV_WIDTH)

    y, yb = _outproj_ln(out_a, out_b, out_c, big['w_out'], l, x, *sp['ln1'])
    y, yb = _ffn_down_ln(_ffn_up(yb, big['w_gate'], big['w_up'], l), big['w_down'], l, y, *sp['ln2'])

    tail = lambda col, width: p3[:, seqlen - nbuf:, col:col + width]
    last = lambda col, width: p3[:, seqlen - 1, col:col + width]
    rw_shift1 = jnp.concatenate([last(C_R, 3 * RWKV_WIDTH), last(C_WA, WA_PAD),
                                 last(C_G, RWKV_R_G)], axis=-1)
    small_new = (tail(C_LX, LRU_WIDTH), lru_h1.reshape(nseq, LRU_WIDTH),
                 tail(C_XBC, SSD_CONV_DIM), rw_shift1)
    return y, yb, small_new, (ssd_out, rw_out)


def kernel(x_prompt, x_sample, state_lru_conv, state_lru_h, state_ssd_conv, state_ssd,
           state_rwkv_shift, state_rwkv, w_in, lru_conv_w, lru_conv_b, lru_wa, lru_ba, lru_wx,
           lru_bx, lru_lambda, ssd_conv_w, ssd_conv_b, ssd_dt_bias, ssd_a_log, ssd_d, ssd_norm_g,
           rwkv_mu, rwkv_w0, rwkv_w_up, rwkv_a0, rwkv_a_up, rwkv_g_up, rwkv_k_k, rwkv_k_a,
           rwkv_r_k, rwkv_ln_g, rwkv_ln_b, w_out, ln1_g, ln1_b, w_gate, w_up, w_down, ln2_g, ln2_b):
    params = dict(
        lru_conv_w=lru_conv_w, lru_conv_b=lru_conv_b, lru_wa=lru_wa, lru_ba=lru_ba,
        lru_wx=lru_wx, lru_bx=lru_bx, lru_lambda=lru_lambda, ssd_conv_w=ssd_conv_w,
        ssd_conv_b=ssd_conv_b, ssd_dt_bias=ssd_dt_bias, ssd_a_log=ssd_a_log, ssd_d=ssd_d,
        ssd_norm_g=ssd_norm_g, rwkv_mu=rwkv_mu, rwkv_w0=rwkv_w0, rwkv_w_up=rwkv_w_up,
        rwkv_a0=rwkv_a0, rwkv_a_up=rwkv_a_up, rwkv_g_up=rwkv_g_up, rwkv_k_k=rwkv_k_k,
        rwkv_k_a=rwkv_k_a, rwkv_r_k=rwkv_r_k.reshape(DEPTH, RWKV_WIDTH), rwkv_ln_g=rwkv_ln_g,
        rwkv_ln_b=rwkv_ln_b, ln1_g=ln1_g, ln1_b=ln1_b, ln2_g=ln2_g, ln2_b=ln2_b)
    big = dict(w_main=jnp.swapaxes(w_in, 1, 2),
               w_tail=jnp.swapaxes(_pack_tail_cols(w_in), 1, 2),
               w_out=w_out.astype(BF16),
               w_gate=w_gate, w_up=w_up, w_down=w_down.astype(BF16))
    bp, lp_len, _ = x_prompt.shape
    bs, ls_len, _ = x_sample.shape
    consts = _constants()
    nbuf = CONV_WIDTH - 1
    zero_small = (jnp.zeros((bp, nbuf, LRU_WIDTH), F32), jnp.zeros((bp, LRU_WIDTH), F32),
                  jnp.zeros((bp, nbuf, SSD_CONV_DIM), F32), jnp.zeros((bp, RWKV_SHIFT), F32))
    zero_ssd = jnp.zeros((1, bp, SSD_HEADS, HEAD_DIM, SSD_STATE), F32)
    zero_rw = jnp.zeros((1, bp, RWKV_HEADS, HEAD_DIM, HEAD_DIM), F32)
    yp = x_prompt.reshape(bp * lp_len, D_MODEL)
    ys = x_sample.reshape(bs * ls_len, D_MODEL)
    ypb = yp.astype(BF16)
    ysb = ys.astype(BF16)
    state_rwkv_t = jnp.transpose(state_rwkv, (0, 2, 3, 4, 1))
    new_p = [[] for _ in range(4)]
    new_s = [[] for _ in range(4)]
    outs_p = (None, None)
    outs_s = (None, None)
    for l in range(DEPTH):
        sp = _small_params(l, params)
        yp, ypb, small_p, outs_p = _layer(yp, ypb, l, zero_small, (zero_ssd, 0), (zero_rw, 0), outs_p, sp,
                                     big, consts, nseq=bp, seqlen=lp_len, lru_nblk=bp,
                                     rwkv_nb=RWKV_PROMPT_NB)
        ys, ysb, small_s, outs_s = _layer(
            ys, ysb, l, (state_lru_conv[l], state_lru_h[l], state_ssd_conv[l], state_rwkv_shift[l]),
            (state_ssd, l), (state_rwkv_t, l), outs_s, sp, big, consts,
            nseq=bs, seqlen=ls_len, lru_nblk=1, rwkv_nb=None)
        for i in range(4):
            new_p[i].append(small_p[i])
            new_s[i].append(small_s[i])
    p_lru_conv, p_lru_h, p_ssd_conv, p_rw_shift = [jnp.stack(v) for v in new_p]
    s_lru_conv, s_lru_h, s_ssd_conv, s_rw_shift = [jnp.stack(v) for v in new_s]
    return (yp.reshape(bp, lp_len, D_MODEL), ys.reshape(bs, ls_len, D_MODEL),
            p_lru_conv, p_lru_h, p_ssd_conv, outs_p[0], p_rw_shift, outs_p[1],
            s_lru_conv, s_lru_h, s_ssd_conv, outs_s[0], s_rw_shift,
            jnp.transpose(outs_s[1], (0, 4, 1, 2, 3)))
```

```python
import functools
import math

import jax
import jax.numpy as jnp
from jax import lax
from jax.experimental import pallas as pl
from jax.experimental.pallas import tpu as pltpu

F32 = jnp.float32
BF16 = jnp.bfloat16

D_MODEL = 2048
DEPTH = 2
D_MIX = 2 * D_MODEL
HEAD_DIM = 64
CONV_WIDTH = 4
LRU_WIDTH = D_MIX // 4
LRU_HEADS = 8
LRU_BLOCK = LRU_WIDTH // LRU_HEADS
LRU_C = 8.0
SSD_WIDTH = D_MIX // 2
SSD_HEADS = SSD_WIDTH // HEAD_DIM
SSD_GROUPS = 8
SSD_STATE = 128
SSD_CHUNK = 128
SSD_CONV_DIM = SSD_WIDTH + 2 * SSD_GROUPS * SSD_STATE
SSD_GROUP_WIDTH = SSD_WIDTH // SSD_GROUPS
SSD_HEADS_PER_GROUP = SSD_HEADS // SSD_GROUPS
RWKV_WIDTH = D_MIX - LRU_WIDTH - SSD_WIDTH
RWKV_HEADS = RWKV_WIDTH // HEAD_DIM
RWKV_R_W = max(32, int(round(1.8 * RWKV_WIDTH ** 0.5 / 32)) * 32)
RWKV_R_A = max(32, int(round(1.8 * RWKV_WIDTH ** 0.5 / 32)) * 32)
RWKV_R_G = max(32, int(round(0.6 * RWKV_WIDTH ** 0.8 / 32)) * 32)
RWKV_SHIFT = 3 * RWKV_WIDTH + RWKV_R_W + RWKV_R_A + RWKV_R_G
RWKV_GN_EPS = 64e-5
RWKV_CHUNK = 64
RWKV_PROMPT_NB = 2
D_FF = -(-(8 * D_MODEL) // (3 * 256)) * 256
ALPHA = (2 * DEPTH) ** 0.25
LN_EPS = 1e-5

LANES = 128
SUBLANES = 8
VMEM_LIMIT = 56 * 1024 * 1024

C_GATE = 0
C_LX = C_GATE + LRU_WIDTH
C_Z = C_LX + LRU_WIDTH
C_XBC = C_Z + SSD_WIDTH
C_R = C_XBC + SSD_CONV_DIM
C_K = C_R + RWKV_WIDTH
C_V = C_K + RWKV_WIDTH
C_G = C_V + RWKV_WIDTH
G_PAD = 2 * LANES
C_WA = C_G + G_PAD
WA_PAD = LANES
C_DT = C_WA + WA_PAD
DT_PAD = LANES
N_PROJ = C_DT + DT_PAD
assert RWKV_R_G <= G_PAD and RWKV_R_W + RWKV_R_A == WA_PAD and SSD_HEADS <= DT_PAD
RW_PACK = 3 * RWKV_WIDTH + G_PAD + WA_PAD

_O_DT = 2 * LRU_WIDTH + SSD_WIDTH + SSD_CONV_DIM
_O_RW = _O_DT + SSD_HEADS
_O_XW = 3 * RWKV_WIDTH
_O_XG = _O_XW + RWKV_R_W + RWKV_R_A


def _cparams(*sem):
    return pltpu.CompilerParams(dimension_semantics=sem, vmem_limit_bytes=VMEM_LIMIT)


_NN = (((1,), (0,)), ((), ()))
_NT = (((1,), (1,)), ((), ()))
_TN = (((0,), (0,)), ((), ()))


def _dg(a, b, dims):
    return lax.dot_general(a, b, dims, preferred_element_type=F32)


def _dot1(a, b, dims=_NN):
    return _dg(a.astype(BF16), b.astype(BF16), dims)


def _split2(x):
    hi = x.astype(BF16)
    lo = (x - hi.astype(F32)).astype(BF16)
    return hi, lo


def _split3(x):
    hi = x.astype(BF16)
    r1 = x - hi.astype(F32)
    mid = r1.astype(BF16)
    lo = (r1 - mid.astype(F32)).astype(BF16)
    return hi, mid, lo


def _dot3(a, b, dims=_NN):
    ah, al = _split2(a)
    bh, bl = _split2(b)
    free_axis = 1 if dims == _TN else 0
    m = a.shape[free_axis]
    both = _dg(jnp.concatenate([ah, al], axis=free_axis), bh, dims)
    return (both[:m] + both[m:]) + _dg(ah, bl, dims)


def _dot_shared_rhs(parts, w_exact, npieces):
    pieces = []
    for p in parts:
        rest = p
        for _ in range(npieces):
            piece = rest.astype(BF16).astype(F32)
            pieces.append(piece)
            rest = rest - piece
    prod = _dg(jnp.concatenate(pieces, axis=0).astype(BF16), w_exact, _NN)
    outs = []
    off = 0
    for p in parts:
        r = p.shape[0]
        acc = prod[off:off + r]
        for i in range(1, npieces):
            acc = acc + prod[off + i * r:off + (i + 1) * r]
        outs.append(acc)
        off += npieces * r
    return outs


def _dotx_left(w_exact, a, npieces=3):
    if npieces == 2:
        hi, lo = _split2(a)
        return _dg(w_exact, hi, _NN) + _dg(w_exact, lo, _NN)
    hi, mid, lo = _split3(a)
    return _dg(w_exact, hi, _NN) + (_dg(w_exact, mid, _NN) + _dg(w_exact, lo, _NN))


def _iota2(shape, dim):
    return lax.broadcasted_iota(jnp.int32, shape, dim)


def _softplus(x):
    return jnp.maximum(x, 0.0) + jnp.log1p(jnp.exp(-jnp.abs(x)))


def _sigmoid(x):
    return 1.0 / (1.0 + jnp.exp(-x))


def _silu(x):
    return x * _sigmoid(x)


def _gelu_tanh(x):
    c = math.sqrt(2.0 / math.pi)
    return 0.5 * x * (1.0 + jnp.tanh(c * (x + 0.044715 * (x * x * x))))


def _layer_norm(y, g, b):
    mu = jnp.mean(y, axis=-1, keepdims=True)
    d = y - mu
    var = jnp.mean(d * d, axis=-1, keepdims=True)
    return d * lax.rsqrt(var + LN_EPS) * g + b


DENSE_TM = 2048
DENSE_TN = 512
N_MAIN = C_R
N_TAIL = N_PROJ - N_MAIN
assert N_MAIN == _O_DT and N_MAIN % DENSE_TN == 0 and N_TAIL % DENSE_TN == 0 and D_FF % DENSE_TN == 0


def _proj_kernel(x_ref, wm_ref, wt_ref, o_ref):
    j = pl.program_id(1)

    @pl.when(j < N_MAIN // DENSE_TN)
    def _():
        o_ref[...] = _dg(x_ref[...], wm_ref[0].astype(BF16), _NT)

    @pl.when(j >= N_MAIN // DENSE_TN)
    def _():
        o_ref[...] = _dg(x_ref[...], wt_ref[0].astype(BF16), _NT)


def _proj(x, w_main_t, w_tail_t, l):
    m = x.shape[0]
    tm = min(m, DENSE_TM)
    tn = DENSE_TN
    nmain = N_MAIN // tn
    return pl.pallas_call(
        _proj_kernel,
        grid=(m // tm, N_PROJ // tn),
        in_specs=[pl.BlockSpec((tm, D_MODEL), lambda i, j: (i, 0)),
                  pl.BlockSpec((1, tn, D_MODEL), lambda i, j: (l, jnp.minimum(j, nmain - 1), 0)),
                  pl.BlockSpec((1, tn, D_MODEL), lambda i, j: (l, jnp.maximum(j - nmain, 0), 0))],
        out_specs=pl.BlockSpec((tm, tn), lambda i, j: (i, j)),
        out_shape=jax.ShapeDtypeStruct((m, N_PROJ), F32),
        compiler_params=_cparams("arbitrary", "arbitrary"),
        name="proj",
    )(x, w_main_t, w_tail_t)


OUT_KSTEPS = D_MIX // LRU_WIDTH


def _outproj_kernel(a_ref, b_ref, c_ref, w_ref, x_ref, g_ref, beta_ref, o_ref, ob_ref, acc_ref):
    k = pl.program_id(1)

    @pl.when(k == 0)
    def _():
        acc_ref[...] = jnp.dot(a_ref[...], w_ref[0], preferred_element_type=F32)

    @pl.when(jnp.logical_and(k > 0, k < OUT_KSTEPS - 1))
    def _():
        acc_ref[...] += jnp.dot(b_ref[...], w_ref[0], preferred_element_type=F32)

    @pl.when(k == OUT_KSTEPS - 1)
    def _():
        mix = acc_ref[...] + jnp.dot(c_ref[...], w_ref[0], preferred_element_type=F32)
        y = _layer_norm(ALPHA * x_ref[...] + mix, g_ref[...], beta_ref[...])
        o_ref[...] = y
        ob_ref[...] = y.astype(BF16)


def _outproj_ln(out_a, out_b, out_c, w_out, l, x, g, beta):
    m = x.shape[0]
    tm = min(m, 512)
    tk = LRU_WIDTH
    nb = SSD_WIDTH // tk
    return pl.pallas_call(
        _outproj_kernel,
        grid=(m // tm, OUT_KSTEPS),
        in_specs=[pl.BlockSpec((tm, tk), lambda i, k: (i, 0)),
                  pl.BlockSpec((tm, tk), lambda i, k: (i, jnp.clip(k - 1, 0, nb - 1))),
                  pl.BlockSpec((tm, tk), lambda i, k: (i, 0)),
                  pl.BlockSpec((1, tk, D_MODEL), lambda i, k: (l, k, 0)),
                  pl.BlockSpec((tm, D_MODEL), lambda i, k: (i, 0)),
                  pl.BlockSpec((1, D_MODEL), lambda i, k: (0, 0)),
                  pl.BlockSpec((1, D_MODEL), lambda i, k: (0, 0))],
        out_specs=[pl.BlockSpec((tm, D_MODEL), lambda i, k: (i, 0))] * 2,
        out_shape=[jax.ShapeDtypeStruct((m, D_MODEL), F32), jax.ShapeDtypeStruct((m, D_MODEL), BF16)],
        scratch_shapes=[pltpu.VMEM((tm, D_MODEL), F32)],
        compiler_params=_cparams("arbitrary", "arbitrary"),
        name="outproj_ln",
    )(out_a, out_b, out_c, w_out, x, g, beta)


def _ffn_up_kernel(x_ref, wg_ref, wu_ref, o_ref):
    xb = x_ref[...]
    gate = jnp.dot(xb, wg_ref[0].astype(BF16), preferred_element_type=F32)
    up = jnp.dot(xb, wu_ref[0].astype(BF16), preferred_element_type=F32)
    o_ref[...] = (_silu(gate) * up).astype(BF16)


def _ffn_up(x, wg, wu, l):
    m = x.shape[0]
    tm = min(m, DENSE_TM)
    tn = DENSE_TN
    return pl.pallas_call(
        _ffn_up_kernel,
        grid=(m // tm, D_FF // tn),
        in_specs=[pl.BlockSpec((tm, D_MODEL), lambda i, j: (i, 0)),
                  pl.BlockSpec((1, D_MODEL, tn), lambda i, j: (l, 0, j)),
                  pl.BlockSpec((1, D_MODEL, tn), lambda i, j: (l, 0, j))],
        out_specs=pl.BlockSpec((tm, tn), lambda i, j: (i, j)),
        out_shape=jax.ShapeDtypeStruct((m, D_FF), BF16),
        compiler_params=_cparams("arbitrary", "arbitrary"),
        name="ffn_up",
    )(x, wg, wu)


FFN_DOWN_KSTEPS = 2


LN_ROW_CHUNK = 128


def _ffn_down_kernel(h_ref, w_ref, x_ref, g_ref, beta_ref, o_ref, ob_ref):
    k = pl.program_id(1)

    @pl.when(k == 0)
    def _():
        o_ref[...] = jnp.dot(h_ref[...], w_ref[0], preferred_element_type=F32)

    @pl.when(k > 0)
    def _():
        o_ref[...] += jnp.dot(h_ref[...], w_ref[0], preferred_element_type=F32)

    @pl.when(k == FFN_DOWN_KSTEPS - 1)
    def _():
        chunk = min(LN_ROW_CHUNK, o_ref.shape[0])

        def ln_rows(r, carry):
            rows = pl.ds(pl.multiple_of(r * chunk, chunk), chunk)
            y = _layer_norm(ALPHA * x_ref[rows, :] + o_ref[rows, :], g_ref[...], beta_ref[...])
            o_ref[rows, :] = y
            ob_ref[rows, :] = y.astype(BF16)
            return carry

        lax.fori_loop(0, o_ref.shape[0] // chunk, ln_rows, 0)


def _ffn_down_ln(h, w_down, l, x, g, beta):
    m = x.shape[0]
    tm = min(m, 512)
    tk = D_FF // FFN_DOWN_KSTEPS
    assert tk * FFN_DOWN_KSTEPS == D_FF and tk % LANES == 0 and tm % min(LN_ROW_CHUNK, tm) == 0
    return pl.pallas_call(
        _ffn_down_kernel,
        grid=(m // tm, FFN_DOWN_KSTEPS),
        in_specs=[pl.BlockSpec((tm, tk), lambda i, k: (i, k)),
                  pl.BlockSpec((1, tk, D_MODEL), lambda i, k: (l, k, 0)),
                  pl.BlockSpec((tm, D_MODEL), lambda i, k: (i, 0)),
                  pl.BlockSpec((1, D_MODEL), lambda i, k: (0, 0)),
                  pl.BlockSpec((1, D_MODEL), lambda i, k: (0, 0))],
        out_specs=[pl.BlockSpec((tm, D_MODEL), lambda i, k: (i, 0))] * 2,
        out_shape=[jax.ShapeDtypeStruct((m, D_MODEL), F32), jax.ShapeDtypeStruct((m, D_MODEL), BF16)],
        compiler_params=_cparams("arbitrary", "arbitrary"),
        name="ffn_down_ln",
    )(h, w_down, x, g, beta)


def _conv_taps(u, first_rows, cw, cb, t_in_seq):
    out = cb + cw[CONV_WIDTH - 1:CONV_WIDTH, :] * u
    for j in range(1, CONV_WIDTH):
        shifted = jnp.where(t_in_seq < j, first_rows(j), pltpu.roll(u, j, 0))
        out = out + cw[CONV_WIDTH - 1 - j:CONV_WIDTH - j, :] * shifted
    return out


def _lru_kernel(gate_ref, lx_ref, buf_ref, h0_ref, cw_ref, cb_ref, wa_ref, ba_ref, wx_ref,
                bx_ref, lam_ref, out_ref, h1_ref, a_scr, b_scr, hin_scr, *, nseq, seqlen):
    rows = nseq * seqlen
    u = lx_ref[...]
    t_in_seq = _iota2((rows, LANES), 0) % seqlen
    if nseq == 1:
        buf = jnp.concatenate([buf_ref[0], jnp.zeros((rows - SUBLANES, LANES), F32)], axis=0) \
            if rows > SUBLANES else buf_ref[0]
    else:
        buf = buf_ref[0]
    nbuf = CONV_WIDTH - 1
    xc = _conv_taps(u, lambda j: pltpu.roll(buf, (j - nbuf) % rows, 0), cw_ref[...], cb_ref[...],
                    t_in_seq)

    r = _sigmoid(_dot1(xc, wa_ref[0]) + ba_ref[...])
    i = _sigmoid(_dot1(xc, wx_ref[0]) + bx_ref[...])
    log_a = (-LRU_C) * r * _softplus(-lam_ref[...])
    a = jnp.exp(log_a)
    b = jnp.sqrt(jnp.tanh(-log_a) * (a * a + 1.0)) * (i * xc)

    t8 = _iota2((rows, LANES), 0) % SUBLANES
    for s in (1, 2, 4):
        m = t8 >= s
        a_sh = pltpu.roll(a, s, 0)
        b_sh = pltpu.roll(b, s, 0)
        b = jnp.where(m, a * b_sh + b, b)
        a = jnp.where(m, a * a_sh, a)

    if seqlen == SUBLANES:
        h0 = h0_ref[0]
        hin = jnp.broadcast_to(h0[:, None, :], (nseq, SUBLANES, LANES)).reshape(rows, LANES)
        h = a * hin + b
        out_ref[...] = (h * _gelu_tanh(gate_ref[...])).astype(out_ref.dtype)
        a_scr[...] = h
        h1_ref[0] = a_scr[pl.ds(SUBLANES - 1, nseq, stride=SUBLANES), :]
    else:
        assert nseq == 1
        a_scr[...] = a
        b_scr[...] = b

        def carry_step(g, carry):
            base = pl.multiple_of(g * SUBLANES, SUBLANES)
            hin_scr[pl.ds(base, SUBLANES), :] = jnp.broadcast_to(carry, (SUBLANES, LANES))
            a7 = a_scr[pl.ds(base + SUBLANES - 1, 1), :]
            b7 = b_scr[pl.ds(base + SUBLANES - 1, 1), :]
            return a7 * carry + b7

        last = lax.fori_loop(0, rows // SUBLANES, carry_step, h0_ref[0])
        h = a_scr[...] * hin_scr[...] + b_scr[...]
        out_ref[...] = (h * _gelu_tanh(gate_ref[...])).astype(out_ref.dtype)
        h1_ref[0] = last


def _lru(proj, buf, h0, cw, cb, wa, ba, wx, bx, lam, *, nblk, nseq, seqlen):
    rows = nseq * seqlen
    gate_blk = C_GATE // LRU_BLOCK
    lx_blk = C_LX // LRU_BLOCK
    row = lambda s, h: (0, h)
    return pl.pallas_call(
        functools.partial(_lru_kernel, nseq=nseq, seqlen=seqlen),
        grid=(nblk, LRU_HEADS),
        in_specs=[pl.BlockSpec((rows, LRU_BLOCK), lambda s, h: (s, gate_blk + h)),
                  pl.BlockSpec((rows, LRU_BLOCK), lambda s, h: (s, lx_blk + h)),
                  pl.BlockSpec((1, nseq * SUBLANES, LRU_BLOCK), lambda s, h: (s, 0, h)),
                  pl.BlockSpec((1, nseq, LRU_BLOCK), lambda s, h: (s, 0, h)),
                  pl.BlockSpec((CONV_WIDTH, LRU_BLOCK), row),
                  pl.BlockSpec((1, LRU_BLOCK), row),
                  pl.BlockSpec((1, LRU_BLOCK, LRU_BLOCK), lambda s, h: (h, 0, 0)),
                  pl.BlockSpec((1, LRU_BLOCK), row),
                  pl.BlockSpec((1, LRU_BLOCK, LRU_BLOCK), lambda s, h: (h, 0, 0)),
                  pl.BlockSpec((1, LRU_BLOCK), row),
                  pl.BlockSpec((1, LRU_BLOCK), row)],
        out_specs=[pl.BlockSpec((rows, LRU_BLOCK), lambda s, h: (s, h)),
                   pl.BlockSpec((1, nseq, LRU_BLOCK), lambda s, h: (s, 0, h))],
        out_shape=[jax.ShapeDtypeStruct((nblk * rows, LRU_WIDTH), BF16),
                   jax.ShapeDtypeStruct((nblk, nseq, LRU_WIDTH), F32)],
        scratch_shapes=[pltpu.VMEM((rows, LANES), F32)] * 3,
        compiler_params=_cparams("arbitrary", "arbitrary"),
        name="lru",
    )(proj, proj, buf, h0, cw, cb, wa, ba, wx, bx, lam)


def _stacked_state_call(kernel_fn, grid, in_specs, operands, row_spec, row_shape, state_dims,
                        l_out, prev_out, scratch_shapes, name, nb=1):
    nseq = grid[0] * nb
    zeros = (0,) * len(state_dims)
    state_shape = jax.ShapeDtypeStruct((DEPTH, nseq) + tuple(state_dims), F32)
    aliases = {}
    if prev_out is None:
        assert l_out == 0
        state_spec = pl.BlockSpec((DEPTH, nb) + tuple(state_dims), lambda b, c: (0, b) + zeros)
    else:
        state_spec = pl.BlockSpec((1, nb) + tuple(state_dims), lambda b, c: (l_out, b) + zeros)
        in_specs = in_specs + [pl.BlockSpec(memory_space=pl.ANY)]
        operands = operands + [prev_out]
        aliases = {len(operands) - 1: 1}
        kernel_fn = functools.partial(_drop_alias_ref, kernel_fn, len(operands) - 1)
    return pl.pallas_call(
        kernel_fn, grid=grid, in_specs=in_specs, out_specs=[row_spec, state_spec],
        out_shape=[row_shape, state_shape], scratch_shapes=scratch_shapes,
        input_output_aliases=aliases, compiler_params=_cparams("arbitrary", "arbitrary"),
        name=name)(*operands)


def _drop_alias_ref(kernel_fn, pos, *refs):
    return kernel_fn(*refs[:pos], *refs[pos + 1:])


def _zero_other_slabs(state_ref):
    if state_ref.shape[0] > 1:
        state_ref[1:] = jnp.zeros((state_ref.shape[0] - 1,) + state_ref.shape[1:], state_ref.dtype)


def _transpose_rows_to_lanes(x, t):
    if t < LANES:
        x = jnp.concatenate([x, jnp.zeros((LANES - t, LANES), F32)], axis=0)
    return x.T[:, :t]


def _ssd_kernel(z_ref, xbc_ref, dt_ref, buf_ref, s0_ref, cw_ref, cb_ref, dtb_ref, alog_ref,
                dch_ref, ng_ref, ehp_ref, out_ref, s1_ref, tail_ref, pad_ref, y_scr, *, T):
    c = pl.program_id(1)
    n_state = SSD_STATE
    gw = SSD_GROUP_WIDTH
    hpg = SSD_HEADS_PER_GROUP

    @pl.when(c == 0)
    def _():
        tail_ref[...] = buf_ref[0]
        s1_ref[0:1] = s0_ref[...]
        _zero_other_slabs(s1_ref)

    u = xbc_ref[...]
    pad_ref[0:SUBLANES, :] = tail_ref[...]
    pad_ref[SUBLANES:SUBLANES + T, :] = u
    cw = cw_ref[...]
    xc = cb_ref[...] + cw[CONV_WIDTH - 1:CONV_WIDTH, :] * u
    for j in range(1, CONV_WIDTH):
        xc = xc + cw[CONV_WIDTH - 1 - j:CONV_WIDTH - j, :] * pad_ref[SUBLANES - j:SUBLANES - j + T, :]
    tail_ref[...] = pad_ref[T:T + SUBLANES, :]

    xa = _silu(xc)
    xs = xa[:, :SSD_WIDTH]
    bm = xa[:, SSD_WIDTH:SSD_WIDTH + SSD_GROUPS * n_state]
    cm = xa[:, SSD_WIDTH + SSD_GROUPS * n_state:]
    dt = _softplus(dt_ref[...] + dtb_ref[...])
    da = dt * (-jnp.exp(alog_ref[...]))
    ii = _iota2((T, T), 0)
    jj = _iota2((T, T), 1)
    causal = ii >= jj
    cum = _dotx_left(causal.astype(BF16), da)
    cum_t = _transpose_rows_to_lanes(cum, T)
    ehp = ehp_ref[...]
    cumx, dtx = _dot_shared_rhs([cum, dt], ehp, 2)
    xdt = xs * dtx
    ecum = jnp.exp(cumx)
    xdtd = xdt * jnp.exp(cumx[T - 1:T, :] - cumx)

    groups = range(SSD_GROUPS)
    heads = range(SSD_HEADS)
    cg = [cm[:, g * n_state:(g + 1) * n_state] for g in groups]
    bg = [bm[:, g * n_state:(g + 1) * n_state] for g in groups]
    sg = [s1_ref[0, 0, g * hpg:(g + 1) * hpg].reshape(gw, n_state) for g in groups]
    cb = [_dot1(cg[g], bg[g], _NT) for g in groups]
    y_off = [_dot1(cg[g], sg[g], _NT) for g in groups]
    st = [_dot1(xdtd[:, g * gw:(g + 1) * gw], bg[g], _TN) for g in groups]
    lm = [jnp.where(causal, jnp.exp(cum[:, h:h + 1] - cum_t[h:h + 1, :]), 0.0) for h in heads]
    y_diag = [_dot1(cb[h // hpg] * lm[h], xdt[:, h * HEAD_DIM:(h + 1) * HEAD_DIM]) for h in heads]
    for g in groups:
        y_scr[:, g * gw:(g + 1) * gw] = (jnp.concatenate(y_diag[g * hpg:(g + 1) * hpg], axis=1)
                                         + y_off[g] * ecum[:, g * gw:(g + 1) * gw])
        decay = jnp.concatenate(
            [jnp.broadcast_to(jnp.exp(cum_t[h:h + 1, T - 1:T]), (HEAD_DIM, n_state))
             for h in range(g * hpg, (g + 1) * hpg)], axis=0)
        s1_ref[0, 0, g * hpg:(g + 1) * hpg] = (sg[g] * decay + st[g]).reshape(hpg, HEAD_DIM, n_state)

    y = (y_scr[...] + dch_ref[...] * xs) * _silu(z_ref[...])
    outs = []
    for g in groups:
        yg = y[:, g * gw:(g + 1) * gw]
        ms = jnp.mean(yg * yg, axis=-1, keepdims=True)
        outs.append(yg * lax.rsqrt(ms + 1e-5))
    out_ref[...] = (jnp.concatenate(outs, axis=1) * ng_ref[...]).astype(out_ref.dtype)


def _ssd(proj, buf, s0, l_in, l_out, prev_out, cw, cb, dtb, alog, dch, ng, ehp, *, nseq, seqlen):
    T = min(SSD_CHUNK, seqlen)
    nc = seqlen // T
    const = lambda b, c: (0, 0)
    sblk = (1, 1, SSD_HEADS, HEAD_DIM, SSD_STATE)
    in_specs = [pl.BlockSpec((T, SSD_WIDTH), lambda b, c: (b * nc + c, C_Z // SSD_WIDTH)),
                pl.BlockSpec((T, SSD_CONV_DIM), lambda b, c: (b * nc + c, C_XBC // SSD_CONV_DIM)),
                pl.BlockSpec((T, DT_PAD), lambda b, c: (b * nc + c, C_DT // DT_PAD)),
                pl.BlockSpec((1, SUBLANES, SSD_CONV_DIM), lambda b, c: (b, 0, 0)),
                pl.BlockSpec(sblk, lambda b, c: (l_in, b, 0, 0, 0)),
                pl.BlockSpec((CONV_WIDTH, SSD_CONV_DIM), const),
                pl.BlockSpec((1, SSD_CONV_DIM), const),
                pl.BlockSpec((1, DT_PAD), const),
                pl.BlockSpec((1, DT_PAD), const),
                pl.BlockSpec((1, SSD_WIDTH), const),
                pl.BlockSpec((1, SSD_WIDTH), const),
                pl.BlockSpec((DT_PAD, SSD_WIDTH), const)]
    return _stacked_state_call(
        functools.partial(_ssd_kernel, T=T), (nseq, nc), in_specs,
        [proj, proj, proj, buf, s0, cw, cb, dtb, alog, dch, ng, ehp],
        pl.BlockSpec((T, SSD_WIDTH), lambda b, c: (b * nc + c, 0)),
        jax.ShapeDtypeStruct((nseq * seqlen, SSD_WIDTH), BF16),
        sblk[2:], l_out, prev_out,
        [pltpu.VMEM((SUBLANES, SSD_CONV_DIM), F32),
         pltpu.VMEM((T + SUBLANES, SSD_CONV_DIM), F32),
         pltpu.VMEM((T, SSD_WIDTH), F32)],
        "ssd")


SSD_SHORT_NB = 4


def _ssd_short_kernel(z_ref, xbc_ref, dt_ref, buf_ref, s0_ref, cw_ref, cb_ref, dtb_ref, alog_ref,
                      dch_ref, ng_ref, ehp_ref, out_ref, s1_ref, y_scr, *, nb, T):
    n_state = SSD_STATE
    gw = SSD_GROUP_WIDTH
    hpg = SSD_HEADS_PER_GROUP
    R = nb * T

    def per_seq_rows(x):
        return jnp.broadcast_to(x[:, None, :], (nb, T, x.shape[-1])).reshape(R, x.shape[-1])

    def last_rows(x):
        return jnp.concatenate([x[(b + 1) * T - 1:(b + 1) * T, :] for b in range(nb)], axis=0)

    u = xbc_ref[...].reshape(R, SSD_CONV_DIM)
    buf = buf_ref[...].reshape(nb * SUBLANES, SSD_CONV_DIM)
    assert T == SUBLANES
    t_in_seq = _iota2((R, SSD_CONV_DIM), 0) % T
    nbuf = CONV_WIDTH - 1
    xc = _conv_taps(u, lambda j: pltpu.roll(buf, (j - nbuf) % R, 0), cw_ref[...], cb_ref[...],
                    t_in_seq)

    xa = _silu(xc)
    xs = xa[:, :SSD_WIDTH]
    bm = xa[:, SSD_WIDTH:SSD_WIDTH + SSD_GROUPS * n_state]
    cm = xa[:, SSD_WIDTH + SSD_GROUPS * n_state:]
    dt = _softplus(dt_ref[...].reshape(R, DT_PAD) + dtb_ref[...])
    da = dt * (-jnp.exp(alog_ref[...]))
    ri = _iota2((R, R), 0)
    rj = _iota2((R, R), 1)
    same_seq_causal = jnp.logical_and(ri >= rj, ri // T == rj // T)
    cum = _dotx_left(same_seq_causal.astype(BF16), da)
    cum_t = _transpose_rows_to_lanes(cum, R)
    cumx, dtx = _dot_shared_rhs([cum, dt], ehp_ref[...], 2)
    xdt = xs * dtx
    ecum = jnp.exp(cumx)
    xdtd = xdt * jnp.exp(per_seq_rows(last_rows(cumx)) - cumx)

    causal = _iota2((T, T), 0) >= _iota2((T, T), 1)
    seqs = range(nb)
    groups = range(SSD_GROUPS)
    heads = range(SSD_HEADS)
    rows = [slice(b * T, (b + 1) * T) for b in seqs]
    cg = [[cm[rows[b], g * n_state:(g + 1) * n_state] for g in groups] for b in seqs]
    bg = [[bm[rows[b], g * n_state:(g + 1) * n_state] for g in groups] for b in seqs]
    sg = [[s0_ref[0, b, g * hpg:(g + 1) * hpg].reshape(gw, n_state) for g in groups] for b in seqs]
    cb = [[_dot1(cg[b][g], bg[b][g], _NT) for g in groups] for b in seqs]
    y_off = [[_dot1(cg[b][g], sg[b][g], _NT) for g in groups] for b in seqs]
    st = [[_dot1(xdtd[rows[b], g * gw:(g + 1) * gw], bg[b][g], _TN) for g in groups] for b in seqs]
    lm = [[jnp.where(causal, jnp.exp(cum[rows[b], h:h + 1] - cum_t[h:h + 1, rows[b]]), 0.0)
           for h in heads] for b in seqs]
    y_diag = [[_dot1(cb[b][h // hpg] * lm[b][h], xdt[rows[b], h * HEAD_DIM:(h + 1) * HEAD_DIM])
               for h in heads] for b in seqs]
    for b in seqs:
        for g in groups:
            y_scr[rows[b], g * gw:(g + 1) * gw] = (
                jnp.concatenate(y_diag[b][g * hpg:(g + 1) * hpg], axis=1)
                + y_off[b][g] * ecum[rows[b], g * gw:(g + 1) * gw])
            last = (b + 1) * T - 1
            decay = jnp.concatenate(
                [jnp.broadcast_to(jnp.exp(cum_t[h:h + 1, last:last + 1]), (HEAD_DIM, n_state))
                 for h in range(g * hpg, (g + 1) * hpg)], axis=0)
            s1_ref[0, b, g * hpg:(g + 1) * hpg] = (sg[b][g] * decay + st[b][g]).reshape(
                hpg, HEAD_DIM, n_state)
    _zero_other_slabs(s1_ref)

    y = (y_scr[...] + dch_ref[...] * xs) * _silu(z_ref[...].reshape(R, SSD_WIDTH))
    outs = []
    for g in groups:
        yg = y[:, g * gw:(g + 1) * gw]
        ms = jnp.mean(yg * yg, axis=-1, keepdims=True)
        outs.append(yg * lax.rsqrt(ms + 1e-5))
    res = jnp.concatenate(outs, axis=1) * ng_ref[...]
    out_ref[...] = res.astype(out_ref.dtype).reshape(out_ref.shape)


def _ssd_short(p3, buf, s0, l_in, l_out, prev_out, cw, cb, dtb, alog, dch, ng, ehp):
    nseq, T, _ = p3.shape
    nb = SSD_SHORT_NB
    const = lambda b, c: (0, 0)
    sdims = (SSD_HEADS, HEAD_DIM, SSD_STATE)
    blk = lambda width, col: pl.BlockSpec((nb, T, width), lambda b, c: (b, 0, col // width))
    in_specs = [blk(SSD_WIDTH, C_Z), blk(SSD_CONV_DIM, C_XBC), blk(DT_PAD, C_DT),
                pl.BlockSpec((nb, SUBLANES, SSD_CONV_DIM), lambda b, c: (b, 0, 0)),
                pl.BlockSpec((1, nb) + sdims, lambda b, c: (l_in, b, 0, 0, 0)),
                pl.BlockSpec((CONV_WIDTH, SSD_CONV_DIM), const),
                pl.BlockSpec((1, SSD_CONV_DIM), const),
                pl.BlockSpec((1, DT_PAD), const),
                pl.BlockSpec((1, DT_PAD), const),
                pl.BlockSpec((1, SSD_WIDTH), const),
                pl.BlockSpec((1, SSD_WIDTH), const),
                pl.BlockSpec((DT_PAD, SSD_WIDTH), const)]
    out, stack = _stacked_state_call(
        functools.partial(_ssd_short_kernel, nb=nb, T=T), (nseq // nb, 1), in_specs,
        [p3, p3, p3, buf, s0, cw, cb, dtb, alog, dch, ng, ehp],
        pl.BlockSpec((nb, T, SSD_WIDTH), lambda b, c: (b, 0, 0)),
        jax.ShapeDtypeStruct((nseq, T, SSD_WIDTH), BF16),
        sdims, l_out, prev_out,
        [pltpu.VMEM((nb * T, SSD_WIDTH), F32)],
        "ssd_short", nb=nb)
    return out.reshape(nseq * T, SSD_WIDTH), stack


def _unit_lower_inverse(a_list, ii, jj, T):
    n = len(a_list)
    pair = (ii >> 1) == (jj >> 1)
    inv = [jnp.where(ii == jj, 1.0, 0.0) + jnp.where(pair, a_list[h], 0.0) for h in range(n)]
    shift = 1
    while (2 << shift) <= T:
        band = jnp.logical_and((ii >> (shift + 1)) == (jj >> (shift + 1)),
                               (ii >> shift) != (jj >> shift))
        x = [_dot1(jnp.where(band, a_list[h], 0.0), inv[h]) for h in range(n)]
        inv = [inv[h] + _dot1(inv[h], x[h]) for h in range(n)]
        shift += 1
    return inv


def _rwkv_kernel(r_ref, k_ref, v_ref, g_ref, wa_ref, sh_ref, s0_ref, mu_ref, w0_ref, wup_ref,
                 a0_ref, aup_ref, gup_ref, kkw_ref, kaw_ref, rkw_ref, lng_ref, lnb_ref, ones_ref,
                 out_ref, s1_ref, prev_scr, s_scr, al_scr, be_scr, kt_scr, rt_scr, bs_scr,
                 ks_scr, v_scr, gam_scr, o_scr, *, nb, T, nchunks):
    c = pl.program_id(1)
    W = RWKV_WIDTH
    D = HEAD_DIM
    R = nb * T
    nh = RWKV_HEADS

    @pl.when(c == 0)
    def _():
        prev_scr[...] = sh_ref[...].reshape(nb, RW_PACK)
        s_scr[...] = s0_ref[0].reshape(nb * nh, D, D)

    def rows_of(ref3):
        return ref3[...].reshape(R, ref3.shape[-1])

    def per_seq_rows(x):
        return jnp.broadcast_to(x[:, None, :], (nb, T, x.shape[-1])).reshape(R, x.shape[-1])

    def last_rows(x):
        return jnp.concatenate([x[(b + 1) * T - 1:(b + 1) * T, :] for b in range(nb)], axis=0)

    def token_shift(p, lo, hi):
        first = _iota2(p.shape, 0) % T == 0
        prev = jnp.where(first, per_seq_rows(prev_scr[:, lo:hi]), pltpu.roll(p, 1, 0))
        prev_scr[:, lo:hi] = last_rows(p)
        return p + (prev - p) * mu_ref[:, lo:hi]

    xr = token_shift(rows_of(r_ref), 0, W)
    xk = token_shift(rows_of(k_ref), W, 2 * W)
    xv = token_shift(rows_of(v_ref), 2 * W, 3 * W)
    xg = token_shift(rows_of(g_ref), 3 * W, 3 * W + G_PAD)
    xwa = token_shift(rows_of(wa_ref), 3 * W + G_PAD, RW_PACK)

    w_lin = w0_ref[...] + _dot1(jnp.tanh(xwa), wup_ref[...])
    a = _sigmoid(a0_ref[...] + _dot1(xwa, aup_ref[...]))
    gate = _dot1(_sigmoid(xg), gup_ref[...])
    lw = -jnp.exp(-_softplus(-w_lin) - 0.5)

    ones = ones_ref[...]

    def head_sums(xs, npieces=3):
        tw = ones.shape[0]
        nt = W // tw
        tiles = [x[:, i * tw:(i + 1) * tw] for x in xs for i in range(nt)]
        sums = _dot_shared_rhs(tiles, ones, npieces)
        return [jnp.concatenate(sums[n * nt:(n + 1) * nt], axis=1) for n in range(len(xs))]

    kk = xk * kkw_ref[...]
    kp = xk * (1.0 + (a - 1.0) * kaw_ref[...])
    kk_sq, rk_sum = head_sums([kk * kk, xr * kp * rkw_ref[...]], 1)
    kk = kk / jnp.maximum(jnp.sqrt(kk_sq), 1e-12)

    ri = _iota2((R, R), 0)
    rj = _iota2((R, R), 1)
    same_seq_causal = jnp.logical_and(ri >= rj, ri // T == rj // T)
    cum = _dotx_left(same_seq_causal.astype(BF16), lw, 3)
    e_neg = jnp.exp(-cum)
    gam = jnp.exp(per_seq_rows(last_rows(cum)))
    be = kk * a * e_neg
    kt = kp * e_neg
    al_scr[...] = -kk * jnp.exp(cum - lw)
    be_scr[...] = be
    kt_scr[...] = kt
    rt_scr[...] = xr * jnp.exp(cum)
    bs_scr[...] = be * gam
    ks_scr[...] = kt * gam
    v_scr[...] = xv
    gam_scr[...] = gam

    ii = _iota2((T, T), 0)
    jj = _iota2((T, T), 1)
    incl = ii >= jj
    strict = ii > jj
    eye_d = _iota2((D, D), 0) == _iota2((D, D), 1)

    units = [(b, h) for b in range(nb) for h in range(nh)]
    nu = range(len(units))

    def per_unit(ref):
        return [ref[b * T:(b + 1) * T, h * D:(h + 1) * D] for b, h in units]

    al, be_u, kt_u, rt = per_unit(al_scr), per_unit(be_scr), per_unit(kt_scr), per_unit(rt_scr)
    bs_u, ks_u, vv, gam_u = per_unit(bs_scr), per_unit(ks_scr), per_unit(v_scr), per_unit(gam_scr)
    gram = [_dot1(jnp.concatenate([al[u], rt[u]], axis=0),
                  jnp.concatenate([be_u[u], kt_u[u]], axis=0), _NT) for u in nu]
    a_ab = [jnp.where(strict, gram[u][:T, :T], 0.0) for u in nu]
    a_ak = [jnp.where(strict, gram[u][:T, T:], 0.0) for u in nu]
    r_b = [jnp.where(incl, gram[u][T:, :T], 0.0) for u in nu]
    r_k = [jnp.where(incl, gram[u][T:, T:], 0.0) for u in nu]
    inv = _unit_lower_inverse(a_ab, ii, jj, T)
    akv = [_dot1(a_ak[u], vv[u]) for u in nu]
    pw = [_dot1(inv[u], jnp.concatenate([al[u], akv[u]], axis=1)) for u in nu]
    qo = [_dot1(r_b[u], pw[u]) for u in nu]
    rkv = [_dot1(r_k[u], vv[u]) for u in nu]
    smat = [s_scr[u] for u in nu]
    qs = [_dot1(rt[u] + qo[u][:, :D], smat[u], _NT) for u in nu]
    gp = [_dot3(bs_u[u], pw[u][:, :D], _TN) for u in nu]
    zt = [_dot3(jnp.concatenate([pw[u][:, D:], vv[u]], axis=0),
                jnp.concatenate([bs_u[u], ks_u[u]], axis=0), _TN) for u in nu]
    gmat = [jnp.where(eye_d, jnp.broadcast_to(gam_u[u][0:1, :], (D, D)), 0.0) + gp[u] for u in nu]
    sg = [_dot3(smat[u], gmat[u], _NT) for u in nu]
    for u in nu:
        s_scr[u] = sg[u] + zt[u]
    pairs = LANES // D
    for b in range(nb):
        for p in range(nh // pairs):
            us = [b * nh + p * pairs + s for s in range(pairs)]
            o_scr[b * T:(b + 1) * T, p * LANES:(p + 1) * LANES] = jnp.concatenate(
                [qs[u] + qo[u][:, D:] + rkv[u] for u in us], axis=1)

    o = o_scr[...]
    mean = head_sums([o])[0] * (1.0 / D)
    d = o - mean
    var = head_sums([d * d])[0] * (1.0 / D)
    on = d * lax.rsqrt(var + RWKV_GN_EPS) * lng_ref[...] + lnb_ref[...]
    bonus = rk_sum * xv
    out_ref[...] = ((on + bonus) * gate).astype(out_ref.dtype).reshape(nb, T, W)

    @pl.when(c == nchunks - 1)
    def _():
        s1_ref[0] = s_scr[...].reshape(nb, nh, D, D)
        _zero_other_slabs(s1_ref)


def _rwkv(proj3, sh, s0, l_in, l_out, prev_out, mu, w0, wup, a0, aup, gup, kkw, kaw, rkw, lng, lnb,
          ones, *, nb):
    nseq, seqlen, _ = proj3.shape
    T = min(RWKV_CHUNK, seqlen)
    nc = seqlen // T
    W = RWKV_WIDTH
    const = lambda b, c: (0, 0)
    rowblk = lambda col, width: pl.BlockSpec((nb, T, width), lambda b, c: (b, c, col // width))
    sdims = (RWKV_HEADS, HEAD_DIM, HEAD_DIM)
    vec = pl.BlockSpec((1, W), const)
    in_specs = [rowblk(C_R, W), rowblk(C_K, W), rowblk(C_V, W), rowblk(C_G, G_PAD),
                rowblk(C_WA, WA_PAD),
                pl.BlockSpec((nb, 1, RW_PACK), lambda b, c: (b, 0, 0)),
                pl.BlockSpec((1, nb) + sdims, lambda b, c: (l_in, b, 0, 0, 0)),
                pl.BlockSpec((1, RW_PACK), const),
                vec, pl.BlockSpec((WA_PAD, W), const),
                vec, pl.BlockSpec((WA_PAD, W), const),
                pl.BlockSpec((G_PAD, W), const),
                vec, vec, vec, vec, vec,
                pl.BlockSpec((2 * LANES, 2 * LANES), const)]
    rows = nb * T
    return _stacked_state_call(
        functools.partial(_rwkv_kernel, nb=nb, T=T, nchunks=nc), (nseq // nb, nc), in_specs,
        [proj3, proj3, proj3, proj3, proj3, sh, s0, mu, w0, wup, a0, aup, gup, kkw, kaw, rkw, lng,
         lnb, ones],
        pl.BlockSpec((nb, T, W), lambda b, c: (b, c, 0)),
        jax.ShapeDtypeStruct((nseq, seqlen, W), BF16),
        sdims, l_out, prev_out,
        [pltpu.VMEM((nb, RW_PACK), F32), pltpu.VMEM((nb * RWKV_HEADS, HEAD_DIM, HEAD_DIM), F32)]
        + [pltpu.VMEM((rows, W), F32)] * 7
        + [pltpu.VMEM((rows, W), F32), pltpu.VMEM((rows, W), F32)],
        "rwkv", nb=nb)


def _rws_pre_kernel(r_ref, k_ref, v_ref, g_ref, wa_ref, shr_ref, shk_ref, shv_ref, shg_ref,
                    shwa_ref, mur_ref, muk_ref, muv_ref, mug_ref, muwa_ref, w0_ref, wup_ref,
                    a0_ref, aup_ref, gup_ref, kkw_ref, kaw_ref, rkw_ref, ones_ref,
                    rt_ref, wt_ref, knt_ref, kat_ref, kpt_ref, vt_ref, gate_ref, bonus_ref):
    steps, nseq, _ = r_ref.shape

    def shifted(ref, sh_ref, mu_ref):
        x = ref[...]
        prev = jnp.concatenate([sh_ref[...][None], x[:steps - 1]], axis=0)
        return (x + (prev - x) * mu_ref[...]).reshape(steps * nseq, x.shape[-1])

    xr = shifted(r_ref, shr_ref, mur_ref)
    xk = shifted(k_ref, shk_ref, muk_ref)
    xv = shifted(v_ref, shv_ref, muv_ref)
    xg = shifted(g_ref, shg_ref, mug_ref)
    xwa = shifted(wa_ref, shwa_ref, muwa_ref)

    w_lin = w0_ref[...] + _dot1(jnp.tanh(xwa), wup_ref[...])
    a = _sigmoid(a0_ref[...] + _dot1(xwa, aup_ref[...]))
    gate = _dot1(_sigmoid(xg), gup_ref[...])
    decay = jnp.exp(-jnp.exp(-_softplus(-w_lin) - 0.5))
    kk = xk * kkw_ref[...]
    kp = xk * (1.0 + (a - 1.0) * kaw_ref[...])
    kk_sq, rk_sum = _dot_shared_rhs([kk * kk, xr * kp * rkw_ref[...]], ones_ref[...], 3)
    kk = kk / jnp.maximum(jnp.sqrt(kk_sq), 1e-12)

    for t in range(steps):
        rows = slice(t * nseq, (t + 1) * nseq)
        rt_ref[t] = xr[rows].T
        wt_ref[t] = decay[rows].T
        knt_ref[t] = (-kk[rows]).T
        kat_ref[t] = (kk[rows] * a[rows]).T
        kpt_ref[t] = kp[rows].T
        vt_ref[t] = xv[rows].T
    gate_ref[...] = gate.reshape(gate_ref.shape)
    bonus_ref[...] = (rk_sum * xv).reshape(bonus_ref.shape)


def _rws_rec_kernel(r_ref, w_ref, kn_ref, ka_ref, kp_ref, v_ref, s0_ref, o_ref, s1_ref):
    steps = r_ref.shape[0]

    def value_rows(vc, carry):
        v0 = pl.multiple_of(vc * SUBLANES, SUBLANES)
        s = [s0_ref[0, 0, v0 + i] for i in range(SUBLANES)]
        for t in range(steps):
            kn, ka, kp, w, r = kn_ref[t], ka_ref[t], kp_ref[t], w_ref[t], r_ref[t]
            vv = v_ref[t, pl.ds(v0, SUBLANES), :]
            outs = []
            for i in range(SUBLANES):
                sa = jnp.sum(s[i] * kn, axis=0, keepdims=True)
                s[i] = s[i] * w + (sa * ka + vv[i:i + 1, :] * kp)
                outs.append(jnp.sum(s[i] * r, axis=0, keepdims=True))
            o_ref[t, pl.ds(v0, SUBLANES), :] = jnp.concatenate(outs, axis=0)
        for i in range(SUBLANES):
            s1_ref[0, 0, v0 + i] = s[i]
        return carry

    lax.fori_loop(0, HEAD_DIM // SUBLANES, value_rows, 0)
    _zero_other_slabs(s1_ref)


def _rws_post_kernel(o_ref, gate_ref, bonus_ref, lng_ref, lnb_ref, ones_ref, out_ref):
    steps, _, nseq = o_ref.shape
    o = jnp.concatenate([o_ref[t].T for t in range(steps)], axis=0)
    ones = ones_ref[...]
    mean = _dot_shared_rhs([o], ones, 3)[0] * (1.0 / HEAD_DIM)
    d = o - mean
    var = _dot_shared_rhs([d * d], ones, 3)[0] * (1.0 / HEAD_DIM)
    on = d * lax.rsqrt(var + RWKV_GN_EPS) * lng_ref[...] + lnb_ref[...]
    rows = steps * nseq
    res = (on + bonus_ref[...].reshape(rows, LANES)) * gate_ref[...].reshape(rows, LANES)
    out_ref[...] = res.astype(out_ref.dtype).reshape(out_ref.shape)


def _rwkv_steps(p3, sh, s0_t, l_in, l_out, prev_out, mu, w0, wup, a0, aup, gup, kkw, kaw, rkw, lng,
                lnb, ones):
    nseq, steps, _ = p3.shape
    W, D, H = RWKV_WIDTH, HEAD_DIM, RWKV_HEADS
    pt = jnp.transpose(p3[:, :, C_R:], (1, 0, 2))
    nblk = W // LANES
    ones_blk = ones[:LANES, :LANES]
    g_col, wa_col = 3 * W, 3 * W + G_PAD

    def cols(rows, width, col):
        return pl.BlockSpec(rows + (width,), lambda p: (0,) * len(rows) + (col // width,))

    def cols_p(rows, base):
        return pl.BlockSpec(rows + (LANES,), lambda p: (0,) * len(rows) + (base // LANES + p,))

    tb = (steps, nseq)
    in_specs = ([cols_p(tb, 0), cols_p(tb, W), cols_p(tb, 2 * W), cols(tb, G_PAD, g_col),
                 cols(tb, WA_PAD, wa_col)]
                + [cols_p((nseq,), 0), cols_p((nseq,), W), cols_p((nseq,), 2 * W),
                   cols((nseq,), G_PAD, g_col), cols((nseq,), WA_PAD, wa_col)]
                + [cols_p((1,), 0), cols_p((1,), W), cols_p((1,), 2 * W), cols((1,), G_PAD, g_col),
                   cols((1,), WA_PAD, wa_col)]
                + [cols_p((1,), 0), cols_p((WA_PAD,), 0), cols_p((1,), 0), cols_p((WA_PAD,), 0),
                   cols_p((G_PAD,), 0), cols_p((1,), 0), cols_p((1,), 0), cols_p((1,), 0),
                   pl.BlockSpec((LANES, LANES), lambda p: (0, 0))])
    t_spec = pl.BlockSpec((steps, LANES, nseq), lambda p: (0, p, 0))
    n_spec = pl.BlockSpec((steps, nseq, LANES), lambda p: (0, 0, p))
    t_shape = jax.ShapeDtypeStruct((steps, W, nseq), F32)
    n_shape = jax.ShapeDtypeStruct((steps, nseq, W), F32)
    pre = pl.pallas_call(
        _rws_pre_kernel, grid=(nblk,), in_specs=in_specs,
        out_specs=[t_spec] * 6 + [n_spec] * 2, out_shape=[t_shape] * 6 + [n_shape] * 2,
        compiler_params=_cparams("arbitrary"), name="rwkv_pre",
    )(pt, pt, pt, pt, pt, sh, sh, sh, sh, sh, mu, mu, mu, mu, mu, w0, wup, a0, aup, gup, kkw, kaw,
      rkw, ones_blk)
    rt, wt, knt, kat, kpt, vt, gate, bonus = pre

    head = pl.BlockSpec((steps, D, nseq), lambda h, c: (0, h, 0))
    o_t, stack = _stacked_state_call(
        _rws_rec_kernel, (H, 1),
        [head] * 6 + [pl.BlockSpec((1, 1, D, D, nseq), lambda h, c: (l_in, h, 0, 0, 0))],
        [rt, wt, knt, kat, kpt, vt, s0_t],
        head, t_shape, (D, D, nseq), l_out, prev_out, [], "rwkv_rec")

    out = pl.pallas_call(
        _rws_post_kernel, grid=(nblk,),
        in_specs=[t_spec, n_spec, n_spec, pl.BlockSpec((1, LANES), lambda p: (0, p)),
                  pl.BlockSpec((1, LANES), lambda p: (0, p)),
                  pl.BlockSpec((LANES, LANES), lambda p: (0, 0))],
        out_specs=n_spec, out_shape=jax.ShapeDtypeStruct((steps, nseq, W), BF16),
        compiler_params=_cparams("arbitrary"), name="rwkv_post",
    )(o_t, gate, bonus, lng, lnb, ones_blk)
    return jnp.transpose(out, (1, 0, 2)).reshape(nseq * steps, W), stack


def _zeros_like_cols(x, n):
    return jnp.zeros(x.shape[:-1] + (n,), x.dtype)


def _pack_rwkv_cols(x):
    return jnp.concatenate([x[..., :_O_XW], x[..., _O_XG:], _zeros_like_cols(x, G_PAD - RWKV_R_G),
                            x[..., _O_XW:_O_XG]], axis=-1)


def _pack_tail_cols(w):
    return jnp.concatenate([_pack_rwkv_cols(w[..., _O_RW:]), w[..., _O_DT:_O_RW],
                            _zeros_like_cols(w, DT_PAD - SSD_HEADS)], axis=-1)


def _pad_rows(x, n_before, n_total):
    b, r, c = x.shape
    return jnp.concatenate([jnp.zeros((b, n_before, c), x.dtype), x,
                            jnp.zeros((b, n_total - n_before - r, c), x.dtype)], axis=1)


def _small_params(l, p):
    row = lambda v: v.reshape(1, -1)
    pad_lanes = lambda v, n: jnp.concatenate([v, jnp.zeros((n - v.shape[0],), v.dtype)]).reshape(1, n)
    zeros_w = jnp.zeros((RWKV_R_W, RWKV_WIDTH), F32)
    return dict(
        lru=(p['lru_conv_w'][l], row(p['lru_conv_b'][l]), p['lru_wa'][l].astype(BF16),
             row(p['lru_ba'][l]), p['lru_wx'][l].astype(BF16), row(p['lru_bx'][l]),
             row(p['lru_lambda'][l])),
        ssd=(p['ssd_conv_w'][l], row(p['ssd_conv_b'][l]), pad_lanes(p['ssd_dt_bias'][l], DT_PAD),
             pad_lanes(p['ssd_a_log'][l], DT_PAD), row(jnp.repeat(p['ssd_d'][l], HEAD_DIM)),
             row(p['ssd_norm_g'][l])),
        rwkv=(row(_pack_rwkv_cols(p['rwkv_mu'][l])), row(p['rwkv_w0'][l]),
              jnp.concatenate([p['rwkv_w_up'][l], zeros_w], axis=0).astype(BF16),
              row(p['rwkv_a0'][l]),
              jnp.concatenate([zeros_w, p['rwkv_a_up'][l]], axis=0).astype(BF16),
              jnp.concatenate([p['rwkv_g_up'][l],
                               jnp.zeros((G_PAD - RWKV_R_G, RWKV_WIDTH), F32)], axis=0).astype(BF16),
              row(p['rwkv_k_k'][l]), row(p['rwkv_k_a'][l]), row(p['rwkv_r_k'][l]),
              row(p['rwkv_ln_g'][l]), row(p['rwkv_ln_b'][l])),
        ln1=(row(p['ln1_g'][l]), row(p['ln1_b'][l])),
        ln2=(row(p['ln2_g'][l]), row(p['ln2_b'][l])),
    )


def _constants():
    lane = jnp.arange(SSD_WIDTH) // HEAD_DIM
    ehp = (jnp.arange(DT_PAD)[:, None] == lane[None, :]).astype(BF16)
    blk = jnp.arange(2 * LANES) // HEAD_DIM
    ones = (blk[:, None] == blk[None, :]).astype(BF16)
    return ehp, ones


def _layer(x, xb, l, small_state, ssd_in, rw_in, prev_outs, sp, big, consts, *, nseq, seqlen,
           lru_nblk, rwkv_nb):
    lru_conv0, lru_h0, ssd_conv0, rw_shift0 = small_state
    ehp, ones = consts
    nbuf = CONV_WIDTH - 1
    proj = _proj(xb, big['w_main'], big['w_tail'], l)

    lru_nseq = nseq // lru_nblk
    out_a, lru_h1 = _lru(
        proj, _pad_rows(lru_conv0, 0, SUBLANES).reshape(lru_nblk, lru_nseq * SUBLANES, LRU_WIDTH),
        lru_h0.reshape(lru_nblk, lru_nseq, LRU_WIDTH), *sp['lru'],
        nblk=lru_nblk, nseq=lru_nseq, seqlen=seqlen)
    p3 = proj.reshape(nseq, seqlen, N_PROJ)
    if seqlen == SUBLANES and nseq % SSD_SHORT_NB == 0:
        out_b, ssd_out = _ssd_short(p3, _pad_rows(ssd_conv0, 0, SUBLANES), ssd_in[0], ssd_in[1], l,
                                    prev_outs[0], *sp['ssd'], ehp)
    else:
        out_b, ssd_out = _ssd(proj, _pad_rows(ssd_conv0, SUBLANES - nbuf, SUBLANES), ssd_in[0],
                              ssd_in[1], l, prev_outs[0], *sp['ssd'], ehp, nseq=nseq, seqlen=seqlen)
    if rwkv_nb is None:
        out_c, rw_out = _rwkv_steps(p3, _pack_rwkv_cols(rw_shift0), rw_in[0], rw_in[1], l,
                                    prev_outs[1], *sp['rwkv'], ones)
    else:
        out_c, rw_out = _rwkv(p3, _pack_rwkv_cols(rw_shift0)[:, None, :], rw_in[0], rw_in[1], l,
                              prev_outs[1], *sp['rwkv'], ones, nb=rwkv_nb)
        out_c = out_c.reshape(nseq * seqlen, RWKV_WIDTH)

    y, yb = _outproj_ln(out_a, out_b, out_c, big['w_out'], l, x, *sp['ln1'])
    y, yb = _ffn_down_ln(_ffn_up(yb, big['w_gate'], big['w_up'], l), big['w_down'], l, y, *sp['ln2'])

    tail = lambda col, width: p3[:, seqlen - nbuf:, col:col + width]
    last = lambda col, width: p3[:, seqlen - 1, col:col + width]
    rw_shift1 = jnp.concatenate([last(C_R, 3 * RWKV_WIDTH), last(C_WA, WA_PAD),
                                 last(C_G, RWKV_R_G)], axis=-1)
    small_new = (tail(C_LX, LRU_WIDTH), lru_h1.reshape(nseq, LRU_WIDTH),
                 tail(C_XBC, SSD_CONV_DIM), rw_shift1)
    return y, yb, small_new, (ssd_out, rw_out)


def kernel(x_prompt, x_sample, state_lru_conv, state_lru_h, state_ssd_conv, state_ssd,
           state_rwkv_shift, state_rwkv, w_in, lru_conv_w, lru_conv_b, lru_wa, lru_ba, lru_wx,
           lru_bx, lru_lambda, ssd_conv_w, ssd_conv_b, ssd_dt_bias, ssd_a_log, ssd_d, ssd_norm_g,
           rwkv_mu, rwkv_w0, rwkv_w_up, rwkv_a0, rwkv_a_up, rwkv_g_up, rwkv_k_k, rwkv_k_a,
           rwkv_r_k, rwkv_ln_g, rwkv_ln_b, w_out, ln1_g, ln1_b, w_gate, w_up, w_down, ln2_g, ln2_b):
    params = dict(
        lru_conv_w=lru_conv_w, lru_conv_b=lru_conv_b, lru_wa=lru_wa, lru_ba=lru_ba,
        lru_wx=lru_wx, lru_bx=lru_bx, lru_lambda=lru_lambda, ssd_conv_w=ssd_conv_w,
        ssd_conv_b=ssd_conv_b, ssd_dt_bias=ssd_dt_bias, ssd_a_log=ssd_a_log, ssd_d=ssd_d,
        ssd_norm_g=ssd_norm_g, rwkv_mu=rwkv_mu, rwkv_w0=rwkv_w0, rwkv_w_up=rwkv_w_up,
        rwkv_a0=rwkv_a0, rwkv_a_up=rwkv_a_up, rwkv_g_up=rwkv_g_up, rwkv_k_k=rwkv_k_k,
        rwkv_k_a=rwkv_k_a, rwkv_r_k=rwkv_r_k.reshape(DEPTH, RWKV_WIDTH), rwkv_ln_g=rwkv_ln_g,
        rwkv_ln_b=rwkv_ln_b, ln1_g=ln1_g, ln1_b=ln1_b, ln2_g=ln2_g, ln2_b=ln2_b)
    big = dict(w_main=jnp.swapaxes(w_in, 1, 2),
               w_tail=jnp.swapaxes(_pack_tail_cols(w_in), 1, 2),
               w_out=w_out.astype(BF16),
               w_gate=w_gate, w_up=w_up, w_down=w_down.astype(BF16))
    bp, lp_len, _ = x_prompt.shape
    bs, ls_len, _ = x_sample.shape
    consts = _constants()
    nbuf = CONV_WIDTH - 1
    zero_small = (jnp.zeros((bp, nbuf, LRU_WIDTH), F32), jnp.zeros((bp, LRU_WIDTH), F32),
                  jnp.zeros((bp, nbuf, SSD_CONV_DIM), F32), jnp.zeros((bp, RWKV_SHIFT), F32))
    zero_ssd = jnp.zeros((1, bp, SSD_HEADS, HEAD_DIM, SSD_STATE), F32)
    zero_rw = jnp.zeros((1, bp, RWKV_HEADS, HEAD_DIM, HEAD_DIM), F32)
    yp = x_prompt.reshape(bp * lp_len, D_MODEL)
    ys = x_sample.reshape(bs * ls_len, D_MODEL)
    ypb = yp.astype(BF16)
    ysb = ys.astype(BF16)
    state_rwkv_t = jnp.transpose(state_rwkv, (0, 2, 3, 4, 1))
    new_p = [[] for _ in range(4)]
    new_s = [[] for _ in range(4)]
    outs_p = (None, None)
    outs_s = (None, None)
    for l in range(DEPTH):
        sp = _small_params(l, params)
        yp, ypb, small_p, outs_p = _layer(yp, ypb, l, zero_small, (zero_ssd, 0), (zero_rw, 0), outs_p, sp,
                                     big, consts, nseq=bp, seqlen=lp_len, lru_nblk=bp,
                                     rwkv_nb=RWKV_PROMPT_NB)
        ys, ysb, small_s, outs_s = _layer(
            ys, ysb, l, (state_lru_conv[l], state_lru_h[l], state_ssd_conv[l], state_rwkv_shift[l]),
            (state_ssd, l), (state_rwkv_t, l), outs_s, sp, big, consts,
            nseq=bs, seqlen=ls_len, lru_nblk=1, rwkv_nb=None)
        for i in range(4):
            new_p[i].append(small_p[i])
            new_s[i].append(small_s[i])
    p_lru_conv, p_lru_h, p_ssd_conv, p_rw_shift = [jnp.stack(v) for v in new_p]
    s_lru_conv, s_lru_h, s_ssd_conv, s_rw_shift = [jnp.stack(v) for v in new_s]
    return (yp.reshape(bp, lp_len, D_MODEL), ys.reshape(bs, ls_len, D_MODEL),
            p_lru_conv, p_lru_h, p_ssd_conv, outs_p[0], p_rw_shift, outs_p[1],
            s_lru_conv, s_lru_h, s_ssd_conv, outs_s[0], s_rw_shift,
            jnp.transpose(outs_s[1], (0, 4, 1, 2, 3)))
```

```python
import functools
import math

import jax
import jax.numpy as jnp
from jax import lax
from jax.experimental import pallas as pl
from jax.experimental.pallas import tpu as pltpu

F32 = jnp.float32
BF16 = jnp.bfloat16

D_MODEL = 2048
DEPTH = 2
D_MIX = 2 * D_MODEL
HEAD_DIM = 64
CONV_WIDTH = 4
LRU_WIDTH = D_MIX // 4
LRU_HEADS = 8
LRU_BLOCK = LRU_WIDTH // LRU_HEADS
LRU_C = 8.0
SSD_WIDTH = D_MIX // 2
SSD_HEADS = SSD_WIDTH // HEAD_DIM
SSD_GROUPS = 8
SSD_STATE = 128
SSD_CHUNK = 128
SSD_CONV_DIM = SSD_WIDTH + 2 * SSD_GROUPS * SSD_STATE
SSD_GROUP_WIDTH = SSD_WIDTH // SSD_GROUPS
SSD_HEADS_PER_GROUP = SSD_HEADS // SSD_GROUPS
RWKV_WIDTH = D_MIX - LRU_WIDTH - SSD_WIDTH
RWKV_HEADS = RWKV_WIDTH // HEAD_DIM
RWKV_R_W = max(32, int(round(1.8 * RWKV_WIDTH ** 0.5 / 32)) * 32)
RWKV_R_A = max(32, int(round(1.8 * RWKV_WIDTH ** 0.5 / 32)) * 32)
RWKV_R_G = max(32, int(round(0.6 * RWKV_WIDTH ** 0.8 / 32)) * 32)
RWKV_SHIFT = 3 * RWKV_WIDTH + RWKV_R_W + RWKV_R_A + RWKV_R_G
RWKV_GN_EPS = 64e-5
RWKV_CHUNK = 64
RWKV_PROMPT_NB = 2
D_FF = -(-(8 * D_MODEL) // (3 * 256)) * 256
ALPHA = (2 * DEPTH) ** 0.25
LN_EPS = 1e-5
LN_ROW_CHUNK = 128

LANES = 128
SUBLANES = 8
VMEM_LIMIT = 56 * 1024 * 1024

C_GATE = 0
C_LX = C_GATE + LRU_WIDTH
C_Z = C_LX + LRU_WIDTH
C_XBC = C_Z + SSD_WIDTH
C_R = C_XBC + SSD_CONV_DIM
C_K = C_R + RWKV_WIDTH
C_V = C_K + RWKV_WIDTH
C_G = C_V + RWKV_WIDTH
G_PAD = 2 * LANES
C_WA = C_G + G_PAD
WA_PAD = LANES
C_DT = C_WA + WA_PAD
DT_PAD = LANES
N_PROJ = C_DT + DT_PAD
assert RWKV_R_G <= G_PAD and RWKV_R_W + RWKV_R_A == WA_PAD and SSD_HEADS <= DT_PAD
RW_PACK = 3 * RWKV_WIDTH + G_PAD + WA_PAD

_O_DT = 2 * LRU_WIDTH + SSD_WIDTH + SSD_CONV_DIM
_O_RW = _O_DT + SSD_HEADS
_O_XW = 3 * RWKV_WIDTH
_O_XG = _O_XW + RWKV_R_W + RWKV_R_A


def _cparams(*sem):
    return pltpu.CompilerParams(dimension_semantics=sem, vmem_limit_bytes=VMEM_LIMIT)


_NN = (((1,), (0,)), ((), ()))
_NT = (((1,), (1,)), ((), ()))
_TN = (((0,), (0,)), ((), ()))


def _dg(a, b, dims):
    return lax.dot_general(a, b, dims, preferred_element_type=F32)


def _dot1(a, b, dims=_NN):
    return _dg(a.astype(BF16), b.astype(BF16), dims)


def _split2(x):
    hi = x.astype(BF16)
    lo = (x - hi.astype(F32)).astype(BF16)
    return hi, lo


def _split3(x):
    hi = x.astype(BF16)
    r1 = x - hi.astype(F32)
    mid = r1.astype(BF16)
    lo = (r1 - mid.astype(F32)).astype(BF16)
    return hi, mid, lo


def _dot_shared_rhs(parts, w_exact, npieces):
    pieces = []
    for p in parts:
        rest = p
        for _ in range(npieces):
            piece = rest.astype(BF16).astype(F32)
            pieces.append(piece)
            rest = rest - piece
    prod = _dg(jnp.concatenate(pieces, axis=0).astype(BF16), w_exact, _NN)
    outs = []
    off = 0
    for p in parts:
        r = p.shape[0]
        acc = prod[off:off + r]
        for i in range(1, npieces):
            acc = acc + prod[off + i * r:off + (i + 1) * r]
        outs.append(acc)
        off += npieces * r
    return outs


def _dotx_left(w_exact, a, npieces=3):
    if npieces == 2:
        hi, lo = _split2(a)
        return _dg(w_exact, hi, _NN) + _dg(w_exact, lo, _NN)
    hi, mid, lo = _split3(a)
    return _dg(w_exact, hi, _NN) + (_dg(w_exact, mid, _NN) + _dg(w_exact, lo, _NN))


def _iota2(shape, dim):
    return lax.broadcasted_iota(jnp.int32, shape, dim)


def _softplus(x):
    return jnp.maximum(x, 0.0) + jnp.log1p(jnp.exp(-jnp.abs(x)))


def _sigmoid(x):
    return 1.0 / (1.0 + jnp.exp(-x))


def _silu(x):
    return x * _sigmoid(x)


def _gelu_tanh(x):
    c = math.sqrt(2.0 / math.pi)
    return 0.5 * x * (1.0 + jnp.tanh(c * (x + 0.044715 * (x * x * x))))


def _layer_norm(y, g, b):
    mu = jnp.mean(y, axis=-1, keepdims=True)
    d = y - mu
    var = jnp.mean(d * d, axis=-1, keepdims=True)
    return d * lax.rsqrt(var + LN_EPS) * g + b


DENSE_TM = 2048
DENSE_TN = 512
N_MAIN = C_R
N_TAIL = N_PROJ - N_MAIN
assert N_MAIN == _O_DT and N_MAIN % DENSE_TN == 0 and N_TAIL % DENSE_TN == 0 and D_FF % DENSE_TN == 0


def _proj_kernel(x_ref, wm_ref, wt_ref, o_ref):
    j = pl.program_id(1)

    @pl.when(j < N_MAIN // DENSE_TN)
    def _():
        o_ref[...] = _dg(x_ref[...], wm_ref[0].astype(BF16), _NT)

    @pl.when(j >= N_MAIN // DENSE_TN)
    def _():
        o_ref[...] = _dg(x_ref[...], wt_ref[0].astype(BF16), _NT)


def _proj(x, w_main_t, w_tail_t, l):
    m = x.shape[0]
    tm = min(m, DENSE_TM)
    tn = DENSE_TN
    nmain = N_MAIN // tn
    return pl.pallas_call(
        _proj_kernel,
        grid=(m // tm, N_PROJ // tn),
        in_specs=[pl.BlockSpec((tm, D_MODEL), lambda i, j: (i, 0)),
                  pl.BlockSpec((1, tn, D_MODEL), lambda i, j: (l, jnp.minimum(j, nmain - 1), 0)),
                  pl.BlockSpec((1, tn, D_MODEL), lambda i, j: (l, jnp.maximum(j - nmain, 0), 0))],
        out_specs=pl.BlockSpec((tm, tn), lambda i, j: (i, j)),
        out_shape=jax.ShapeDtypeStruct((m, N_PROJ), F32),
        compiler_params=_cparams("arbitrary", "arbitrary"),
        name="proj",
    )(x, w_main_t, w_tail_t)


OUT_KSTEPS = 2


def _outproj_kernel(a_ref, b_ref, c_ref, w_ref, x_ref, g_ref, beta_ref, o_ref, ob_ref):
    k = pl.program_id(1)
    half = LRU_WIDTH

    @pl.when(k == 0)
    def _():
        o_ref[...] = (jnp.dot(a_ref[...], w_ref[0, :half], preferred_element_type=F32)
                      + jnp.dot(b_ref[...], w_ref[0, half:], preferred_element_type=F32))

    @pl.when(k == OUT_KSTEPS - 1)
    def _():
        o_ref[...] += (jnp.dot(b_ref[...], w_ref[0, :half], preferred_element_type=F32)
                       + jnp.dot(c_ref[...], w_ref[0, half:], preferred_element_type=F32))
        _residual_ln_rows(o_ref, ob_ref, x_ref, g_ref, beta_ref)


def _residual_ln_rows(o_ref, ob_ref, x_ref, g_ref, beta_ref):
    chunk = min(LN_ROW_CHUNK, o_ref.shape[0])

    def ln_rows(r, carry):
        rows = pl.ds(pl.multiple_of(r * chunk, chunk), chunk)
        y = _layer_norm(ALPHA * x_ref[rows, :] + o_ref[rows, :], g_ref[...], beta_ref[...])
        o_ref[rows, :] = y
        ob_ref[rows, :] = y.astype(BF16)
        return carry

    lax.fori_loop(0, o_ref.shape[0] // chunk, ln_rows, 0)


def _outproj_ln(out_a, out_b, out_c, w_out, l, x, g, beta):
    m = x.shape[0]
    tm = min(m, 512)
    tk = LRU_WIDTH
    assert SSD_WIDTH == 2 * tk and RWKV_WIDTH == tk and D_MIX == OUT_KSTEPS * 2 * tk
    return pl.pallas_call(
        _outproj_kernel,
        grid=(m // tm, OUT_KSTEPS),
        in_specs=[pl.BlockSpec((tm, tk), lambda i, k: (i, 0)),
                  pl.BlockSpec((tm, tk), lambda i, k: (i, k)),
                  pl.BlockSpec((tm, tk), lambda i, k: (i, 0)),
                  pl.BlockSpec((1, 2 * tk, D_MODEL), lambda i, k: (l, k, 0)),
                  pl.BlockSpec((tm, D_MODEL), lambda i, k: (i, 0)),
                  pl.BlockSpec((1, D_MODEL), lambda i, k: (0, 0)),
                  pl.BlockSpec((1, D_MODEL), lambda i, k: (0, 0))],
        out_specs=[pl.BlockSpec((tm, D_MODEL), lambda i, k: (i, 0))] * 2,
        out_shape=[jax.ShapeDtypeStruct((m, D_MODEL), F32), jax.ShapeDtypeStruct((m, D_MODEL), BF16)],
        compiler_params=_cparams("arbitrary", "arbitrary"),
        name="outproj_ln",
    )(out_a, out_b, out_c, w_out, x, g, beta)


def _ffn_up_kernel(x_ref, wg_ref, wu_ref, o_ref):
    xb = x_ref[...]
    gate = jnp.dot(xb, wg_ref[0].astype(BF16), preferred_element_type=F32)
    up = jnp.dot(xb, wu_ref[0].astype(BF16), preferred_element_type=F32)
    o_ref[...] = (_silu(gate) * up).astype(BF16)


def _ffn_up(x, wg, wu, l):
    m = x.shape[0]
    tm = min(m, DENSE_TM)
    tn = DENSE_TN
    return pl.pallas_call(
        _ffn_up_kernel,
        grid=(m // tm, D_FF // tn),
        in_specs=[pl.BlockSpec((tm, D_MODEL), lambda i, j: (i, 0)),
                  pl.BlockSpec((1, D_MODEL, tn), lambda i, j: (l, 0, j)),
                  pl.BlockSpec((1, D_MODEL, tn), lambda i, j: (l, 0, j))],
        out_specs=pl.BlockSpec((tm, tn), lambda i, j: (i, j)),
        out_shape=jax.ShapeDtypeStruct((m, D_FF), BF16),
        compiler_params=_cparams("arbitrary", "arbitrary"),
        name="ffn_up",
    )(x, wg, wu)


FFN_DOWN_KSTEPS = 2


def _ffn_down_kernel(h_ref, w_ref, x_ref, g_ref, beta_ref, o_ref, ob_ref):
    k = pl.program_id(1)

    @pl.when(k == 0)
    def _():
        o_ref[...] = jnp.dot(h_ref[...], w_ref[0], preferred_element_type=F32)

    @pl.when(k > 0)
    def _():
        o_ref[...] += jnp.dot(h_ref[...], w_ref[0], preferred_element_type=F32)

    @pl.when(k == FFN_DOWN_KSTEPS - 1)
    def _():
        _residual_ln_rows(o_ref, ob_ref, x_ref, g_ref, beta_ref)


def _ffn_down_ln(h, w_down, l, x, g, beta):
    m = x.shape[0]
    tm = min(m, 512)
    tk = D_FF // FFN_DOWN_KSTEPS
    assert tk * FFN_DOWN_KSTEPS == D_FF and tk % LANES == 0 and tm % min(LN_ROW_CHUNK, tm) == 0
    return pl.pallas_call(
        _ffn_down_kernel,
        grid=(m // tm, FFN_DOWN_KSTEPS),
        in_specs=[pl.BlockSpec((tm, tk), lambda i, k: (i, k)),
                  pl.BlockSpec((1, tk, D_MODEL), lambda i, k: (l, k, 0)),
                  pl.BlockSpec((tm, D_MODEL), lambda i, k: (i, 0)),
                  pl.BlockSpec((1, D_MODEL), lambda i, k: (0, 0)),
                  pl.BlockSpec((1, D_MODEL), lambda i, k: (0, 0))],
        out_specs=[pl.BlockSpec((tm, D_MODEL), lambda i, k: (i, 0))] * 2,
        out_shape=[jax.ShapeDtypeStruct((m, D_MODEL), F32), jax.ShapeDtypeStruct((m, D_MODEL), BF16)],
        compiler_params=_cparams("arbitrary", "arbitrary"),
        name="ffn_down_ln",
    )(h, w_down, x, g, beta)


def _conv_taps(u, first_rows, cw, cb, t_in_seq):
    out = cb + cw[CONV_WIDTH - 1:CONV_WIDTH, :] * u
    for j in range(1, CONV_WIDTH):
        shifted = jnp.where(t_in_seq < j, first_rows(j), pltpu.roll(u, j, 0))
        out = out + cw[CONV_WIDTH - 1 - j:CONV_WIDTH - j, :] * shifted
    return out


def _lru_kernel(gate_ref, lx_ref, buf_ref, h0_ref, cw_ref, cb_ref, wa_ref, ba_ref, wx_ref,
                bx_ref, lam_ref, out_ref, h1_ref, a_scr, b_scr, hin_scr, *, nseq, seqlen):
    rows = nseq * seqlen
    u = lx_ref[...]
    t_in_seq = _iota2((rows, LANES), 0) % seqlen
    if nseq == 1:
        buf = jnp.concatenate([buf_ref[0], jnp.zeros((rows - SUBLANES, LANES), F32)], axis=0) \
            if rows > SUBLANES else buf_ref[0]
    else:
        buf = buf_ref[0]
    nbuf = CONV_WIDTH - 1
    xc = _conv_taps(u, lambda j: pltpu.roll(buf, (j - nbuf) % rows, 0), cw_ref[...], cb_ref[...],
                    t_in_seq)

    r = _sigmoid(_dot1(xc, wa_ref[0]) + ba_ref[...])
    i = _sigmoid(_dot1(xc, wx_ref[0]) + bx_ref[...])
    log_a = (-LRU_C) * r * _softplus(-lam_ref[...])
    a = jnp.exp(log_a)
    b = jnp.sqrt(jnp.tanh(-log_a) * (a * a + 1.0)) * (i * xc)

    t8 = _iota2((rows, LANES), 0) % SUBLANES
    for s in (1, 2, 4):
        m = t8 >= s
        a_sh = pltpu.roll(a, s, 0)
        b_sh = pltpu.roll(b, s, 0)
        b = jnp.where(m, a * b_sh + b, b)
        a = jnp.where(m, a * a_sh, a)

    if seqlen == SUBLANES:
        h0 = h0_ref[0]
        hin = jnp.broadcast_to(h0[:, None, :], (nseq, SUBLANES, LANES)).reshape(rows, LANES)
        h = a * hin + b
        out_ref[...] = (h * _gelu_tanh(gate_ref[...])).astype(out_ref.dtype)
        a_scr[...] = h
        h1_ref[0] = a_scr[pl.ds(SUBLANES - 1, nseq, stride=SUBLANES), :]
    else:
        assert nseq == 1
        a_scr[...] = a
        b_scr[...] = b

        def carry_step(g, carry):
            base = pl.multiple_of(g * SUBLANES, SUBLANES)
            hin_scr[pl.ds(base, SUBLANES), :] = jnp.broadcast_to(carry, (SUBLANES, LANES))
            a7 = a_scr[pl.ds(base + SUBLANES - 1, 1), :]
            b7 = b_scr[pl.ds(base + SUBLANES - 1, 1), :]
            return a7 * carry + b7

        last = lax.fori_loop(0, rows // SUBLANES, carry_step, h0_ref[0])
        h = a_scr[...] * hin_scr[...] + b_scr[...]
        out_ref[...] = (h * _gelu_tanh(gate_ref[...])).astype(out_ref.dtype)
        h1_ref[0] = last


def _lru(proj, buf, h0, cw, cb, wa, ba, wx, bx, lam, *, nblk, nseq, seqlen):
    rows = nseq * seqlen
    gate_blk = C_GATE // LRU_BLOCK
    lx_blk = C_LX // LRU_BLOCK
    row = lambda s, h: (0, h)
    return pl.pallas_call(
        functools.partial(_lru_kernel, nseq=nseq, seqlen=seqlen),
        grid=(nblk, LRU_HEADS),
        in_specs=[pl.BlockSpec((rows, LRU_BLOCK), lambda s, h: (s, gate_blk + h)),
                  pl.BlockSpec((rows, LRU_BLOCK), lambda s, h: (s, lx_blk + h)),
                  pl.BlockSpec((1, nseq * SUBLANES, LRU_BLOCK), lambda s, h: (s, 0, h)),
                  pl.BlockSpec((1, nseq, LRU_BLOCK), lambda s, h: (s, 0, h)),
                  pl.BlockSpec((CONV_WIDTH, LRU_BLOCK), row),
                  pl.BlockSpec((1, LRU_BLOCK), row),
                  pl.BlockSpec((1, LRU_BLOCK, LRU_BLOCK), lambda s, h: (h, 0, 0)),
                  pl.BlockSpec((1, LRU_BLOCK), row),
                  pl.BlockSpec((1, LRU_BLOCK, LRU_BLOCK), lambda s, h: (h, 0, 0)),
                  pl.BlockSpec((1, LRU_BLOCK), row),
                  pl.BlockSpec((1, LRU_BLOCK), row)],
        out_specs=[pl.BlockSpec((rows, LRU_BLOCK), lambda s, h: (s, h)),
                   pl.BlockSpec((1, nseq, LRU_BLOCK), lambda s, h: (s, 0, h))],
        out_shape=[jax.ShapeDtypeStruct((nblk * rows, LRU_WIDTH), BF16),
                   jax.ShapeDtypeStruct((nblk, nseq, LRU_WIDTH), F32)],
        scratch_shapes=[pltpu.VMEM((rows, LANES), F32)] * 3,
        compiler_params=_cparams("arbitrary", "arbitrary"),
        name="lru",
    )(proj, proj, buf, h0, cw, cb, wa, ba, wx, bx, lam)


def _stacked_state_call(kernel_fn, grid, in_specs, operands, row_spec, row_shape, state_dims,
                        l_out, prev_out, scratch_shapes, name, nb=1):
    nseq = grid[0] * nb
    zeros = (0,) * len(state_dims)
    state_shape = jax.ShapeDtypeStruct((DEPTH, nseq) + tuple(state_dims), F32)
    aliases = {}
    if prev_out is None:
        assert l_out == 0
        state_spec = pl.BlockSpec((DEPTH, nb) + tuple(state_dims), lambda b, c: (0, b) + zeros)
    else:
        state_spec = pl.BlockSpec((1, nb) + tuple(state_dims), lambda b, c: (l_out, b) + zeros)
        in_specs = in_specs + [pl.BlockSpec(memory_space=pl.ANY)]
        operands = operands + [prev_out]
        aliases = {len(operands) - 1: 1}
        kernel_fn = functools.partial(_drop_alias_ref, kernel_fn, len(operands) - 1)
    return pl.pallas_call(
        kernel_fn, grid=grid, in_specs=in_specs, out_specs=[row_spec, state_spec],
        out_shape=[row_shape, state_shape], scratch_shapes=scratch_shapes,
        input_output_aliases=aliases, compiler_params=_cparams("arbitrary", "arbitrary"),
        name=name)(*operands)


def _drop_alias_ref(kernel_fn, pos, *refs):
    return kernel_fn(*refs[:pos], *refs[pos + 1:])


def _zero_other_slabs(state_ref):
    if state_ref.shape[0] > 1:
        state_ref[1:] = jnp.zeros((state_ref.shape[0] - 1,) + state_ref.shape[1:], state_ref.dtype)


def _transpose_rows_to_lanes(x, t):
    if t < LANES:
        x = jnp.concatenate([x, jnp.zeros((LANES - t, LANES), F32)], axis=0)
    return x.T[:, :t]


def _ssd_kernel(z_ref, xbc_ref, dt_ref, buf_ref, s0_ref, cw_ref, cb_ref, dtb_ref, alog_ref,
                dch_ref, ng_ref, ehp_ref, out_ref, s1_ref, tail_ref, pad_ref, y_scr, *, T):
    c = pl.program_id(1)
    n_state = SSD_STATE
    gw = SSD_GROUP_WIDTH
    hpg = SSD_HEADS_PER_GROUP

    @pl.when(c == 0)
    def _():
        tail_ref[...] = buf_ref[0]
        s1_ref[0:1] = s0_ref[...]
        _zero_other_slabs(s1_ref)

    u = xbc_ref[...]
    pad_ref[0:SUBLANES, :] = tail_ref[...]
    pad_ref[SUBLANES:SUBLANES + T, :] = u
    cw = cw_ref[...]
    xc = cb_ref[...] + cw[CONV_WIDTH - 1:CONV_WIDTH, :] * u
    for j in range(1, CONV_WIDTH):
        xc = xc + cw[CONV_WIDTH - 1 - j:CONV_WIDTH - j, :] * pad_ref[SUBLANES - j:SUBLANES - j + T, :]
    tail_ref[...] = pad_ref[T:T + SUBLANES, :]

    xa = _silu(xc)
    xs = xa[:, :SSD_WIDTH]
    bm = xa[:, SSD_WIDTH:SSD_WIDTH + SSD_GROUPS * n_state]
    cm = xa[:, SSD_WIDTH + SSD_GROUPS * n_state:]
    dt = _softplus(dt_ref[...] + dtb_ref[...])
    da = dt * (-jnp.exp(alog_ref[...]))
    ii = _iota2((T, T), 0)
    jj = _iota2((T, T), 1)
    causal = ii >= jj
    cum = _dotx_left(causal.astype(BF16), da)
    cum_t = _transpose_rows_to_lanes(cum, T)
    ehp = ehp_ref[...]
    cumx, dtx = _dot_shared_rhs([cum, dt], ehp, 2)
    xdt = xs * dtx
    ecum = jnp.exp(cumx)
    xdtd = xdt * jnp.exp(cumx[T - 1:T, :] - cumx)

    groups = range(SSD_GROUPS)
    heads = range(SSD_HEADS)
    cg = [cm[:, g * n_state:(g + 1) * n_state] for g in groups]
    bg = [bm[:, g * n_state:(g + 1) * n_state] for g in groups]
    sg = [s1_ref[0, 0, g * hpg:(g + 1) * hpg].reshape(gw, n_state) for g in groups]
    cb = [_dot1(cg[g], bg[g], _NT) for g in groups]
    y_off = [_dot1(cg[g], sg[g], _NT) for g in groups]
    st = [_dot1(xdtd[:, g * gw:(g + 1) * gw], bg[g], _TN) for g in groups]
    lm = [jnp.where(causal, jnp.exp(cum[:, h:h + 1] - cum_t[h:h + 1, :]), 0.0) for h in heads]
    y_diag = [_dot1(cb[h // hpg] * lm[h], xdt[:, h * HEAD_DIM:(h + 1) * HEAD_DIM]) for h in heads]
    for g in groups:
        y_scr[:, g * gw:(g + 1) * gw] = (jnp.concatenate(y_diag[g * hpg:(g + 1) * hpg], axis=1)
                                         + y_off[g] * ecum[:, g * gw:(g + 1) * gw])
        decay = jnp.concatenate(
            [jnp.broadcast_to(jnp.exp(cum_t[h:h + 1, T - 1:T]), (HEAD_DIM, n_state))
             for h in range(g * hpg, (g + 1) * hpg)], axis=0)
        s1_ref[0, 0, g * hpg:(g + 1) * hpg] = (sg[g] * decay + st[g]).reshape(hpg, HEAD_DIM, n_state)

    y = (y_scr[...] + dch_ref[...] * xs) * _silu(z_ref[...])
    outs = []
    for g in groups:
        yg = y[:, g * gw:(g + 1) * gw]
        ms = jnp.mean(yg * yg, axis=-1, keepdims=True)
        outs.append(yg * lax.rsqrt(ms + 1e-5))
    out_ref[...] = (jnp.concatenate(outs, axis=1) * ng_ref[...]).astype(out_ref.dtype)


def _ssd(proj, buf, s0, l_in, l_out, prev_out, cw, cb, dtb, alog, dch, ng, ehp, *, nseq, seqlen):
    T = min(SSD_CHUNK, seqlen)
    nc = seqlen // T
    const = lambda b, c: (0, 0)
    sblk = (1, 1, SSD_HEADS, HEAD_DIM, SSD_STATE)
    in_specs = [pl.BlockSpec((T, SSD_WIDTH), lambda b, c: (b * nc + c, C_Z // SSD_WIDTH)),
                pl.BlockSpec((T, SSD_CONV_DIM), lambda b, c: (b * nc + c, C_XBC // SSD_CONV_DIM)),
                pl.BlockSpec((T, DT_PAD), lambda b, c: (b * nc + c, C_DT // DT_PAD)),
                pl.BlockSpec((1, SUBLANES, SSD_CONV_DIM), lambda b, c: (b, 0, 0)),
                pl.BlockSpec(sblk, lambda b, c: (l_in, b, 0, 0, 0)),
                pl.BlockSpec((CONV_WIDTH, SSD_CONV_DIM), const),
                pl.BlockSpec((1, SSD_CONV_DIM), const),
                pl.BlockSpec((1, DT_PAD), const),
                pl.BlockSpec((1, DT_PAD), const),
                pl.BlockSpec((1, SSD_WIDTH), const),
                pl.BlockSpec((1, SSD_WIDTH), const),
                pl.BlockSpec((DT_PAD, SSD_WIDTH), const)]
    return _stacked_state_call(
        functools.partial(_ssd_kernel, T=T), (nseq, nc), in_specs,
        [proj, proj, proj, buf, s0, cw, cb, dtb, alog, dch, ng, ehp],
        pl.BlockSpec((T, SSD_WIDTH), lambda b, c: (b * nc + c, 0)),
        jax.ShapeDtypeStruct((nseq * seqlen, SSD_WIDTH), BF16),
        sblk[2:], l_out, prev_out,
        [pltpu.VMEM((SUBLANES, SSD_CONV_DIM), F32),
         pltpu.VMEM((T + SUBLANES, SSD_CONV_DIM), F32),
         pltpu.VMEM((T, SSD_WIDTH), F32)],
        "ssd")


SSD_SHORT_NB = 4


def _ssd_short_kernel(z_ref, xbc_ref, dt_ref, buf_ref, s0_ref, cw_ref, cb_ref, dtb_ref, alog_ref,
                      dch_ref, ng_ref, ehp_ref, out_ref, s1_ref, y_scr, *, nb, T):
    n_state = SSD_STATE
    gw = SSD_GROUP_WIDTH
    hpg = SSD_HEADS_PER_GROUP
    R = nb * T

    def per_seq_rows(x):
        return jnp.broadcast_to(x[:, None, :], (nb, T, x.shape[-1])).reshape(R, x.shape[-1])

    def last_rows(x):
        return jnp.concatenate([x[(b + 1) * T - 1:(b + 1) * T, :] for b in range(nb)], axis=0)

    u = xbc_ref[...].reshape(R, SSD_CONV_DIM)
    buf = buf_ref[...].reshape(nb * SUBLANES, SSD_CONV_DIM)
    assert T == SUBLANES
    t_in_seq = _iota2((R, SSD_CONV_DIM), 0) % T
    nbuf = CONV_WIDTH - 1
    xc = _conv_taps(u, lambda j: pltpu.roll(buf, (j - nbuf) % R, 0), cw_ref[...], cb_ref[...],
                    t_in_seq)

    xa = _silu(xc)
    xs = xa[:, :SSD_WIDTH]
    bm = xa[:, SSD_WIDTH:SSD_WIDTH + SSD_GROUPS * n_state]
    cm = xa[:, SSD_WIDTH + SSD_GROUPS * n_state:]
    dt = _softplus(dt_ref[...].reshape(R, DT_PAD) + dtb_ref[...])
    da = dt * (-jnp.exp(alog_ref[...]))
    ri = _iota2((R, R), 0)
    rj = _iota2((R, R), 1)
    same_seq_causal = jnp.logical_and(ri >= rj, ri // T == rj // T)
    cum = _dotx_left(same_seq_causal.astype(BF16), da)
    cum_t = _transpose_rows_to_lanes(cum, R)
    cumx, dtx = _dot_shared_rhs([cum, dt], ehp_ref[...], 2)
    xdt = xs * dtx
    ecum = jnp.exp(cumx)
    xdtd = xdt * jnp.exp(per_seq_rows(last_rows(cumx)) - cumx)

    causal = _iota2((T, T), 0) >= _iota2((T, T), 1)
    seqs = range(nb)
    groups = range(SSD_GROUPS)
    heads = range(SSD_HEADS)
    rows = [slice(b * T, (b + 1) * T) for b in seqs]
    cg = [[cm[rows[b], g * n_state:(g + 1) * n_state] for g in groups] for b in seqs]
    bg = [[bm[rows[b], g * n_state:(g + 1) * n_state] for g in groups] for b in seqs]
    sg = [[s0_ref[0, b, g * hpg:(g + 1) * hpg].reshape(gw, n_state) for g in groups] for b in seqs]
    cb = [[_dot1(cg[b][g], bg[b][g], _NT) for g in groups] for b in seqs]
    y_off = [[_dot1(cg[b][g], sg[b][g], _NT) for g in groups] for b in seqs]
    st = [[_dot1(xdtd[rows[b], g * gw:(g + 1) * gw], bg[b][g], _TN) for g in groups] for b in seqs]
    lm = [[jnp.where(causal, jnp.exp(cum[rows[b], h:h + 1] - cum_t[h:h + 1, rows[b]]), 0.0)
           for h in heads] for b in seqs]
    y_diag = [[_dot1(cb[b][h // hpg] * lm[b][h], xdt[rows[b], h * HEAD_DIM:(h + 1) * HEAD_DIM])
               for h in heads] for b in seqs]
    for b in seqs:
        for g in groups:
            y_scr[rows[b], g * gw:(g + 1) * gw] = (
                jnp.concatenate(y_diag[b][g * hpg:(g + 1) * hpg], axis=1)
                + y_off[b][g] * ecum[rows[b], g * gw:(g + 1) * gw])
            last = (b + 1) * T - 1
            decay = jnp.concatenate(
                [jnp.broadcast_to(jnp.exp(cum_t[h:h + 1, last:last + 1]), (HEAD_DIM, n_state))
                 for h in range(g * hpg, (g + 1) * hpg)], axis=0)
            s1_ref[0, b, g * hpg:(g + 1) * hpg] = (sg[b][g] * decay + st[b][g]).reshape(
                hpg, HEAD_DIM, n_state)
    _zero_other_slabs(s1_ref)

    y = (y_scr[...] + dch_ref[...] * xs) * _silu(z_ref[...].reshape(R, SSD_WIDTH))
    outs = []
    for g in groups:
        yg = y[:, g * gw:(g + 1) * gw]
        ms = jnp.mean(yg * yg, axis=-1, keepdims=True)
        outs.append(yg * lax.rsqrt(ms + 1e-5))
    res = jnp.concatenate(outs, axis=1) * ng_ref[...]
    out_ref[...] = res.astype(out_ref.dtype).reshape(out_ref.shape)


def _ssd_short(p3, buf, s0, l_in, l_out, prev_out, cw, cb, dtb, alog, dch, ng, ehp):
    nseq, T, _ = p3.shape
    nb = SSD_SHORT_NB
    const = lambda b, c: (0, 0)
    sdims = (SSD_HEADS, HEAD_DIM, SSD_STATE)
    blk = lambda width, col: pl.BlockSpec((nb, T, width), lambda b, c: (b, 0, col // width))
    in_specs = [blk(SSD_WIDTH, C_Z), blk(SSD_CONV_DIM, C_XBC), blk(DT_PAD, C_DT),
                pl.BlockSpec((nb, SUBLANES, SSD_CONV_DIM), lambda b, c: (b, 0, 0)),
                pl.BlockSpec((1, nb) + sdims, lambda b, c: (l_in, b, 0, 0, 0)),
                pl.BlockSpec((CONV_WIDTH, SSD_CONV_DIM), const),
                pl.BlockSpec((1, SSD_CONV_DIM), const),
                pl.BlockSpec((1, DT_PAD), const),
                pl.BlockSpec((1, DT_PAD), const),
                pl.BlockSpec((1, SSD_WIDTH), const),
                pl.BlockSpec((1, SSD_WIDTH), const),
                pl.BlockSpec((DT_PAD, SSD_WIDTH), const)]
    out, stack = _stacked_state_call(
        functools.partial(_ssd_short_kernel, nb=nb, T=T), (nseq // nb, 1), in_specs,
        [p3, p3, p3, buf, s0, cw, cb, dtb, alog, dch, ng, ehp],
        pl.BlockSpec((nb, T, SSD_WIDTH), lambda b, c: (b, 0, 0)),
        jax.ShapeDtypeStruct((nseq, T, SSD_WIDTH), BF16),
        sdims, l_out, prev_out,
        [pltpu.VMEM((nb * T, SSD_WIDTH), F32)],
        "ssd_short", nb=nb)
    return out.reshape(nseq * T, SSD_WIDTH), stack


def _unit_lower_inverse(a_list, ii, jj, T):
    n = len(a_list)
    pair = (ii >> 1) == (jj >> 1)
    inv = [jnp.where(ii == jj, 1.0, 0.0) + jnp.where(pair, a_list[h], 0.0) for h in range(n)]
    shift = 1
    while (2 << shift) <= T:
        band = jnp.logical_and((ii >> (shift + 1)) == (jj >> (shift + 1)),
                               (ii >> shift) != (jj >> shift))
        x = [_dot1(jnp.where(band, a_list[h], 0.0), inv[h]) for h in range(n)]
        inv = [inv[h] + _dot1(inv[h], x[h]) for h in range(n)]
        shift += 1
    return inv


def _rwkv_kernel(r_ref, k_ref, v_ref, g_ref, wa_ref, sh_ref, s0_ref, mu_ref, w0_ref, wup_ref,
                 a0_ref, aup_ref, gup_ref, kkw_ref, kaw_ref, rkw_ref, lng_ref, lnb_ref, ones_ref,
                 out_ref, s1_ref, prev_scr, s_scr, al_scr, be_scr, kt_scr, rt_scr, bs_scr,
                 ks_scr, v_scr, gam_scr, o_scr, *, nb, T, nchunks):
    c = pl.program_id(1)
    W = RWKV_WIDTH
    D = HEAD_DIM
    R = nb * T
    nh = RWKV_HEADS

    @pl.when(c == 0)
    def _():
        prev_scr[...] = sh_ref[...].reshape(nb, RW_PACK)
        s_scr[...] = s0_ref[0].reshape(nb * nh, D, D)

    def rows_of(ref3):
        return ref3[...].reshape(R, ref3.shape[-1])

    def per_seq_rows(x):
        return jnp.broadcast_to(x[:, None, :], (nb, T, x.shape[-1])).reshape(R, x.shape[-1])

    def last_rows(x):
        return jnp.concatenate([x[(b + 1) * T - 1:(b + 1) * T, :] for b in range(nb)], axis=0)

    def token_shift(p, lo, hi):
        first = _iota2(p.shape, 0) % T == 0
        prev = jnp.where(first, per_seq_rows(prev_scr[:, lo:hi]), pltpu.roll(p, 1, 0))
        prev_scr[:, lo:hi] = last_rows(p)
        return p + (prev - p) * mu_ref[:, lo:hi]

    xr = token_shift(rows_of(r_ref), 0, W)
    xk = token_shift(rows_of(k_ref), W, 2 * W)
    xv = token_shift(rows_of(v_ref), 2 * W, 3 * W)
    xg = token_shift(rows_of(g_ref), 3 * W, 3 * W + G_PAD)
    xwa = token_shift(rows_of(wa_ref), 3 * W + G_PAD, RW_PACK)

    w_lin = w0_ref[...] + _dot1(jnp.tanh(xwa), wup_ref[...])
    a = _sigmoid(a0_ref[...] + _dot1(xwa, aup_ref[...]))
    gate = _dot1(_sigmoid(xg), gup_ref[...])
    lw = -jnp.exp(-_softplus(-w_lin) - 0.5)

    ones = ones_ref[...]

    def head_sums(xs, npieces=3):
        tw = ones.shape[0]
        nt = W // tw
        tiles = [x[:, i * tw:(i + 1) * tw] for x in xs for i in range(nt)]
        sums = _dot_shared_rhs(tiles, ones, npieces)
        return [jnp.concatenate(sums[n * nt:(n + 1) * nt], axis=1) for n in range(len(xs))]

    kk = xk * kkw_ref[...]
    kp = xk * (1.0 + (a - 1.0) * kaw_ref[...])
    kk_sq, rk_sum = head_sums([kk * kk, xr * kp * rkw_ref[...]], 1)
    kk = kk / jnp.maximum(jnp.sqrt(kk_sq), 1e-12)

    ri = _iota2((R, R), 0)
    rj = _iota2((R, R), 1)
    same_seq_causal = jnp.logical_and(ri >= rj, ri // T == rj // T)
    cum = _dotx_left(same_seq_causal.astype(BF16), lw, 3)
    e_neg = jnp.exp(-cum)
    gam = jnp.exp(per_seq_rows(last_rows(cum)))
    be = kk * a * e_neg
    kt = kp * e_neg
    al_scr[...] = -kk * jnp.exp(cum - lw)
    be_scr[...] = be
    kt_scr[...] = kt
    rt_scr[...] = xr * jnp.exp(cum)
    bs_scr[...] = be * gam
    ks_scr[...] = kt * gam
    v_scr[...] = xv
    gam_scr[...] = gam

    ii = _iota2((T, T), 0)
    jj = _iota2((T, T), 1)
    incl = ii >= jj
    strict = ii > jj
    eye_d = _iota2((D, D), 0) == _iota2((D, D), 1)

    units = [(b, h) for b in range(nb) for h in range(nh)]
    nu = range(len(units))

    def per_unit(ref):
        return [ref[b * T:(b + 1) * T, h * D:(h + 1) * D] for b, h in units]

    al, be_u, kt_u, rt = per_unit(al_scr), per_unit(be_scr), per_unit(kt_scr), per_unit(rt_scr)
    bs_u, ks_u, vv, gam_u = per_unit(bs_scr), per_unit(ks_scr), per_unit(v_scr), per_unit(gam_scr)
    gram = [_dot1(jnp.concatenate([al[u], rt[u]], axis=0),
                  jnp.concatenate([be_u[u], kt_u[u]], axis=0), _NT) for u in nu]
    a_ab = [jnp.where(strict, gram[u][:T, :T], 0.0) for u in nu]
    a_ak = [jnp.where(strict, gram[u][:T, T:], 0.0) for u in nu]
    r_b = [jnp.where(incl, gram[u][T:, :T], 0.0) for u in nu]
    r_k = [jnp.where(incl, gram[u][T:, T:], 0.0) for u in nu]
    inv = _unit_lower_inverse(a_ab, ii, jj, T)
    akv = [_dot1(a_ak[u], vv[u]) for u in nu]
    pw = [_dot1(inv[u], jnp.concatenate([al[u], akv[u]], axis=1)) for u in nu]
    qo = [_dot1(r_b[u], pw[u]) for u in nu]
    rkv = [_dot1(r_k[u], vv[u]) for u in nu]
    smat = [s_scr[u] for u in nu]
    qs = [_dot1(rt[u] + qo[u][:, :D], smat[u], _NT) for u in nu]
    gp = [_dot1(bs_u[u], pw[u][:, :D], _TN) for u in nu]
    zt = [_dot1(jnp.concatenate([pw[u][:, D:], vv[u]], axis=0),
                jnp.concatenate([bs_u[u], ks_u[u]], axis=0), _TN) for u in nu]
    gmat = [jnp.where(eye_d, jnp.broadcast_to(gam_u[u][0:1, :], (D, D)), 0.0) + gp[u] for u in nu]
    sg = [_dot1(smat[u], gmat[u], _NT) for u in nu]
    for u in nu:
        s_scr[u] = sg[u] + zt[u]
    pairs = LANES // D
    for b in range(nb):
        for p in range(nh // pairs):
            us = [b * nh + p * pairs + s for s in range(pairs)]
            o_scr[b * T:(b + 1) * T, p * LANES:(p + 1) * LANES] = jnp.concatenate(
                [qs[u] + qo[u][:, D:] + rkv[u] for u in us], axis=1)

    o = o_scr[...]
    mean = head_sums([o])[0] * (1.0 / D)
    d = o - mean
    var = head_sums([d * d])[0] * (1.0 / D)
    on = d * lax.rsqrt(var + RWKV_GN_EPS) * lng_ref[...] + lnb_ref[...]
    bonus = rk_sum * xv
    out_ref[...] = ((on + bonus) * gate).astype(out_ref.dtype).reshape(nb, T, W)

    @pl.when(c == nchunks - 1)
    def _():
        s1_ref[0] = s_scr[...].reshape(nb, nh, D, D)
        _zero_other_slabs(s1_ref)


def _rwkv(proj3, sh, s0, l_in, l_out, prev_out, mu, w0, wup, a0, aup, gup, kkw, kaw, rkw, lng, lnb,
          ones, *, nb):
    nseq, seqlen, _ = proj3.shape
    T = min(RWKV_CHUNK, seqlen)
    nc = seqlen // T
    W = RWKV_WIDTH
    const = lambda b, c: (0, 0)
    rowblk = lambda col, width: pl.BlockSpec((nb, T, width), lambda b, c: (b, c, col // width))
    sdims = (RWKV_HEADS, HEAD_DIM, HEAD_DIM)
    vec = pl.BlockSpec((1, W), const)
    in_specs = [rowblk(C_R, W), rowblk(C_K, W), rowblk(C_V, W), rowblk(C_G, G_PAD),
                rowblk(C_WA, WA_PAD),
                pl.BlockSpec((nb, 1, RW_PACK), lambda b, c: (b, 0, 0)),
                pl.BlockSpec((1, nb) + sdims, lambda b, c: (l_in, b, 0, 0, 0)),
                pl.BlockSpec((1, RW_PACK), const),
                vec, pl.BlockSpec((WA_PAD, W), const),
                vec, pl.BlockSpec((WA_PAD, W), const),
                pl.BlockSpec((G_PAD, W), const),
                vec, vec, vec, vec, vec,
                pl.BlockSpec((2 * LANES, 2 * LANES), const)]
    rows = nb * T
    return _stacked_state_call(
        functools.partial(_rwkv_kernel, nb=nb, T=T, nchunks=nc), (nseq // nb, nc), in_specs,
        [proj3, proj3, proj3, proj3, proj3, sh, s0, mu, w0, wup, a0, aup, gup, kkw, kaw, rkw, lng,
         lnb, ones],
        pl.BlockSpec((nb, T, W), lambda b, c: (b, c, 0)),
        jax.ShapeDtypeStruct((nseq, seqlen, W), BF16),
        sdims, l_out, prev_out,
        [pltpu.VMEM((nb, RW_PACK), F32), pltpu.VMEM((nb * RWKV_HEADS, HEAD_DIM, HEAD_DIM), F32)]
        + [pltpu.VMEM((rows, W), F32)] * 7
        + [pltpu.VMEM((rows, W), F32), pltpu.VMEM((rows, W), F32)],
        "rwkv", nb=nb)


def _rws_pre_kernel(r_ref, k_ref, v_ref, g_ref, wa_ref, shr_ref, shk_ref, shv_ref, shg_ref,
                    shwa_ref, mur_ref, muk_ref, muv_ref, mug_ref, muwa_ref, w0_ref, wup_ref,
                    a0_ref, aup_ref, gup_ref, kkw_ref, kaw_ref, rkw_ref, ones_ref,
                    rt_ref, wt_ref, knt_ref, kat_ref, kpt_ref, vt_ref, gate_ref, bonus_ref):
    steps, nseq, _ = r_ref.shape

    def shifted(ref, sh_ref, mu_ref):
        x = ref[...]
        prev = jnp.concatenate([sh_ref[...][None], x[:steps - 1]], axis=0)
        return (x + (prev - x) * mu_ref[...]).reshape(steps * nseq, x.shape[-1])

    xr = shifted(r_ref, shr_ref, mur_ref)
    xk = shifted(k_ref, shk_ref, muk_ref)
    xv = shifted(v_ref, shv_ref, muv_ref)
    xg = shifted(g_ref, shg_ref, mug_ref)
    xwa = shifted(wa_ref, shwa_ref, muwa_ref)

    w_lin = w0_ref[...] + _dot1(jnp.tanh(xwa), wup_ref[...])
    a = _sigmoid(a0_ref[...] + _dot1(xwa, aup_ref[...]))
    gate = _dot1(_sigmoid(xg), gup_ref[...])
    decay = jnp.exp(-jnp.exp(-_softplus(-w_lin) - 0.5))
    kk = xk * kkw_ref[...]
    kp = xk * (1.0 + (a - 1.0) * kaw_ref[...])
    kk_sq, rk_sum = _dot_shared_rhs([kk * kk, xr * kp * rkw_ref[...]], ones_ref[...], 3)
    kk = kk / jnp.maximum(jnp.sqrt(kk_sq), 1e-12)

    for t in range(steps):
        rows = slice(t * nseq, (t + 1) * nseq)
        rt_ref[t] = xr[rows].T
        wt_ref[t] = decay[rows].T
        knt_ref[t] = (-kk[rows]).T
        kat_ref[t] = (kk[rows] * a[rows]).T
        kpt_ref[t] = kp[rows].T
        vt_ref[t] = xv[rows].T
    gate_ref[...] = gate.reshape(gate_ref.shape)
    bonus_ref[...] = (rk_sum * xv).reshape(bonus_ref.shape)


def _rws_rec_kernel(r_ref, w_ref, kn_ref, ka_ref, kp_ref, v_ref, s0_ref, o_ref, s1_ref):
    steps = r_ref.shape[0]

    def value_rows(vc, carry):
        v0 = pl.multiple_of(vc * SUBLANES, SUBLANES)
        s = [s0_ref[0, 0, v0 + i] for i in range(SUBLANES)]
        for t in range(steps):
            kn, ka, kp, w, r = kn_ref[t], ka_ref[t], kp_ref[t], w_ref[t], r_ref[t]
            vv = v_ref[t, pl.ds(v0, SUBLANES), :]
            outs = []
            for i in range(SUBLANES):
                sa = jnp.sum(s[i] * kn, axis=0, keepdims=True)
                s[i] = s[i] * w + (sa * ka + vv[i:i + 1, :] * kp)
                outs.append(jnp.sum(s[i] * r, axis=0, keepdims=True))
            o_ref[t, pl.ds(v0, SUBLANES), :] = jnp.concatenate(outs, axis=0)
        for i in range(SUBLANES):
            s1_ref[0, 0, v0 + i] = s[i]
        return carry

    lax.fori_loop(0, HEAD_DIM // SUBLANES, value_rows, 0)
    _zero_other_slabs(s1_ref)


def _rws_post_kernel(o_ref, gate_ref, bonus_ref, lng_ref, lnb_ref, ones_ref, out_ref):
    steps, _, nseq = o_ref.shape
    o = jnp.concatenate([o_ref[t].T for t in range(steps)], axis=0)
    ones = ones_ref[...]
    mean = _dot_shared_rhs([o], ones, 3)[0] * (1.0 / HEAD_DIM)
    d = o - mean
    var = _dot_shared_rhs([d * d], ones, 3)[0] * (1.0 / HEAD_DIM)
    on = d * lax.rsqrt(var + RWKV_GN_EPS) * lng_ref[...] + lnb_ref[...]
    rows = steps * nseq
    res = (on + bonus_ref[...].reshape(rows, LANES)) * gate_ref[...].reshape(rows, LANES)
    out_ref[...] = res.astype(out_ref.dtype).reshape(out_ref.shape)


def _rwkv_steps(p3, sh, s0_t, l_in, l_out, prev_out, mu, w0, wup, a0, aup, gup, kkw, kaw, rkw, lng,
                lnb, ones):
    nseq, steps, _ = p3.shape
    W, D, H = RWKV_WIDTH, HEAD_DIM, RWKV_HEADS
    pt = jnp.transpose(p3[:, :, C_R:], (1, 0, 2))
    nblk = W // LANES
    ones_blk = ones[:LANES, :LANES]
    g_col, wa_col = 3 * W, 3 * W + G_PAD

    def cols(rows, width, col):
        return pl.BlockSpec(rows + (width,), lambda p: (0,) * len(rows) + (col // width,))

    def cols_p(rows, base):
        return pl.BlockSpec(rows + (LANES,), lambda p: (0,) * len(rows) + (base // LANES + p,))

    tb = (steps, nseq)
    in_specs = ([cols_p(tb, 0), cols_p(tb, W), cols_p(tb, 2 * W), cols(tb, G_PAD, g_col),
                 cols(tb, WA_PAD, wa_col)]
                + [cols_p((nseq,), 0), cols_p((nseq,), W), cols_p((nseq,), 2 * W),
                   cols((nseq,), G_PAD, g_col), cols((nseq,), WA_PAD, wa_col)]
                + [cols_p((1,), 0), cols_p((1,), W), cols_p((1,), 2 * W), cols((1,), G_PAD, g_col),
                   cols((1,), WA_PAD, wa_col)]
                + [cols_p((1,), 0), cols_p((WA_PAD,), 0), cols_p((1,), 0), cols_p((WA_PAD,), 0),
                   cols_p((G_PAD,), 0), cols_p((1,), 0), cols_p((1,), 0), cols_p((1,), 0),
                   pl.BlockSpec((LANES, LANES), lambda p: (0, 0))])
    t_spec = pl.BlockSpec((steps, LANES, nseq), lambda p: (0, p, 0))
    n_spec = pl.BlockSpec((steps, nseq, LANES), lambda p: (0, 0, p))
    t_shape = jax.ShapeDtypeStruct((steps, W, nseq), F32)
    n_shape = jax.ShapeDtypeStruct((steps, nseq, W), F32)
    pre = pl.pallas_call(
        _rws_pre_kernel, grid=(nblk,), in_specs=in_specs,
        out_specs=[t_spec] * 6 + [n_spec] * 2, out_shape=[t_shape] * 6 + [n_shape] * 2,
        compiler_params=_cparams("arbitrary"), name="rwkv_pre",
    )(pt, pt, pt, pt, pt, sh, sh, sh, sh, sh, mu, mu, mu, mu, mu, w0, wup, a0, aup, gup, kkw, kaw,
      rkw, ones_blk)
    rt, wt, knt, kat, kpt, vt, gate, bonus = pre

    head = pl.BlockSpec((steps, D, nseq), lambda h, c: (0, h, 0))
    o_t, stack = _stacked_state_call(
        _rws_rec_kernel, (H, 1),
        [head] * 6 + [pl.BlockSpec((1, 1, D, D, nseq), lambda h, c: (l_in, h, 0, 0, 0))],
        [rt, wt, knt, kat, kpt, vt, s0_t],
        head, t_shape, (D, D, nseq), l_out, prev_out, [], "rwkv_rec")

    out = pl.pallas_call(
        _rws_post_kernel, grid=(nblk,),
        in_specs=[t_spec, n_spec, n_spec, pl.BlockSpec((1, LANES), lambda p: (0, p)),
                  pl.BlockSpec((1, LANES), lambda p: (0, p)),
                  pl.BlockSpec((LANES, LANES), lambda p: (0, 0))],
        out_specs=n_spec, out_shape=jax.ShapeDtypeStruct((steps, nseq, W), BF16),
        compiler_params=_cparams("arbitrary"), name="rwkv_post",
    )(o_t, gate, bonus, lng, lnb, ones_blk)
    return jnp.transpose(out, (1, 0, 2)).reshape(nseq * steps, W), stack


def _zeros_like_cols(x, n):
    return jnp.zeros(x.shape[:-1] + (n,), x.dtype)


def _pack_rwkv_cols(x):
    return jnp.concatenate([x[..., :_O_XW], x[..., _O_XG:], _zeros_like_cols(x, G_PAD - RWKV_R_G),
                            x[..., _O_XW:_O_XG]], axis=-1)


def _pack_tail_cols(w):
    return jnp.concatenate([_pack_rwkv_cols(w[..., _O_RW:]), w[..., _O_DT:_O_RW],
                            _zeros_like_cols(w, DT_PAD - SSD_HEADS)], axis=-1)


def _pad_rows(x, n_before, n_total):
    b, r, c = x.shape
    return jnp.concatenate([jnp.zeros((b, n_before, c), x.dtype), x,
                            jnp.zeros((b, n_total - n_before - r, c), x.dtype)], axis=1)


def _small_params(l, p):
    row = lambda v: v.reshape(1, -1)
    pad_lanes = lambda v, n: jnp.concatenate([v, jnp.zeros((n - v.shape[0],), v.dtype)]).reshape(1, n)
    zeros_w = jnp.zeros((RWKV_R_W, RWKV_WIDTH), F32)
    return dict(
        lru=(p['lru_conv_w'][l], row(p['lru_conv_b'][l]), p['lru_wa'][l].astype(BF16),
             row(p['lru_ba'][l]), p['lru_wx'][l].astype(BF16), row(p['lru_bx'][l]),
             row(p['lru_lambda'][l])),
        ssd=(p['ssd_conv_w'][l], row(p['ssd_conv_b'][l]), pad_lanes(p['ssd_dt_bias'][l], DT_PAD),
             pad_lanes(p['ssd_a_log'][l], DT_PAD), row(jnp.repeat(p['ssd_d'][l], HEAD_DIM)),
             row(p['ssd_norm_g'][l])),
        rwkv=(row(_pack_rwkv_cols(p['rwkv_mu'][l])), row(p['rwkv_w0'][l]),
              jnp.concatenate([p['rwkv_w_up'][l], zeros_w], axis=0).astype(BF16),
              row(p['rwkv_a0'][l]),
              jnp.concatenate([zeros_w, p['rwkv_a_up'][l]], axis=0).astype(BF16),
              jnp.concatenate([p['rwkv_g_up'][l],
                               jnp.zeros((G_PAD - RWKV_R_G, RWKV_WIDTH), F32)], axis=0).astype(BF16),
              row(p['rwkv_k_k'][l]), row(p['rwkv_k_a'][l]), row(p['rwkv_r_k'][l]),
              row(p['rwkv_ln_g'][l]), row(p['rwkv_ln_b'][l])),
        ln1=(row(p['ln1_g'][l]), row(p['ln1_b'][l])),
        ln2=(row(p['ln2_g'][l]), row(p['ln2_b'][l])),
    )


def _constants():
    lane = jnp.arange(SSD_WIDTH) // HEAD_DIM
    ehp = (jnp.arange(DT_PAD)[:, None] == lane[None, :]).astype(BF16)
    blk = jnp.arange(2 * LANES) // HEAD_DIM
    ones = (blk[:, None] == blk[None, :]).astype(BF16)
    return ehp, ones


def _layer(x, xb, l, small_state, ssd_in, rw_in, prev_outs, sp, big, consts, *, nseq, seqlen,
           lru_nblk, rwkv_nb):
    lru_conv0, lru_h0, ssd_conv0, rw_shift0 = small_state
    ehp, ones = consts
    nbuf = CONV_WIDTH - 1
    proj = _proj(xb, big['w_main'], big['w_tail'], l)

    lru_nseq = nseq // lru_nblk
    out_a, lru_h1 = _lru(
        proj, _pad_rows(lru_conv0, 0, SUBLANES).reshape(lru_nblk, lru_nseq * SUBLANES, LRU_WIDTH),
        lru_h0.reshape(lru_nblk, lru_nseq, LRU_WIDTH), *sp['lru'],
        nblk=lru_nblk, nseq=lru_nseq, seqlen=seqlen)
    p3 = proj.reshape(nseq, seqlen, N_PROJ)
    if seqlen == SUBLANES and nseq % SSD_SHORT_NB == 0:
        out_b, ssd_out = _ssd_short(p3, _pad_rows(ssd_conv0, 0, SUBLANES), ssd_in[0], ssd_in[1], l,
                                    prev_outs[0], *sp['ssd'], ehp)
    else:
        out_b, ssd_out = _ssd(proj, _pad_rows(ssd_conv0, SUBLANES - nbuf, SUBLANES), ssd_in[0],
                              ssd_in[1], l, prev_outs[0], *sp['ssd'], ehp, nseq=nseq, seqlen=seqlen)
    if rwkv_nb is None:
        out_c, rw_out = _rwkv_steps(p3, _pack_rwkv_cols(rw_shift0), rw_in[0], rw_in[1], l,
                                    prev_outs[1], *sp['rwkv'], ones)
    else:
        out_c, rw_out = _rwkv(p3, _pack_rwkv_cols(rw_shift0)[:, None, :], rw_in[0], rw_in[1], l,
                              prev_outs[1], *sp['rwkv'], ones, nb=rwkv_nb)
        out_c = out_c.reshape(nseq * seqlen, RWKV_WIDTH)

    y, yb = _outproj_ln(out_a, out_b, out_c, big['w_out'], l, x, *sp['ln1'])
    y, yb = _ffn_down_ln(_ffn_up(yb, big['w_gate'], big['w_up'], l), big['w_down'], l, y, *sp['ln2'])

    tail = lambda col, width: p3[:, seqlen - nbuf:, col:col + width]
    last = lambda col, width: p3[:, seqlen - 1, col:col + width]
    rw_shift1 = jnp.concatenate([last(C_R, 3 * RWKV_WIDTH), last(C_WA, WA_PAD),
                                 last(C_G, RWKV_R_G)], axis=-1)
    small_new = (tail(C_LX, LRU_WIDTH), lru_h1.reshape(nseq, LRU_WIDTH),
                 tail(C_XBC, SSD_CONV_DIM), rw_shift1)
    return y, yb, small_new, (ssd_out, rw_out)


def kernel(x_prompt, x_sample, state_lru_conv, state_lru_h, state_ssd_conv, state_ssd,
           state_rwkv_shift, state_rwkv, w_in, lru_conv_w, lru_conv_b, lru_wa, lru_ba, lru_wx,
           lru_bx, lru_lambda, ssd_conv_w, ssd_conv_b, ssd_dt_bias, ssd_a_log, ssd_d, ssd_norm_g,
           rwkv_mu, rwkv_w0, rwkv_w_up, rwkv_a0, rwkv_a_up, rwkv_g_up, rwkv_k_k, rwkv_k_a,
           rwkv_r_k, rwkv_ln_g, rwkv_ln_b, w_out, ln1_g, ln1_b, w_gate, w_up, w_down, ln2_g, ln2_b):
    params = dict(
        lru_conv_w=lru_conv_w, lru_conv_b=lru_conv_b, lru_wa=lru_wa, lru_ba=lru_ba,
        lru_wx=lru_wx, lru_bx=lru_bx, lru_lambda=lru_lambda, ssd_conv_w=ssd_conv_w,
        ssd_conv_b=ssd_conv_b, ssd_dt_bias=ssd_dt_bias, ssd_a_log=ssd_a_log, ssd_d=ssd_d,
        ssd_norm_g=ssd_norm_g, rwkv_mu=rwkv_mu, rwkv_w0=rwkv_w0, rwkv_w_up=rwkv_w_up,
        rwkv_a0=rwkv_a0, rwkv_a_up=rwkv_a_up, rwkv_g_up=rwkv_g_up, rwkv_k_k=rwkv_k_k,
        rwkv_k_a=rwkv_k_a, rwkv_r_k=rwkv_r_k.reshape(DEPTH, RWKV_WIDTH), rwkv_ln_g=rwkv_ln_g,
        rwkv_ln_b=rwkv_ln_b, ln1_g=ln1_g, ln1_b=ln1_b, ln2_g=ln2_g, ln2_b=ln2_b)
    big = dict(w_main=jnp.swapaxes(w_in, 1, 2),
               w_tail=jnp.swapaxes(_pack_tail_cols(w_in), 1, 2),
               w_out=w_out.astype(BF16),
               w_gate=w_gate, w_up=w_up, w_down=w_down.astype(BF16))
    bp, lp_len, _ = x_prompt.shape
    bs, ls_len, _ = x_sample.shape
    consts = _constants()
    nbuf = CONV_WIDTH - 1
    zero_small = (jnp.zeros((bp, nbuf, LRU_WIDTH), F32), jnp.zeros((bp, LRU_WIDTH), F32),
                  jnp.zeros((bp, nbuf, SSD_CONV_DIM), F32), jnp.zeros((bp, RWKV_SHIFT), F32))
    zero_ssd = jnp.zeros((1, bp, SSD_HEADS, HEAD_DIM, SSD_STATE), F32)
    zero_rw = jnp.zeros((1, bp, RWKV_HEADS, HEAD_DIM, HEAD_DIM), F32)
    yp = x_prompt.reshape(bp * lp_len, D_MODEL)
    ys = x_sample.reshape(bs * ls_len, D_MODEL)
    ypb = yp.astype(BF16)
    ysb = ys.astype(BF16)
    state_rwkv_t = jnp.transpose(state_rwkv, (0, 2, 3, 4, 1))
    new_p = [[] for _ in range(4)]
    new_s = [[] for _ in range(4)]
    outs_p = (None, None)
    outs_s = (None, None)
    for l in range(DEPTH):
        sp = _small_params(l, params)
        yp, ypb, small_p, outs_p = _layer(yp, ypb, l, zero_small, (zero_ssd, 0), (zero_rw, 0), outs_p, sp,
                                     big, consts, nseq=bp, seqlen=lp_len, lru_nblk=bp,
                                     rwkv_nb=RWKV_PROMPT_NB)
        ys, ysb, small_s, outs_s = _layer(
            ys, ysb, l, (state_lru_conv[l], state_lru_h[l], state_ssd_conv[l], state_rwkv_shift[l]),
            (state_ssd, l), (state_rwkv_t, l), outs_s, sp, big, consts,
            nseq=bs, seqlen=ls_len, lru_nblk=1, rwkv_nb=None)
        for i in range(4):
            new_p[i].append(small_p[i])
            new_s[i].append(small_s[i])
    p_lru_conv, p_lru_h, p_ssd_conv, p_rw_shift = [jnp.stack(v) for v in new_p]
    s_lru_conv, s_lru_h, s_ssd_conv, s_rw_shift = [jnp.stack(v) for v in new_s]
    return (yp.reshape(bp, lp_len, D_MODEL), ys.reshape(bs, ls_len, D_MODEL),
            p_lru_conv, p_lru_h, p_ssd_conv, outs_p[0], p_rw_shift, outs_p[1],
            s_lru_conv, s_lru_h, s_ssd_conv, outs_s[0], s_rw_shift,
            jnp.transpose(outs_s[1], (0, 4, 1, 2, 3)))
```

```python
import functools
import math

import jax
import jax.numpy as jnp
from jax import lax
from jax.experimental import pallas as pl
from jax.experimental.pallas import tpu as pltpu

F32 = jnp.float32
BF16 = jnp.bfloat16

D_MODEL = 2048
DEPTH = 2
D_MIX = 2 * D_MODEL
HEAD_DIM = 64
CONV_WIDTH = 4
LRU_WIDTH = D_MIX // 4
LRU_HEADS = 8
LRU_BLOCK = LRU_WIDTH // LRU_HEADS
LRU_C = 8.0
SSD_WIDTH = D_MIX // 2
SSD_HEADS = SSD_WIDTH // HEAD_DIM
SSD_GROUPS = 8
SSD_STATE = 128
SSD_CHUNK = 128
SSD_CONV_DIM = SSD_WIDTH + 2 * SSD_GROUPS * SSD_STATE
SSD_GROUP_WIDTH = SSD_WIDTH // SSD_GROUPS
SSD_HEADS_PER_GROUP = SSD_HEADS // SSD_GROUPS
RWKV_WIDTH = D_MIX - LRU_WIDTH - SSD_WIDTH
RWKV_HEADS = RWKV_WIDTH // HEAD_DIM
RWKV_R_W = max(32, int(round(1.8 * RWKV_WIDTH ** 0.5 / 32)) * 32)
RWKV_R_A = max(32, int(round(1.8 * RWKV_WIDTH ** 0.5 / 32)) * 32)
RWKV_R_G = max(32, int(round(0.6 * RWKV_WIDTH ** 0.8 / 32)) * 32)
RWKV_SHIFT = 3 * RWKV_WIDTH + RWKV_R_W + RWKV_R_A + RWKV_R_G
RWKV_GN_EPS = 64e-5
RWKV_CHUNK = 64
RWKV_PROMPT_NB = 4
D_FF = -(-(8 * D_MODEL) // (3 * 256)) * 256
ALPHA = (2 * DEPTH) ** 0.25
LN_EPS = 1e-5
LN_ROW_CHUNK = 128

LANES = 128
SUBLANES = 8
VMEM_LIMIT = 56 * 1024 * 1024

C_GATE = 0
C_LX = C_GATE + LRU_WIDTH
C_Z = C_LX + LRU_WIDTH
C_XBC = C_Z + SSD_WIDTH
C_R = C_XBC + SSD_CONV_DIM
C_K = C_R + RWKV_WIDTH
C_V = C_K + RWKV_WIDTH
C_G = C_V + RWKV_WIDTH
G_PAD = 2 * LANES
C_WA = C_G + G_PAD
WA_PAD = LANES
C_DT = C_WA + WA_PAD
DT_PAD = LANES
N_PROJ = C_DT + DT_PAD
assert RWKV_R_G <= G_PAD and RWKV_R_W + RWKV_R_A == WA_PAD and SSD_HEADS <= DT_PAD
RW_PACK = 3 * RWKV_WIDTH + G_PAD + WA_PAD

_O_DT = 2 * LRU_WIDTH + SSD_WIDTH + SSD_CONV_DIM
_O_RW = _O_DT + SSD_HEADS
_O_XW = 3 * RWKV_WIDTH
_O_XG = _O_XW + RWKV_R_W + RWKV_R_A


def _cparams(*sem):
    return pltpu.CompilerParams(dimension_semantics=sem, vmem_limit_bytes=VMEM_LIMIT)


_NN = (((1,), (0,)), ((), ()))
_NT = (((1,), (1,)), ((), ()))
_TN = (((0,), (0,)), ((), ()))


def _dg(a, b, dims):
    return lax.dot_general(a, b, dims, preferred_element_type=F32)


def _dot1(a, b, dims=_NN):
    return _dg(a.astype(BF16), b.astype(BF16), dims)


def _split2(x):
    hi = x.astype(BF16)
    lo = (x - hi.astype(F32)).astype(BF16)
    return hi, lo


def _split3(x):
    hi = x.astype(BF16)
    r1 = x - hi.astype(F32)
    mid = r1.astype(BF16)
    lo = (r1 - mid.astype(F32)).astype(BF16)
    return hi, mid, lo


def _dot_shared_rhs(parts, w_exact, npieces):
    pieces = []
    for p in parts:
        rest = p
        for _ in range(npieces):
            piece = rest.astype(BF16).astype(F32)
            pieces.append(piece)
            rest = rest - piece
    prod = _dg(jnp.concatenate(pieces, axis=0).astype(BF16), w_exact, _NN)
    outs = []
    off = 0
    for p in parts:
        r = p.shape[0]
        acc = prod[off:off + r]
        for i in range(1, npieces):
            acc = acc + prod[off + i * r:off + (i + 1) * r]
        outs.append(acc)
        off += npieces * r
    return outs


def _dotx_left(w_exact, a, npieces=3):
    if npieces == 2:
        hi, lo = _split2(a)
        return _dg(w_exact, hi, _NN) + _dg(w_exact, lo, _NN)
    hi, mid, lo = _split3(a)
    return _dg(w_exact, hi, _NN) + (_dg(w_exact, mid, _NN) + _dg(w_exact, lo, _NN))


def _iota2(shape, dim):
    return lax.broadcasted_iota(jnp.int32, shape, dim)


def _softplus(x):
    return jnp.maximum(x, 0.0) + jnp.log1p(jnp.exp(-jnp.abs(x)))


def _sigmoid(x):
    return 1.0 / (1.0 + jnp.exp(-x))


def _silu(x):
    return x * _sigmoid(x)


def _gelu_tanh(x):
    c = math.sqrt(2.0 / math.pi)
    return 0.5 * x * (1.0 + jnp.tanh(c * (x + 0.044715 * (x * x * x))))


def _layer_norm(y, g, b):
    mu = jnp.mean(y, axis=-1, keepdims=True)
    d = y - mu
    var = jnp.mean(d * d, axis=-1, keepdims=True)
    return d * lax.rsqrt(var + LN_EPS) * g + b


DENSE_TM = 2048
DENSE_TN = 512
N_MAIN = C_R
N_TAIL = N_PROJ - N_MAIN
assert N_MAIN == _O_DT and N_MAIN % DENSE_TN == 0 and N_TAIL % DENSE_TN == 0 and D_FF % DENSE_TN == 0


def _proj_kernel(x_ref, wm_ref, wt_ref, o_ref):
    j = pl.program_id(1)

    @pl.when(j < N_MAIN // DENSE_TN)
    def _():
        o_ref[...] = _dg(x_ref[...], wm_ref[0].astype(BF16), _NT)

    @pl.when(j >= N_MAIN // DENSE_TN)
    def _():
        o_ref[...] = _dg(x_ref[...], wt_ref[0].astype(BF16), _NT)


def _proj(x, w_main_t, w_tail_t, l):
    m = x.shape[0]
    tm = min(m, DENSE_TM)
    tn = DENSE_TN
    nmain = N_MAIN // tn
    return pl.pallas_call(
        _proj_kernel,
        grid=(m // tm, N_PROJ // tn),
        in_specs=[pl.BlockSpec((tm, D_MODEL), lambda i, j: (i, 0)),
                  pl.BlockSpec((1, tn, D_MODEL), lambda i, j: (l, jnp.minimum(j, nmain - 1), 0)),
                  pl.BlockSpec((1, tn, D_MODEL), lambda i, j: (l, jnp.maximum(j - nmain, 0), 0))],
        out_specs=pl.BlockSpec((tm, tn), lambda i, j: (i, j)),
        out_shape=jax.ShapeDtypeStruct((m, N_PROJ), F32),
        compiler_params=_cparams("arbitrary", "arbitrary"),
        name="proj",
    )(x, w_main_t, w_tail_t)


OUT_KSTEPS = 2


def _outproj_kernel(a_ref, b_ref, c_ref, w_ref, x_ref, g_ref, beta_ref, o_ref, ob_ref):
    k = pl.program_id(1)
    half = LRU_WIDTH

    @pl.when(k == 0)
    def _():
        o_ref[...] = (jnp.dot(a_ref[...], w_ref[0, :half], preferred_element_type=F32)
                      + jnp.dot(b_ref[...], w_ref[0, half:], preferred_element_type=F32))

    @pl.when(k == OUT_KSTEPS - 1)
    def _():
        o_ref[...] += (jnp.dot(b_ref[...], w_ref[0, :half], preferred_element_type=F32)
                       + jnp.dot(c_ref[...], w_ref[0, half:], preferred_element_type=F32))
        _residual_ln_rows(o_ref, ob_ref, x_ref, g_ref, beta_ref)


def _residual_ln_rows(o_ref, ob_ref, x_ref, g_ref, beta_ref):
    chunk = min(LN_ROW_CHUNK, o_ref.shape[0])

    def ln_rows(r, carry):
        rows = pl.ds(pl.multiple_of(r * chunk, chunk), chunk)
        y = _layer_norm(ALPHA * x_ref[rows, :] + o_ref[rows, :], g_ref[...], beta_ref[...])
        o_ref[rows, :] = y
        ob_ref[rows, :] = y.astype(BF16)
        return carry

    lax.fori_loop(0, o_ref.shape[0] // chunk, ln_rows, 0)


def _outproj_ln(out_a, out_b, out_c, w_out, l, x, g, beta):
    m = x.shape[0]
    tm = min(m, 512)
    tk = LRU_WIDTH
    assert SSD_WIDTH == 2 * tk and RWKV_WIDTH == tk and D_MIX == OUT_KSTEPS * 2 * tk
    return pl.pallas_call(
        _outproj_kernel,
        grid=(m // tm, OUT_KSTEPS),
        in_specs=[pl.BlockSpec((tm, tk), lambda i, k: (i, 0)),
                  pl.BlockSpec((tm, tk), lambda i, k: (i, k)),
                  pl.BlockSpec((tm, tk), lambda i, k: (i, 0)),
                  pl.BlockSpec((1, 2 * tk, D_MODEL), lambda i, k: (l, k, 0)),
                  pl.BlockSpec((tm, D_MODEL), lambda i, k: (i, 0)),
                  pl.BlockSpec((1, D_MODEL), lambda i, k: (0, 0)),
                  pl.BlockSpec((1, D_MODEL), lambda i, k: (0, 0))],
        out_specs=[pl.BlockSpec((tm, D_MODEL), lambda i, k: (i, 0))] * 2,
        out_shape=[jax.ShapeDtypeStruct((m, D_MODEL), F32), jax.ShapeDtypeStruct((m, D_MODEL), BF16)],
        compiler_params=_cparams("arbitrary", "arbitrary"),
        name="outproj_ln",
    )(out_a, out_b, out_c, w_out, x, g, beta)


def _ffn_up_kernel(x_ref, wg_ref, wu_ref, o_ref):
    xb = x_ref[...]
    gate = jnp.dot(xb, wg_ref[0].astype(BF16), preferred_element_type=F32)
    up = jnp.dot(xb, wu_ref[0].astype(BF16), preferred_element_type=F32)
    o_ref[...] = (_silu(gate) * up).astype(BF16)


def _ffn_up(x, wg, wu, l):
    m = x.shape[0]
    tm = min(m, DENSE_TM)
    tn = DENSE_TN
    return pl.pallas_call(
        _ffn_up_kernel,
        grid=(m // tm, D_FF // tn),
        in_specs=[pl.BlockSpec((tm, D_MODEL), lambda i, j: (i, 0)),
                  pl.BlockSpec((1, D_MODEL, tn), lambda i, j: (l, 0, j)),
                  pl.BlockSpec((1, D_MODEL, tn), lambda i, j: (l, 0, j))],
        out_specs=pl.BlockSpec((tm, tn), lambda i, j: (i, j)),
        out_shape=jax.ShapeDtypeStruct((m, D_FF), BF16),
        compiler_params=_cparams("arbitrary", "arbitrary"),
        name="ffn_up",
    )(x, wg, wu)


FFN_DOWN_KSTEPS = 2


def _ffn_down_kernel(h_ref, w_ref, x_ref, g_ref, beta_ref, o_ref, ob_ref):
    k = pl.program_id(1)

    @pl.when(k == 0)
    def _():
        o_ref[...] = jnp.dot(h_ref[...], w_ref[0], preferred_element_type=F32)

    @pl.when(k > 0)
    def _():
        o_ref[...] += jnp.dot(h_ref[...], w_ref[0], preferred_element_type=F32)

    @pl.when(k == FFN_DOWN_KSTEPS - 1)
    def _():
        _residual_ln_rows(o_ref, ob_ref, x_ref, g_ref, beta_ref)


def _ffn_down_ln(h, w_down, l, x, g, beta):
    m = x.shape[0]
    tm = min(m, 512)
    tk = D_FF // FFN_DOWN_KSTEPS
    assert tk * FFN_DOWN_KSTEPS == D_FF and tk % LANES == 0 and tm % min(LN_ROW_CHUNK, tm) == 0
    return pl.pallas_call(
        _ffn_down_kernel,
        grid=(m // tm, FFN_DOWN_KSTEPS),
        in_specs=[pl.BlockSpec((tm, tk), lambda i, k: (i, k)),
                  pl.BlockSpec((1, tk, D_MODEL), lambda i, k: (l, k, 0)),
                  pl.BlockSpec((tm, D_MODEL), lambda i, k: (i, 0)),
                  pl.BlockSpec((1, D_MODEL), lambda i, k: (0, 0)),
                  pl.BlockSpec((1, D_MODEL), lambda i, k: (0, 0))],
        out_specs=[pl.BlockSpec((tm, D_MODEL), lambda i, k: (i, 0))] * 2,
        out_shape=[jax.ShapeDtypeStruct((m, D_MODEL), F32), jax.ShapeDtypeStruct((m, D_MODEL), BF16)],
        compiler_params=_cparams("arbitrary", "arbitrary"),
        name="ffn_down_ln",
    )(h, w_down, x, g, beta)


def _conv_taps(u, first_rows, cw, cb, t_in_seq):
    out = cb + cw[CONV_WIDTH - 1:CONV_WIDTH, :] * u
    for j in range(1, CONV_WIDTH):
        shifted = jnp.where(t_in_seq < j, first_rows(j), pltpu.roll(u, j, 0))
        out = out + cw[CONV_WIDTH - 1 - j:CONV_WIDTH - j, :] * shifted
    return out


def _lru_kernel(gate_ref, lx_ref, buf_ref, h0_ref, cw_ref, cb_ref, wa_ref, ba_ref, wx_ref,
                bx_ref, lam_ref, out_ref, h1_ref, a_scr, b_scr, hin_scr, *, nseq, seqlen):
    rows = nseq * seqlen
    u = lx_ref[...]
    t_in_seq = _iota2((rows, LANES), 0) % seqlen
    if nseq == 1:
        buf = jnp.concatenate([buf_ref[0], jnp.zeros((rows - SUBLANES, LANES), F32)], axis=0) \
            if rows > SUBLANES else buf_ref[0]
    else:
        buf = buf_ref[0]
    nbuf = CONV_WIDTH - 1
    xc = _conv_taps(u, lambda j: pltpu.roll(buf, (j - nbuf) % rows, 0), cw_ref[...], cb_ref[...],
                    t_in_seq)

    r = _sigmoid(_dot1(xc, wa_ref[0]) + ba_ref[...])
    i = _sigmoid(_dot1(xc, wx_ref[0]) + bx_ref[...])
    log_a = (-LRU_C) * r * _softplus(-lam_ref[...])
    a = jnp.exp(log_a)
    b = jnp.sqrt(jnp.tanh(-log_a) * (a * a + 1.0)) * (i * xc)

    t8 = _iota2((rows, LANES), 0) % SUBLANES
    for s in (1, 2, 4):
        m = t8 >= s
        a_sh = pltpu.roll(a, s, 0)
        b_sh = pltpu.roll(b, s, 0)
        b = jnp.where(m, a * b_sh + b, b)
        a = jnp.where(m, a * a_sh, a)

    if seqlen == SUBLANES:
        h0 = h0_ref[0]
        hin = jnp.broadcast_to(h0[:, None, :], (nseq, SUBLANES, LANES)).reshape(rows, LANES)
        h = a * hin + b
        out_ref[...] = (h * _gelu_tanh(gate_ref[...])).astype(out_ref.dtype)
        a_scr[...] = h
        h1_ref[0] = a_scr[pl.ds(SUBLANES - 1, nseq, stride=SUBLANES), :]
    else:
        assert nseq == 1
        a_scr[...] = a
        b_scr[...] = b

        def carry_step(g, carry):
            base = pl.multiple_of(g * SUBLANES, SUBLANES)
            hin_scr[pl.ds(base, SUBLANES), :] = jnp.broadcast_to(carry, (SUBLANES, LANES))
            a7 = a_scr[pl.ds(base + SUBLANES - 1, 1), :]
            b7 = b_scr[pl.ds(base + SUBLANES - 1, 1), :]
            return a7 * carry + b7

        last = lax.fori_loop(0, rows // SUBLANES, carry_step, h0_ref[0])
        h = a_scr[...] * hin_scr[...] + b_scr[...]
        out_ref[...] = (h * _gelu_tanh(gate_ref[...])).astype(out_ref.dtype)
        h1_ref[0] = last


def _lru(proj, buf, h0, cw, cb, wa, ba, wx, bx, lam, *, nblk, nseq, seqlen):
    rows = nseq * seqlen
    gate_blk = C_GATE // LRU_BLOCK
    lx_blk = C_LX // LRU_BLOCK
    row = lambda s, h: (0, h)
    return pl.pallas_call(
        functools.partial(_lru_kernel, nseq=nseq, seqlen=seqlen),
        grid=(nblk, LRU_HEADS),
        in_specs=[pl.BlockSpec((rows, LRU_BLOCK), lambda s, h: (s, gate_blk + h)),
                  pl.BlockSpec((rows, LRU_BLOCK), lambda s, h: (s, lx_blk + h)),
                  pl.BlockSpec((1, nseq * SUBLANES, LRU_BLOCK), lambda s, h: (s, 0, h)),
                  pl.BlockSpec((1, nseq, LRU_BLOCK), lambda s, h: (s, 0, h)),
                  pl.BlockSpec((CONV_WIDTH, LRU_BLOCK), row),
                  pl.BlockSpec((1, LRU_BLOCK), row),
                  pl.BlockSpec((1, LRU_BLOCK, LRU_BLOCK), lambda s, h: (h, 0, 0)),
                  pl.BlockSpec((1, LRU_BLOCK), row),
                  pl.BlockSpec((1, LRU_BLOCK, LRU_BLOCK), lambda s, h: (h, 0, 0)),
                  pl.BlockSpec((1, LRU_BLOCK), row),
                  pl.BlockSpec((1, LRU_BLOCK), row)],
        out_specs=[pl.BlockSpec((rows, LRU_BLOCK), lambda s, h: (s, h)),
                   pl.BlockSpec((1, nseq, LRU_BLOCK), lambda s, h: (s, 0, h))],
        out_shape=[jax.ShapeDtypeStruct((nblk * rows, LRU_WIDTH), BF16),
                   jax.ShapeDtypeStruct((nblk, nseq, LRU_WIDTH), F32)],
        scratch_shapes=[pltpu.VMEM((rows, LANES), F32)] * 3,
        compiler_params=_cparams("arbitrary", "arbitrary"),
        name="lru",
    )(proj, proj, buf, h0, cw, cb, wa, ba, wx, bx, lam)


def _stacked_state_call(kernel_fn, grid, in_specs, operands, row_spec, row_shape, state_dims,
                        l_out, prev_out, scratch_shapes, name, nb=1):
    nseq = grid[0] * nb
    zeros = (0,) * len(state_dims)
    state_shape = jax.ShapeDtypeStruct((DEPTH, nseq) + tuple(state_dims), F32)
    aliases = {}
    if prev_out is None:
        assert l_out == 0
        state_spec = pl.BlockSpec((DEPTH, nb) + tuple(state_dims), lambda b, c: (0, b) + zeros)
    else:
        state_spec = pl.BlockSpec((1, nb) + tuple(state_dims), lambda b, c: (l_out, b) + zeros)
        in_specs = in_specs + [pl.BlockSpec(memory_space=pl.ANY)]
        operands = operands + [prev_out]
        aliases = {len(operands) - 1: 1}
        kernel_fn = functools.partial(_drop_alias_ref, kernel_fn, len(operands) - 1)
    return pl.pallas_call(
        kernel_fn, grid=grid, in_specs=in_specs, out_specs=[row_spec, state_spec],
        out_shape=[row_shape, state_shape], scratch_shapes=scratch_shapes,
        input_output_aliases=aliases, compiler_params=_cparams("arbitrary", "arbitrary"),
        name=name)(*operands)


def _drop_alias_ref(kernel_fn, pos, *refs):
    return kernel_fn(*refs[:pos], *refs[pos + 1:])


def _zero_other_slabs(state_ref):
    if state_ref.shape[0] > 1:
        state_ref[1:] = jnp.zeros((state_ref.shape[0] - 1,) + state_ref.shape[1:], state_ref.dtype)


def _transpose_rows_to_lanes(x, t):
    if t < LANES:
        x = jnp.concatenate([x, jnp.zeros((LANES - t, LANES), F32)], axis=0)
    return x.T[:, :t]


def _ssd_kernel(z_ref, xbc_ref, dt_ref, buf_ref, s0_ref, cw_ref, cb_ref, dtb_ref, alog_ref,
                dch_ref, ng_ref, ehp_ref, out_ref, s1_ref, tail_ref, pad_ref, y_scr, *, T):
    c = pl.program_id(1)
    n_state = SSD_STATE
    gw = SSD_GROUP_WIDTH
    hpg = SSD_HEADS_PER_GROUP

    @pl.when(c == 0)
    def _():
        tail_ref[...] = buf_ref[0]
        s1_ref[0:1] = s0_ref[...]
        _zero_other_slabs(s1_ref)

    u = xbc_ref[...]
    pad_ref[0:SUBLANES, :] = tail_ref[...]
    pad_ref[SUBLANES:SUBLANES + T, :] = u
    cw = cw_ref[...]
    xc = cb_ref[...] + cw[CONV_WIDTH - 1:CONV_WIDTH, :] * u
    for j in range(1, CONV_WIDTH):
        xc = xc + cw[CONV_WIDTH - 1 - j:CONV_WIDTH - j, :] * pad_ref[SUBLANES - j:SUBLANES - j + T, :]
    tail_ref[...] = pad_ref[T:T + SUBLANES, :]

    xa = _silu(xc)
    xs = xa[:, :SSD_WIDTH]
    bm = xa[:, SSD_WIDTH:SSD_WIDTH + SSD_GROUPS * n_state]
    cm = xa[:, SSD_WIDTH + SSD_GROUPS * n_state:]
    dt = _softplus(dt_ref[...] + dtb_ref[...])
    da = dt * (-jnp.exp(alog_ref[...]))
    ii = _iota2((T, T), 0)
    jj = _iota2((T, T), 1)
    causal = ii >= jj
    cum = _dotx_left(causal.astype(BF16), da)
    cum_t = _transpose_rows_to_lanes(cum, T)
    ehp = ehp_ref[...]
    cumx, dtx = _dot_shared_rhs([cum, dt], ehp, 2)
    xdt = xs * dtx
    ecum = jnp.exp(cumx)
    xdtd = xdt * jnp.exp(cumx[T - 1:T, :] - cumx)

    groups = range(SSD_GROUPS)
    heads = range(SSD_HEADS)
    cg = [cm[:, g * n_state:(g + 1) * n_state] for g in groups]
    bg = [bm[:, g * n_state:(g + 1) * n_state] for g in groups]
    sg = [s1_ref[0, 0, g * hpg:(g + 1) * hpg].reshape(gw, n_state) for g in groups]
    cb = [_dot1(cg[g], bg[g], _NT) for g in groups]
    y_off = [_dot1(cg[g], sg[g], _NT) for g in groups]
    st = [_dot1(xdtd[:, g * gw:(g + 1) * gw], bg[g], _TN) for g in groups]
    lm = [jnp.where(causal, jnp.exp(cum[:, h:h + 1] - cum_t[h:h + 1, :]), 0.0) for h in heads]
    y_diag = [_dot1(cb[h // hpg] * lm[h], xdt[:, h * HEAD_DIM:(h + 1) * HEAD_DIM]) for h in heads]
    for g in groups:
        y_scr[:, g * gw:(g + 1) * gw] = (jnp.concatenate(y_diag[g * hpg:(g + 1) * hpg], axis=1)
                                         + y_off[g] * ecum[:, g * gw:(g + 1) * gw])
        decay = jnp.concatenate(
            [jnp.broadcast_to(jnp.exp(cum_t[h:h + 1, T - 1:T]), (HEAD_DIM, n_state))
             for h in range(g * hpg, (g + 1) * hpg)], axis=0)
        s1_ref[0, 0, g * hpg:(g + 1) * hpg] = (sg[g] * decay + st[g]).reshape(hpg, HEAD_DIM, n_state)

    y = (y_scr[...] + dch_ref[...] * xs) * _silu(z_ref[...])
    outs = []
    for g in groups:
        yg = y[:, g * gw:(g + 1) * gw]
        ms = jnp.mean(yg * yg, axis=-1, keepdims=True)
        outs.append(yg * lax.rsqrt(ms + 1e-5))
    out_ref[...] = (jnp.concatenate(outs, axis=1) * ng_ref[...]).astype(out_ref.dtype)


def _ssd(proj, buf, s0, l_in, l_out, prev_out, cw, cb, dtb, alog, dch, ng, ehp, *, nseq, seqlen):
    T = min(SSD_CHUNK, seqlen)
    nc = seqlen // T
    const = lambda b, c: (0, 0)
    sblk = (1, 1, SSD_HEADS, HEAD_DIM, SSD_STATE)
    in_specs = [pl.BlockSpec((T, SSD_WIDTH), lambda b, c: (b * nc + c, C_Z // SSD_WIDTH)),
                pl.BlockSpec((T, SSD_CONV_DIM), lambda b, c: (b * nc + c, C_XBC // SSD_CONV_DIM)),
                pl.BlockSpec((T, DT_PAD), lambda b, c: (b * nc + c, C_DT // DT_PAD)),
                pl.BlockSpec((1, SUBLANES, SSD_CONV_DIM), lambda b, c: (b, 0, 0)),
                pl.BlockSpec(sblk, lambda b, c: (l_in, b, 0, 0, 0)),
                pl.BlockSpec((CONV_WIDTH, SSD_CONV_DIM), const),
                pl.BlockSpec((1, SSD_CONV_DIM), const),
                pl.BlockSpec((1, DT_PAD), const),
                pl.BlockSpec((1, DT_PAD), const),
                pl.BlockSpec((1, SSD_WIDTH), const),
                pl.BlockSpec((1, SSD_WIDTH), const),
                pl.BlockSpec((DT_PAD, SSD_WIDTH), const)]
    return _stacked_state_call(
        functools.partial(_ssd_kernel, T=T), (nseq, nc), in_specs,
        [proj, proj, proj, buf, s0, cw, cb, dtb, alog, dch, ng, ehp],
        pl.BlockSpec((T, SSD_WIDTH), lambda b, c: (b * nc + c, 0)),
        jax.ShapeDtypeStruct((nseq * seqlen, SSD_WIDTH), BF16),
        sblk[2:], l_out, prev_out,
        [pltpu.VMEM((SUBLANES, SSD_CONV_DIM), F32),
         pltpu.VMEM((T + SUBLANES, SSD_CONV_DIM), F32),
         pltpu.VMEM((T, SSD_WIDTH), F32)],
        "ssd")


SSD_SHORT_NB = 4


def _ssd_short_kernel(z_ref, xbc_ref, dt_ref, buf_ref, s0_ref, cw_ref, cb_ref, dtb_ref, alog_ref,
                      dch_ref, ng_ref, ehp_ref, out_ref, s1_ref, y_scr, *, nb, T):
    n_state = SSD_STATE
    gw = SSD_GROUP_WIDTH
    hpg = SSD_HEADS_PER_GROUP
    R = nb * T

    def per_seq_rows(x):
        return jnp.broadcast_to(x[:, None, :], (nb, T, x.shape[-1])).reshape(R, x.shape[-1])

    def last_rows(x):
        return jnp.concatenate([x[(b + 1) * T - 1:(b + 1) * T, :] for b in range(nb)], axis=0)

    u = xbc_ref[...].reshape(R, SSD_CONV_DIM)
    buf = buf_ref[...].reshape(nb * SUBLANES, SSD_CONV_DIM)
    assert T == SUBLANES
    t_in_seq = _iota2((R, SSD_CONV_DIM), 0) % T
    nbuf = CONV_WIDTH - 1
    xc = _conv_taps(u, lambda j: pltpu.roll(buf, (j - nbuf) % R, 0), cw_ref[...], cb_ref[...],
                    t_in_seq)

    xa = _silu(xc)
    xs = xa[:, :SSD_WIDTH]
    bm = xa[:, SSD_WIDTH:SSD_WIDTH + SSD_GROUPS * n_state]
    cm = xa[:, SSD_WIDTH + SSD_GROUPS * n_state:]
    dt = _softplus(dt_ref[...].reshape(R, DT_PAD) + dtb_ref[...])
    da = dt * (-jnp.exp(alog_ref[...]))
    ri = _iota2((R, R), 0)
    rj = _iota2((R, R), 1)
    same_seq_causal = jnp.logical_and(ri >= rj, ri // T == rj // T)
    cum = _dotx_left(same_seq_causal.astype(BF16), da)
    cum_t = _transpose_rows_to_lanes(cum, R)
    cumx, dtx = _dot_shared_rhs([cum, dt], ehp_ref[...], 2)
    xdt = xs * dtx
    ecum = jnp.exp(cumx)
    xdtd = xdt * jnp.exp(per_seq_rows(last_rows(cumx)) - cumx)

    causal = _iota2((T, T), 0) >= _iota2((T, T), 1)
    seqs = range(nb)
    groups = range(SSD_GROUPS)
    heads = range(SSD_HEADS)
    rows = [slice(b * T, (b + 1) * T) for b in seqs]
    cg = [[cm[rows[b], g * n_state:(g + 1) * n_state] for g in groups] for b in seqs]
    bg = [[bm[rows[b], g * n_state:(g + 1) * n_state] for g in groups] for b in seqs]
    sg = [[s0_ref[0, b, g * hpg:(g + 1) * hpg].reshape(gw, n_state) for g in groups] for b in seqs]
    cb = [[_dot1(cg[b][g], bg[b][g], _NT) for g in groups] for b in seqs]
    y_off = [[_dot1(cg[b][g], sg[b][g], _NT) for g in groups] for b in seqs]
    st = [[_dot1(xdtd[rows[b], g * gw:(g + 1) * gw], bg[b][g], _TN) for g in groups] for b in seqs]
    lm = [[jnp.where(causal, jnp.exp(cum[rows[b], h:h + 1] - cum_t[h:h + 1, rows[b]]), 0.0)
           for h in heads] for b in seqs]
    y_diag = [[_dot1(cb[b][h // hpg] * lm[b][h], xdt[rows[b], h * HEAD_DIM:(h + 1) * HEAD_DIM])
               for h in heads] for b in seqs]
    for b in seqs:
        for g in groups:
            y_scr[rows[b], g * gw:(g + 1) * gw] = (
                jnp.concatenate(y_diag[b][g * hpg:(g + 1) * hpg], axis=1)
                + y_off[b][g] * ecum[rows[b], g * gw:(g + 1) * gw])
            last = (b + 1) * T - 1
            decay = jnp.concatenate(
                [jnp.broadcast_to(jnp.exp(cum_t[h:h + 1, last:last + 1]), (HEAD_DIM, n_state))
                 for h in range(g * hpg, (g + 1) * hpg)], axis=0)
            s1_ref[0, b, g * hpg:(g + 1) * hpg] = (sg[b][g] * decay + st[b][g]).reshape(
                hpg, HEAD_DIM, n_state)
    _zero_other_slabs(s1_ref)

    y = (y_scr[...] + dch_ref[...] * xs) * _silu(z_ref[...].reshape(R, SSD_WIDTH))
    outs = []
    for g in groups:
        yg = y[:, g * gw:(g + 1) * gw]
        ms = jnp.mean(yg * yg, axis=-1, keepdims=True)
        outs.append(yg * lax.rsqrt(ms + 1e-5))
    res = jnp.concatenate(outs, axis=1) * ng_ref[...]
    out_ref[...] = res.astype(out_ref.dtype).reshape(out_ref.shape)


def _ssd_short(p3, buf, s0, l_in, l_out, prev_out, cw, cb, dtb, alog, dch, ng, ehp):
    nseq, T, _ = p3.shape
    nb = SSD_SHORT_NB
    const = lambda b, c: (0, 0)
    sdims = (SSD_HEADS, HEAD_DIM, SSD_STATE)
    blk = lambda width, col: pl.BlockSpec((nb, T, width), lambda b, c: (b, 0, col // width))
    in_specs = [blk(SSD_WIDTH, C_Z), blk(SSD_CONV_DIM, C_XBC), blk(DT_PAD, C_DT),
                pl.BlockSpec((nb, SUBLANES, SSD_CONV_DIM), lambda b, c: (b, 0, 0)),
                pl.BlockSpec((1, nb) + sdims, lambda b, c: (l_in, b, 0, 0, 0)),
                pl.BlockSpec((CONV_WIDTH, SSD_CONV_DIM), const),
                pl.BlockSpec((1, SSD_CONV_DIM), const),
                pl.BlockSpec((1, DT_PAD), const),
                pl.BlockSpec((1, DT_PAD), const),
                pl.BlockSpec((1, SSD_WIDTH), const),
                pl.BlockSpec((1, SSD_WIDTH), const),
                pl.BlockSpec((DT_PAD, SSD_WIDTH), const)]
    out, stack = _stacked_state_call(
        functools.partial(_ssd_short_kernel, nb=nb, T=T), (nseq // nb, 1), in_specs,
        [p3, p3, p3, buf, s0, cw, cb, dtb, alog, dch, ng, ehp],
        pl.BlockSpec((nb, T, SSD_WIDTH), lambda b, c: (b, 0, 0)),
        jax.ShapeDtypeStruct((nseq, T, SSD_WIDTH), BF16),
        sdims, l_out, prev_out,
        [pltpu.VMEM((nb * T, SSD_WIDTH), F32)],
        "ssd_short", nb=nb)
    return out.reshape(nseq * T, SSD_WIDTH), stack


def _unit_lower_inverse(a_list, ii, jj, T):
    n = len(a_list)
    pair = (ii >> 1) == (jj >> 1)
    inv = [jnp.where(ii == jj, 1.0, 0.0) + jnp.where(pair, a_list[h], 0.0) for h in range(n)]
    shift = 1
    while (2 << shift) <= T:
        band = jnp.logical_and((ii >> (shift + 1)) == (jj >> (shift + 1)),
                               (ii >> shift) != (jj >> shift))
        x = [_dot1(jnp.where(band, a_list[h], 0.0), inv[h]) for h in range(n)]
        inv = [inv[h] + _dot1(inv[h], x[h]) for h in range(n)]
        shift += 1
    return inv


def _rwkv_kernel(r_ref, k_ref, v_ref, g_ref, wa_ref, sh_ref, s0_ref, mu_ref, w0_ref, wup_ref,
                 a0_ref, aup_ref, gup_ref, kkw_ref, kaw_ref, rkw_ref, lng_ref, lnb_ref, ones_ref,
                 out_ref, s1_ref, prev_scr, s_scr, al_scr, be_scr, kt_scr, rt_scr, bs_scr,
                 ks_scr, v_scr, gam_scr, o_scr, *, nb, T, nchunks):
    c = pl.program_id(1)
    W = RWKV_WIDTH
    D = HEAD_DIM
    R = nb * T
    nh = RWKV_HEADS

    @pl.when(c == 0)
    def _():
        prev_scr[...] = sh_ref[...].reshape(nb, RW_PACK)
        s_scr[...] = s0_ref[0].reshape(nb * nh, D, D)

    def rows_of(ref3):
        return ref3[...].reshape(R, ref3.shape[-1])

    def per_seq_rows(x):
        return jnp.broadcast_to(x[:, None, :], (nb, T, x.shape[-1])).reshape(R, x.shape[-1])

    def last_rows(x):
        return jnp.concatenate([x[(b + 1) * T - 1:(b + 1) * T, :] for b in range(nb)], axis=0)

    def token_shift(p, lo, hi):
        first = _iota2(p.shape, 0) % T == 0
        prev = jnp.where(first, per_seq_rows(prev_scr[:, lo:hi]), pltpu.roll(p, 1, 0))
        prev_scr[:, lo:hi] = last_rows(p)
        return p + (prev - p) * mu_ref[:, lo:hi]

    xr = token_shift(rows_of(r_ref), 0, W)
    xk = token_shift(rows_of(k_ref), W, 2 * W)
    xv = token_shift(rows_of(v_ref), 2 * W, 3 * W)
    xg = token_shift(rows_of(g_ref), 3 * W, 3 * W + G_PAD)
    xwa = token_shift(rows_of(wa_ref), 3 * W + G_PAD, RW_PACK)

    w_lin = w0_ref[...] + _dot1(jnp.tanh(xwa), wup_ref[...])
    a = _sigmoid(a0_ref[...] + _dot1(xwa, aup_ref[...]))
    gate = _dot1(_sigmoid(xg), gup_ref[...])
    lw = -jnp.exp(-_softplus(-w_lin) - 0.5)

    ones = ones_ref[...]

    def head_sums(xs, npieces=3):
        tw = ones.shape[0]
        nt = W // tw
        tiles = [x[:, i * tw:(i + 1) * tw] for x in xs for i in range(nt)]
        sums = _dot_shared_rhs(tiles, ones, npieces)
        return [jnp.concatenate(sums[n * nt:(n + 1) * nt], axis=1) for n in range(len(xs))]

    kk = xk * kkw_ref[...]
    kp = xk * (1.0 + (a - 1.0) * kaw_ref[...])
    kk_sq, rk_sum = head_sums([kk * kk, xr * kp * rkw_ref[...]], 1)
    kk = kk / jnp.maximum(jnp.sqrt(kk_sq), 1e-12)

    ri = _iota2((R, R), 0)
    rj = _iota2((R, R), 1)
    same_seq_causal = jnp.logical_and(ri >= rj, ri // T == rj // T)
    cum = _dotx_left(same_seq_causal.astype(BF16), lw, 3)
    e_neg = jnp.exp(-cum)
    gam = jnp.exp(per_seq_rows(last_rows(cum)))
    be = kk * a * e_neg
    kt = kp * e_neg
    al_scr[...] = -kk * jnp.exp(cum - lw)
    be_scr[...] = be
    kt_scr[...] = kt
    rt_scr[...] = xr * jnp.exp(cum)
    bs_scr[...] = be * gam
    ks_scr[...] = kt * gam
    v_scr[...] = xv
    gam_scr[...] = gam

    ii = _iota2((T, T), 0)
    jj = _iota2((T, T), 1)
    incl = ii >= jj
    strict = ii > jj
    eye_d = _iota2((D, D), 0) == _iota2((D, D), 1)

    units = [(b, h) for b in range(nb) for h in range(nh)]
    nu = range(len(units))

    def per_unit(ref):
        return [ref[b * T:(b + 1) * T, h * D:(h + 1) * D] for b, h in units]

    al, be_u, kt_u, rt = per_unit(al_scr), per_unit(be_scr), per_unit(kt_scr), per_unit(rt_scr)
    bs_u, ks_u, vv, gam_u = per_unit(bs_scr), per_unit(ks_scr), per_unit(v_scr), per_unit(gam_scr)
    gram = [_dot1(jnp.concatenate([al[u], rt[u]], axis=0),
                  jnp.concatenate([be_u[u], kt_u[u]], axis=0), _NT) for u in nu]
    a_ab = [jnp.where(strict, gram[u][:T, :T], 0.0) for u in nu]
    a_ak = [jnp.where(strict, gram[u][:T, T:], 0.0) for u in nu]
    r_b = [jnp.where(incl, gram[u][T:, :T], 0.0) for u in nu]
    r_k = [jnp.where(incl, gram[u][T:, T:], 0.0) for u in nu]
    inv = _unit_lower_inverse(a_ab, ii, jj, T)
    akv = [_dot1(a_ak[u], vv[u]) for u in nu]
    pw = [_dot1(inv[u], jnp.concatenate([al[u], akv[u]], axis=1)) for u in nu]
    qo = [_dot1(r_b[u], pw[u]) for u in nu]
    rkv = [_dot1(r_k[u], vv[u]) for u in nu]
    smat = [s_scr[u] for u in nu]
    qs = [_dot1(rt[u] + qo[u][:, :D], smat[u], _NT) for u in nu]
    gp = [_dot1(bs_u[u], pw[u][:, :D], _TN) for u in nu]
    zt = [_dot1(jnp.concatenate([pw[u][:, D:], vv[u]], axis=0),
                jnp.concatenate([bs_u[u], ks_u[u]], axis=0), _TN) for u in nu]
    gmat = [jnp.where(eye_d, jnp.broadcast_to(gam_u[u][0:1, :], (D, D)), 0.0) + gp[u] for u in nu]
    sg = [_dot1(smat[u], gmat[u], _NT) for u in nu]
    for u in nu:
        s_scr[u] = sg[u] + zt[u]
    pairs = LANES // D
    for b in range(nb):
        for p in range(nh // pairs):
            us = [b * nh + p * pairs + s for s in range(pairs)]
            o_scr[b * T:(b + 1) * T, p * LANES:(p + 1) * LANES] = jnp.concatenate(
                [qs[u] + qo[u][:, D:] + rkv[u] for u in us], axis=1)

    o = o_scr[...]
    mean = head_sums([o])[0] * (1.0 / D)
    d = o - mean
    var = head_sums([d * d])[0] * (1.0 / D)
    on = d * lax.rsqrt(var + RWKV_GN_EPS) * lng_ref[...] + lnb_ref[...]
    bonus = rk_sum * xv
    out_ref[...] = ((on + bonus) * gate).astype(out_ref.dtype).reshape(nb, T, W)

    @pl.when(c == nchunks - 1)
    def _():
        s1_ref[0] = s_scr[...].reshape(nb, nh, D, D)
        _zero_other_slabs(s1_ref)


def _rwkv(proj3, sh, s0, l_in, l_out, prev_out, mu, w0, wup, a0, aup, gup, kkw, kaw, rkw, lng, lnb,
          ones, *, nb):
    nseq, seqlen, _ = proj3.shape
    T = min(RWKV_CHUNK, seqlen)
    nc = seqlen // T
    W = RWKV_WIDTH
    const = lambda b, c: (0, 0)
    rowblk = lambda col, width: pl.BlockSpec((nb, T, width), lambda b, c: (b, c, col // width))
    sdims = (RWKV_HEADS, HEAD_DIM, HEAD_DIM)
    vec = pl.BlockSpec((1, W), const)
    in_specs = [rowblk(C_R, W), rowblk(C_K, W), rowblk(C_V, W), rowblk(C_G, G_PAD),
                rowblk(C_WA, WA_PAD),
                pl.BlockSpec((nb, 1, RW_PACK), lambda b, c: (b, 0, 0)),
                pl.BlockSpec((1, nb) + sdims, lambda b, c: (l_in, b, 0, 0, 0)),
                pl.BlockSpec((1, RW_PACK), const),
                vec, pl.BlockSpec((WA_PAD, W), const),
                vec, pl.BlockSpec((WA_PAD, W), const),
                pl.BlockSpec((G_PAD, W), const),
                vec, vec, vec, vec, vec,
                pl.BlockSpec((2 * LANES, 2 * LANES), const)]
    rows = nb * T
    return _stacked_state_call(
        functools.partial(_rwkv_kernel, nb=nb, T=T, nchunks=nc), (nseq // nb, nc), in_specs,
        [proj3, proj3, proj3, proj3, proj3, sh, s0, mu, w0, wup, a0, aup, gup, kkw, kaw, rkw, lng,
         lnb, ones],
        pl.BlockSpec((nb, T, W), lambda b, c: (b, c, 0)),
        jax.ShapeDtypeStruct((nseq, seqlen, W), BF16),
        sdims, l_out, prev_out,
        [pltpu.VMEM((nb, RW_PACK), F32), pltpu.VMEM((nb * RWKV_HEADS, HEAD_DIM, HEAD_DIM), F32)]
        + [pltpu.VMEM((rows, W), F32)] * 7
        + [pltpu.VMEM((rows, W), F32), pltpu.VMEM((rows, W), F32)],
        "rwkv", nb=nb)


def _rws_pre_kernel(r_ref, k_ref, v_ref, g_ref, wa_ref, shr_ref, shk_ref, shv_ref, shg_ref,
                    shwa_ref, mur_ref, muk_ref, muv_ref, mug_ref, muwa_ref, w0_ref, wup_ref,
                    a0_ref, aup_ref, gup_ref, kkw_ref, kaw_ref, rkw_ref, ones_ref,
                    rt_ref, wt_ref, knt_ref, kat_ref, kpt_ref, vt_ref, gate_ref, bonus_ref):
    steps, nseq, _ = r_ref.shape

    def shifted(ref, sh_ref, mu_ref):
        x = ref[...]
        prev = jnp.concatenate([sh_ref[...][None], x[:steps - 1]], axis=0)
        return (x + (prev - x) * mu_ref[...]).reshape(steps * nseq, x.shape[-1])

    xr = shifted(r_ref, shr_ref, mur_ref)
    xk = shifted(k_ref, shk_ref, muk_ref)
    xv = shifted(v_ref, shv_ref, muv_ref)
    xg = shifted(g_ref, shg_ref, mug_ref)
    xwa = shifted(wa_ref, shwa_ref, muwa_ref)

    w_lin = w0_ref[...] + _dot1(jnp.tanh(xwa), wup_ref[...])
    a = _sigmoid(a0_ref[...] + _dot1(xwa, aup_ref[...]))
    gate = _dot1(_sigmoid(xg), gup_ref[...])
    decay = jnp.exp(-jnp.exp(-_softplus(-w_lin) - 0.5))
    kk = xk * kkw_ref[...]
    kp = xk * (1.0 + (a - 1.0) * kaw_ref[...])
    kk_sq, rk_sum = _dot_shared_rhs([kk * kk, xr * kp * rkw_ref[...]], ones_ref[...], 3)
    kk = kk / jnp.maximum(jnp.sqrt(kk_sq), 1e-12)

    for t in range(steps):
        rows = slice(t * nseq, (t + 1) * nseq)
        rt_ref[t] = xr[rows].T
        wt_ref[t] = decay[rows].T
        knt_ref[t] = (-kk[rows]).T
        kat_ref[t] = (kk[rows] * a[rows]).T
        kpt_ref[t] = kp[rows].T
        vt_ref[t] = xv[rows].T
    gate_ref[...] = gate.reshape(gate_ref.shape)
    bonus_ref[...] = (rk_sum * xv).reshape(bonus_ref.shape)


def _rws_rec_kernel(r_ref, w_ref, kn_ref, ka_ref, kp_ref, v_ref, s0_ref, o_ref, s1_ref):
    steps = r_ref.shape[0]

    def value_rows(vc, carry):
        v0 = pl.multiple_of(vc * SUBLANES, SUBLANES)
        s = [s0_ref[0, 0, v0 + i] for i in range(SUBLANES)]
        for t in range(steps):
            kn, ka, kp, w, r = kn_ref[t], ka_ref[t], kp_ref[t], w_ref[t], r_ref[t]
            vv = v_ref[t, pl.ds(v0, SUBLANES), :]
            outs = []
            for i in range(SUBLANES):
                sa = jnp.sum(s[i] * kn, axis=0, keepdims=True)
                s[i] = s[i] * w + (sa * ka + vv[i:i + 1, :] * kp)
                outs.append(jnp.sum(s[i] * r, axis=0, keepdims=True))
            o_ref[t, pl.ds(v0, SUBLANES), :] = jnp.concatenate(outs, axis=0)
        for i in range(SUBLANES):
            s1_ref[0, 0, v0 + i] = s[i]
        return carry

    lax.fori_loop(0, HEAD_DIM // SUBLANES, value_rows, 0)
    _zero_other_slabs(s1_ref)


def _rws_post_kernel(o_ref, gate_ref, bonus_ref, lng_ref, lnb_ref, ones_ref, out_ref):
    steps, _, nseq = o_ref.shape
    o = jnp.concatenate([o_ref[t].T for t in range(steps)], axis=0)
    ones = ones_ref[...]
    mean = _dot_shared_rhs([o], ones, 3)[0] * (1.0 / HEAD_DIM)
    d = o - mean
    var = _dot_shared_rhs([d * d], ones, 3)[0] * (1.0 / HEAD_DIM)
    on = d * lax.rsqrt(var + RWKV_GN_EPS) * lng_ref[...] + lnb_ref[...]
    rows = steps * nseq
    res = (on + bonus_ref[...].reshape(rows, LANES)) * gate_ref[...].reshape(rows, LANES)
    out_ref[...] = res.astype(out_ref.dtype).reshape(out_ref.shape)


def _rwkv_steps(p3, sh, s0_t, l_in, l_out, prev_out, mu, w0, wup, a0, aup, gup, kkw, kaw, rkw, lng,
                lnb, ones):
    nseq, steps, _ = p3.shape
    W, D, H = RWKV_WIDTH, HEAD_DIM, RWKV_HEADS
    pt = jnp.transpose(p3[:, :, C_R:], (1, 0, 2))
    nblk = W // LANES
    ones_blk = ones[:LANES, :LANES]
    g_col, wa_col = 3 * W, 3 * W + G_PAD

    def cols(rows, width, col):
        return pl.BlockSpec(rows + (width,), lambda p: (0,) * len(rows) + (col // width,))

    def cols_p(rows, base):
        return pl.BlockSpec(rows + (LANES,), lambda p: (0,) * len(rows) + (base // LANES + p,))

    tb = (steps, nseq)
    in_specs = ([cols_p(tb, 0), cols_p(tb, W), cols_p(tb, 2 * W), cols(tb, G_PAD, g_col),
                 cols(tb, WA_PAD, wa_col)]
                + [cols_p((nseq,), 0), cols_p((nseq,), W), cols_p((nseq,), 2 * W),
                   cols((nseq,), G_PAD, g_col), cols((nseq,), WA_PAD, wa_col)]
                + [cols_p((1,), 0), cols_p((1,), W), cols_p((1,), 2 * W), cols((1,), G_PAD, g_col),
                   cols((1,), WA_PAD, wa_col)]
                + [cols_p((1,), 0), cols_p((WA_PAD,), 0), cols_p((1,), 0), cols_p((WA_PAD,), 0),
                   cols_p((G_PAD,), 0), cols_p((1,), 0), cols_p((1,), 0), cols_p((1,), 0),
                   pl.BlockSpec((LANES, LANES), lambda p: (0, 0))])
    t_spec = pl.BlockSpec((steps, LANES, nseq), lambda p: (0, p, 0))
    n_spec = pl.BlockSpec((steps, nseq, LANES), lambda p: (0, 0, p))
    t_shape = jax.ShapeDtypeStruct((steps, W, nseq), F32)
    n_shape = jax.ShapeDtypeStruct((steps, nseq, W), F32)
    pre = pl.pallas_call(
        _rws_pre_kernel, grid=(nblk,), in_specs=in_specs,
        out_specs=[t_spec] * 6 + [n_spec] * 2, out_shape=[t_shape] * 6 + [n_shape] * 2,
        compiler_params=_cparams("arbitrary"), name="rwkv_pre",
    )(pt, pt, pt, pt, pt, sh, sh, sh, sh, sh, mu, mu, mu, mu, mu, w0, wup, a0, aup, gup, kkw, kaw,
      rkw, ones_blk)
    rt, wt, knt, kat, kpt, vt, gate, bonus = pre

    head = pl.BlockSpec((steps, D, nseq), lambda h, c: (0, h, 0))
    o_t, stack = _stacked_state_call(
        _rws_rec_kernel, (H, 1),
        [head] * 6 + [pl.BlockSpec((1, 1, D, D, nseq), lambda h, c: (l_in, h, 0, 0, 0))],
        [rt, wt, knt, kat, kpt, vt, s0_t],
        head, t_shape, (D, D, nseq), l_out, prev_out, [], "rwkv_rec")

    out = pl.pallas_call(
        _rws_post_kernel, grid=(nblk,),
        in_specs=[t_spec, n_spec, n_spec, pl.BlockSpec((1, LANES), lambda p: (0, p)),
                  pl.BlockSpec((1, LANES), lambda p: (0, p)),
                  pl.BlockSpec((LANES, LANES), lambda p: (0, 0))],
        out_specs=n_spec, out_shape=jax.ShapeDtypeStruct((steps, nseq, W), BF16),
        compiler_params=_cparams("arbitrary"), name="rwkv_post",
    )(o_t, gate, bonus, lng, lnb, ones_blk)
    return jnp.transpose(out, (1, 0, 2)).reshape(nseq * steps, W), stack


def _zeros_like_cols(x, n):
    return jnp.zeros(x.shape[:-1] + (n,), x.dtype)


def _pack_rwkv_cols(x):
    return jnp.concatenate([x[..., :_O_XW], x[..., _O_XG:], _zeros_like_cols(x, G_PAD - RWKV_R_G),
                            x[..., _O_XW:_O_XG]], axis=-1)


def _pack_tail_cols(w):
    return jnp.concatenate([_pack_rwkv_cols(w[..., _O_RW:]), w[..., _O_DT:_O_RW],
                            _zeros_like_cols(w, DT_PAD - SSD_HEADS)], axis=-1)


def _pad_rows(x, n_before, n_total):
    b, r, c = x.shape
    return jnp.concatenate([jnp.zeros((b, n_before, c), x.dtype), x,
                            jnp.zeros((b, n_total - n_before - r, c), x.dtype)], axis=1)


def _small_params(l, p):
    row = lambda v: v.reshape(1, -1)
    pad_lanes = lambda v, n: jnp.concatenate([v, jnp.zeros((n - v.shape[0],), v.dtype)]).reshape(1, n)
    zeros_w = jnp.zeros((RWKV_R_W, RWKV_WIDTH), F32)
    return dict(
        lru=(p['lru_conv_w'][l], row(p['lru_conv_b'][l]), p['lru_wa'][l].astype(BF16),
             row(p['lru_ba'][l]), p['lru_wx'][l].astype(BF16), row(p['lru_bx'][l]),
             row(p['lru_lambda'][l])),
        ssd=(p['ssd_conv_w'][l], row(p['ssd_conv_b'][l]), pad_lanes(p['ssd_dt_bias'][l], DT_PAD),
             pad_lanes(p['ssd_a_log'][l], DT_PAD), row(jnp.repeat(p['ssd_d'][l], HEAD_DIM)),
             row(p['ssd_norm_g'][l])),
        rwkv=(row(_pack_rwkv_cols(p['rwkv_mu'][l])), row(p['rwkv_w0'][l]),
              jnp.concatenate([p['rwkv_w_up'][l], zeros_w], axis=0).astype(BF16),
              row(p['rwkv_a0'][l]),
              jnp.concatenate([zeros_w, p['rwkv_a_up'][l]], axis=0).astype(BF16),
              jnp.concatenate([p['rwkv_g_up'][l],
                               jnp.zeros((G_PAD - RWKV_R_G, RWKV_WIDTH), F32)], axis=0).astype(BF16),
              row(p['rwkv_k_k'][l]), row(p['rwkv_k_a'][l]), row(p['rwkv_r_k'][l]),
              row(p['rwkv_ln_g'][l]), row(p['rwkv_ln_b'][l])),
        ln1=(row(p['ln1_g'][l]), row(p['ln1_b'][l])),
        ln2=(row(p['ln2_g'][l]), row(p['ln2_b'][l])),
    )


def _constants():
    lane = jnp.arange(SSD_WIDTH) // HEAD_DIM
    ehp = (jnp.arange(DT_PAD)[:, None] == lane[None, :]).astype(BF16)
    blk = jnp.arange(2 * LANES) // HEAD_DIM
    ones = (blk[:, None] == blk[None, :]).astype(BF16)
    return ehp, ones


def _layer(x, xb, l, small_state, ssd_in, rw_in, prev_outs, sp, big, consts, *, nseq, seqlen,
           lru_nblk, rwkv_nb):
    lru_conv0, lru_h0, ssd_conv0, rw_shift0 = small_state
    ehp, ones = consts
    nbuf = CONV_WIDTH - 1
    proj = _proj(xb, big['w_main'], big['w_tail'], l)

    lru_nseq = nseq // lru_nblk
    out_a, lru_h1 = _lru(
        proj, _pad_rows(lru_conv0, 0, SUBLANES).reshape(lru_nblk, lru_nseq * SUBLANES, LRU_WIDTH),
        lru_h0.reshape(lru_nblk, lru_nseq, LRU_WIDTH), *sp['lru'],
        nblk=lru_nblk, nseq=lru_nseq, seqlen=seqlen)
    p3 = proj.reshape(nseq, seqlen, N_PROJ)
    if seqlen == SUBLANES and nseq % SSD_SHORT_NB == 0:
        out_b, ssd_out = _ssd_short(p3, _pad_rows(ssd_conv0, 0, SUBLANES), ssd_in[0], ssd_in[1], l,
                                    prev_outs[0], *sp['ssd'], ehp)
    else:
        out_b, ssd_out = _ssd(proj, _pad_rows(ssd_conv0, SUBLANES - nbuf, SUBLANES), ssd_in[0],
                              ssd_in[1], l, prev_outs[0], *sp['ssd'], ehp, nseq=nseq, seqlen=seqlen)
    if rwkv_nb is None:
        out_c, rw_out = _rwkv_steps(p3, _pack_rwkv_cols(rw_shift0), rw_in[0], rw_in[1], l,
                                    prev_outs[1], *sp['rwkv'], ones)
    else:
        out_c, rw_out = _rwkv(p3, _pack_rwkv_cols(rw_shift0)[:, None, :], rw_in[0], rw_in[1], l,
                              prev_outs[1], *sp['rwkv'], ones, nb=rwkv_nb)
        out_c = out_c.reshape(nseq * seqlen, RWKV_WIDTH)

    y, yb = _outproj_ln(out_a, out_b, out_c, big['w_out'], l, x, *sp['ln1'])
    y, yb = _ffn_down_ln(_ffn_up(yb, big['w_gate'], big['w_up'], l), big['w_down'], l, y, *sp['ln2'])

    tail = lambda col, width: p3[:, seqlen - nbuf:, col:col + width]
    last = lambda col, width: p3[:, seqlen - 1, col:col + width]
    rw_shift1 = jnp.concatenate([last(C_R, 3 * RWKV_WIDTH), last(C_WA, WA_PAD),
                                 last(C_G, RWKV_R_G)], axis=-1)
    small_new = (tail(C_LX, LRU_WIDTH), lru_h1.reshape(nseq, LRU_WIDTH),
                 tail(C_XBC, SSD_CONV_DIM), rw_shift1)
    return y, yb, small_new, (ssd_out, rw_out)


def kernel(x_prompt, x_sample, state_lru_conv, state_lru_h, state_ssd_conv, state_ssd,
           state_rwkv_shift, state_rwkv, w_in, lru_conv_w, lru_conv_b, lru_wa, lru_ba, lru_wx,
           lru_bx, lru_lambda, ssd_conv_w, ssd_conv_b, ssd_dt_bias, ssd_a_log, ssd_d, ssd_norm_g,
           rwkv_mu, rwkv_w0, rwkv_w_up, rwkv_a0, rwkv_a_up, rwkv_g_up, rwkv_k_k, rwkv_k_a,
           rwkv_r_k, rwkv_ln_g, rwkv_ln_b, w_out, ln1_g, ln1_b, w_gate, w_up, w_down, ln2_g, ln2_b):
    params = dict(
        lru_conv_w=lru_conv_w, lru_conv_b=lru_conv_b, lru_wa=lru_wa, lru_ba=lru_ba,
        lru_wx=lru_wx, lru_bx=lru_bx, lru_lambda=lru_lambda, ssd_conv_w=ssd_conv_w,
        ssd_conv_b=ssd_conv_b, ssd_dt_bias=ssd_dt_bias, ssd_a_log=ssd_a_log, ssd_d=ssd_d,
        ssd_norm_g=ssd_norm_g, rwkv_mu=rwkv_mu, rwkv_w0=rwkv_w0, rwkv_w_up=rwkv_w_up,
        rwkv_a0=rwkv_a0, rwkv_a_up=rwkv_a_up, rwkv_g_up=rwkv_g_up, rwkv_k_k=rwkv_k_k,
        rwkv_k_a=rwkv_k_a, rwkv_r_k=rwkv_r_k.reshape(DEPTH, RWKV_WIDTH), rwkv_ln_g=rwkv_ln_g,
        rwkv_ln_b=rwkv_ln_b, ln1_g=ln1_g, ln1_b=ln1_b, ln2_g=ln2_g, ln2_b=ln2_b)
    big = dict(w_main=jnp.swapaxes(w_in, 1, 2),
               w_tail=jnp.swapaxes(_pack_tail_cols(w_in), 1, 2),
               w_out=w_out.astype(BF16),
               w_gate=w_gate, w_up=w_up, w_down=w_down.astype(BF16))
    bp, lp_len, _ = x_prompt.shape
    bs, ls_len, _ = x_sample.shape
    consts = _constants()
    nbuf = CONV_WIDTH - 1
    zero_small = (jnp.zeros((bp, nbuf, LRU_WIDTH), F32), jnp.zeros((bp, LRU_WIDTH), F32),
                  jnp.zeros((bp, nbuf, SSD_CONV_DIM), F32), jnp.zeros((bp, RWKV_SHIFT), F32))
    zero_ssd = jnp.zeros((1, bp, SSD_HEADS, HEAD_DIM, SSD_STATE), F32)
    zero_rw = jnp.zeros((1, bp, RWKV_HEADS, HEAD_DIM, HEAD_DIM), F32)
    yp = x_prompt.reshape(bp * lp_len, D_MODEL)
    ys = x_sample.reshape(bs * ls_len, D_MODEL)
    ypb = yp.astype(BF16)
    ysb = ys.astype(BF16)
    state_rwkv_t = jnp.transpose(state_rwkv, (0, 2, 3, 4, 1))
    new_p = [[] for _ in range(4)]
    new_s = [[] for _ in range(4)]
    outs_p = (None, None)
    outs_s = (None, None)
    for l in range(DEPTH):
        sp = _small_params(l, params)
        yp, ypb, small_p, outs_p = _layer(yp, ypb, l, zero_small, (zero_ssd, 0), (zero_rw, 0), outs_p, sp,
                                     big, consts, nseq=bp, seqlen=lp_len, lru_nblk=bp,
                                     rwkv_nb=RWKV_PROMPT_NB)
        ys, ysb, small_s, outs_s = _layer(
            ys, ysb, l, (state_lru_conv[l], state_lru_h[l], state_ssd_conv[l], state_rwkv_shift[l]),
            (state_ssd, l), (state_rwkv_t, l), outs_s, sp, big, consts,
            nseq=bs, seqlen=ls_len, lru_nblk=1, rwkv_nb=None)
        for i in range(4):
            new_p[i].append(small_p[i])
            new_s[i].append(small_s[i])
    p_lru_conv, p_lru_h, p_ssd_conv, p_rw_shift = [jnp.stack(v) for v in new_p]
    s_lru_conv, s_lru_h, s_ssd_conv, s_rw_shift = [jnp.stack(v) for v in new_s]
    return (yp.reshape(bp, lp_len, D_MODEL), ys.reshape(bs, ls_len, D_MODEL),
            p_lru_conv, p_lru_h, p_ssd_conv, outs_p[0], p_rw_shift, outs_p[1],
            s_lru_conv, s_lru_h, s_ssd_conv, outs_s[0], s_rw_shift,
            jnp.transpose(outs_s[1], (0, 4, 1, 2, 3)))
```

```python
import functools
import math

import jax
import jax.numpy as jnp
from jax import lax
from jax.experimental import pallas as pl
from jax.experimental.pallas import tpu as pltpu

F32 = jnp.float32
BF16 = jnp.bfloat16

D_MODEL = 2048
DEPTH = 2
D_MIX = 2 * D_MODEL
HEAD_DIM = 64
CONV_WIDTH = 4
LRU_WIDTH = D_MIX // 4
LRU_HEADS = 8
LRU_BLOCK = LRU_WIDTH // LRU_HEADS
LRU_C = 8.0
SSD_WIDTH = D_MIX // 2
SSD_HEADS = SSD_WIDTH // HEAD_DIM
SSD_GROUPS = 8
SSD_STATE = 128
SSD_CHUNK = 128
SSD_CONV_DIM = SSD_WIDTH + 2 * SSD_GROUPS * SSD_STATE
SSD_GROUP_WIDTH = SSD_WIDTH // SSD_GROUPS
SSD_HEADS_PER_GROUP = SSD_HEADS // SSD_GROUPS
RWKV_WIDTH = D_MIX - LRU_WIDTH - SSD_WIDTH
RWKV_HEADS = RWKV_WIDTH // HEAD_DIM
RWKV_R_W = max(32, int(round(1.8 * RWKV_WIDTH ** 0.5 / 32)) * 32)
RWKV_R_A = max(32, int(round(1.8 * RWKV_WIDTH ** 0.5 / 32)) * 32)
RWKV_R_G = max(32, int(round(0.6 * RWKV_WIDTH ** 0.8 / 32)) * 32)
RWKV_SHIFT = 3 * RWKV_WIDTH + RWKV_R_W + RWKV_R_A + RWKV_R_G
RWKV_GN_EPS = 64e-5
RWKV_CHUNK = 64
RWKV_PROMPT_NB = 4
D_FF = -(-(8 * D_MODEL) // (3 * 256)) * 256
ALPHA = (2 * DEPTH) ** 0.25
LN_EPS = 1e-5
LN_ROW_CHUNK = 128

LANES = 128
SUBLANES = 8
VMEM_LIMIT = 56 * 1024 * 1024

C_GATE = 0
C_LX = C_GATE + LRU_WIDTH
C_Z = C_LX + LRU_WIDTH
C_XBC = C_Z + SSD_WIDTH
C_R = C_XBC + SSD_CONV_DIM
C_K = C_R + RWKV_WIDTH
C_V = C_K + RWKV_WIDTH
C_G = C_V + RWKV_WIDTH
G_PAD = 2 * LANES
C_WA = C_G + G_PAD
WA_PAD = LANES
C_DT = C_WA + WA_PAD
DT_PAD = LANES
N_PROJ = C_DT + DT_PAD
assert RWKV_R_G <= G_PAD and RWKV_R_W + RWKV_R_A == WA_PAD and SSD_HEADS <= DT_PAD
RW_PACK = 3 * RWKV_WIDTH + G_PAD + WA_PAD

_O_DT = 2 * LRU_WIDTH + SSD_WIDTH + SSD_CONV_DIM
_O_RW = _O_DT + SSD_HEADS
_O_XW = 3 * RWKV_WIDTH
_O_XG = _O_XW + RWKV_R_W + RWKV_R_A


def _cparams(*sem):
    return pltpu.CompilerParams(dimension_semantics=sem, vmem_limit_bytes=VMEM_LIMIT)


_NN = (((1,), (0,)), ((), ()))
_NT = (((1,), (1,)), ((), ()))
_TN = (((0,), (0,)), ((), ()))


def _dg(a, b, dims):
    return lax.dot_general(a, b, dims, preferred_element_type=F32)


def _dot1(a, b, dims=_NN):
    return _dg(a.astype(BF16), b.astype(BF16), dims)


def _split2(x):
    hi = x.astype(BF16)
    lo = (x - hi.astype(F32)).astype(BF16)
    return hi, lo


def _split3(x):
    hi = x.astype(BF16)
    r1 = x - hi.astype(F32)
    mid = r1.astype(BF16)
    lo = (r1 - mid.astype(F32)).astype(BF16)
    return hi, mid, lo


def _dot3(a, b, dims=_NN):
    ah, al = _split2(a)
    bh, bl = _split2(b)
    free_axis = 1 if dims == _TN else 0
    m = a.shape[free_axis]
    both = _dg(jnp.concatenate([ah, al], axis=free_axis), bh, dims)
    return (both[:m] + both[m:]) + _dg(ah, bl, dims)


def _dot_shared_rhs(parts, w_exact, npieces):
    pieces = []
    for p in parts:
        rest = p
        for _ in range(npieces):
            piece = rest.astype(BF16).astype(F32)
            pieces.append(piece)
            rest = rest - piece
    prod = _dg(jnp.concatenate(pieces, axis=0).astype(BF16), w_exact, _NN)
    outs = []
    off = 0
    for p in parts:
        r = p.shape[0]
        acc = prod[off:off + r]
        for i in range(1, npieces):
            acc = acc + prod[off + i * r:off + (i + 1) * r]
        outs.append(acc)
        off += npieces * r
    return outs


def _dotx_left(w_exact, a, npieces=3):
    if npieces == 2:
        hi, lo = _split2(a)
        return _dg(w_exact, hi, _NN) + _dg(w_exact, lo, _NN)
    hi, mid, lo = _split3(a)
    return _dg(w_exact, hi, _NN) + (_dg(w_exact, mid, _NN) + _dg(w_exact, lo, _NN))


def _iota2(shape, dim):
    return lax.broadcasted_iota(jnp.int32, shape, dim)


def _softplus(x):
    return jnp.maximum(x, 0.0) + jnp.log1p(jnp.exp(-jnp.abs(x)))


def _sigmoid(x):
    return 1.0 / (1.0 + jnp.exp(-x))


def _silu(x):
    return x * _sigmoid(x)


def _gelu_tanh(x):
    c = math.sqrt(2.0 / math.pi)
    return 0.5 * x * (1.0 + jnp.tanh(c * (x + 0.044715 * (x * x * x))))


def _layer_norm(y, g, b):
    mu = jnp.mean(y, axis=-1, keepdims=True)
    d = y - mu
    var = jnp.mean(d * d, axis=-1, keepdims=True)
    return d * lax.rsqrt(var + LN_EPS) * g + b


DENSE_TM = 2048
DENSE_TN = 512
N_MAIN = C_R
N_TAIL = N_PROJ - N_MAIN
assert N_MAIN == _O_DT and N_MAIN % DENSE_TN == 0 and N_TAIL % DENSE_TN == 0 and D_FF % DENSE_TN == 0


def _proj_kernel(x_ref, wm_ref, wt_ref, o_ref):
    j = pl.program_id(1)

    @pl.when(j < N_MAIN // DENSE_TN)
    def _():
        o_ref[...] = _dg(x_ref[...], wm_ref[0].astype(BF16), _NT)

    @pl.when(j >= N_MAIN // DENSE_TN)
    def _():
        o_ref[...] = _dg(x_ref[...], wt_ref[0].astype(BF16), _NT)


def _proj(x, w_main_t, w_tail_t, l):
    m = x.shape[0]
    tm = min(m, DENSE_TM)
    tn = DENSE_TN
    nmain = N_MAIN // tn
    return pl.pallas_call(
        _proj_kernel,
        grid=(m // tm, N_PROJ // tn),
        in_specs=[pl.BlockSpec((tm, D_MODEL), lambda i, j: (i, 0)),
                  pl.BlockSpec((1, tn, D_MODEL), lambda i, j: (l, jnp.minimum(j, nmain - 1), 0)),
                  pl.BlockSpec((1, tn, D_MODEL), lambda i, j: (l, jnp.maximum(j - nmain, 0), 0))],
        out_specs=pl.BlockSpec((tm, tn), lambda i, j: (i, j)),
        out_shape=jax.ShapeDtypeStruct((m, N_PROJ), F32),
        compiler_params=_cparams("arbitrary", "arbitrary"),
        name="proj",
    )(x, w_main_t, w_tail_t)


OUT_KSTEPS = 2


def _outproj_kernel(a_ref, b_ref, c_ref, w_ref, x_ref, g_ref, beta_ref, o_ref, ob_ref):
    k = pl.program_id(1)
    half = LRU_WIDTH

    @pl.when(k == 0)
    def _():
        o_ref[...] = (jnp.dot(a_ref[...], w_ref[0, :half], preferred_element_type=F32)
                      + jnp.dot(b_ref[...], w_ref[0, half:], preferred_element_type=F32))

    @pl.when(k == OUT_KSTEPS - 1)
    def _():
        o_ref[...] += (jnp.dot(b_ref[...], w_ref[0, :half], preferred_element_type=F32)
                       + jnp.dot(c_ref[...], w_ref[0, half:], preferred_element_type=F32))
        _residual_ln_rows(o_ref, ob_ref, x_ref, g_ref, beta_ref)


def _residual_ln_rows(o_ref, ob_ref, x_ref, g_ref, beta_ref):
    chunk = min(LN_ROW_CHUNK, o_ref.shape[0])

    def ln_rows(r, carry):
        rows = pl.ds(pl.multiple_of(r * chunk, chunk), chunk)
        y = _layer_norm(ALPHA * x_ref[rows, :] + o_ref[rows, :], g_ref[...], beta_ref[...])
        o_ref[rows, :] = y
        ob_ref[rows, :] = y.astype(BF16)
        return carry

    lax.fori_loop(0, o_ref.shape[0] // chunk, ln_rows, 0)


def _outproj_ln(out_a, out_b, out_c, w_out, l, x, g, beta):
    m = x.shape[0]
    tm = min(m, 512)
    tk = LRU_WIDTH
    assert SSD_WIDTH == 2 * tk and RWKV_WIDTH == tk and D_MIX == OUT_KSTEPS * 2 * tk
    return pl.pallas_call(
        _outproj_kernel,
        grid=(m // tm, OUT_KSTEPS),
        in_specs=[pl.BlockSpec((tm, tk), lambda i, k: (i, 0)),
                  pl.BlockSpec((tm, tk), lambda i, k: (i, k)),
                  pl.BlockSpec((tm, tk), lambda i, k: (i, 0)),
                  pl.BlockSpec((1, 2 * tk, D_MODEL), lambda i, k: (l, k, 0)),
                  pl.BlockSpec((tm, D_MODEL), lambda i, k: (i, 0)),
                  pl.BlockSpec((1, D_MODEL), lambda i, k: (0, 0)),
                  pl.BlockSpec((1, D_MODEL), lambda i, k: (0, 0))],
        out_specs=[pl.BlockSpec((tm, D_MODEL), lambda i, k: (i, 0))] * 2,
        out_shape=[jax.ShapeDtypeStruct((m, D_MODEL), F32), jax.ShapeDtypeStruct((m, D_MODEL), BF16)],
        compiler_params=_cparams("arbitrary", "arbitrary"),
        name="outproj_ln",
    )(out_a, out_b, out_c, w_out, x, g, beta)


def _ffn_up_kernel(x_ref, wg_ref, wu_ref, o_ref):
    xb = x_ref[...]
    gate = jnp.dot(xb, wg_ref[0].astype(BF16), preferred_element_type=F32)
    up = jnp.dot(xb, wu_ref[0].astype(BF16), preferred_element_type=F32)
    o_ref[...] = (_silu(gate) * up).astype(BF16)


def _ffn_up(x, wg, wu, l):
    m = x.shape[0]
    tm = min(m, DENSE_TM)
    tn = DENSE_TN
    return pl.pallas_call(
        _ffn_up_kernel,
        grid=(m // tm, D_FF // tn),
        in_specs=[pl.BlockSpec((tm, D_MODEL), lambda i, j: (i, 0)),
                  pl.BlockSpec((1, D_MODEL, tn), lambda i, j: (l, 0, j)),
                  pl.BlockSpec((1, D_MODEL, tn), lambda i, j: (l, 0, j))],
        out_specs=pl.BlockSpec((tm, tn), lambda i, j: (i, j)),
        out_shape=jax.ShapeDtypeStruct((m, D_FF), BF16),
        compiler_params=_cparams("arbitrary", "arbitrary"),
        name="ffn_up",
    )(x, wg, wu)


FFN_DOWN_KSTEPS = 2


def _ffn_down_kernel(h_ref, w_ref, x_ref, g_ref, beta_ref, o_ref, ob_ref):
    k = pl.program_id(1)

    @pl.when(k == 0)
    def _():
        o_ref[...] = jnp.dot(h_ref[...], w_ref[0], preferred_element_type=F32)

    @pl.when(k > 0)
    def _():
        o_ref[...] += jnp.dot(h_ref[...], w_ref[0], preferred_element_type=F32)

    @pl.when(k == FFN_DOWN_KSTEPS - 1)
    def _():
        _residual_ln_rows(o_ref, ob_ref, x_ref, g_ref, beta_ref)


def _ffn_down_ln(h, w_down, l, x, g, beta):
    m = x.shape[0]
    tm = min(m, 512)
    tk = D_FF // FFN_DOWN_KSTEPS
    assert tk * FFN_DOWN_KSTEPS == D_FF and tk % LANES == 0 and tm % min(LN_ROW_CHUNK, tm) == 0
    return pl.pallas_call(
        _ffn_down_kernel,
        grid=(m // tm, FFN_DOWN_KSTEPS),
        in_specs=[pl.BlockSpec((tm, tk), lambda i, k: (i, k)),
                  pl.BlockSpec((1, tk, D_MODEL), lambda i, k: (l, k, 0)),
                  pl.BlockSpec((tm, D_MODEL), lambda i, k: (i, 0)),
                  pl.BlockSpec((1, D_MODEL), lambda i, k: (0, 0)),
                  pl.BlockSpec((1, D_MODEL), lambda i, k: (0, 0))],
        out_specs=[pl.BlockSpec((tm, D_MODEL), lambda i, k: (i, 0))] * 2,
        out_shape=[jax.ShapeDtypeStruct((m, D_MODEL), F32), jax.ShapeDtypeStruct((m, D_MODEL), BF16)],
        compiler_params=_cparams("arbitrary", "arbitrary"),
        name="ffn_down_ln",
    )(h, w_down, x, g, beta)


def _conv_taps(u, first_rows, cw, cb, t_in_seq):
    out = cb + cw[CONV_WIDTH - 1:CONV_WIDTH, :] * u
    for j in range(1, CONV_WIDTH):
        shifted = jnp.where(t_in_seq < j, first_rows(j), pltpu.roll(u, j, 0))
        out = out + cw[CONV_WIDTH - 1 - j:CONV_WIDTH - j, :] * shifted
    return out


def _lru_kernel(gate_ref, lx_ref, buf_ref, h0_ref, cw_ref, cb_ref, wa_ref, ba_ref, wx_ref,
                bx_ref, lam_ref, out_ref, h1_ref, a_scr, b_scr, hin_scr, *, nseq, seqlen):
    rows = nseq * seqlen
    u = lx_ref[...]
    t_in_seq = _iota2((rows, LANES), 0) % seqlen
    if nseq == 1:
        buf = jnp.concatenate([buf_ref[0], jnp.zeros((rows - SUBLANES, LANES), F32)], axis=0) \
            if rows > SUBLANES else buf_ref[0]
    else:
        buf = buf_ref[0]
    nbuf = CONV_WIDTH - 1
    xc = _conv_taps(u, lambda j: pltpu.roll(buf, (j - nbuf) % rows, 0), cw_ref[...], cb_ref[...],
                    t_in_seq)

    r = _sigmoid(_dot1(xc, wa_ref[0]) + ba_ref[...])
    i = _sigmoid(_dot1(xc, wx_ref[0]) + bx_ref[...])
    log_a = (-LRU_C) * r * _softplus(-lam_ref[...])
    a = jnp.exp(log_a)
    b = jnp.sqrt(jnp.tanh(-log_a) * (a * a + 1.0)) * (i * xc)

    t8 = _iota2((rows, LANES), 0) % SUBLANES
    for s in (1, 2, 4):
        m = t8 >= s
        a_sh = pltpu.roll(a, s, 0)
        b_sh = pltpu.roll(b, s, 0)
        b = jnp.where(m, a * b_sh + b, b)
        a = jnp.where(m, a * a_sh, a)

    if seqlen == SUBLANES:
        h0 = h0_ref[0]
        hin = jnp.broadcast_to(h0[:, None, :], (nseq, SUBLANES, LANES)).reshape(rows, LANES)
        h = a * hin + b
        out_ref[...] = (h * _gelu_tanh(gate_ref[...])).astype(out_ref.dtype)
        a_scr[...] = h
        h1_ref[0] = a_scr[pl.ds(SUBLANES - 1, nseq, stride=SUBLANES), :]
    else:
        assert nseq == 1
        a_scr[...] = a
        b_scr[...] = b

        def carry_step(g, carry):
            base = pl.multiple_of(g * SUBLANES, SUBLANES)
            hin_scr[pl.ds(base, SUBLANES), :] = jnp.broadcast_to(carry, (SUBLANES, LANES))
            a7 = a_scr[pl.ds(base + SUBLANES - 1, 1), :]
            b7 = b_scr[pl.ds(base + SUBLANES - 1, 1), :]
            return a7 * carry + b7

        last = lax.fori_loop(0, rows // SUBLANES, carry_step, h0_ref[0])
        h = a_scr[...] * hin_scr[...] + b_scr[...]
        out_ref[...] = (h * _gelu_tanh(gate_ref[...])).astype(out_ref.dtype)
        h1_ref[0] = last


def _lru(proj, buf, h0, cw, cb, wa, ba, wx, bx, lam, *, nblk, nseq, seqlen):
    rows = nseq * seqlen
    gate_blk = C_GATE // LRU_BLOCK
    lx_blk = C_LX // LRU_BLOCK
    row = lambda s, h: (0, h)
    return pl.pallas_call(
        functools.partial(_lru_kernel, nseq=nseq, seqlen=seqlen),
        grid=(nblk, LRU_HEADS),
        in_specs=[pl.BlockSpec((rows, LRU_BLOCK), lambda s, h: (s, gate_blk + h)),
                  pl.BlockSpec((rows, LRU_BLOCK), lambda s, h: (s, lx_blk + h)),
                  pl.BlockSpec((1, nseq * SUBLANES, LRU_BLOCK), lambda s, h: (s, 0, h)),
                  pl.BlockSpec((1, nseq, LRU_BLOCK), lambda s, h: (s, 0, h)),
                  pl.BlockSpec((CONV_WIDTH, LRU_BLOCK), row),
                  pl.BlockSpec((1, LRU_BLOCK), row),
                  pl.BlockSpec((1, LRU_BLOCK, LRU_BLOCK), lambda s, h: (h, 0, 0)),
                  pl.BlockSpec((1, LRU_BLOCK), row),
                  pl.BlockSpec((1, LRU_BLOCK, LRU_BLOCK), lambda s, h: (h, 0, 0)),
                  pl.BlockSpec((1, LRU_BLOCK), row),
                  pl.BlockSpec((1, LRU_BLOCK), row)],
        out_specs=[pl.BlockSpec((rows, LRU_BLOCK), lambda s, h: (s, h)),
                   pl.BlockSpec((1, nseq, LRU_BLOCK), lambda s, h: (s, 0, h))],
        out_shape=[jax.ShapeDtypeStruct((nblk * rows, LRU_WIDTH), BF16),
                   jax.ShapeDtypeStruct((nblk, nseq, LRU_WIDTH), F32)],
        scratch_shapes=[pltpu.VMEM((rows, LANES), F32)] * 3,
        compiler_params=_cparams("arbitrary", "arbitrary"),
        name="lru",
    )(proj, proj, buf, h0, cw, cb, wa, ba, wx, bx, lam)


def _stacked_state_call(kernel_fn, grid, in_specs, operands, row_spec, row_shape, state_dims,
                        l_out, prev_out, scratch_shapes, name, nb=1):
    nseq = grid[0] * nb
    zeros = (0,) * len(state_dims)
    state_shape = jax.ShapeDtypeStruct((DEPTH, nseq) + tuple(state_dims), F32)
    aliases = {}
    if prev_out is None:
        assert l_out == 0
        state_spec = pl.BlockSpec((DEPTH, nb) + tuple(state_dims), lambda b, c: (0, b) + zeros)
    else:
        state_spec = pl.BlockSpec((1, nb) + tuple(state_dims), lambda b, c: (l_out, b) + zeros)
        in_specs = in_specs + [pl.BlockSpec(memory_space=pl.ANY)]
        operands = operands + [prev_out]
        aliases = {len(operands) - 1: 1}
        kernel_fn = functools.partial(_drop_alias_ref, kernel_fn, len(operands) - 1)
    return pl.pallas_call(
        kernel_fn, grid=grid, in_specs=in_specs, out_specs=[row_spec, state_spec],
        out_shape=[row_shape, state_shape], scratch_shapes=scratch_shapes,
        input_output_aliases=aliases, compiler_params=_cparams("arbitrary", "arbitrary"),
        name=name)(*operands)


def _drop_alias_ref(kernel_fn, pos, *refs):
    return kernel_fn(*refs[:pos], *refs[pos + 1:])


def _zero_other_slabs(state_ref):
    if state_ref.shape[0] > 1:
        state_ref[1:] = jnp.zeros((state_ref.shape[0] - 1,) + state_ref.shape[1:], state_ref.dtype)


def _transpose_rows_to_lanes(x, t):
    if t < LANES:
        x = jnp.concatenate([x, jnp.zeros((LANES - t, LANES), F32)], axis=0)
    return x.T[:, :t]


def _ssd_kernel(z_ref, xbc_ref, dt_ref, buf_ref, s0_ref, cw_ref, cb_ref, dtb_ref, alog_ref,
                dch_ref, ng_ref, ehp_ref, out_ref, s1_ref, tail_ref, pad_ref, y_scr, *, T):
    c = pl.program_id(1)
    n_state = SSD_STATE
    gw = SSD_GROUP_WIDTH
    hpg = SSD_HEADS_PER_GROUP

    @pl.when(c == 0)
    def _():
        tail_ref[...] = buf_ref[0]
        s1_ref[0:1] = s0_ref[...]
        _zero_other_slabs(s1_ref)

    u = xbc_ref[...]
    pad_ref[0:SUBLANES, :] = tail_ref[...]
    pad_ref[SUBLANES:SUBLANES + T, :] = u
    cw = cw_ref[...]
    xc = cb_ref[...] + cw[CONV_WIDTH - 1:CONV_WIDTH, :] * u
    for j in range(1, CONV_WIDTH):
        xc = xc + cw[CONV_WIDTH - 1 - j:CONV_WIDTH - j, :] * pad_ref[SUBLANES - j:SUBLANES - j + T, :]
    tail_ref[...] = pad_ref[T:T + SUBLANES, :]

    xa = _silu(xc)
    xs = xa[:, :SSD_WIDTH]
    bm = xa[:, SSD_WIDTH:SSD_WIDTH + SSD_GROUPS * n_state]
    cm = xa[:, SSD_WIDTH + SSD_GROUPS * n_state:]
    dt = _softplus(dt_ref[...] + dtb_ref[...])
    da = dt * (-jnp.exp(alog_ref[...]))
    ii = _iota2((T, T), 0)
    jj = _iota2((T, T), 1)
    causal = ii >= jj
    cum = _dotx_left(causal.astype(BF16), da)
    cum_t = _transpose_rows_to_lanes(cum, T)
    ehp = ehp_ref[...]
    cumx, dtx = _dot_shared_rhs([cum, dt], ehp, 2)
    xdt = xs * dtx
    ecum = jnp.exp(cumx)
    xdtd = xdt * jnp.exp(cumx[T - 1:T, :] - cumx)

    groups = range(SSD_GROUPS)
    heads = range(SSD_HEADS)
    cg = [cm[:, g * n_state:(g + 1) * n_state] for g in groups]
    bg = [bm[:, g * n_state:(g + 1) * n_state] for g in groups]
    sg = [s1_ref[0, 0, g * hpg:(g + 1) * hpg].reshape(gw, n_state) for g in groups]
    cb = [_dot1(cg[g], bg[g], _NT) for g in groups]
    y_off = [_dot1(cg[g], sg[g], _NT) for g in groups]
    st = [_dot1(xdtd[:, g * gw:(g + 1) * gw], bg[g], _TN) for g in groups]
    lm = [jnp.where(causal, jnp.exp(cum[:, h:h + 1] - cum_t[h:h + 1, :]), 0.0) for h in heads]
    y_diag = [_dot1(cb[h // hpg] * lm[h], xdt[:, h * HEAD_DIM:(h + 1) * HEAD_DIM]) for h in heads]
    for g in groups:
        y_scr[:, g * gw:(g + 1) * gw] = (jnp.concatenate(y_diag[g * hpg:(g + 1) * hpg], axis=1)
                                         + y_off[g] * ecum[:, g * gw:(g + 1) * gw])
        decay = jnp.concatenate(
            [jnp.broadcast_to(jnp.exp(cum_t[h:h + 1, T - 1:T]), (HEAD_DIM, n_state))
             for h in range(g * hpg, (g + 1) * hpg)], axis=0)
        s1_ref[0, 0, g * hpg:(g + 1) * hpg] = (sg[g] * decay + st[g]).reshape(hpg, HEAD_DIM, n_state)

    y = (y_scr[...] + dch_ref[...] * xs) * _silu(z_ref[...])
    outs = []
    for g in groups:
        yg = y[:, g * gw:(g + 1) * gw]
        ms = jnp.mean(yg * yg, axis=-1, keepdims=True)
        outs.append(yg * lax.rsqrt(ms + 1e-5))
    out_ref[...] = (jnp.concatenate(outs, axis=1) * ng_ref[...]).astype(out_ref.dtype)


def _ssd(proj, buf, s0, l_in, l_out, prev_out, cw, cb, dtb, alog, dch, ng, ehp, *, nseq, seqlen):
    T = min(SSD_CHUNK, seqlen)
    nc = seqlen // T
    const = lambda b, c: (0, 0)
    sblk = (1, 1, SSD_HEADS, HEAD_DIM, SSD_STATE)
    in_specs = [pl.BlockSpec((T, SSD_WIDTH), lambda b, c: (b * nc + c, C_Z // SSD_WIDTH)),
                pl.BlockSpec((T, SSD_CONV_DIM), lambda b, c: (b * nc + c, C_XBC // SSD_CONV_DIM)),
                pl.BlockSpec((T, DT_PAD), lambda b, c: (b * nc + c, C_DT // DT_PAD)),
                pl.BlockSpec((1, SUBLANES, SSD_CONV_DIM), lambda b, c: (b, 0, 0)),
                pl.BlockSpec(sblk, lambda b, c: (l_in, b, 0, 0, 0)),
                pl.BlockSpec((CONV_WIDTH, SSD_CONV_DIM), const),
                pl.BlockSpec((1, SSD_CONV_DIM), const),
                pl.BlockSpec((1, DT_PAD), const),
                pl.BlockSpec((1, DT_PAD), const),
                pl.BlockSpec((1, SSD_WIDTH), const),
                pl.BlockSpec((1, SSD_WIDTH), const),
                pl.BlockSpec((DT_PAD, SSD_WIDTH), const)]
    return _stacked_state_call(
        functools.partial(_ssd_kernel, T=T), (nseq, nc), in_specs,
        [proj, proj, proj, buf, s0, cw, cb, dtb, alog, dch, ng, ehp],
        pl.BlockSpec((T, SSD_WIDTH), lambda b, c: (b * nc + c, 0)),
        jax.ShapeDtypeStruct((nseq * seqlen, SSD_WIDTH), BF16),
        sblk[2:], l_out, prev_out,
        [pltpu.VMEM((SUBLANES, SSD_CONV_DIM), F32),
         pltpu.VMEM((T + SUBLANES, SSD_CONV_DIM), F32),
         pltpu.VMEM((T, SSD_WIDTH), F32)],
        "ssd")


SSD_SHORT_NB = 4


def _ssd_short_kernel(z_ref, xbc_ref, dt_ref, buf_ref, s0_ref, cw_ref, cb_ref, dtb_ref, alog_ref,
                      dch_ref, ng_ref, ehp_ref, out_ref, s1_ref, y_scr, *, nb, T):
    n_state = SSD_STATE
    gw = SSD_GROUP_WIDTH
    hpg = SSD_HEADS_PER_GROUP
    R = nb * T

    def per_seq_rows(x):
        return jnp.broadcast_to(x[:, None, :], (nb, T, x.shape[-1])).reshape(R, x.shape[-1])

    def last_rows(x):
        return jnp.concatenate([x[(b + 1) * T - 1:(b + 1) * T, :] for b in range(nb)], axis=0)

    u = xbc_ref[...].reshape(R, SSD_CONV_DIM)
    buf = buf_ref[...].reshape(nb * SUBLANES, SSD_CONV_DIM)
    assert T == SUBLANES
    t_in_seq = _iota2((R, SSD_CONV_DIM), 0) % T
    nbuf = CONV_WIDTH - 1
    xc = _conv_taps(u, lambda j: pltpu.roll(buf, (j - nbuf) % R, 0), cw_ref[...], cb_ref[...],
                    t_in_seq)

    xa = _silu(xc)
    xs = xa[:, :SSD_WIDTH]
    bm = xa[:, SSD_WIDTH:SSD_WIDTH + SSD_GROUPS * n_state]
    cm = xa[:, SSD_WIDTH + SSD_GROUPS * n_state:]
    dt = _softplus(dt_ref[...].reshape(R, DT_PAD) + dtb_ref[...])
    da = dt * (-jnp.exp(alog_ref[...]))
    ri = _iota2((R, R), 0)
    rj = _iota2((R, R), 1)
    same_seq_causal = jnp.logical_and(ri >= rj, ri // T == rj // T)
    cum = _dotx_left(same_seq_causal.astype(BF16), da)
    cum_t = _transpose_rows_to_lanes(cum, R)
    cumx, dtx = _dot_shared_rhs([cum, dt], ehp_ref[...], 2)
    xdt = xs * dtx
    ecum = jnp.exp(cumx)
    xdtd = xdt * jnp.exp(per_seq_rows(last_rows(cumx)) - cumx)

    causal = _iota2((T, T), 0) >= _iota2((T, T), 1)
    seqs = range(nb)
    groups = range(SSD_GROUPS)
    heads = range(SSD_HEADS)
    rows = [slice(b * T, (b + 1) * T) for b in seqs]
    cg = [[cm[rows[b], g * n_state:(g + 1) * n_state] for g in groups] for b in seqs]
    bg = [[bm[rows[b], g * n_state:(g + 1) * n_state] for g in groups] for b in seqs]
    sg = [[s0_ref[0, b, g * hpg:(g + 1) * hpg].reshape(gw, n_state) for g in groups] for b in seqs]
    cb = [[_dot1(cg[b][g], bg[b][g], _NT) for g in groups] for b in seqs]
    y_off = [[_dot1(cg[b][g], sg[b][g], _NT) for g in groups] for b in seqs]
    st = [[_dot1(xdtd[rows[b], g * gw:(g + 1) * gw], bg[b][g], _TN) for g in groups] for b in seqs]
    lm = [[jnp.where(causal, jnp.exp(cum[rows[b], h:h + 1] - cum_t[h:h + 1, rows[b]]), 0.0)
           for h in heads] for b in seqs]
    y_diag = [[_dot1(cb[b][h // hpg] * lm[b][h], xdt[rows[b], h * HEAD_DIM:(h + 1) * HEAD_DIM])
               for h in heads] for b in seqs]
    for b in seqs:
        for g in groups:
            y_scr[rows[b], g * gw:(g + 1) * gw] = (
                jnp.concatenate(y_diag[b][g * hpg:(g + 1) * hpg], axis=1)
                + y_off[b][g] * ecum[rows[b], g * gw:(g + 1) * gw])
            last = (b + 1) * T - 1
            decay = jnp.concatenate(
                [jnp.broadcast_to(jnp.exp(cum_t[h:h + 1, last:last + 1]), (HEAD_DIM, n_state))
                 for h in range(g * hpg, (g + 1) * hpg)], axis=0)
            s1_ref[0, b, g * hpg:(g + 1) * hpg] = (sg[b][g] * decay + st[b][g]).reshape(
                hpg, HEAD_DIM, n_state)
    _zero_other_slabs(s1_ref)

    y = (y_scr[...] + dch_ref[...] * xs) * _silu(z_ref[...].reshape(R, SSD_WIDTH))
    outs = []
    for g in groups:
        yg = y[:, g * gw:(g + 1) * gw]
        ms = jnp.mean(yg * yg, axis=-1, keepdims=True)
        outs.append(yg * lax.rsqrt(ms + 1e-5))
    res = jnp.concatenate(outs, axis=1) * ng_ref[...]
    out_ref[...] = res.astype(out_ref.dtype).reshape(out_ref.shape)


def _ssd_short(p3, buf, s0, l_in, l_out, prev_out, cw, cb, dtb, alog, dch, ng, ehp):
    nseq, T, _ = p3.shape
    nb = SSD_SHORT_NB
    const = lambda b, c: (0, 0)
    sdims = (SSD_HEADS, HEAD_DIM, SSD_STATE)
    blk = lambda width, col: pl.BlockSpec((nb, T, width), lambda b, c: (b, 0, col // width))
    in_specs = [blk(SSD_WIDTH, C_Z), blk(SSD_CONV_DIM, C_XBC), blk(DT_PAD, C_DT),
                pl.BlockSpec((nb, SUBLANES, SSD_CONV_DIM), lambda b, c: (b, 0, 0)),
                pl.BlockSpec((1, nb) + sdims, lambda b, c: (l_in, b, 0, 0, 0)),
                pl.BlockSpec((CONV_WIDTH, SSD_CONV_DIM), const),
                pl.BlockSpec((1, SSD_CONV_DIM), const),
                pl.BlockSpec((1, DT_PAD), const),
                pl.BlockSpec((1, DT_PAD), const),
                pl.BlockSpec((1, SSD_WIDTH), const),
                pl.BlockSpec((1, SSD_WIDTH), const),
                pl.BlockSpec((DT_PAD, SSD_WIDTH), const)]
    out, stack = _stacked_state_call(
        functools.partial(_ssd_short_kernel, nb=nb, T=T), (nseq // nb, 1), in_specs,
        [p3, p3, p3, buf, s0, cw, cb, dtb, alog, dch, ng, ehp],
        pl.BlockSpec((nb, T, SSD_WIDTH), lambda b, c: (b, 0, 0)),
        jax.ShapeDtypeStruct((nseq, T, SSD_WIDTH), BF16),
        sdims, l_out, prev_out,
        [pltpu.VMEM((nb * T, SSD_WIDTH), F32)],
        "ssd_short", nb=nb)
    return out.reshape(nseq * T, SSD_WIDTH), stack


def _unit_lower_inverse(a_list, ii, jj, T):
    n = len(a_list)
    pair = (ii >> 1) == (jj >> 1)
    inv = [jnp.where(ii == jj, 1.0, 0.0) + jnp.where(pair, a_list[h], 0.0) for h in range(n)]
    shift = 1
    while (2 << shift) <= T:
        band = jnp.logical_and((ii >> (shift + 1)) == (jj >> (shift + 1)),
                               (ii >> shift) != (jj >> shift))
        x = [_dot3(jnp.where(band, a_list[h], 0.0), inv[h]) for h in range(n)]
        inv = [inv[h] + _dot3(inv[h], x[h]) for h in range(n)]
        shift += 1
    return inv


def _rwkv_kernel(r_ref, k_ref, v_ref, g_ref, wa_ref, sh_ref, s0_ref, mu_ref, w0_ref, wup_ref,
                 a0_ref, aup_ref, gup_ref, kkw_ref, kaw_ref, rkw_ref, lng_ref, lnb_ref, ones_ref,
                 out_ref, s1_ref, prev_scr, s_scr, al_scr, be_scr, kt_scr, rt_scr, bs_scr,
                 ks_scr, v_scr, gam_scr, o_scr, *, nb, T, nchunks):
    c = pl.program_id(1)
    W = RWKV_WIDTH
    D = HEAD_DIM
    R = nb * T
    nh = RWKV_HEADS

    @pl.when(c == 0)
    def _():
        prev_scr[...] = sh_ref[...].reshape(nb, RW_PACK)
        s_scr[...] = s0_ref[0].reshape(nb * nh, D, D)

    def rows_of(ref3):
        return ref3[...].reshape(R, ref3.shape[-1])

    def per_seq_rows(x):
        return jnp.broadcast_to(x[:, None, :], (nb, T, x.shape[-1])).reshape(R, x.shape[-1])

    def last_rows(x):
        return jnp.concatenate([x[(b + 1) * T - 1:(b + 1) * T, :] for b in range(nb)], axis=0)

    def token_shift(p, lo, hi):
        first = _iota2(p.shape, 0) % T == 0
        prev = jnp.where(first, per_seq_rows(prev_scr[:, lo:hi]), pltpu.roll(p, 1, 0))
        prev_scr[:, lo:hi] = last_rows(p)
        return p + (prev - p) * mu_ref[:, lo:hi]

    xr = token_shift(rows_of(r_ref), 0, W)
    xk = token_shift(rows_of(k_ref), W, 2 * W)
    xv = token_shift(rows_of(v_ref), 2 * W, 3 * W)
    xg = token_shift(rows_of(g_ref), 3 * W, 3 * W + G_PAD)
    xwa = token_shift(rows_of(wa_ref), 3 * W + G_PAD, RW_PACK)

    w_lin = w0_ref[...] + _dot1(jnp.tanh(xwa), wup_ref[...])
    a = _sigmoid(a0_ref[...] + _dot1(xwa, aup_ref[...]))
    gate = _dot1(_sigmoid(xg), gup_ref[...])
    lw = -jnp.exp(-_softplus(-w_lin) - 0.5)

    ones = ones_ref[...]

    def head_sums(xs, npieces=3):
        tw = ones.shape[0]
        nt = W // tw
        tiles = [x[:, i * tw:(i + 1) * tw] for x in xs for i in range(nt)]
        sums = _dot_shared_rhs(tiles, ones, npieces)
        return [jnp.concatenate(sums[n * nt:(n + 1) * nt], axis=1) for n in range(len(xs))]

    kk = xk * kkw_ref[...]
    kp = xk * (1.0 + (a - 1.0) * kaw_ref[...])
    kk_sq, rk_sum = head_sums([kk * kk, xr * kp * rkw_ref[...]], 1)
    kk = kk / jnp.maximum(jnp.sqrt(kk_sq), 1e-12)

    ri = _iota2((R, R), 0)
    rj = _iota2((R, R), 1)
    same_seq_causal = jnp.logical_and(ri >= rj, ri // T == rj // T)
    cum = _dotx_left(same_seq_causal.astype(BF16), lw, 3)
    e_neg = jnp.exp(-cum)
    gam = jnp.exp(per_seq_rows(last_rows(cum)))
    be = kk * a * e_neg
    kt = kp * e_neg
    al_scr[...] = -kk * jnp.exp(cum - lw)
    be_scr[...] = be
    kt_scr[...] = kt
    rt_scr[...] = xr * jnp.exp(cum)
    bs_scr[...] = be * gam
    ks_scr[...] = kt * gam
    v_scr[...] = xv
    gam_scr[...] = gam

    ii = _iota2((T, T), 0)
    jj = _iota2((T, T), 1)
    incl = ii >= jj
    strict = ii > jj
    eye_d = _iota2((D, D), 0) == _iota2((D, D), 1)

    units = [(b, h) for b in range(nb) for h in range(nh)]
    nu = range(len(units))

    def per_unit(ref):
        return [ref[b * T:(b + 1) * T, h * D:(h + 1) * D] for b, h in units]

    al, be_u, kt_u, rt = per_unit(al_scr), per_unit(be_scr), per_unit(kt_scr), per_unit(rt_scr)
    bs_u, ks_u, vv, gam_u = per_unit(bs_scr), per_unit(ks_scr), per_unit(v_scr), per_unit(gam_scr)
    gram = [_dot3(jnp.concatenate([al[u], rt[u]], axis=0),
                  jnp.concatenate([be_u[u], kt_u[u]], axis=0), _NT) for u in nu]
    a_ab = [jnp.where(strict, gram[u][:T, :T], 0.0) for u in nu]
    a_ak = [jnp.where(strict, gram[u][:T, T:], 0.0) for u in nu]
    r_b = [jnp.where(incl, gram[u][T:, :T], 0.0) for u in nu]
    r_k = [jnp.where(incl, gram[u][T:, T:], 0.0) for u in nu]
    inv = _unit_lower_inverse(a_ab, ii, jj, T)
    akv = [_dot3(a_ak[u], vv[u]) for u in nu]
    pw = [_dot3(inv[u], jnp.concatenate([al[u], akv[u]], axis=1)) for u in nu]
    qo = [_dot3(r_b[u], pw[u]) for u in nu]
    rkv = [_dot3(r_k[u], vv[u]) for u in nu]
    smat = [s_scr[u] for u in nu]
    qs = [_dot3(rt[u] + qo[u][:, :D], smat[u], _NT) for u in nu]
    gp = [_dot3(bs_u[u], pw[u][:, :D], _TN) for u in nu]
    zt = [_dot3(jnp.concatenate([pw[u][:, D:], vv[u]], axis=0),
                jnp.concatenate([bs_u[u], ks_u[u]], axis=0), _TN) for u in nu]
    gmat = [jnp.where(eye_d, jnp.broadcast_to(gam_u[u][0:1, :], (D, D)), 0.0) + gp[u] for u in nu]
    sg = [_dot3(smat[u], gmat[u], _NT) for u in nu]
    for u in nu:
        s_scr[u] = sg[u] + zt[u]
    pairs = LANES // D
    for b in range(nb):
        for p in range(nh // pairs):
            us = [b * nh + p * pairs + s for s in range(pairs)]
            o_scr[b * T:(b + 1) * T, p * LANES:(p + 1) * LANES] = jnp.concatenate(
                [qs[u] + qo[u][:, D:] + rkv[u] for u in us], axis=1)

    o = o_scr[...]
    mean = head_sums([o])[0] * (1.0 / D)
    d = o - mean
    var = head_sums([d * d])[0] * (1.0 / D)
    on = d * lax.rsqrt(var + RWKV_GN_EPS) * lng_ref[...] + lnb_ref[...]
    bonus = rk_sum * xv
    out_ref[...] = ((on + bonus) * gate).astype(out_ref.dtype).reshape(nb, T, W)

    @pl.when(c == nchunks - 1)
    def _():
        s1_ref[0] = s_scr[...].reshape(nb, nh, D, D)
        _zero_other_slabs(s1_ref)


def _rwkv(proj3, sh, s0, l_in, l_out, prev_out, mu, w0, wup, a0, aup, gup, kkw, kaw, rkw, lng, lnb,
          ones, *, nb):
    nseq, seqlen, _ = proj3.shape
    T = min(RWKV_CHUNK, seqlen)
    nc = seqlen // T
    W = RWKV_WIDTH
    const = lambda b, c: (0, 0)
    rowblk = lambda col, width: pl.BlockSpec((nb, T, width), lambda b, c: (b, c, col // width))
    sdims = (RWKV_HEADS, HEAD_DIM, HEAD_DIM)
    vec = pl.BlockSpec((1, W), const)
    in_specs = [rowblk(C_R, W), rowblk(C_K, W), rowblk(C_V, W), rowblk(C_G, G_PAD),
                rowblk(C_WA, WA_PAD),
                pl.BlockSpec((nb, 1, RW_PACK), lambda b, c: (b, 0, 0)),
                pl.BlockSpec((1, nb) + sdims, lambda b, c: (l_in, b, 0, 0, 0)),
                pl.BlockSpec((1, RW_PACK), const),
                vec, pl.BlockSpec((WA_PAD, W), const),
                vec, pl.BlockSpec((WA_PAD, W), const),
                pl.BlockSpec((G_PAD, W), const),
                vec, vec, vec, vec, vec,
                pl.BlockSpec((2 * LANES, 2 * LANES), const)]
    rows = nb * T
    return _stacked_state_call(
        functools.partial(_rwkv_kernel, nb=nb, T=T, nchunks=nc), (nseq // nb, nc), in_specs,
        [proj3, proj3, proj3, proj3, proj3, sh, s0, mu, w0, wup, a0, aup, gup, kkw, kaw, rkw, lng,
         lnb, ones],
        pl.BlockSpec((nb, T, W), lambda b, c: (b, c, 0)),
        jax.ShapeDtypeStruct((nseq, seqlen, W), BF16),
        sdims, l_out, prev_out,
        [pltpu.VMEM((nb, RW_PACK), F32), pltpu.VMEM((nb * RWKV_HEADS, HEAD_DIM, HEAD_DIM), F32)]
        + [pltpu.VMEM((rows, W), F32)] * 7
        + [pltpu.VMEM((rows, W), F32), pltpu.VMEM((rows, W), F32)],
        "rwkv", nb=nb)


def _rws_pre_kernel(r_ref, k_ref, v_ref, g_ref, wa_ref, shr_ref, shk_ref, shv_ref, shg_ref,
                    shwa_ref, mur_ref, muk_ref, muv_ref, mug_ref, muwa_ref, w0_ref, wup_ref,
                    a0_ref, aup_ref, gup_ref, kkw_ref, kaw_ref, rkw_ref, ones_ref,
                    rt_ref, wt_ref, knt_ref, kat_ref, kpt_ref, vt_ref, gate_ref, bonus_ref):
    steps, nseq, _ = r_ref.shape

    def shifted(ref, sh_ref, mu_ref):
        x = ref[...]
        prev = jnp.concatenate([sh_ref[...][None], x[:steps - 1]], axis=0)
        return (x + (prev - x) * mu_ref[...]).reshape(steps * nseq, x.shape[-1])

    xr = shifted(r_ref, shr_ref, mur_ref)
    xk = shifted(k_ref, shk_ref, muk_ref)
    xv = shifted(v_ref, shv_ref, muv_ref)
    xg = shifted(g_ref, shg_ref, mug_ref)
    xwa = shifted(wa_ref, shwa_ref, muwa_ref)

    w_lin = w0_ref[...] + _dot1(jnp.tanh(xwa), wup_ref[...])
    a = _sigmoid(a0_ref[...] + _dot1(xwa, aup_ref[...]))
    gate = _dot1(_sigmoid(xg), gup_ref[...])
    decay = jnp.exp(-jnp.exp(-_softplus(-w_lin) - 0.5))
    kk = xk * kkw_ref[...]
    kp = xk * (1.0 + (a - 1.0) * kaw_ref[...])
    kk_sq, rk_sum = _dot_shared_rhs([kk * kk, xr * kp * rkw_ref[...]], ones_ref[...], 3)
    kk = kk / jnp.maximum(jnp.sqrt(kk_sq), 1e-12)

    for t in range(steps):
        rows = slice(t * nseq, (t + 1) * nseq)
        rt_ref[t] = xr[rows].T
        wt_ref[t] = decay[rows].T
        knt_ref[t] = (-kk[rows]).T
        kat_ref[t] = (kk[rows] * a[rows]).T
        kpt_ref[t] = kp[rows].T
        vt_ref[t] = xv[rows].T
    gate_ref[...] = gate.reshape(gate_ref.shape)
    bonus_ref[...] = (rk_sum * xv).reshape(bonus_ref.shape)


def _rws_rec_kernel(r_ref, w_ref, kn_ref, ka_ref, kp_ref, v_ref, s0_ref, o_ref, s1_ref):
    steps = r_ref.shape[0]

    def value_rows(vc, carry):
        v0 = pl.multiple_of(vc * SUBLANES, SUBLANES)
        s = [s0_ref[0, 0, v0 + i] for i in range(SUBLANES)]
        for t in range(steps):
            kn, ka, kp, w, r = kn_ref[t], ka_ref[t], kp_ref[t], w_ref[t], r_ref[t]
            vv = v_ref[t, pl.ds(v0, SUBLANES), :]
            outs = []
            for i in range(SUBLANES):
                sa = jnp.sum(s[i] * kn, axis=0, keepdims=True)
                s[i] = s[i] * w + (sa * ka + vv[i:i + 1, :] * kp)
                outs.append(jnp.sum(s[i] * r, axis=0, keepdims=True))
            o_ref[t, pl.ds(v0, SUBLANES), :] = jnp.concatenate(outs, axis=0)
        for i in range(SUBLANES):
            s1_ref[0, 0, v0 + i] = s[i]
        return carry

    lax.fori_loop(0, HEAD_DIM // SUBLANES, value_rows, 0)
    _zero_other_slabs(s1_ref)


def _rws_post_kernel(o_ref, gate_ref, bonus_ref, lng_ref, lnb_ref, ones_ref, out_ref):
    steps, _, nseq = o_ref.shape
    o = jnp.concatenate([o_ref[t].T for t in range(steps)], axis=0)
    ones = ones_ref[...]
    mean = _dot_shared_rhs([o], ones, 3)[0] * (1.0 / HEAD_DIM)
    d = o - mean
    var = _dot_shared_rhs([d * d], ones, 3)[0] * (1.0 / HEAD_DIM)
    on = d * lax.rsqrt(var + RWKV_GN_EPS) * lng_ref[...] + lnb_ref[...]
    rows = steps * nseq
    res = (on + bonus_ref[...].reshape(rows, LANES)) * gate_ref[...].reshape(rows, LANES)
    out_ref[...] = res.astype(out_ref.dtype).reshape(out_ref.shape)


def _rwkv_steps(p3, sh, s0_t, l_in, l_out, prev_out, mu, w0, wup, a0, aup, gup, kkw, kaw, rkw, lng,
                lnb, ones):
    nseq, steps, _ = p3.shape
    W, D, H = RWKV_WIDTH, HEAD_DIM, RWKV_HEADS
    pt = jnp.transpose(p3[:, :, C_R:], (1, 0, 2))
    nblk = W // LANES
    ones_blk = ones[:LANES, :LANES]
    g_col, wa_col = 3 * W, 3 * W + G_PAD

    def cols(rows, width, col):
        return pl.BlockSpec(rows + (width,), lambda p: (0,) * len(rows) + (col // width,))

    def cols_p(rows, base):
        return pl.BlockSpec(rows + (LANES,), lambda p: (0,) * len(rows) + (base // LANES + p,))

    tb = (steps, nseq)
    in_specs = ([cols_p(tb, 0), cols_p(tb, W), cols_p(tb, 2 * W), cols(tb, G_PAD, g_col),
                 cols(tb, WA_PAD, wa_col)]
                + [cols_p((nseq,), 0), cols_p((nseq,), W), cols_p((nseq,), 2 * W),
                   cols((nseq,), G_PAD, g_col), cols((nseq,), WA_PAD, wa_col)]
                + [cols_p((1,), 0), cols_p((1,), W), cols_p((1,), 2 * W), cols((1,), G_PAD, g_col),
                   cols((1,), WA_PAD, wa_col)]
                + [cols_p((1,), 0), cols_p((WA_PAD,), 0), cols_p((1,), 0), cols_p((WA_PAD,), 0),
                   cols_p((G_PAD,), 0), cols_p((1,), 0), cols_p((1,), 0), cols_p((1,), 0),
                   pl.BlockSpec((LANES, LANES), lambda p: (0, 0))])
    t_spec = pl.BlockSpec((steps, LANES, nseq), lambda p: (0, p, 0))
    n_spec = pl.BlockSpec((steps, nseq, LANES), lambda p: (0, 0, p))
    t_shape = jax.ShapeDtypeStruct((steps, W, nseq), F32)
    n_shape = jax.ShapeDtypeStruct((steps, nseq, W), F32)
    pre = pl.pallas_call(
        _rws_pre_kernel, grid=(nblk,), in_specs=in_specs,
        out_specs=[t_spec] * 6 + [n_spec] * 2, out_shape=[t_shape] * 6 + [n_shape] * 2,
        compiler_params=_cparams("arbitrary"), name="rwkv_pre",
    )(pt, pt, pt, pt, pt, sh, sh, sh, sh, sh, mu, mu, mu, mu, mu, w0, wup, a0, aup, gup, kkw, kaw,
      rkw, ones_blk)
    rt, wt, knt, kat, kpt, vt, gate, bonus = pre

    head = pl.BlockSpec((steps, D, nseq), lambda h, c: (0, h, 0))
    o_t, stack = _stacked_state_call(
        _rws_rec_kernel, (H, 1),
        [head] * 6 + [pl.BlockSpec((1, 1, D, D, nseq), lambda h, c: (l_in, h, 0, 0, 0))],
        [rt, wt, knt, kat, kpt, vt, s0_t],
        head, t_shape, (D, D, nseq), l_out, prev_out, [], "rwkv_rec")

    out = pl.pallas_call(
        _rws_post_kernel, grid=(nblk,),
        in_specs=[t_spec, n_spec, n_spec, pl.BlockSpec((1, LANES), lambda p: (0, p)),
                  pl.BlockSpec((1, LANES), lambda p: (0, p)),
                  pl.BlockSpec((LANES, LANES), lambda p: (0, 0))],
        out_specs=n_spec, out_shape=jax.ShapeDtypeStruct((steps, nseq, W), BF16),
        compiler_params=_cparams("arbitrary"), name="rwkv_post",
    )(o_t, gate, bonus, lng, lnb, ones_blk)
    return jnp.transpose(out, (1, 0, 2)).reshape(nseq * steps, W), stack


def _zeros_like_cols(x, n):
    return jnp.zeros(x.shape[:-1] + (n,), x.dtype)


def _pack_rwkv_cols(x):
    return jnp.concatenate([x[..., :_O_XW], x[..., _O_XG:], _zeros_like_cols(x, G_PAD - RWKV_R_G),
                            x[..., _O_XW:_O_XG]], axis=-1)


def _pack_tail_cols(w):
    return jnp.concatenate([_pack_rwkv_cols(w[..., _O_RW:]), w[..., _O_DT:_O_RW],
                            _zeros_like_cols(w, DT_PAD - SSD_HEADS)], axis=-1)


def _pad_rows(x, n_before, n_total):
    b, r, c = x.shape
    return jnp.concatenate([jnp.zeros((b, n_before, c), x.dtype), x,
                            jnp.zeros((b, n_total - n_before - r, c), x.dtype)], axis=1)


def _small_params(l, p):
    row = lambda v: v.reshape(1, -1)
    pad_lanes = lambda v, n: jnp.concatenate([v, jnp.zeros((n - v.shape[0],), v.dtype)]).reshape(1, n)
    zeros_w = jnp.zeros((RWKV_R_W, RWKV_WIDTH), F32)
    return dict(
        lru=(p['lru_conv_w'][l], row(p['lru_conv_b'][l]), p['lru_wa'][l].astype(BF16),
             row(p['lru_ba'][l]), p['lru_wx'][l].astype(BF16), row(p['lru_bx'][l]),
             row(p['lru_lambda'][l])),
        ssd=(p['ssd_conv_w'][l], row(p['ssd_conv_b'][l]), pad_lanes(p['ssd_dt_bias'][l], DT_PAD),
             pad_lanes(p['ssd_a_log'][l], DT_PAD), row(jnp.repeat(p['ssd_d'][l], HEAD_DIM)),
             row(p['ssd_norm_g'][l])),
        rwkv=(row(_pack_rwkv_cols(p['rwkv_mu'][l])), row(p['rwkv_w0'][l]),
              jnp.concatenate([p['rwkv_w_up'][l], zeros_w], axis=0).astype(BF16),
              row(p['rwkv_a0'][l]),
              jnp.concatenate([zeros_w, p['rwkv_a_up'][l]], axis=0).astype(BF16),
              jnp.concatenate([p['rwkv_g_up'][l],
                               jnp.zeros((G_PAD - RWKV_R_G, RWKV_WIDTH), F32)], axis=0).astype(BF16),
              row(p['rwkv_k_k'][l]), row(p['rwkv_k_a'][l]), row(p['rwkv_r_k'][l]),
              row(p['rwkv_ln_g'][l]), row(p['rwkv_ln_b'][l])),
        ln1=(row(p['ln1_g'][l]), row(p['ln1_b'][l])),
        ln2=(row(p['ln2_g'][l]), row(p['ln2_b'][l])),
    )


def _constants():
    lane = jnp.arange(SSD_WIDTH) // HEAD_DIM
    ehp = (jnp.arange(DT_PAD)[:, None] == lane[None, :]).astype(BF16)
    blk = jnp.arange(2 * LANES) // HEAD_DIM
    ones = (blk[:, None] == blk[None, :]).astype(BF16)
    return ehp, ones


def _layer(x, xb, l, small_state, ssd_in, rw_in, prev_outs, sp, big, consts, *, nseq, seqlen,
           lru_nblk, rwkv_nb):
    lru_conv0, lru_h0, ssd_conv0, rw_shift0 = small_state
    ehp, ones = consts
    nbuf = CONV_WIDTH - 1
    proj = _proj(xb, big['w_main'], big['w_tail'], l)

    lru_nseq = nseq // lru_nblk
    out_a, lru_h1 = _lru(
        proj, _pad_rows(lru_conv0, 0, SUBLANES).reshape(lru_nblk, lru_nseq * SUBLANES, LRU_WIDTH),
        lru_h0.reshape(lru_nblk, lru_nseq, LRU_WIDTH), *sp['lru'],
        nblk=lru_nblk, nseq=lru_nseq, seqlen=seqlen)
    p3 = proj.reshape(nseq, seqlen, N_PROJ)
    if seqlen == SUBLANES and nseq % SSD_SHORT_NB == 0:
        out_b, ssd_out = _ssd_short(p3, _pad_rows(ssd_conv0, 0, SUBLANES), ssd_in[0], ssd_in[1], l,
                                    prev_outs[0], *sp['ssd'], ehp)
    else:
        out_b, ssd_out = _ssd(proj, _pad_rows(ssd_conv0, SUBLANES - nbuf, SUBLANES), ssd_in[0],
                              ssd_in[1], l, prev_outs[0], *sp['ssd'], ehp, nseq=nseq, seqlen=seqlen)
    if rwkv_nb is None:
        out_c, rw_out = _rwkv_steps(p3, _pack_rwkv_cols(rw_shift0), rw_in[0], rw_in[1], l,
                                    prev_outs[1], *sp['rwkv'], ones)
    else:
        out_c, rw_out = _rwkv(p3, _pack_rwkv_cols(rw_shift0)[:, None, :], rw_in[0], rw_in[1], l,
                              prev_outs[1], *sp['rwkv'], ones, nb=rwkv_nb)
        out_c = out_c.reshape(nseq * seqlen, RWKV_WIDTH)

    y, yb = _outproj_ln(out_a, out_b, out_c, big['w_out'], l, x, *sp['ln1'])
    y, yb = _ffn_down_ln(_ffn_up(yb, big['w_gate'], big['w_up'], l), big['w_down'], l, y, *sp['ln2'])

    tail = lambda col, width: p3[:, seqlen - nbuf:, col:col + width]
    last = lambda col, width: p3[:, seqlen - 1, col:col + width]
    rw_shift1 = jnp.concatenate([last(C_R, 3 * RWKV_WIDTH), last(C_WA, WA_PAD),
                                 last(C_G, RWKV_R_G)], axis=-1)
    small_new = (tail(C_LX, LRU_WIDTH), lru_h1.reshape(nseq, LRU_WIDTH),
                 tail(C_XBC, SSD_CONV_DIM), rw_shift1)
    return y, yb, small_new, (ssd_out, rw_out)


def kernel(x_prompt, x_sample, state_lru_conv, state_lru_h, state_ssd_conv, state_ssd,
           state_rwkv_shift, state_rwkv, w_in, lru_conv_w, lru_conv_b, lru_wa, lru_ba, lru_wx,
           lru_bx, lru_lambda, ssd_conv_w, ssd_conv_b, ssd_dt_bias, ssd_a_log, ssd_d, ssd_norm_g,
           rwkv_mu, rwkv_w0, rwkv_w_up, rwkv_a0, rwkv_a_up, rwkv_g_up, rwkv_k_k, rwkv_k_a,
           rwkv_r_k, rwkv_ln_g, rwkv_ln_b, w_out, ln1_g, ln1_b, w_gate, w_up, w_down, ln2_g, ln2_b):
    params = dict(
        lru_conv_w=lru_conv_w, lru_conv_b=lru_conv_b, lru_wa=lru_wa, lru_ba=lru_ba,
        lru_wx=lru_wx, lru_bx=lru_bx, lru_lambda=lru_lambda, ssd_conv_w=ssd_conv_w,
        ssd_conv_b=ssd_conv_b, ssd_dt_bias=ssd_dt_bias, ssd_a_log=ssd_a_log, ssd_d=ssd_d,
        ssd_norm_g=ssd_norm_g, rwkv_mu=rwkv_mu, rwkv_w0=rwkv_w0, rwkv_w_up=rwkv_w_up,
        rwkv_a0=rwkv_a0, rwkv_a_up=rwkv_a_up, rwkv_g_up=rwkv_g_up, rwkv_k_k=rwkv_k_k,
        rwkv_k_a=rwkv_k_a, rwkv_r_k=rwkv_r_k.reshape(DEPTH, RWKV_WIDTH), rwkv_ln_g=rwkv_ln_g,
        rwkv_ln_b=rwkv_ln_b, ln1_g=ln1_g, ln1_b=ln1_b, ln2_g=ln2_g, ln2_b=ln2_b)
    big = dict(w_main=jnp.swapaxes(w_in, 1, 2),
               w_tail=jnp.swapaxes(_pack_tail_cols(w_in), 1, 2),
               w_out=w_out.astype(BF16),
               w_gate=w_gate, w_up=w_up, w_down=w_down.astype(BF16))
    bp, lp_len, _ = x_prompt.shape
    bs, ls_len, _ = x_sample.shape
    consts = _constants()
    nbuf = CONV_WIDTH - 1
    zero_small = (jnp.zeros((bp, nbuf, LRU_WIDTH), F32), jnp.zeros((bp, LRU_WIDTH), F32),
                  jnp.zeros((bp, nbuf, SSD_CONV_DIM), F32), jnp.zeros((bp, RWKV_SHIFT), F32))
    zero_ssd = jnp.zeros((1, bp, SSD_HEADS, HEAD_DIM, SSD_STATE), F32)
    zero_rw = jnp.zeros((1, bp, RWKV_HEADS, HEAD_DIM, HEAD_DIM), F32)
    yp = x_prompt.reshape(bp * lp_len, D_MODEL)
    ys = x_sample.reshape(bs * ls_len, D_MODEL)
    ypb = yp.astype(BF16)
    ysb = ys.astype(BF16)
    state_rwkv_t = jnp.transpose(state_rwkv, (0, 2, 3, 4, 1))
    new_p = [[] for _ in range(4)]
    new_s = [[] for _ in range(4)]
    outs_p = (None, None)
    outs_s = (None, None)
    for l in range(DEPTH):
        sp = _small_params(l, params)
        yp, ypb, small_p, outs_p = _layer(yp, ypb, l, zero_small, (zero_ssd, 0), (zero_rw, 0), outs_p, sp,
                                     big, consts, nseq=bp, seqlen=lp_len, lru_nblk=bp,
                                     rwkv_nb=RWKV_PROMPT_NB)
        ys, ysb, small_s, outs_s = _layer(
            ys, ysb, l, (state_lru_conv[l], state_lru_h[l], state_ssd_conv[l], state_rwkv_shift[l]),
            (state_ssd, l), (state_rwkv_t, l), outs_s, sp, big, consts,
            nseq=bs, seqlen=ls_len, lru_nblk=1, rwkv_nb=None)
        for i in range(4):
            new_p[i].append(small_p[i])
            new_s[i].append(small_s[i])
    p_lru_conv, p_lru_h, p_ssd_conv, p_rw_shift = [jnp.stack(v) for v in new_p]
    s_lru_conv, s_lru_h, s_ssd_conv, s_rw_shift = [jnp.stack(v) for v in new_s]
    return (yp.reshape(bp, lp_len, D_MODEL), ys.reshape(bs, ls_len, D_MODEL),
            p_lru_conv, p_lru_h, p_ssd_conv, outs_p[0], p_rw_shift, outs_p[1],
            s_lru_conv, s_lru_h, s_ssd_conv, outs_s[0], s_rw_shift,
            jnp.transpose(outs_s[1], (0, 4, 1, 2, 3)))
```
